```python
import math
import jax, jax.numpy as jnp
from jax import lax
import numpy as np

D_MODEL = 2048
BATCH = 2
SEQ = 4096
DEPTH = 1
DEC_BATCH = 8
DEC_SEQ = 1
PAST_LEN = 16384
PAGE_SIZE = 128

HEAD_DIM = 128
WIDTH_A = D_MODEL // 2
HEADS_A = WIDTH_A // HEAD_DIM
DILATED = ((128, 1), (512, 4), (2048, 16))
WIN_MAX = 2048
N_BUCKETS = 32
MAX_DISTANCE = WIN_MAX
DK = 128
DV = 128
V_HEADS_B = (D_MODEL - WIDTH_A) // DV
QK_HEADS_B = V_HEADS_B // 2
WIDTH_BQK = QK_HEADS_B * DK
WIDTH_BV = V_HEADS_B * DV
CONV_W = 4
CONV_DIM = 2 * WIDTH_BQK + WIDTH_BV
CHUNK = 64
D_FF = 5632
FFN_CONV_W = 3
EPS = 1e-6
NEG = -1e30

SPLITS = (WIDTH_A, WIDTH_A, WIDTH_A, WIDTH_BQK, WIDTH_BQK, WIDTH_BV, WIDTH_BV, V_HEADS_B, V_HEADS_B)
SPLIT_IDX = tuple(sum(SPLITS[:i + 1]) for i in range(len(SPLITS) - 1))
PROJ_DIM = sum(SPLITS)
MIX_OUT = WIDTH_A + WIDTH_BV

kernel_name = "hybrid_dilated_swa_gated_deltanet_convffn_step"


def rmsnorm(x, w):
    xf = x.astype(jnp.float32)
    y = xf * lax.rsqrt(jnp.mean(xf * xf, axis=-1, keepdims=True) + EPS)
    return (y * w.astype(jnp.float32)).astype(x.dtype)


def l2norm(x):
    xf = x.astype(jnp.float32)
    return xf * lax.rsqrt(jnp.sum(xf * xf, axis=-1, keepdims=True) + EPS)


def rel_bucket(dist):
    max_exact = N_BUCKETS // 2
    d = jnp.maximum(dist, 1).astype(jnp.float32)
    large = max_exact + (jnp.log(d / max_exact) / math.log(MAX_DISTANCE / max_exact)
                         * (N_BUCKETS - max_exact)).astype(jnp.int32)
    large = jnp.minimum(large, N_BUCKETS - 1)
    return jnp.where(dist < max_exact, dist, large)


def causal_dwconv(u, buf, w):
    full = jnp.concatenate([buf.astype(u.dtype), u], axis=1)
    out = lax.conv_general_dilated(full, w[:, None, :].astype(u.dtype), (1,), 'VALID',
                                   dimension_numbers=('NWC', 'WIO', 'NWC'),
                                   feature_group_count=u.shape[-1])
    return out, full[:, full.shape[1] - (w.shape[0] - 1):]


def dilated_branch_prompt(q, k, v, window, dil, rel_bias):
    B, S, H, Dh = q.shape
    blk = window // dil
    L = -(-S // dil)
    nb = -(-L // blk)
    Lp = nb * blk

    def to_sub(t):
        t = jnp.pad(t, ((0, 0), (0, L * dil - S), (0, 0), (0, 0)))
        t = t.reshape(B, L, dil, H, Dh).transpose(0, 2, 1, 3, 4)
        return jnp.pad(t, ((0, 0), (0, 0), (0, Lp - L), (0, 0), (0, 0)))

    def band(t):
        t = jnp.pad(t, ((0, 0), (0, 0), (blk, 0), (0, 0), (0, 0))).reshape(B, dil, nb + 1, blk, H, Dh)
        return jnp.concatenate([t[:, :, :-1], t[:, :, 1:]], axis=3)

    qb = to_sub(q).reshape(B, dil, nb, blk, H, Dh)
    kb = band(to_sub(k))
    vb = band(to_sub(v))
    qi = jnp.arange(blk)[:, None]
    kj = jnp.arange(2 * blk)[None, :]
    delta = blk + qi - kj
    inwin = (delta >= 0) & (delta <= blk)
    bias = jnp.transpose(rel_bias[rel_bucket(jnp.clip(delta, 0, blk) * dil)], (2, 0, 1))
    blk_ok = (jnp.arange(nb)[:, None, None] > 0) | (jnp.arange(2 * blk) >= blk)[None, None, :]
    valid = inwin[None] & blk_ok
    s = jnp.einsum('bdnqhe,bdnkhe->bdnhqk', qb, kb).astype(jnp.float32) * (Dh ** -0.5) + bias
    s = jnp.where(valid[:, None], s, NEG)
    m = jnp.max(s, axis=-1)
    p = jnp.exp(s - m[..., None])
    l = jnp.sum(p, axis=-1)
    acc = jnp.einsum('bdnhqk,bdnkhe->bdnqhe', p, vb.astype(jnp.float32))

    def from_sub(t):
        tail = t.shape[5:]
        t = t.reshape((B, dil, Lp, H) + tail)[:, :, :L]
        t = jnp.moveaxis(t, 1, 2).reshape((B, L * dil, H) + tail)
        return t[:, :S]

    return from_sub(jnp.swapaxes(m, 3, 4)), from_sub(jnp.swapaxes(l, 3, 4)), from_sub(acc)


def dilated_branch_sample(q, kk, vv, past, window, dil, rel_bias):
    T, Dh = q.shape[1], q.shape[3]
    j = jnp.arange(window // dil + 1)
    idx = (past + jnp.arange(T))[:, None] - dil * j[None, :]
    valid = idx >= 0
    idx = jnp.maximum(idx, 0)
    kg = kk[:, idx]
    vg = vv[:, idx]
    bias = jnp.transpose(rel_bias[rel_bucket(j * dil)])
    s = jnp.einsum('bthe,btjhe->bthj', q, kg).astype(jnp.float32) * (Dh ** -0.5) + bias
    s = jnp.where(valid[None, :, None, :], s, NEG)
    m = jnp.max(s, axis=-1)
    p = jnp.exp(s - m[..., None])
    l = jnp.sum(p, axis=-1)
    acc = jnp.einsum('bthj,btjhe->bthe', p, vg.astype(jnp.float32))
    return m, l, acc


def merge_branches(parts):
    M = parts[0][0]
    for m, _, _ in parts[1:]:
        M = jnp.maximum(M, m)
    num = sum(jnp.exp(m - M)[..., None] * acc for m, _, acc in parts)
    den = sum(jnp.exp(m - M) * l for m, l, _ in parts)
    return num / den[..., None]


def gated_delta_rule(q, k, v, g, beta, S0):
    B, T, H, _ = q.shape
    pad = (-T) % CHUNK
    n = (T + pad) // CHUNK

    def chunks(t):
        t = jnp.pad(t.astype(jnp.float32), ((0, 0), (0, pad)) + ((0, 0),) * (t.ndim - 2))
        t = t.reshape((B, n, CHUNK) + t.shape[2:])
        return jnp.moveaxis(t, (1, 3), (0, 2))

    qc = chunks(q) * (DK ** -0.5)
    kc = chunks(k)
    vc = chunks(v)
    bc = chunks(beta)
    gc = jnp.cumsum(chunks(g), axis=-1)
    tri = jnp.tril(jnp.ones((CHUNK, CHUNK), bool))
    strict = jnp.tril(jnp.ones((CHUNK, CHUNK), bool), -1)
    decay = jnp.exp(jnp.where(tri, gc[..., :, None] - gc[..., None, :], NEG))
    kbeta = kc * bc[..., None]
    A = jnp.where(strict, jnp.einsum('nbhid,nbhjd->nbhij', kbeta, kc) * decay, 0.0)
    eye = jnp.eye(CHUNK, dtype=jnp.float32)
    Tm = lax.linalg.triangular_solve(eye + A, jnp.broadcast_to(eye, A.shape), left_side=True,
                                     lower=True, unit_diagonal=True)
    w = jnp.einsum('nbhij,nbhjd->nbhid', Tm, kbeta * jnp.exp(gc)[..., None])
    u = jnp.einsum('nbhij,nbhjd->nbhid', Tm, vc * bc[..., None])
    attn = jnp.einsum('nbhid,nbhjd->nbhij', qc, kc) * decay
    qg = qc * jnp.exp(gc)[..., None]
    kd = kc * jnp.exp(gc[..., -1:] - gc)[..., None]
    glast = jnp.exp(gc[..., -1])

    def step(S, xs):
        w_i, u_i, qg_i, attn_i, kd_i, gl_i = xs
        v_new = u_i - w_i @ S
        o = qg_i @ S + attn_i @ v_new
        S = S * gl_i[..., None, None] + jnp.swapaxes(kd_i, -1, -2) @ v_new
        return S, o

    S_T, o = lax.scan(step, S0.astype(jnp.float32), (w, u, qg, attn, kd, glast))
    o = jnp.moveaxis(o, (0, 2), (1, 3)).reshape(B, n * CHUNK, H, DV)[:, :T]
    return o, S_T


def delta_mixer(bq, bk, bv, bz, bb, ba, conv_buf, S0, dn_conv_w, dn_A_log, dn_dt_bias, dn_norm_w):
    B, T, _ = bq.shape
    qkv, conv_new = causal_dwconv(jnp.concatenate([bq, bk, bv], axis=-1), conv_buf, dn_conv_w)
    qkv = jax.nn.silu(qkv)
    q, k, v = jnp.split(qkv, (WIDTH_BQK, 2 * WIDTH_BQK), axis=-1)
    rep = V_HEADS_B // QK_HEADS_B
    q = jnp.repeat(l2norm(q.reshape(B, T, QK_HEADS_B, DK)), rep, axis=2)
    k = jnp.repeat(l2norm(k.reshape(B, T, QK_HEADS_B, DK)), rep, axis=2)
    v = v.reshape(B, T, V_HEADS_B, DV)
    beta = jax.nn.sigmoid(bb.astype(jnp.float32))
    g = -jnp.exp(dn_A_log.astype(jnp.float32)) * jax.nn.softplus(ba.astype(jnp.float32) + dn_dt_bias.astype(jnp.float32))
    o, S_new = gated_delta_rule(q, k, v, g, beta, S0)
    z = bz.reshape(B, T, V_HEADS_B, DV).astype(jnp.float32)
    o = o * lax.rsqrt(jnp.mean(o * o, axis=-1, keepdims=True) + EPS) * dn_norm_w.astype(jnp.float32) * jax.nn.silu(z)
    return o.reshape(B, T, WIDTH_BV).astype(bq.dtype), conv_new, S_new


def trunk_layer(x, win_k, win_v, dn_conv_buf, dn_state, ffn_buf, rel_bias,
                ln_mix_pre, w_in, dn_conv_w, dn_A_log, dn_dt_bias, dn_norm_w, w_out, ln_mix_post,
                ln_ffn_pre, w_ffn_in, ffn_conv_w, ffn_conv_b, w_ffn_out, ln_ffn_post):
    B, T, _ = x.shape
    h = rmsnorm(x, ln_mix_pre)
    proj = h @ w_in
    aq, ak, av, bq, bk, bv, bz, bb, ba = jnp.split(proj, SPLIT_IDX, axis=-1)
    aq = aq.reshape(B, T, HEADS_A, HEAD_DIM)
    ak = ak.reshape(B, T, HEADS_A, HEAD_DIM)
    av = av.reshape(B, T, HEADS_A, HEAD_DIM)
    if win_k is None:
        parts = [dilated_branch_prompt(aq, ak, av, wnd, dil, rel_bias) for wnd, dil in DILATED]
        kk, vv = ak, av
    else:
        past = win_k.shape[1]
        kk = jnp.concatenate([win_k.astype(ak.dtype), ak], axis=1)
        vv = jnp.concatenate([win_v.astype(av.dtype), av], axis=1)
        parts = [dilated_branch_sample(aq, kk, vv, past, wnd, dil, rel_bias) for wnd, dil in DILATED]
    keep = min(WIN_MAX, kk.shape[1])
    new_wk = kk[:, kk.shape[1] - keep:]
    new_wv = vv[:, vv.shape[1] - keep:]
    att = merge_branches(parts).reshape(B, T, WIDTH_A).astype(x.dtype)
    dn_out, conv_new, S_new = delta_mixer(bq, bk, bv, bz, bb, ba, dn_conv_buf, dn_state,
                                          dn_conv_w, dn_A_log, dn_dt_bias, dn_norm_w)
    mix = jnp.concatenate([att, dn_out], axis=-1) @ w_out
    x = x + rmsnorm(mix, ln_mix_post)
    h = rmsnorm(x, ln_ffn_pre)
    up, ffn_new = causal_dwconv(h @ w_ffn_in, ffn_buf, ffn_conv_w)
    gate, val = jnp.split(up + ffn_conv_b, 2, axis=-1)
    f = (jax.nn.gelu(gate, approximate=True) * val) @ w_ffn_out
    x = x + rmsnorm(f, ln_ffn_post)
    return x, new_wk, new_wv, conv_new, S_new, ffn_new


def setup_inputs(seed: int = 0) -> dict:
    key = jax.random.key(seed)
    ks = jax.random.split(key, 24)
    f32 = jnp.float32

    def nrm(k, shape, scale):
        return jax.random.normal(k, shape, f32) * scale

    win_buf = min(WIN_MAX, PAST_LEN)
    dt = jnp.exp(jax.random.uniform(ks[12], (DEPTH, V_HEADS_B), f32, math.log(1e-3), math.log(1e-1)))
    return {
        "x_prompt": nrm(ks[0], (BATCH, SEQ, D_MODEL), 1.0),
        "x_sample": nrm(ks[1], (DEC_BATCH, DEC_SEQ, D_MODEL), 1.0),
        "cache_win_k": nrm(ks[2], (DEPTH, DEC_BATCH, win_buf, HEADS_A, HEAD_DIM), 1.0),
        "cache_win_v": nrm(ks[3], (DEPTH, DEC_BATCH, win_buf, HEADS_A, HEAD_DIM), 1.0),
        "state_dn_conv": nrm(ks[4], (DEPTH, DEC_BATCH, CONV_W - 1, CONV_DIM), 1.0),
        "state_dn_rec": nrm(ks[5], (DEPTH, DEC_BATCH, V_HEADS_B, DK, DV), 0.1),
        "state_ffn_conv": nrm(ks[6], (DEPTH, DEC_BATCH, FFN_CONV_W - 1, 2 * D_FF), 1.0),
        "rel_bias": nrm(ks[7], (N_BUCKETS, HEADS_A), 0.5),
        "ln_mix_pre": 1.0 + nrm(ks[8], (DEPTH, D_MODEL), 0.05),
        "w_in": nrm(ks[9], (DEPTH, D_MODEL, PROJ_DIM), D_MODEL ** -0.5),
        "dn_conv_w": nrm(ks[10], (DEPTH, CONV_W, CONV_DIM), CONV_W ** -0.5),
        "dn_A_log": jnp.log(jax.random.uniform(ks[11], (DEPTH, V_HEADS_B), f32, 1.0, 16.0)),
        "dn_dt_bias": dt + jnp.log(-jnp.expm1(-dt)),
        "dn_norm_w": 1.0 + nrm(ks[13], (DEPTH, DV), 0.05),
        "w_out": nrm(ks[14], (DEPTH, MIX_OUT, D_MODEL), MIX_OUT ** -0.5),
        "ln_mix_post": 1.0 + nrm(ks[15], (DEPTH, D_MODEL), 0.05),
        "ln_ffn_pre": 1.0 + nrm(ks[16], (DEPTH, D_MODEL), 0.05),
        "w_ffn_in": nrm(ks[17], (DEPTH, D_MODEL, 2 * D_FF), D_MODEL ** -0.5),
        "ffn_conv_w": nrm(ks[18], (DEPTH, FFN_CONV_W, 2 * D_FF), FFN_CONV_W ** -0.5),
        "ffn_conv_b": nrm(ks[19], (DEPTH, 2 * D_FF), 0.02),
        "w_ffn_out": nrm(ks[20], (DEPTH, D_FF, D_MODEL), D_FF ** -0.5),
        "ln_ffn_post": 1.0 + nrm(ks[21], (DEPTH, D_MODEL), 0.05),
    }


def reference(x_prompt, x_sample, cache_win_k, cache_win_v, state_dn_conv, state_dn_rec, state_ffn_conv,
              rel_bias, ln_mix_pre, w_in, dn_conv_w, dn_A_log, dn_dt_bias, dn_norm_w, w_out, ln_mix_post,
              ln_ffn_pre, w_ffn_in, ffn_conv_w, ffn_conv_b, w_ffn_out, ln_ffn_post):
    yp, ys = x_prompt, x_sample
    bp = x_prompt.shape[0]
    p_wk, p_wv, p_dc, p_dr, p_fc = [], [], [], [], []
    s_wk, s_wv, s_dc, s_dr, s_fc = [], [], [], [], []
    for l in range(DEPTH):
        weights = (rel_bias, ln_mix_pre[l], w_in[l], dn_conv_w[l], dn_A_log[l], dn_dt_bias[l], dn_norm_w[l],
                   w_out[l], ln_mix_post[l], ln_ffn_pre[l], w_ffn_in[l], ffn_conv_w[l], ffn_conv_b[l],
                   w_ffn_out[l], ln_ffn_post[l])
        yp, wk, wv, dc, dr, fc = trunk_layer(
            yp, None, None,
            jnp.zeros((bp, CONV_W - 1, CONV_DIM), yp.dtype),
            jnp.zeros((bp, V_HEADS_B, DK, DV), jnp.float32),
            jnp.zeros((bp, FFN_CONV_W - 1, 2 * D_FF), yp.dtype),
            *weights)
        p_wk.append(wk); p_wv.append(wv); p_dc.append(dc); p_dr.append(dr); p_fc.append(fc)
        ys, wk, wv, dc, dr, fc = trunk_layer(
            ys, cache_win_k[l], cache_win_v[l], state_dn_conv[l], state_dn_rec[l], state_ffn_conv[l],
            *weights)
        s_wk.append(wk); s_wv.append(wv); s_dc.append(dc); s_dr.append(dr); s_fc.append(fc)
    return (yp, ys,
            jnp.stack(p_wk), jnp.stack(p_wv), jnp.stack(p_dc), jnp.stack(p_dr), jnp.stack(p_fc),
            jnp.stack(s_wk), jnp.stack(s_wv), jnp.stack(s_dc), jnp.stack(s_dr), jnp.stack(s_fc))
```

```python
import functools
import math

import numpy as np
import jax
import jax.numpy as jnp
from jax import lax
from jax.experimental import pallas as pl
from jax.experimental.pallas import tpu as pltpu

F32 = jnp.float32
BF16 = jnp.bfloat16

D_MODEL = 2048
HEAD_DIM = 128
WIDTH_A = 1024
HEADS_A = 8
DILATIONS = (1, 4, 16)
BLK = 128
N_BUCKETS = 32
MAX_DISTANCE = 2048
DK = 128
DV = 128
V_HEADS_B = 8
QK_HEADS_B = 4
WIDTH_BQK = 512
WIDTH_BV = 1024
CONV_W = 4
CONV_DIM = 2048
CHUNK = 64
D_FF = 5632
FFN_CONV_W = 3
EPS = 1e-6
NEG = -1e30
ATT_SCALE = HEAD_DIM ** -0.5
QK_SCALE = DK ** -0.5

OFF_AQ, OFF_AK, OFF_AV = 0, 1024, 2048
OFF_BQ, OFF_BK, OFF_BV, OFF_BZ = 3072, 3584, 4096, 5120
OFF_GATES = 6144
PROJ_MAIN = 6144

MIB = 2 ** 20


def _params(semantics, vmem_mib):
    return pltpu.CompilerParams(dimension_semantics=semantics, vmem_limit_bytes=vmem_mib * MIB)


def _bdot(a, b):
    return jnp.dot(a.astype(BF16), b.astype(BF16), preferred_element_type=F32)


def _bdot_nt(a, b):
    return lax.dot_general(a.astype(BF16), b.astype(BF16), (((1,), (1,)), ((), ())),
                           preferred_element_type=F32)


def _bdot_tn(a, b):
    return lax.dot_general(a.astype(BF16), b.astype(BF16), (((0,), (0,)), ((), ())),
                           preferred_element_type=F32)


def _fdot(a, b):
    return jnp.dot(a, b, preferred_element_type=F32, precision=lax.Precision.HIGHEST)


def _silu(x):
    return x * (1.0 / (1.0 + jnp.exp(-x)))


def _sigmoid(x):
    return 1.0 / (1.0 + jnp.exp(-x))


def _softplus(x):
    return jnp.maximum(x, 0.0) + jnp.log(1.0 + jnp.exp(-jnp.abs(x)))


def _gelu_tanh(x):
    return 0.5 * x * (1.0 + jnp.tanh(math.sqrt(2.0 / math.pi) * (x + 0.044715 * (x * x * x))))


def _rms(x, w):
    return x * lax.rsqrt(jnp.mean(x * x, axis=-1, keepdims=True) + EPS) * w


def _rmsnorm_body(x_ref, w_ref, o_ref):
    o_ref[...] = _rms(x_ref[...], w_ref[...]).astype(o_ref.dtype)


def _rmsnorm(x, w, tm):
    m, d = x.shape
    return pl.pallas_call(
        _rmsnorm_body,
        grid=(m // tm,),
        in_specs=[pl.BlockSpec((tm, d), lambda i: (i, 0)),
                  pl.BlockSpec((1, d), lambda i: (0, 0))],
        out_specs=pl.BlockSpec((tm, d), lambda i: (i, 0)),
        out_shape=jax.ShapeDtypeStruct((m, d), BF16),
        compiler_params=_params(("arbitrary",), 40),
        name="rmsnorm",
    )(x, w)


def _matmul_body(x_ref, w_ref, o_ref):
    o_ref[...] = jnp.dot(x_ref[...], w_ref[...], preferred_element_type=F32)


def _matmul(x, w, tm, tn, name):
    m, k = x.shape
    n = w.shape[1]
    return pl.pallas_call(
        _matmul_body,
        grid=(n // tn, m // tm),
        in_specs=[pl.BlockSpec((tm, k), lambda j, i: (i, 0)),
                  pl.BlockSpec((k, tn), lambda j, i: (0, j))],
        out_specs=pl.BlockSpec((tm, tn), lambda j, i: (i, j)),
        out_shape=jax.ShapeDtypeStruct((m, n), F32),
        compiler_params=_params(("arbitrary", "arbitrary"), 48),
        name=name,
    )(x, w)


def _rel_bucket_np(dist):
    dist = np.asarray(dist, np.int64)
    max_exact = N_BUCKETS // 2
    d = np.maximum(dist, 1).astype(np.float64)
    val = np.log(d / max_exact) / math.log(MAX_DISTANCE / max_exact) * (N_BUCKETS - max_exact)
    frac = np.abs(val - np.round(val))
    near = (frac < 2e-5) &(dist >= max_exact) & (dist != max_exact) & (dist < MAX_DISTANCE)
    assert not near.any(), "distance on a bucket boundary"
    val = np.where(dist == max_exact, 0.0, val)
    large = np.minimum(max_exact + np.trunc(val).astype(np.int64), N_BUCKETS - 1)
    return np.where(dist < max_exact, dist, large).astype(np.int32)


def _prompt_bucket_tables():
    qi = np.arange(BLK)[:, None]
    kj = np.arange(2 * BLK)[None, :]
    delta = BLK + qi - kj
    inwin = (delta >= 0) & (delta <= BLK)
    tabs = []
    for dil in DILATIONS:
        b = _rel_bucket_np(np.clip(delta, 0, BLK) * dil)
        tabs.append(np.where(inwin, b, -1))
    return np.stack(tabs).astype(np.int32)


def _sample_bucket_tables():
    j = BLK - np.arange(BLK)
    return np.stack([_rel_bucket_np(j * dil)[None, :] for dil in DILATIONS]).astype(np.int32)


def _attn_prompt_body(bucket_ref, relb_ref, q_ref, k_ref, v_ref, o_ref,
                      bias_scr, acc_scr, m_scr, l_scr):
    h = pl.program_id(1)
    col = lax.broadcasted_iota(jnp.int32, (BLK, 2 * BLK), 1)
    for br in range(3):
        bk = bucket_ref[br]
        bias = jnp.zeros((BLK, 2 * BLK), F32)
        for kb in range(N_BUCKETS):
            bias = jnp.where(bk == kb, relb_ref[kb, h], bias)
        full = jnp.where(bk >= 0, bias, NEG)
        bias_scr[2 * br] = full
        bias_scr[2 * br + 1] = jnp.where(col >= BLK, full, NEG)

    def run_branch(br, dil, is_first_branch, is_last_branch):
        shift = int(math.log2(dil))
        span = BLK * dil
        stride = None if dil == 1 else dil

        def rows(start):
            return pl.ds(start, BLK, stride=stride) if stride else pl.ds(start, BLK)

        def task(t, carry):
            r = t & (dil - 1)
            n = t >> shift
            q_start = n * span + r
            first = jnp.where(n == 0, 1, 0)
            p_start = q_start - span * (1 - first)
            q = q_ref[0, rows(q_start), :]
            kk = jnp.concatenate([k_ref[0, rows(p_start), :], k_ref[0, rows(q_start), :]], axis=0)
            vv = jnp.concatenate([v_ref[0, rows(p_start), :], v_ref[0, rows(q_start), :]], axis=0)
            s = _bdot_nt(q, kk) * ATT_SCALE + bias_scr[2 * br + first]
            m_t = jnp.max(s, axis=-1, keepdims=True)
            p = jnp.exp(s - m_t)
            l_t = jnp.sum(p, axis=-1, keepdims=True)
            acc_t = _bdot(p, vv)
            m_b = jnp.broadcast_to(m_t, (BLK, HEAD_DIM))
            l_b = jnp.broadcast_to(l_t, (BLK, HEAD_DIM))
            if not is_first_branch:
                m_run = m_scr[rows(q_start), :]
                l_run = l_scr[rows(q_start), :]
                acc_run = acc_scr[rows(q_start), :]
                m_new = jnp.maximum(m_run, m_b)
                a = jnp.exp(m_run - m_new)
                b = jnp.exp(m_b - m_new)
                acc_t = a * acc_run + b * acc_t
                l_b = a * l_run + b * l_b
                m_b = m_new
            if is_last_branch:
                o_ref[0, rows(q_start), :] = (acc_t / l_b).astype(o_ref.dtype)
            else:
                m_scr[rows(q_start), :] = m_b
                l_scr[rows(q_start), :] = l_b
                acc_scr[rows(q_start), :] = acc_t
            return carry

        lax.fori_loop(0, 32, task, 0)

    run_branch(2, 16, True, False)
    run_branch(1, 4, False, False)
    run_branch(0, 1, False, True)


def _attn_prompt(proj3, rel_bias):
    b, s, _ = proj3.shape
    buckets = jnp.asarray(_prompt_bucket_tables())
    blk = (1, s, HEAD_DIM)
    return pl.pallas_call(
        _attn_prompt_body,
        grid=(b, HEADS_A),
        in_specs=[pl.BlockSpec((3, BLK, 2 * BLK), lambda i, h: (0, 0, 0)),
                  pl.BlockSpec(memory_space=pltpu.SMEM),
                  pl.BlockSpec(blk, lambda i, h: (i, 0, OFF_AQ // HEAD_DIM + h)),
                  pl.BlockSpec(blk, lambda i, h: (i, 0, OFF_AK // HEAD_DIM + h)),
                  pl.BlockSpec(blk, lambda i, h: (i, 0, OFF_AV // HEAD_DIM + h))],
        out_specs=pl.BlockSpec(blk, lambda i, h: (i, 0, h)),
        out_shape=jax.ShapeDtypeStruct((b, s, WIDTH_A), BF16),
        scratch_shapes=[pltpu.VMEM((6, BLK, 2 * BLK), F32),
                        pltpu.VMEM((s, HEAD_DIM), F32),
                        pltpu.VMEM((s, HEAD_DIM), F32),
                        pltpu.VMEM((s, HEAD_DIM), F32)],
        compiler_params=_params(("arbitrary", "arbitrary"), 40),
        name="attn_prompt",
    )(buckets, rel_bias, proj3, proj3, proj3)


def _attn_sample_body(bucket_ref, relbt_ref, q_ref, kn_ref, vn_ref,
                      k1_ref, k4_ref, k16_ref, v1_ref, v4_ref, v16_ref, o_ref, bias_scr):
    relbt = relbt_ref[...]
    tile = (HEADS_A, HEAD_DIM)

    @pl.when(pl.program_id(0) == 0)
    def _():
        for br in range(3):
            bk = bucket_ref[br]
            bias = jnp.zeros((BLK,) + tile, F32)
            for kb in range(N_BUCKETS):
                col = jnp.broadcast_to(relbt[:, kb:kb + 1], tile)
                bias = jnp.where(bk == kb, col[None], bias)
            bias_scr[br] = bias

    def lane_sum(x):
        return jnp.broadcast_to(jnp.sum(x, axis=-1, keepdims=True), x.shape)

    q = q_ref[0]
    s_self = lane_sum(q * kn_ref[0]) * ATT_SCALE + jnp.broadcast_to(relbt[:, 0:1], tile)
    scores = []
    m = s_self
    for br, k_ref in enumerate((k1_ref, k4_ref, k16_ref)):
        s = lane_sum(k_ref[...] * q[None]) * ATT_SCALE + bias_scr[br]
        scores.append(s)
        m = jnp.maximum(m, jnp.max(s, axis=0))
    p_self = 3.0 * jnp.exp(s_self - m)
    l = p_self
    acc = p_self * vn_ref[0]
    for s, v_ref in zip(scores, (v1_ref, v4_ref, v16_ref)):
        p = jnp.exp(s - m[None])
        l = l + jnp.sum(p, axis=0)
        acc = acc + jnp.sum(p * v_ref[...], axis=0)
    o_ref[0] = (acc / l).astype(o_ref.dtype)


def _attn_sample(q, k_new, v_new, cache_k, cache_v, rel_bias):
    b, past = cache_k.shape[:2]
    tile = (HEADS_A, HEAD_DIM)
    buckets = jnp.asarray(np.broadcast_to(_sample_bucket_tables().reshape(3, BLK, 1, 1), (3, BLK) + tile))
    row = pl.BlockSpec((1,) + tile, lambda i: (i, 0, 0))
    views, specs = [], []
    for cache in (cache_k, cache_v):
        for dil in DILATIONS:
            views.append(cache.reshape((b, past // dil, dil) + tile))
            last = past // dil // BLK - 1
            specs.append(pl.BlockSpec((None, BLK, None) + tile,
                                      functools.partial(lambda last, i: (i, last, 0, 0, 0), last)))
    return pl.pallas_call(
        _attn_sample_body,
        grid=(b,),
        in_specs=[pl.BlockSpec((3, BLK) + tile, lambda i: (0, 0, 0, 0)),
                  pl.BlockSpec((HEADS_A, N_BUCKETS), lambda i: (0, 0)),
                  row, row, row] + specs,
        out_specs=row,
        out_shape=jax.ShapeDtypeStruct((b,) + tile, BF16),
        scratch_shapes=[pltpu.VMEM((3, BLK) + tile, F32)],
        compiler_params=_params(("arbitrary",), 40),
        name="attn_sample",
    )(buckets, rel_bias.T, q, k_new, v_new, *views)


def _neumann_inverse(a):
    eye = (lax.broadcasted_iota(jnp.int32, a.shape, 0)
           == lax.broadcasted_iota(jnp.int32, a.shape, 1)).astype(F32)
    inv = eye - a
    pw = a
    for _ in range(5):
        pw = _fdot(pw, pw)
        inv = inv + _fdot(inv, pw)
    return inv


def _dn_prompt_body(q_ref, k_ref, v_ref, z_ref, gates_ref, wq_ref, wk_ref, wv_ref,
                    cq_ref, ck_ref, cv_ref, s0_ref, alog_ref, dtb_ref, nw_ref,
                    o_ref, s_out_ref,
                    s_scr, eq_scr, ek_scr, ev_scr, qn_scr, kn_scr, vv_scr, g_scr, beta_scr, *, tt):
    hq = pl.program_id(1)
    t = pl.program_id(2)
    nt = pl.num_programs(2)

    @pl.when(t == 0)
    def _():
        s_scr[...] = s0_ref[0]
        eq_scr[5:8, :] = cq_ref[0]
        ek_scr[5:8, :] = ck_ref[0]
        ev_scr[5:8, :] = cv_ref[0]

    def conv_silu(x_ref, w_ref, e_scr):
        e_scr[8:8 + tt, :] = x_ref[0]
        w = w_ref[...]
        y = w[0:1, :] * e_scr[5:5 + tt, :]
        for i in range(1, CONV_W):
            y = y + w[i:i + 1, :] * e_scr[5 + i:5 + i + tt, :]
        tail = e_scr[tt + 5:tt + 8, :]
        e_scr[5:8, :] = tail
        return _silu(y)

    def l2n(x):
        return x * lax.rsqrt(jnp.sum(x * x, axis=-1, keepdims=True) + EPS)

    qn_scr[...] = l2n(conv_silu(q_ref, wq_ref, eq_scr)) * QK_SCALE
    kn_scr[...] = l2n(conv_silu(k_ref, wk_ref, ek_scr))
    vv_scr[...] = conv_silu(v_ref, wv_ref, ev_scr)
    gates = gates_ref[0]
    beta_scr[...] = _sigmoid(gates)
    g_scr[...] = -jnp.exp(alog_ref[...]) * _softplus(gates + dtb_ref[...])

    ri = lax.broadcasted_iota(jnp.int32, (CHUNK, CHUNK), 0)
    ci = lax.broadcasted_iota(jnp.int32, (CHUNK, CHUNK), 1)
    tri = ri >= ci
    strict = ri > ci
    tril_ones = tri.astype(F32)
    lane = lax.broadcasted_iota(jnp.int32, (CHUNK, 128), 1)
    sub = lax.broadcasted_iota(jnp.int32, (128, CHUNK), 0)
    nw = nw_ref[...]

    def chunk(c, carry):
        r0 = pl.multiple_of(c * CHUNK, CHUNK)
        rows = pl.ds(r0, CHUNK)
        qn = qn_scr[rows, :]
        kn = kn_scr[rows, :]
        beta_all = beta_scr[rows, :]
        gc_all = _fdot(tril_ones, g_scr[rows, :])
        gc_all_t = gc_all.T
        kk = _bdot_nt(kn, kn)
        qk = _bdot_nt(qn, kn)
        for j in range(2):
            hv = 2 * hq + j
            beta = jnp.sum(jnp.where(lane == hv, beta_all, 0.0), axis=-1, keepdims=True)
            gc = jnp.sum(jnp.where(lane == V_HEADS_B + hv, gc_all, 0.0), axis=-1, keepdims=True)
            gc_row = jnp.sum(jnp.where(sub == V_HEADS_B + hv, gc_all_t, 0.0), axis=0, keepdims=True)
            gc_last = gc_row[:, CHUNK - 1:CHUNK]
            decay = jnp.exp(jnp.where(tri, gc - gc_row, NEG))
            a = jnp.where(strict, beta * kk * decay, 0.0)
            tm = _neumann_inverse(a)
            egc = jnp.exp(gc)
            v = vv_scr[rows, j * DV:(j + 1) * DV]
            w = _bdot(tm, kn * (beta * egc))
            u = _bdot(tm, v * beta)
            attn = qk * decay
            qg = qn * egc
            kd = kn * jnp.exp(gc_last - gc)
            state = s_scr[j]
            v_new = u - _bdot(w, state)
            o = _bdot(qg, state) + _bdot(attn, v_new)
            s_scr[j] = state * jnp.exp(gc_last) + _bdot_tn(kd, v_new)
            z = z_ref[0, rows, j * DV:(j + 1) * DV]
            o = o * lax.rsqrt(jnp.mean(o * o, axis=-1, keepdims=True) + EPS) * nw * _silu(z)
            o_ref[0, rows, j * DV:(j + 1) * DV] = o.astype(o_ref.dtype)
        return carry

    lax.fori_loop(0, tt // CHUNK, chunk, 0)

    @pl.when(t == nt - 1)
    def _():
        s_out_ref[0] = s_scr[...]


def _dn_prompt(proj3, gates3, conv_state, s0, conv_w, alog_vec, dtb_vec, norm_w, tt):
    b, s, _ = proj3.shape
    body = functools.partial(_dn_prompt_body, tt=tt)
    c128 = lambda off: off // 128
    c256 = lambda off: off // 256
    in_specs = [
        pl.BlockSpec((1, tt, 128), lambda i, h, t: (i, t, c128(OFF_BQ) + h)),
        pl.BlockSpec((1, tt, 128), lambda i, h, t: (i, t, c128(OFF_BK) + h)),
        pl.BlockSpec((1, tt, 256), lambda i, h, t: (i, t, c256(OFF_BV) + h)),
        pl.BlockSpec((1, tt, 256), lambda i, h, t: (i, t, c256(OFF_BZ) + h)),
        pl.BlockSpec((1, tt, 128), lambda i, h, t: (i, t, 0)),
        pl.BlockSpec((CONV_W, 128), lambda i, h, t: (0, h)),
        pl.BlockSpec((CONV_W, 128), lambda i, h, t: (0, c128(WIDTH_BQK) + h)),
        pl.BlockSpec((CONV_W, 256), lambda i, h, t: (0, c256(2 * WIDTH_BQK) + h)),
        pl.BlockSpec((1, CONV_W - 1, 128), lambda i, h, t: (i, 0, h)),
        pl.BlockSpec((1, CONV_W - 1, 128), lambda i, h, t: (i, 0, c128(WIDTH_BQK) + h)),
        pl.BlockSpec((1, CONV_W - 1, 256), lambda i, h, t: (i, 0, c256(2 * WIDTH_BQK) + h)),
        pl.BlockSpec((1, 2, DK, DV), lambda i, h, t: (i, h, 0, 0)),
        pl.BlockSpec((1, 128), lambda i, h, t: (0, 0)),
        pl.BlockSpec((1, 128), lambda i, h, t: (0, 0)),
        pl.BlockSpec((1, DV), lambda i, h, t: (0, 0)),
    ]
    return pl.pallas_call(
        body,
        grid=(b, QK_HEADS_B, s // tt),
        in_specs=in_specs,
        out_specs=[pl.BlockSpec((1, tt, 256), lambda i, h, t: (i, t, h)),
                   pl.BlockSpec((1, 2, DK, DV), lambda i, h, t: (i, h, 0, 0))],
        out_shape=[jax.ShapeDtypeStruct((b, s, WIDTH_BV), BF16),
                   jax.ShapeDtypeStruct((b, V_HEADS_B, DK, DV), F32)],
        scratch_shapes=[pltpu.VMEM((2, DK, DV), F32),
                        pltpu.VMEM((tt + 8, 128), F32),
                        pltpu.VMEM((tt + 8, 128), F32),
                        pltpu.VMEM((tt + 8, 256), F32),
                        pltpu.VMEM((tt, 128), F32),
                        pltpu.VMEM((tt, 128), F32),
                        pltpu.VMEM((tt, 256), F32),
                        pltpu.VMEM((tt, 128), F32),
                        pltpu.VMEM((tt, 128), F32)],
        compiler_params=_params(("arbitrary", "arbitrary", "arbitrary"), 40),
        name="deltanet_prompt",
    )(proj3, proj3, proj3, proj3, gates3, conv_w, conv_w, conv_w,
      conv_state, conv_state, conv_state, s0, alog_vec, dtb_vec, norm_w)


def _dn_sample_body(proj_ref, gates_ref, cw_ref, cs_ref, s0_ref, alog_ref, dtb_ref, nw_ref,
                    o_ref, cs_out_ref, s_out_ref):
    pre = proj_ref[0, :, OFF_BQ:OFF_BQ + CONV_DIM]
    buf = cs_ref[0]
    w = cw_ref[...]
    y = w[CONV_W - 1:CONV_W, :] * pre
    for i in range(CONV_W - 1):
        y = y + w[i:i + 1, :] * buf[i:i + 1, :]
    y = _silu(y)
    cs_out_ref[0, 0:CONV_W - 2, :] = buf[1:CONV_W - 1, :]
    cs_out_ref[0, CONV_W - 2:CONV_W - 1, :] = pre

    gates = gates_ref[0]
    beta_all = _sigmoid(gates)
    g_all = -jnp.exp(alog_ref[...]) * _softplus(gates + dtb_ref[...])
    nw = nw_ref[...]

    def l2n(x):
        return x * lax.rsqrt(jnp.sum(x * x, axis=-1, keepdims=True) + EPS)

    row8 = lax.broadcasted_iota(jnp.int32, (8, DK), 0) == 0
    for hv in range(V_HEADS_B):
        hq = hv // 2
        q = l2n(y[:, hq * DK:(hq + 1) * DK]) * QK_SCALE
        k = l2n(y[:, WIDTH_BQK + hq * DK:WIDTH_BQK + (hq + 1) * DK])
        v = y[:, 2 * WIDTH_BQK + hv * DV:2 * WIDTH_BQK + (hv + 1) * DV]
        beta = beta_all[:, hv:hv + 1]
        g = g_all[:, V_HEADS_B + hv:V_HEADS_B + hv + 1]
        eg = jnp.exp(g)
        state = s0_ref[0, hv]

        def pad8(x):
            return jnp.where(row8, jnp.broadcast_to(x, (8, x.shape[-1])), 0.0)

        v_new = v * beta - _bdot(pad8(k * (beta * eg)), state)[0:1, :]
        qk = jnp.sum(q.astype(BF16).astype(F32) * k.astype(BF16).astype(F32), axis=-1, keepdims=True)
        o = _bdot(pad8(q * eg), state)[0:1, :] + qk.astype(BF16).astype(F32) * v_new.astype(BF16).astype(F32)
        s_out_ref[0, hv] = state * eg + _bdot_tn(pad8(k), pad8(v_new))
        z = proj_ref[0, :, OFF_BZ + hv * DV:OFF_BZ + (hv + 1) * DV]
        o = o * lax.rsqrt(jnp.mean(o * o, axis=-1, keepdims=True) + EPS) * nw * _silu(z)
        o_ref[0, :, hv * DV:(hv + 1) * DV] = o.astype(o_ref.dtype)


def _dn_sample(proj, gates, conv_state, s0, conv_w, alog_vec, dtb_vec, norm_w):
    b = proj.shape[0]
    return pl.pallas_call(
        _dn_sample_body,
        grid=(b,),
        in_specs=[pl.BlockSpec((1, 1, PROJ_MAIN), lambda i: (i, 0, 0)),
                  pl.BlockSpec((1, 1, 128), lambda i: (i, 0, 0)),
                  pl.BlockSpec((CONV_W, CONV_DIM), lambda i: (0, 0)),
                  pl.BlockSpec((1, CONV_W - 1, CONV_DIM), lambda i: (i, 0, 0)),
                  pl.BlockSpec((1, V_HEADS_B, DK, DV), lambda i: (i, 0, 0, 0)),
                  pl.BlockSpec((1, 128), lambda i: (0, 0)),
                  pl.BlockSpec((1, 128), lambda i: (0, 0)),
                  pl.BlockSpec((1, DV), lambda i: (0, 0))],
        out_specs=[pl.BlockSpec((1, 1, WIDTH_BV), lambda i: (i, 0, 0)),
                   pl.BlockSpec((1, CONV_W - 1, CONV_DIM), lambda i: (i, 0, 0)),
                   pl.BlockSpec((1, V_HEADS_B, DK, DV), lambda i: (i, 0, 0, 0))],
        out_shape=[jax.ShapeDtypeStruct((b, 1, WIDTH_BV), BF16),
                   jax.ShapeDtypeStruct((b, CONV_W - 1, CONV_DIM), F32),
                   jax.ShapeDtypeStruct((b, V_HEADS_B, DK, DV), F32)],
        compiler_params=_params(("arbitrary",), 40),
        name="deltanet_sample",
    )(proj, gates, conv_w, conv_state, s0, alog_vec, dtb_vec, norm_w)


def _outproj_body(att_ref, dn_ref, wa_ref, wb_ref, x_ref, lnpost_ref, lnpre_ref, x1_ref, h2_ref):
    mix = (jnp.dot(att_ref[...], wa_ref[...], preferred_element_type=F32)
           + jnp.dot(dn_ref[...], wb_ref[...], preferred_element_type=F32))
    x1 = x_ref[...] + _rms(mix, lnpost_ref[...])
    x1_ref[...] = x1
    h2_ref[...] = _rms(x1, lnpre_ref[...]).astype(h2_ref.dtype)


def _outproj(att, dn, wa, wb, x, ln_post, ln_pre, tm):
    m, d = x.shape
    return pl.pallas_call(
        _outproj_body,
        grid=(m // tm,),
        in_specs=[pl.BlockSpec((tm, WIDTH_A), lambda i: (i, 0)),
                  pl.BlockSpec((tm, WIDTH_BV), lambda i: (i, 0)),
                  pl.BlockSpec((WIDTH_A, d), lambda i: (0, 0)),
                  pl.BlockSpec((WIDTH_BV, d), lambda i: (0, 0)),
                  pl.BlockSpec((tm, d), lambda i: (i, 0)),
                  pl.BlockSpec((1, d), lambda i: (0, 0)),
                  pl.BlockSpec((1, d), lambda i: (0, 0))],
        out_specs=[pl.BlockSpec((tm, d), lambda i: (i, 0)),
                   pl.BlockSpec((tm, d), lambda i: (i, 0))],
        out_shape=[jax.ShapeDtypeStruct((m, d), F32),
                   jax.ShapeDtypeStruct((m, d), BF16)],
        compiler_params=_params(("arbitrary",), 48),
        name="outproj",
    )(att, dn, wa, wb, x, ln_post, ln_pre)


def _ffn_body(h_ref, wg_ref, wv_ref, cwg_ref, cwv_ref, cbg_ref, cbv_ref, wo_ref, x1_ref, ln_ref,
              pg_ref, pv_ref, o_ref, ng_ref, nv_ref, eg_scr, ev_scr, carry_scr,
              *, tm, tiles_per_seq, single_token):
    i = pl.program_id(0)
    j = pl.program_id(1)
    nj = pl.num_programs(1)
    h = h_ref[...]

    def up_conv(w_ref, cw_ref, cb_ref, prev_ref, new_ref, e_scr, slot):
        up = jnp.dot(h, w_ref[...], preferred_element_type=F32)
        cw = cw_ref[...]
        if single_token:
            new_ref[...] = up
            return cw[0:1, :] * prev_ref[0] + cw[1:2, :] * prev_ref[1] + cw[2:3, :] * up + cb_ref[...]
        e_scr[8:8 + tm, :] = up

        @pl.when(i % tiles_per_seq == 0)
        def _():
            e_scr[6:8, :] = prev_ref[0]

        @pl.when(i % tiles_per_seq != 0)
        def _():
            e_scr[6:8, :] = carry_scr[slot, j, 6:8, :]

        tail = up[tm - 2:tm, :]
        carry_scr[slot, j, 6:8, :] = tail
        new_ref[0] = tail
        return (cw[0:1, :] * e_scr[6:6 + tm, :] + cw[1:2, :] * e_scr[7:7 + tm, :]
                + cw[2:3, :] * up + cb_ref[...])

    gate = up_conv(wg_ref, cwg_ref, cbg_ref, pg_ref, ng_ref, eg_scr, 0)
    val = up_conv(wv_ref, cwv_ref, cbv_ref, pv_ref, nv_ref, ev_scr, 1)
    act = (_gelu_tanh(gate) * val).astype(BF16)
    part = jnp.dot(act, wo_ref[...], preferred_element_type=F32)

    @pl.when(j == 0)
    def _():
        o_ref[...] = part

    @pl.when(j != 0)
    def _():
        o_ref[...] += part

    @pl.when(j == nj - 1)
    def _():
        o_ref[...] = x1_ref[...] + _rms(o_ref[...], ln_ref[...])


def _ffn(h2, w_in, conv_w, conv_b, w_out, x1, ln_post, prev, tm, tf, seq_len):
    m, d = h2.shape
    single = seq_len == 1
    nj = D_FF // tf
    tiles_per_seq = 1 if single else seq_len // tm
    if single:
        prev_g = pl.BlockSpec((2, tm, tf), lambda i, j: (0, i, j))
        prev_v = pl.BlockSpec((2, tm, tf), lambda i, j: (0, i, nj + j))
        new_g = pl.BlockSpec((tm, tf), lambda i, j: (i, j))
        new_shape = jax.ShapeDtypeStruct((m, D_FF), F32)
    else:
        prev_g = pl.BlockSpec((1, 2, tf), lambda i, j: (i // tiles_per_seq, 0, j))
        prev_v = pl.BlockSpec((1, 2, tf), lambda i, j: (i // tiles_per_seq, 0, nj + j))
        new_g = pl.BlockSpec((1, 2, tf), lambda i, j: (i, 0, j))
        new_shape = jax.ShapeDtypeStruct((m // tm, 2, D_FF), F32)
    body = functools.partial(_ffn_body, tm=tm, tiles_per_seq=tiles_per_seq, single_token=single)
    return pl.pallas_call(
        body,
        grid=(m // tm, nj),
        in_specs=[pl.BlockSpec((tm, d), lambda i, j: (i, 0)),
                  pl.BlockSpec((d, tf), lambda i, j: (0, j)),
                  pl.BlockSpec((d, tf), lambda i, j: (0, nj + j)),
                  pl.BlockSpec((FFN_CONV_W, tf), lambda i, j: (0, j)),
                  pl.BlockSpec((FFN_CONV_W, tf), lambda i, j: (0, nj + j)),
                  pl.BlockSpec((1, tf), lambda i, j: (0, j)),
                  pl.BlockSpec((1, tf), lambda i, j: (0, nj + j)),
                  pl.BlockSpec((tf, d), lambda i, j: (j, 0)),
                  pl.BlockSpec((tm, d), lambda i, j: (i, 0)),
                  pl.BlockSpec((1, d), lambda i, j: (0, 0)),
                  prev_g, prev_v],
        out_specs=[pl.BlockSpec((tm, d), lambda i, j: (i, 0)), new_g, new_g],
        out_shape=[jax.ShapeDtypeStruct((m, d), F32), new_shape, new_shape],
        scratch_shapes=[pltpu.VMEM((tm + 8, tf), F32),
                        pltpu.VMEM((tm + 8, tf), F32),
                        pltpu.VMEM((2, nj, 8, tf), F32)],
        compiler_params=_params(("arbitrary", "arbitrary"), 56),
        name="convffn",
    )(h2, w_in, w_in, conv_w, conv_w, conv_b, conv_b, w_out, x1, ln_post, prev, prev)


def _cache_shift_body(ck_ref, cv_ref, nk_ref, nv_ref, ok_ref, ov_ref, sem):
    nb, rows = ck_ref.shape[0], ck_ref.shape[1]
    copies = []
    for which, (c_ref, n_ref, o_ref) in enumerate(((ck_ref, nk_ref, ok_ref), (cv_ref, nv_ref, ov_ref))):
        for b in range(nb):
            copies.append(pltpu.make_async_copy(c_ref.at[b, pl.ds(1, rows - 1)],
                                                o_ref.at[b, pl.ds(0, rows - 1)],
                                                sem.at[which, b]))
        copies.append(pltpu.make_async_copy(n_ref, o_ref.at[:, pl.ds(rows - 1, 1)], sem.at[which, nb]))
    for c in copies:
        c.start()
    for c in copies:
        c.wait()


def _cache_shift(cache_k, cache_v, new_k, new_v):
    nb = cache_k.shape[0]
    any_spec = pl.BlockSpec(memory_space=pl.ANY)
    shape = jax.ShapeDtypeStruct(cache_k.shape, cache_k.dtype)
    return pl.pallas_call(
        _cache_shift_body,
        in_specs=[any_spec] * 4,
        out_specs=[any_spec, any_spec],
        out_shape=[shape, shape],
        scratch_shapes=[pltpu.SemaphoreType.DMA((2, nb + 1))],
        name="cache_shift",
    )(cache_k, cache_v, new_k, new_v)


def _lane_vec(values, offset):
    return jnp.zeros((1, 128), F32).at[0, offset:offset + V_HEADS_B].set(values.astype(F32))


def kernel(x_prompt, x_sample, cache_win_k, cache_win_v, state_dn_conv, state_dn_rec, state_ffn_conv,
           rel_bias, ln_mix_pre, w_in, dn_conv_w, dn_A_log, dn_dt_bias, dn_norm_w, w_out, ln_mix_post,
           ln_ffn_pre, w_ffn_in, ffn_conv_w, ffn_conv_b, w_ffn_out, ln_ffn_post):
    bp, sp, d = x_prompt.shape
    bs = x_sample.shape[0]
    l = 0

    w_main = w_in[l, :, :PROJ_MAIN].astype(BF16)
    w_gate = jnp.pad(w_in[l, :, PROJ_MAIN:], ((0, 0), (0, 128 - 2 * V_HEADS_B))).astype(BF16)
    wo_a = w_out[l, :WIDTH_A].astype(BF16)
    wo_b = w_out[l, WIDTH_A:].astype(BF16)
    wf_in = w_ffn_in[l].astype(BF16)
    wf_out = w_ffn_out[l].astype(BF16)
    ln1 = ln_mix_pre[l][None, :]
    ln2 = ln_mix_post[l][None, :]
    ln3 = ln_ffn_pre[l][None, :]
    ln4 = ln_ffn_post[l][None, :]
    conv_w = dn_conv_w[l]
    alog_vec = _lane_vec(dn_A_log[l], V_HEADS_B)
    dtb_vec = _lane_vec(dn_dt_bias[l], V_HEADS_B)
    norm_w = dn_norm_w[l][None, :]
    fcw = ffn_conv_w[l]
    fcb = ffn_conv_b[l][None, :]

    xp = x_prompt.reshape(bp * sp, d)
    hp = _rmsnorm(xp, ln1, 512)
    proj_p = _matmul(hp, w_main, 1024, 1024, "inproj_prompt")
    gates_p = _matmul(hp, w_gate, 1024, 128, "gates_prompt")
    proj3 = proj_p.reshape(bp, sp, PROJ_MAIN)
    att_p = _attn_prompt(proj3, rel_bias)
    dn_p, p_dn_rec = _dn_prompt(
        proj3, gates_p.reshape(bp, sp, 128),
        jnp.zeros((bp, CONV_W - 1, CONV_DIM), F32), jnp.zeros((bp, V_HEADS_B, DK, DV), F32),
        conv_w, alog_vec, dtb_vec, norm_w, 512)
    x1_p, h2_p = _outproj(att_p.reshape(bp * sp, WIDTH_A), dn_p.reshape(bp * sp, WIDTH_BV),
                          wo_a, wo_b, xp, ln2, ln3, 512)
    y_p, fc_g, fc_v = _ffn(h2_p, wf_in, fcw, fcb, wf_out, x1_p, ln4,
                           jnp.zeros((bp, FFN_CONV_W - 1, 2 * D_FF), F32), 512, 512, sp)
    keep = min(MAX_DISTANCE, sp)
    p_win_k = proj3[:, sp - keep:, OFF_AK:OFF_AK + WIDTH_A].reshape(1, bp, keep, HEADS_A, HEAD_DIM)
    p_win_v = proj3[:, sp - keep:, OFF_AV:OFF_AV + WIDTH_A].reshape(1, bp, keep, HEADS_A, HEAD_DIM)
    p_dn_conv = proj3[:, sp - (CONV_W - 1):, OFF_BQ:OFF_BQ + CONV_DIM][None]
    tiles = sp // 512
    p_ffn_conv = jnp.concatenate([fc_g[tiles - 1::tiles], fc_v[tiles - 1::tiles]], axis=-1)[None]

    xs = x_sample.reshape(bs, d)
    hs = _rmsnorm(xs, ln1, bs)
    proj_s = _matmul(hs, w_main, bs, 1024, "inproj_sample")
    gates_s = _matmul(hs, w_gate, bs, 128, "gates_sample")
    past = cache_win_k.shape[2]
    ck = cache_win_k[l]
    cv = cache_win_v[l]
    new_k = proj_s[:, OFF_AK:OFF_AK + WIDTH_A]
    new_v = proj_s[:, OFF_AV:OFF_AV + WIDTH_A]
    new_q = proj_s[:, OFF_AQ:OFF_AQ + WIDTH_A].reshape(bs, HEADS_A, HEAD_DIM)
    new_k = new_k.reshape(bs, HEADS_A, HEAD_DIM)
    new_v = new_v.reshape(bs, HEADS_A, HEAD_DIM)
    att_s = _attn_sample(new_q, new_k, new_v, ck, cv, rel_bias)
    s_win_k, s_win_v = _cache_shift(ck, cv, new_k[:, None], new_v[:, None])
    dn_s, s_dn_conv, s_dn_rec = _dn_sample(proj_s[:, None], gates_s[:, None], state_dn_conv[l],
                                           state_dn_rec[l], conv_w, alog_vec, dtb_vec, norm_w)
    x1_s, h2_s = _outproj(att_s.reshape(bs, WIDTH_A), dn_s.reshape(bs, WIDTH_BV),
                          wo_a, wo_b, xs, ln2, ln3, bs)
    prev_s = jnp.swapaxes(state_ffn_conv[l], 0, 1)
    y_s, up_g, up_v = _ffn(h2_s, wf_in, fcw, fcb, wf_out, x1_s, ln4, prev_s, bs, 512, 1)
    s_ffn_conv = jnp.stack([prev_s[1], jnp.concatenate([up_g, up_v], axis=-1)], axis=1)[None]

    return (y_p.reshape(bp, sp, d), y_s.reshape(bs, 1, d),
            p_win_k, p_win_v, p_dn_conv, p_dn_rec[None], p_ffn_conv,
            s_win_k[None], s_win_v[None], s_dn_conv[None], s_dn_rec[None], s_ffn_conv)
```

```python
import functools
import math

import numpy as np
import jax
import jax.numpy as jnp
from jax import lax
from jax.experimental import pallas as pl
from jax.experimental.pallas import tpu as pltpu

F32 = jnp.float32
BF16 = jnp.bfloat16

D_MODEL = 2048
HEAD_DIM = 128
WIDTH_A = 1024
HEADS_A = 8
DILATIONS = (1, 4, 16)
BLK = 128
N_BUCKETS = 32
MAX_DISTANCE = 2048
DK = 128
DV = 128
V_HEADS_B = 8
QK_HEADS_B = 4
WIDTH_BQK = 512
WIDTH_BV = 1024
CONV_W = 4
CONV_DIM = 2048
CHUNK = 64
D_FF = 5632
FFN_CONV_W = 3
EPS = 1e-6
NEG = -1e30
ATT_SCALE = HEAD_DIM ** -0.5
QK_SCALE = DK ** -0.5

OFF_AQ, OFF_AK, OFF_AV = 0, 1024, 2048
OFF_BQ, OFF_BK, OFF_BV, OFF_BZ = 3072, 3584, 4096, 5120
OFF_GATES = 6144
PROJ_MAIN = 6144

MIB = 2 ** 20


def _params(semantics, vmem_mib):
    return pltpu.CompilerParams(dimension_semantics=semantics, vmem_limit_bytes=vmem_mib * MIB)


def _bdot(a, b):
    return jnp.dot(a.astype(BF16), b.astype(BF16), preferred_element_type=F32)


def _bdot_nt(a, b):
    return lax.dot_general(a.astype(BF16), b.astype(BF16), (((1,), (1,)), ((), ())),
                           preferred_element_type=F32)


def _bdot_tn(a, b):
    return lax.dot_general(a.astype(BF16), b.astype(BF16), (((0,), (0,)), ((), ())),
                           preferred_element_type=F32)


def _fdot(a, b):
    return jnp.dot(a, b, preferred_element_type=F32, precision=lax.Precision.HIGHEST)


def _silu(x):
    return x * (1.0 / (1.0 + jnp.exp(-x)))


def _sigmoid(x):
    return 1.0 / (1.0 + jnp.exp(-x))


def _softplus(x):
    return jnp.maximum(x, 0.0) + jnp.log(1.0 + jnp.exp(-jnp.abs(x)))


def _gelu_tanh(x):
    return 0.5 * x * (1.0 + jnp.tanh(math.sqrt(2.0 / math.pi) * (x + 0.044715 * (x * x * x))))


def _rms(x, w):
    return x * lax.rsqrt(jnp.mean(x * x, axis=-1, keepdims=True) + EPS) * w


def _rmsnorm_body(x_ref, w_ref, o_ref):
    o_ref[...] = _rms(x_ref[...], w_ref[...]).astype(o_ref.dtype)


def _rmsnorm(x, w, tm):
    m, d = x.shape
    return pl.pallas_call(
        _rmsnorm_body,
        grid=(m // tm,),
        in_specs=[pl.BlockSpec((tm, d), lambda i: (i, 0)),
                  pl.BlockSpec((1, d), lambda i: (0, 0))],
        out_specs=pl.BlockSpec((tm, d), lambda i: (i, 0)),
        out_shape=jax.ShapeDtypeStruct((m, d), BF16),
        compiler_params=_params(("arbitrary",), 40),
        name="rmsnorm",
    )(x, w)


def _matmul_body(x_ref, w_ref, o_ref):
    o_ref[...] = jnp.dot(x_ref[...], w_ref[...], preferred_element_type=F32)


def _matmul(x, w, tm, tn, name):
    m, k = x.shape
    n = w.shape[1]
    return pl.pallas_call(
        _matmul_body,
        grid=(n // tn, m // tm),
        in_specs=[pl.BlockSpec((tm, k), lambda j, i: (i, 0)),
                  pl.BlockSpec((k, tn), lambda j, i: (0, j))],
        out_specs=pl.BlockSpec((tm, tn), lambda j, i: (i, j)),
        out_shape=jax.ShapeDtypeStruct((m, n), F32),
        compiler_params=_params(("arbitrary", "arbitrary"), 48),
        name=name,
    )(x, w)


def _rel_bucket_np(dist):
    dist = np.asarray(dist, np.int64)
    max_exact = N_BUCKETS // 2
    d = np.maximum(dist, 1).astype(np.float64)
    val = np.log(d / max_exact) / math.log(MAX_DISTANCE / max_exact) * (N_BUCKETS - max_exact)
    frac = np.abs(val - np.round(val))
    near = (frac < 2e-5) &(dist >= max_exact) & (dist != max_exact) & (dist < MAX_DISTANCE)
    assert not near.any(), "distance on a bucket boundary"
    val = np.where(dist == max_exact, 0.0, val)
    large = np.minimum(max_exact + np.trunc(val).astype(np.int64), N_BUCKETS - 1)
    return np.where(dist < max_exact, dist, large).astype(np.int32)


def _prompt_bucket_tables():
    qi = np.arange(BLK)[:, None]
    kj = np.arange(2 * BLK)[None, :]
    delta = BLK + qi - kj
    inwin = (delta >= 0) & (delta <= BLK)
    tabs = []
    for dil in DILATIONS:
        b = _rel_bucket_np(np.clip(delta, 0, BLK) * dil)
        tabs.append(np.where(inwin, b, -1))
    return np.stack(tabs).astype(np.int32)


def _sample_bucket_tables():
    j = BLK - np.arange(BLK)
    return np.stack([_rel_bucket_np(j * dil)[None, :] for dil in DILATIONS]).astype(np.int32)


def _attn_prompt_body(bucket_ref, relb_ref, q_ref, k_ref, v_ref, o_ref,
                      bias_scr, acc_scr, m_scr, l_scr):
    h = pl.program_id(1)
    col = lax.broadcasted_iota(jnp.int32, (BLK, 2 * BLK), 1)
    for br in range(3):
        bk = bucket_ref[br]
        bias = jnp.zeros((BLK, 2 * BLK), F32)
        for kb in range(N_BUCKETS):
            bias = jnp.where(bk == kb, relb_ref[kb, h], bias)
        full = jnp.where(bk >= 0, bias, NEG)
        bias_scr[2 * br] = full
        bias_scr[2 * br + 1] = jnp.where(col >= BLK, full, NEG)

    def run_branch(br, dil, is_first_branch, is_last_branch):
        shift = int(math.log2(dil))
        span = BLK * dil
        stride = None if dil == 1 else dil

        def rows(start):
            return pl.ds(start, BLK, stride=stride) if stride else pl.ds(start, BLK)

        def task(t, carry):
            r = t & (dil - 1)
            n = t >> shift
            q_start = n * span + r
            first = jnp.where(n == 0, 1, 0)
            p_start = q_start - span * (1 - first)
            q = q_ref[0, rows(q_start), :]
            kk = jnp.concatenate([k_ref[0, rows(p_start), :], k_ref[0, rows(q_start), :]], axis=0)
            vv = jnp.concatenate([v_ref[0, rows(p_start), :], v_ref[0, rows(q_start), :]], axis=0)
            s = _bdot_nt(q, kk) * ATT_SCALE + bias_scr[2 * br + first]
            m_t = jnp.max(s, axis=-1, keepdims=True)
            p = jnp.exp(s - m_t)
            l_t = jnp.sum(p, axis=-1, keepdims=True)
            acc_t = _bdot(p, vv)
            m_b = jnp.broadcast_to(m_t, (BLK, HEAD_DIM))
            l_b = jnp.broadcast_to(l_t, (BLK, HEAD_DIM))
            if not is_first_branch:
                m_run = m_scr[rows(q_start), :]
                l_run = l_scr[rows(q_start), :]
                acc_run = acc_scr[rows(q_start), :]
                m_new = jnp.maximum(m_run, m_b)
                a = jnp.exp(m_run - m_new)
                b = jnp.exp(m_b - m_new)
                acc_t = a * acc_run + b * acc_t
                l_b = a * l_run + b * l_b
                m_b = m_new
            if is_last_branch:
                o_ref[0, rows(q_start), :] = (acc_t / l_b).astype(o_ref.dtype)
            else:
                m_scr[rows(q_start), :] = m_b
                l_scr[rows(q_start), :] = l_b
                acc_scr[rows(q_start), :] = acc_t
            return carry

        lax.fori_loop(0, 32, task, 0)

    run_branch(2, 16, True, False)
    run_branch(1, 4, False, False)
    run_branch(0, 1, False, True)


def _attn_prompt(proj3, rel_bias):
    b, s, _ = proj3.shape
    buckets = jnp.asarray(_prompt_bucket_tables())
    blk = (1, s, HEAD_DIM)
    return pl.pallas_call(
        _attn_prompt_body,
        grid=(b, HEADS_A),
        in_specs=[pl.BlockSpec((3, BLK, 2 * BLK), lambda i, h: (0, 0, 0)),
                  pl.BlockSpec(memory_space=pltpu.SMEM),
                  pl.BlockSpec(blk, lambda i, h: (i, 0, OFF_AQ // HEAD_DIM + h)),
                  pl.BlockSpec(blk, lambda i, h: (i, 0, OFF_AK // HEAD_DIM + h)),
                  pl.BlockSpec(blk, lambda i, h: (i, 0, OFF_AV // HEAD_DIM + h))],
        out_specs=pl.BlockSpec(blk, lambda i, h: (i, 0, h)),
        out_shape=jax.ShapeDtypeStruct((b, s, WIDTH_A), BF16),
        scratch_shapes=[pltpu.VMEM((6, BLK, 2 * BLK), F32),
                        pltpu.VMEM((s, HEAD_DIM), F32),
                        pltpu.VMEM((s, HEAD_DIM), F32),
                        pltpu.VMEM((s, HEAD_DIM), F32)],
        compiler_params=_params(("arbitrary", "arbitrary"), 40),
        name="attn_prompt",
    )(buckets, rel_bias, proj3, proj3, proj3)


def _attn_sample_body(bucket_ref, relbt_ref, q_ref, kn_ref, vn_ref,
                      k1_ref, k4_ref, k16_ref, v1_ref, v4_ref, v16_ref, o_ref, bias_scr):
    relbt = relbt_ref[...]
    tile = (HEADS_A, HEAD_DIM)

    @pl.when(pl.program_id(0) == 0)
    def _():
        for br in range(3):
            bk = bucket_ref[br]
            bias = jnp.zeros((BLK,) + tile, F32)
            for kb in range(N_BUCKETS):
                col = jnp.broadcast_to(relbt[:, kb:kb + 1], tile)
                bias = jnp.where(bk == kb, col[None], bias)
            bias_scr[br] = bias

    def lane_sum(x):
        return jnp.broadcast_to(jnp.sum(x, axis=-1, keepdims=True), x.shape)

    q = q_ref[0]
    s_self = lane_sum(q * kn_ref[0]) * ATT_SCALE + jnp.broadcast_to(relbt[:, 0:1], tile)
    scores = []
    m = s_self
    for br, k_ref in enumerate((k1_ref, k4_ref, k16_ref)):
        s = lane_sum(k_ref[...] * q[None]) * ATT_SCALE + bias_scr[br]
        scores.append(s)
        m = jnp.maximum(m, jnp.max(s, axis=0))
    p_self = 3.0 * jnp.exp(s_self - m)
    l = p_self
    acc = p_self * vn_ref[0]
    for s, v_ref in zip(scores, (v1_ref, v4_ref, v16_ref)):
        p = jnp.exp(s - m[None])
        l = l + jnp.sum(p, axis=0)
        acc = acc + jnp.sum(p * v_ref[...], axis=0)
    o_ref[0] = (acc / l).astype(o_ref.dtype)


def _attn_sample(q, k_new, v_new, cache_k, cache_v, rel_bias):
    b, past = cache_k.shape[:2]
    tile = (HEADS_A, HEAD_DIM)
    buckets = jnp.asarray(np.broadcast_to(_sample_bucket_tables().reshape(3, BLK, 1, 1), (3, BLK) + tile))
    row = pl.BlockSpec((1,) + tile, lambda i: (i, 0, 0))
    views, specs = [], []
    for cache in (cache_k, cache_v):
        for dil in DILATIONS:
            views.append(cache.reshape((b, past // dil, dil) + tile))
            last = past // dil // BLK - 1
            specs.append(pl.BlockSpec((None, BLK, None) + tile,
                                      functools.partial(lambda last, i: (i, last, 0, 0, 0), last)))
    return pl.pallas_call(
        _attn_sample_body,
        grid=(b,),
        in_specs=[pl.BlockSpec((3, BLK) + tile, lambda i: (0, 0, 0, 0)),
                  pl.BlockSpec((HEADS_A, N_BUCKETS), lambda i: (0, 0)),
                  row, row, row] + specs,
        out_specs=row,
        out_shape=jax.ShapeDtypeStruct((b,) + tile, BF16),
        scratch_shapes=[pltpu.VMEM((3, BLK) + tile, F32)],
        compiler_params=_params(("arbitrary",), 40),
        name="attn_sample",
    )(buckets, rel_bias.T, q, k_new, v_new, *views)


def _neumann_inverse(a):
    eye = (lax.broadcasted_iota(jnp.int32, a.shape, 0)
           == lax.broadcasted_iota(jnp.int32, a.shape, 1)).astype(F32)
    inv = eye - a
    pw = a
    for _ in range(5):
        pw = _fdot(pw, pw)
        inv = inv + _fdot(inv, pw)
    return inv


def _dn_prompt_body(q_ref, k_ref, v_ref, z_ref, gates_ref, wq_ref, wk_ref, wv_ref,
                    cq_ref, ck_ref, cv_ref, s0_ref, alog_ref, dtb_ref, nw_ref,
                    o_ref, s_out_ref,
                    s_scr, eq_scr, ek_scr, ev_scr, qn_scr, kn_scr, vv_scr, g_scr, beta_scr, *, tt):
    hq = pl.program_id(1)
    t = pl.program_id(2)
    nt = pl.num_programs(2)

    @pl.when(t == 0)
    def _():
        s_scr[...] = s0_ref[0]
        eq_scr[5:8, :] = cq_ref[0]
        ek_scr[5:8, :] = ck_ref[0]
        ev_scr[5:8, :] = cv_ref[0]

    def conv_silu(x_ref, w_ref, e_scr):
        e_scr[8:8 + tt, :] = x_ref[0]
        w = w_ref[...]
        y = w[0:1, :] * e_scr[5:5 + tt, :]
        for i in range(1, CONV_W):
            y = y + w[i:i + 1, :] * e_scr[5 + i:5 + i + tt, :]
        tail = e_scr[tt + 5:tt + 8, :]
        e_scr[5:8, :] = tail
        return _silu(y)

    def l2n(x):
        return x * lax.rsqrt(jnp.sum(x * x, axis=-1, keepdims=True) + EPS)

    qn_scr[...] = l2n(conv_silu(q_ref, wq_ref, eq_scr)) * QK_SCALE
    kn_scr[...] = l2n(conv_silu(k_ref, wk_ref, ek_scr))
    vv_scr[...] = conv_silu(v_ref, wv_ref, ev_scr)
    gates = gates_ref[0]
    beta_scr[...] = _sigmoid(gates)
    g_scr[...] = -jnp.exp(alog_ref[...]) * _softplus(gates + dtb_ref[...])

    ri = lax.broadcasted_iota(jnp.int32, (CHUNK, CHUNK), 0)
    ci = lax.broadcasted_iota(jnp.int32, (CHUNK, CHUNK), 1)
    tri = ri >= ci
    strict = ri > ci
    tril_ones = tri.astype(F32)
    lane = lax.broadcasted_iota(jnp.int32, (CHUNK, 128), 1)
    sub = lax.broadcasted_iota(jnp.int32, (128, CHUNK), 0)
    nw = nw_ref[...]

    def chunk(c, carry):
        r0 = pl.multiple_of(c * CHUNK, CHUNK)
        rows = pl.ds(r0, CHUNK)
        qn = qn_scr[rows, :]
        kn = kn_scr[rows, :]
        beta_all = beta_scr[rows, :]
        gc_all = _fdot(tril_ones, g_scr[rows, :])
        gc_all_t = gc_all.T
        kk = _bdot_nt(kn, kn)
        qk = _bdot_nt(qn, kn)
        for j in range(2):
            hv = 2 * hq + j
            beta = jnp.sum(jnp.where(lane == hv, beta_all, 0.0), axis=-1, keepdims=True)
            gc = jnp.sum(jnp.where(lane == V_HEADS_B + hv, gc_all, 0.0), axis=-1, keepdims=True)
            gc_row = jnp.sum(jnp.where(sub == V_HEADS_B + hv, gc_all_t, 0.0), axis=0, keepdims=True)
            gc_last = gc_row[:, CHUNK - 1:CHUNK]
            decay = jnp.exp(jnp.where(tri, gc - gc_row, NEG))
            a = jnp.where(strict, beta * kk * decay, 0.0)
            tm = _neumann_inverse(a)
            egc = jnp.exp(gc)
            v = vv_scr[rows, j * DV:(j + 1) * DV]
            w = _bdot(tm, kn * (beta * egc))
            u = _bdot(tm, v * beta)
            attn = qk * decay
            qg = qn * egc
            kd = kn * jnp.exp(gc_last - gc)
            state = s_scr[j]
            v_new = u - _bdot(w, state)
            o = _bdot(qg, state) + _bdot(attn, v_new)
            s_scr[j] = state * jnp.exp(gc_last) + _bdot_tn(kd, v_new)
            z = z_ref[0, rows, j * DV:(j + 1) * DV]
            o = o * lax.rsqrt(jnp.mean(o * o, axis=-1, keepdims=True) + EPS) * nw * _silu(z)
            o_ref[0, rows, j * DV:(j + 1) * DV] = o.astype(o_ref.dtype)
        return carry

    lax.fori_loop(0, tt // CHUNK, chunk, 0)

    @pl.when(t == nt - 1)
    def _():
        s_out_ref[0] = s_scr[...]


def _dn_prompt(proj3, gates3, conv_state, s0, conv_w, alog_vec, dtb_vec, norm_w, tt):
    b, s, _ = proj3.shape
    body = functools.partial(_dn_prompt_body, tt=tt)
    c128 = lambda off: off // 128
    c256 = lambda off: off // 256
    in_specs = [
        pl.BlockSpec((1, tt, 128), lambda i, h, t: (i, t, c128(OFF_BQ) + h)),
        pl.BlockSpec((1, tt, 128), lambda i, h, t: (i, t, c128(OFF_BK) + h)),
        pl.BlockSpec((1, tt, 256), lambda i, h, t: (i, t, c256(OFF_BV) + h)),
        pl.BlockSpec((1, tt, 256), lambda i, h, t: (i, t, c256(OFF_BZ) + h)),
        pl.BlockSpec((1, tt, 128), lambda i, h, t: (i, t, 0)),
        pl.BlockSpec((CONV_W, 128), lambda i, h, t: (0, h)),
        pl.BlockSpec((CONV_W, 128), lambda i, h, t: (0, c128(WIDTH_BQK) + h)),
        pl.BlockSpec((CONV_W, 256), lambda i, h, t: (0, c256(2 * WIDTH_BQK) + h)),
        pl.BlockSpec((1, CONV_W - 1, 128), lambda i, h, t: (i, 0, h)),
        pl.BlockSpec((1, CONV_W - 1, 128), lambda i, h, t: (i, 0, c128(WIDTH_BQK) + h)),
        pl.BlockSpec((1, CONV_W - 1, 256), lambda i, h, t: (i, 0, c256(2 * WIDTH_BQK) + h)),
        pl.BlockSpec((1, 2, DK, DV), lambda i, h, t: (i, h, 0, 0)),
        pl.BlockSpec((1, 128), lambda i, h, t: (0, 0)),
        pl.BlockSpec((1, 128), lambda i, h, t: (0, 0)),
        pl.BlockSpec((1, DV), lambda i, h, t: (0, 0)),
    ]
    return pl.pallas_call(
        body,
        grid=(b, QK_HEADS_B, s // tt),
        in_specs=in_specs,
        out_specs=[pl.BlockSpec((1, tt, 256), lambda i, h, t: (i, t, h)),
                   pl.BlockSpec((1, 2, DK, DV), lambda i, h, t: (i, h, 0, 0))],
        out_shape=[jax.ShapeDtypeStruct((b, s, WIDTH_BV), BF16),
                   jax.ShapeDtypeStruct((b, V_HEADS_B, DK, DV), F32)],
        scratch_shapes=[pltpu.VMEM((2, DK, DV), F32),
                        pltpu.VMEM((tt + 8, 128), F32),
                        pltpu.VMEM((tt + 8, 128), F32),
                        pltpu.VMEM((tt + 8, 256), F32),
                        pltpu.VMEM((tt, 128), F32),
                        pltpu.VMEM((tt, 128), F32),
                        pltpu.VMEM((tt, 256), F32),
                        pltpu.VMEM((tt, 128), F32),
                        pltpu.VMEM((tt, 128), F32)],
        compiler_params=_params(("arbitrary", "arbitrary", "arbitrary"), 40),
        name="deltanet_prompt",
    )(proj3, proj3, proj3, proj3, gates3, conv_w, conv_w, conv_w,
      conv_state, conv_state, conv_state, s0, alog_vec, dtb_vec, norm_w)


def _dn_sample_body(proj_ref, gates_ref, cw_ref, cs_ref, s0_ref, alog_ref, dtb_ref, nw_ref,
                    o_ref, cs_out_ref, s_out_ref):
    pre = proj_ref[0, :, OFF_BQ:OFF_BQ + CONV_DIM]
    buf = cs_ref[0]
    w = cw_ref[...]
    y = w[CONV_W - 1:CONV_W, :] * pre
    for i in range(CONV_W - 1):
        y = y + w[i:i + 1, :] * buf[i:i + 1, :]
    y = _silu(y)
    cs_out_ref[0, 0:CONV_W - 2, :] = buf[1:CONV_W - 1, :]
    cs_out_ref[0, CONV_W - 2:CONV_W - 1, :] = pre

    gates = gates_ref[0]
    beta_all = _sigmoid(gates)
    g_all = -jnp.exp(alog_ref[...]) * _softplus(gates + dtb_ref[...])
    nw = nw_ref[...]

    def l2n(x):
        return x * lax.rsqrt(jnp.sum(x * x, axis=-1, keepdims=True) + EPS)

    row8 = lax.broadcasted_iota(jnp.int32, (8, DK), 0) == 0
    for hv in range(V_HEADS_B):
        hq = hv // 2
        q = l2n(y[:, hq * DK:(hq + 1) * DK]) * QK_SCALE
        k = l2n(y[:, WIDTH_BQK + hq * DK:WIDTH_BQK + (hq + 1) * DK])
        v = y[:, 2 * WIDTH_BQK + hv * DV:2 * WIDTH_BQK + (hv + 1) * DV]
        beta = beta_all[:, hv:hv + 1]
        g = g_all[:, V_HEADS_B + hv:V_HEADS_B + hv + 1]
        eg = jnp.exp(g)
        state = s0_ref[0, hv]

        def pad8(x):
            return jnp.where(row8, jnp.broadcast_to(x, (8, x.shape[-1])), 0.0)

        v_new = v * beta - _bdot(pad8(k * (beta * eg)), state)[0:1, :]
        qk = jnp.sum(q.astype(BF16).astype(F32) * k.astype(BF16).astype(F32), axis=-1, keepdims=True)
        o = _bdot(pad8(q * eg), state)[0:1, :] + qk.astype(BF16).astype(F32) * v_new.astype(BF16).astype(F32)
        s_out_ref[0, hv] = state * eg + _bdot_tn(pad8(k), pad8(v_new))
        z = proj_ref[0, :, OFF_BZ + hv * DV:OFF_BZ + (hv + 1) * DV]
        o = o * lax.rsqrt(jnp.mean(o * o, axis=-1, keepdims=True) + EPS) * nw * _silu(z)
        o_ref[0, :, hv * DV:(hv + 1) * DV] = o.astype(o_ref.dtype)


def _dn_sample(proj, gates, conv_state, s0, conv_w, alog_vec, dtb_vec, norm_w):
    b = proj.shape[0]
    return pl.pallas_call(
        _dn_sample_body,
        grid=(b,),
        in_specs=[pl.BlockSpec((1, 1, PROJ_MAIN), lambda i: (i, 0, 0)),
                  pl.BlockSpec((1, 1, 128), lambda i: (i, 0, 0)),
                  pl.BlockSpec((CONV_W, CONV_DIM), lambda i: (0, 0)),
                  pl.BlockSpec((1, CONV_W - 1, CONV_DIM), lambda i: (i, 0, 0)),
                  pl.BlockSpec((1, V_HEADS_B, DK, DV), lambda i: (i, 0, 0, 0)),
                  pl.BlockSpec((1, 128), lambda i: (0, 0)),
                  pl.BlockSpec((1, 128), lambda i: (0, 0)),
                  pl.BlockSpec((1, DV), lambda i: (0, 0))],
        out_specs=[pl.BlockSpec((1, 1, WIDTH_BV), lambda i: (i, 0, 0)),
                   pl.BlockSpec((1, CONV_W - 1, CONV_DIM), lambda i: (i, 0, 0)),
                   pl.BlockSpec((1, V_HEADS_B, DK, DV), lambda i: (i, 0, 0, 0))],
        out_shape=[jax.ShapeDtypeStruct((b, 1, WIDTH_BV), BF16),
                   jax.ShapeDtypeStruct((b, CONV_W - 1, CONV_DIM), F32),
                   jax.ShapeDtypeStruct((b, V_HEADS_B, DK, DV), F32)],
        compiler_params=_params(("arbitrary",), 40),
        name="deltanet_sample",
    )(proj, gates, conv_w, conv_state, s0, alog_vec, dtb_vec, norm_w)


def _outproj_body(att_ref, dn_ref, wa_ref, wb_ref, x_ref, lnpost_ref, lnpre_ref, x1_ref, h2_ref):
    mix = (jnp.dot(att_ref[...], wa_ref[...], preferred_element_type=F32)
           + jnp.dot(dn_ref[...], wb_ref[...], preferred_element_type=F32))
    x1 = x_ref[...] + _rms(mix, lnpost_ref[...])
    x1_ref[...] = x1
    h2_ref[...] = _rms(x1, lnpre_ref[...]).astype(h2_ref.dtype)


def _outproj(att, dn, wa, wb, x, ln_post, ln_pre, tm):
    m, d = x.shape
    return pl.pallas_call(
        _outproj_body,
        grid=(m // tm,),
        in_specs=[pl.BlockSpec((tm, WIDTH_A), lambda i: (i, 0)),
                  pl.BlockSpec((tm, WIDTH_BV), lambda i: (i, 0)),
                  pl.BlockSpec((WIDTH_A, d), lambda i: (0, 0)),
                  pl.BlockSpec((WIDTH_BV, d), lambda i: (0, 0)),
                  pl.BlockSpec((tm, d), lambda i: (i, 0)),
                  pl.BlockSpec((1, d), lambda i: (0, 0)),
                  pl.BlockSpec((1, d), lambda i: (0, 0))],
        out_specs=[pl.BlockSpec((tm, d), lambda i: (i, 0)),
                   pl.BlockSpec((tm, d), lambda i: (i, 0))],
        out_shape=[jax.ShapeDtypeStruct((m, d), F32),
                   jax.ShapeDtypeStruct((m, d), BF16)],
        compiler_params=_params(("arbitrary",), 48),
        name="outproj",
    )(att, dn, wa, wb, x, ln_post, ln_pre)


def _ffn_body(h_ref, wg_ref, wv_ref, cwg_ref, cwv_ref, cbg_ref, cbv_ref, wo_ref, x1_ref, ln_ref,
              pg_ref, pv_ref, o_ref, ng_ref, nv_ref, eg_scr, ev_scr, carry_scr,
              *, tm, tiles_per_seq, single_token):
    i = pl.program_id(0)
    j = pl.program_id(1)
    nj = pl.num_programs(1)
    h = h_ref[...]

    def up_conv(w_ref, cw_ref, cb_ref, prev_ref, new_ref, e_scr, slot):
        up = jnp.dot(h, w_ref[...], preferred_element_type=F32)
        cw = cw_ref[...]
        if single_token:
            new_ref[...] = up
            return cw[0:1, :] * prev_ref[0] + cw[1:2, :] * prev_ref[1] + cw[2:3, :] * up + cb_ref[...]
        e_scr[8:8 + tm, :] = up

        @pl.when(i % tiles_per_seq == 0)
        def _():
            e_scr[6:8, :] = prev_ref[0]

        @pl.when(i % tiles_per_seq != 0)
        def _():
            e_scr[6:8, :] = carry_scr[slot, j, 6:8, :]

        tail = up[tm - 2:tm, :]
        carry_scr[slot, j, 6:8, :] = tail
        new_ref[0] = tail
        return (cw[0:1, :] * e_scr[6:6 + tm, :] + cw[1:2, :] * e_scr[7:7 + tm, :]
                + cw[2:3, :] * up + cb_ref[...])

    gate = up_conv(wg_ref, cwg_ref, cbg_ref, pg_ref, ng_ref, eg_scr, 0)
    val = up_conv(wv_ref, cwv_ref, cbv_ref, pv_ref, nv_ref, ev_scr, 1)
    act = (_gelu_tanh(gate) * val).astype(BF16)
    part = jnp.dot(act, wo_ref[...], preferred_element_type=F32)

    @pl.when(j == 0)
    def _():
        o_ref[...] = part

    @pl.when(j != 0)
    def _():
        o_ref[...] += part

    @pl.when(j == nj - 1)
    def _():
        o_ref[...] = x1_ref[...] + _rms(o_ref[...], ln_ref[...])


def _ffn(h2, w_in, conv_w, conv_b, w_out, x1, ln_post, prev, tm, tf, seq_len):
    m, d = h2.shape
    single = seq_len == 1
    nj = D_FF // tf
    tiles_per_seq = 1 if single else seq_len // tm
    if single:
        prev_g = pl.BlockSpec((2, tm, tf), lambda i, j: (0, i, j))
        prev_v = pl.BlockSpec((2, tm, tf), lambda i, j: (0, i, nj + j))
        new_g = pl.BlockSpec((tm, tf), lambda i, j: (i, j))
        new_shape = jax.ShapeDtypeStruct((m, D_FF), F32)
    else:
        prev_g = pl.BlockSpec((1, 2, tf), lambda i, j: (i // tiles_per_seq, 0, j))
        prev_v = pl.BlockSpec((1, 2, tf), lambda i, j: (i // tiles_per_seq, 0, nj + j))
        new_g = pl.BlockSpec((1, 2, tf), lambda i, j: (i, 0, j))
        new_shape = jax.ShapeDtypeStruct((m // tm, 2, D_FF), F32)
    body = functools.partial(_ffn_body, tm=tm, tiles_per_seq=tiles_per_seq, single_token=single)
    return pl.pallas_call(
        body,
        grid=(m // tm, nj),
        in_specs=[pl.BlockSpec((tm, d), lambda i, j: (i, 0)),
                  pl.BlockSpec((d, tf), lambda i, j: (0, j)),
                  pl.BlockSpec((d, tf), lambda i, j: (0, nj + j)),
                  pl.BlockSpec((FFN_CONV_W, tf), lambda i, j: (0, j)),
                  pl.BlockSpec((FFN_CONV_W, tf), lambda i, j: (0, nj + j)),
                  pl.BlockSpec((1, tf), lambda i, j: (0, j)),
                  pl.BlockSpec((1, tf), lambda i, j: (0, nj + j)),
                  pl.BlockSpec((tf, d), lambda i, j: (j, 0)),
                  pl.BlockSpec((tm, d), lambda i, j: (i, 0)),
                  pl.BlockSpec((1, d), lambda i, j: (0, 0)),
                  prev_g, prev_v],
        out_specs=[pl.BlockSpec((tm, d), lambda i, j: (i, 0)), new_g, new_g],
        out_shape=[jax.ShapeDtypeStruct((m, d), F32), new_shape, new_shape],
        scratch_shapes=[pltpu.VMEM((tm + 8, tf), F32),
                        pltpu.VMEM((tm + 8, tf), F32),
                        pltpu.VMEM((2, nj, 8, tf), F32)],
        compiler_params=_params(("arbitrary", "arbitrary"), 56),
        name="convffn",
    )(h2, w_in, w_in, conv_w, conv_w, conv_b, conv_b, w_out, x1, ln_post, prev, prev)


def _cache_shift_body(ck_ref, cv_ref, ck_next_ref, cv_next_ref, nk_ref, nv_ref, ok_ref, ov_ref, *, tr):
    last = pl.program_id(1) == pl.num_programs(1) - 1
    for c_ref, nxt_ref, n_ref, o_ref in ((ck_ref, ck_next_ref, nk_ref, ok_ref),
                                         (cv_ref, cv_next_ref, nv_ref, ov_ref)):
        o_ref[0, 0:tr - 1] = c_ref[0, 1:tr]
        o_ref[0, tr - 1] = jnp.where(last, n_ref[0, 0], nxt_ref[0, 0])


def _cache_shift(cache_k, cache_v, new_k, new_v, tr):
    nb, rows, nh, dh = cache_k.shape
    main = pl.BlockSpec((1, tr, nh, dh), lambda b, i: (b, i, 0, 0))
    nxt = pl.BlockSpec((1, 1, nh, dh), lambda b, i: (b, jnp.minimum((i + 1) * tr, rows - 1), 0, 0))
    new = pl.BlockSpec((1, 1, nh, dh), lambda b, i: (b, 0, 0, 0))
    shape = jax.ShapeDtypeStruct(cache_k.shape, cache_k.dtype)
    return pl.pallas_call(
        functools.partial(_cache_shift_body, tr=tr),
        grid=(nb, rows // tr),
        in_specs=[main, main, nxt, nxt, new, new],
        out_specs=[main, main],
        out_shape=[shape, shape],
        compiler_params=_params(("arbitrary", "arbitrary"), 40),
        name="cache_shift",
    )(cache_k, cache_v, cache_k, cache_v, new_k, new_v)


def _lane_vec(values, offset):
    return jnp.zeros((1, 128), F32).at[0, offset:offset + V_HEADS_B].set(values.astype(F32))


def kernel(x_prompt, x_sample, cache_win_k, cache_win_v, state_dn_conv, state_dn_rec, state_ffn_conv,
           rel_bias, ln_mix_pre, w_in, dn_conv_w, dn_A_log, dn_dt_bias, dn_norm_w, w_out, ln_mix_post,
           ln_ffn_pre, w_ffn_in, ffn_conv_w, ffn_conv_b, w_ffn_out, ln_ffn_post):
    bp, sp, d = x_prompt.shape
    bs = x_sample.shape[0]
    l = 0

    w_main = w_in[l, :, :PROJ_MAIN].astype(BF16)
    w_gate = jnp.pad(w_in[l, :, PROJ_MAIN:], ((0, 0), (0, 128 - 2 * V_HEADS_B))).astype(BF16)
    wo_a = w_out[l, :WIDTH_A].astype(BF16)
    wo_b = w_out[l, WIDTH_A:].astype(BF16)
    wf_in = w_ffn_in[l].astype(BF16)
    wf_out = w_ffn_out[l].astype(BF16)
    ln1 = ln_mix_pre[l][None, :]
    ln2 = ln_mix_post[l][None, :]
    ln3 = ln_ffn_pre[l][None, :]
    ln4 = ln_ffn_post[l][None, :]
    conv_w = dn_conv_w[l]
    alog_vec = _lane_vec(dn_A_log[l], V_HEADS_B)
    dtb_vec = _lane_vec(dn_dt_bias[l], V_HEADS_B)
    norm_w = dn_norm_w[l][None, :]
    fcw = ffn_conv_w[l]
    fcb = ffn_conv_b[l][None, :]

    xp = x_prompt.reshape(bp * sp, d)
    hp = _rmsnorm(xp, ln1, 512)
    proj_p = _matmul(hp, w_main, 1024, 1024, "inproj_prompt")
    gates_p = _matmul(hp, w_gate, 1024, 128, "gates_prompt")
    proj3 = proj_p.reshape(bp, sp, PROJ_MAIN)
    att_p = _attn_prompt(proj3, rel_bias)
    dn_p, p_dn_rec = _dn_prompt(
        proj3, gates_p.reshape(bp, sp, 128),
        jnp.zeros((bp, CONV_W - 1, CONV_DIM), F32), jnp.zeros((bp, V_HEADS_B, DK, DV), F32),
        conv_w, alog_vec, dtb_vec, norm_w, 512)
    x1_p, h2_p = _outproj(att_p.reshape(bp * sp, WIDTH_A), dn_p.reshape(bp * sp, WIDTH_BV),
                          wo_a, wo_b, xp, ln2, ln3, 512)
    y_p, fc_g, fc_v = _ffn(h2_p, wf_in, fcw, fcb, wf_out, x1_p, ln4,
                           jnp.zeros((bp, FFN_CONV_W - 1, 2 * D_FF), F32), 512, 512, sp)
    keep = min(MAX_DISTANCE, sp)
    p_win_k = proj3[:, sp - keep:, OFF_AK:OFF_AK + WIDTH_A].reshape(1, bp, keep, HEADS_A, HEAD_DIM)
    p_win_v = proj3[:, sp - keep:, OFF_AV:OFF_AV + WIDTH_A].reshape(1, bp, keep, HEADS_A, HEAD_DIM)
    p_dn_conv = proj3[:, sp - (CONV_W - 1):, OFF_BQ:OFF_BQ + CONV_DIM][None]
    tiles = sp // 512
    p_ffn_conv = jnp.concatenate([fc_g[tiles - 1::tiles], fc_v[tiles - 1::tiles]], axis=-1)[None]

    xs = x_sample.reshape(bs, d)
    hs = _rmsnorm(xs, ln1, bs)
    proj_s = _matmul(hs, w_main, bs, 1024, "inproj_sample")
    gates_s = _matmul(hs, w_gate, bs, 128, "gates_sample")
    past = cache_win_k.shape[2]
    ck = cache_win_k[l]
    cv = cache_win_v[l]
    new_k = proj_s[:, OFF_AK:OFF_AK + WIDTH_A]
    new_v = proj_s[:, OFF_AV:OFF_AV + WIDTH_A]
    new_q = proj_s[:, OFF_AQ:OFF_AQ + WIDTH_A].reshape(bs, HEADS_A, HEAD_DIM)
    new_k = new_k.reshape(bs, HEADS_A, HEAD_DIM)
    new_v = new_v.reshape(bs, HEADS_A, HEAD_DIM)
    att_s = _attn_sample(new_q, new_k, new_v, ck, cv, rel_bias)
    s_win_k, s_win_v = _cache_shift(ck, cv, new_k[:, None], new_v[:, None], 512)
    dn_s, s_dn_conv, s_dn_rec = _dn_sample(proj_s[:, None], gates_s[:, None], state_dn_conv[l],
                                           state_dn_rec[l], conv_w, alog_vec, dtb_vec, norm_w)
    x1_s, h2_s = _outproj(att_s.reshape(bs, WIDTH_A), dn_s.reshape(bs, WIDTH_BV),
                          wo_a, wo_b, xs, ln2, ln3, bs)
    prev_s = jnp.swapaxes(state_ffn_conv[l], 0, 1)
    y_s, up_g, up_v = _ffn(h2_s, wf_in, fcw, fcb, wf_out, x1_s, ln4, prev_s, bs, 512, 1)
    s_ffn_conv = jnp.stack([prev_s[1], jnp.concatenate([up_g, up_v], axis=-1)], axis=1)[None]

    return (y_p.reshape(bp, sp, d), y_s.reshape(bs, 1, d),
            p_win_k, p_win_v, p_dn_conv, p_dn_rec[None], p_ffn_conv,
            s_win_k[None], s_win_v[None], s_dn_conv[None], s_dn_rec[None], s_ffn_conv)
```

```python
import functools
import math

import numpy as np
import jax
import jax.numpy as jnp
from jax import lax
from jax.experimental import pallas as pl
from jax.experimental.pallas import tpu as pltpu

F32 = jnp.float32
BF16 = jnp.bfloat16

D_MODEL = 2048
HEAD_DIM = 128
WIDTH_A = 1024
HEADS_A = 8
DILATIONS = (1, 4, 16)
BLK = 128
N_BUCKETS = 32
MAX_DISTANCE = 2048
DK = 128
DV = 128
V_HEADS_B = 8
QK_HEADS_B = 4
WIDTH_BQK = 512
WIDTH_BV = 1024
CONV_W = 4
CONV_DIM = 2048
CHUNK = 64
D_FF = 5632
FFN_CONV_W = 3
EPS = 1e-6
NEG = -1e30
ATT_SCALE = HEAD_DIM ** -0.5
QK_SCALE = DK ** -0.5

OFF_AQ, OFF_AK, OFF_AV = 0, 1024, 2048
OFF_BQ, OFF_BK, OFF_BV, OFF_BZ = 3072, 3584, 4096, 5120
OFF_GATES = 6144
PROJ_MAIN = 6144

MIB = 2 ** 20


def _params(semantics, vmem_mib):
    return pltpu.CompilerParams(dimension_semantics=semantics, vmem_limit_bytes=vmem_mib * MIB)


def _bdot(a, b):
    return jnp.dot(a.astype(BF16), b.astype(BF16), preferred_element_type=F32)


def _bdot_nt(a, b):
    return lax.dot_general(a.astype(BF16), b.astype(BF16), (((1,), (1,)), ((), ())),
                           preferred_element_type=F32)


def _bdot_tn(a, b):
    return lax.dot_general(a.astype(BF16), b.astype(BF16), (((0,), (0,)), ((), ())),
                           preferred_element_type=F32)


def _fdot(a, b):
    return jnp.dot(a, b, preferred_element_type=F32, precision=lax.Precision.HIGHEST)


def _silu(x):
    return x * (1.0 / (1.0 + jnp.exp(-x)))


def _sigmoid(x):
    return 1.0 / (1.0 + jnp.exp(-x))


def _softplus(x):
    return jnp.maximum(x, 0.0) + jnp.log(1.0 + jnp.exp(-jnp.abs(x)))


def _gelu_tanh(x):
    return 0.5 * x * (1.0 + jnp.tanh(math.sqrt(2.0 / math.pi) * (x + 0.044715 * (x * x * x))))


def _rms(x, w):
    return x * lax.rsqrt(jnp.mean(x * x, axis=-1, keepdims=True) + EPS) * w


def _rmsnorm_body(x_ref, w_ref, o_ref):
    o_ref[...] = _rms(x_ref[...], w_ref[...]).astype(o_ref.dtype)


def _rmsnorm(x, w, tm):
    m, d = x.shape
    return pl.pallas_call(
        _rmsnorm_body,
        grid=(m // tm,),
        in_specs=[pl.BlockSpec((tm, d), lambda i: (i, 0)),
                  pl.BlockSpec((1, d), lambda i: (0, 0))],
        out_specs=pl.BlockSpec((tm, d), lambda i: (i, 0)),
        out_shape=jax.ShapeDtypeStruct((m, d), BF16),
        compiler_params=_params(("arbitrary",), 40),
        name="rmsnorm",
    )(x, w)


def _matmul_body(x_ref, w_ref, o_ref):
    o_ref[...] = jnp.dot(x_ref[...], w_ref[...], preferred_element_type=F32)


def _matmul(x, w, tm, tn, name):
    m, k = x.shape
    n = w.shape[1]
    return pl.pallas_call(
        _matmul_body,
        grid=(n // tn, m // tm),
        in_specs=[pl.BlockSpec((tm, k), lambda j, i: (i, 0)),
                  pl.BlockSpec((k, tn), lambda j, i: (0, j))],
        out_specs=pl.BlockSpec((tm, tn), lambda j, i: (i, j)),
        out_shape=jax.ShapeDtypeStruct((m, n), F32),
        compiler_params=_params(("arbitrary", "arbitrary"), 48),
        name=name,
    )(x, w)


def _rel_bucket_np(dist):
    dist = np.asarray(dist, np.int64)
    max_exact = N_BUCKETS // 2
    d = np.maximum(dist, 1).astype(np.float64)
    val = np.log(d / max_exact) / math.log(MAX_DISTANCE / max_exact) * (N_BUCKETS - max_exact)
    frac = np.abs(val - np.round(val))
    near = (frac < 2e-5) &(dist >= max_exact) & (dist != max_exact) & (dist < MAX_DISTANCE)
    assert not near.any(), "distance on a bucket boundary"
    val = np.where(dist == max_exact, 0.0, val)
    large = np.minimum(max_exact + np.trunc(val).astype(np.int64), N_BUCKETS - 1)
    return np.where(dist < max_exact, dist, large).astype(np.int32)


def _prompt_bucket_tables():
    qi = np.arange(BLK)[:, None]
    kj = np.arange(2 * BLK)[None, :]
    delta = BLK + qi - kj
    inwin = (delta >= 0) & (delta <= BLK)
    tabs = []
    for dil in DILATIONS:
        b = _rel_bucket_np(np.clip(delta, 0, BLK) * dil)
        tabs.append(np.where(inwin, b, -1))
    return np.stack(tabs).astype(np.int32)


def _sample_bucket_tables():
    j = BLK - np.arange(BLK)
    return np.stack([_rel_bucket_np(j * dil)[None, :] for dil in DILATIONS]).astype(np.int32)


def _attn_prompt_body(bucket_ref, relb_ref, q_ref, k_ref, v_ref, o_ref,
                      bias_scr, acc_scr, m_scr, l_scr):
    h = pl.program_id(1)
    col = lax.broadcasted_iota(jnp.int32, (BLK, 2 * BLK), 1)
    tables = _prompt_bucket_tables()
    for br in range(3):
        bk = bucket_ref[br]
        bias = jnp.zeros((BLK, 2 * BLK), F32)
        for kb in sorted(set(tables[br].ravel().tolist()) - {-1}):
            bias = jnp.where(bk == kb, relb_ref[kb, h], bias)
        full = jnp.where(bk >= 0, bias, NEG)
        bias_scr[2 * br] = full
        bias_scr[2 * br + 1] = jnp.where(col >= BLK, full, NEG)

    def run_branch(br, dil, is_first_branch, is_last_branch):
        shift = int(math.log2(dil))
        span = BLK * dil
        stride = None if dil == 1 else dil

        def rows(start):
            return pl.ds(start, BLK, stride=stride) if stride else pl.ds(start, BLK)

        def tasks(it, carry):
            ts = [it * ATTN_UNROLL + u for u in range(ATTN_UNROLL)]
            q_starts = [(t >> shift) * span + (t & (dil - 1)) for t in ts]
            firsts = [jnp.where((t >> shift) == 0, 1, 0) for t in ts]
            p_starts = [qs - span * (1 - f) for qs, f in zip(q_starts, firsts)]
            qs_ = [q_ref[0, rows(qs), :].astype(BF16) for qs in q_starts]
            ks_ = [jnp.concatenate([k_ref[0, rows(ps), :], k_ref[0, rows(qs), :]], axis=0).astype(BF16)
                   for ps, qs in zip(p_starts, q_starts)]
            vs_ = [jnp.concatenate([v_ref[0, rows(ps), :], v_ref[0, rows(qs), :]], axis=0).astype(BF16)
                   for ps, qs in zip(p_starts, q_starts)]
            if not is_first_branch:
                runs = [(m_scr[rows(qs), :], l_scr[rows(qs), :], acc_scr[rows(qs), :]) for qs in q_starts]
            ss = [_bdot_nt(q, k) * ATT_SCALE + bias_scr[2 * br + f] for q, k, f in zip(qs_, ks_, firsts)]
            ms = [jnp.max(s, axis=-1, keepdims=True) for s in ss]
            ps_ = [jnp.exp(s - m) for s, m in zip(ss, ms)]
            ls = [jnp.sum(p, axis=-1, keepdims=True) for p in ps_]
            accs = [_bdot(p, v) for p, v in zip(ps_, vs_)]
            outs = []
            for u in range(ATTN_UNROLL):
                m_b = jnp.broadcast_to(ms[u], (BLK, HEAD_DIM))
                l_b = jnp.broadcast_to(ls[u], (BLK, HEAD_DIM))
                acc_t = accs[u]
                if not is_first_branch:
                    m_run, l_run, acc_run = runs[u]
                    m_new = jnp.maximum(m_run, m_b)
                    a = jnp.exp(m_run - m_new)
                    b = jnp.exp(m_b - m_new)
                    acc_t = a * acc_run + b * acc_t
                    l_b = a * l_run + b * l_b
                    m_b = m_new
                outs.append((m_b, l_b, acc_t))
            for qs, (m_b, l_b, acc_t) in zip(q_starts, outs):
                if is_last_branch:
                    o_ref[0, rows(qs), :] = (acc_t / l_b).astype(o_ref.dtype)
                else:
                    m_scr[rows(qs), :] = m_b
                    l_scr[rows(qs), :] = l_b
                    acc_scr[rows(qs), :] = acc_t
            return carry

        lax.fori_loop(0, 32 // ATTN_UNROLL, tasks, 0)

    run_branch(2, 16, True, False)
    run_branch(1, 4, False, False)
    run_branch(0, 1, False, True)


def _attn_prompt(proj3, rel_bias):
    b, s, _ = proj3.shape
    buckets = jnp.asarray(_prompt_bucket_tables())
    blk = (1, s, HEAD_DIM)
    return pl.pallas_call(
        _attn_prompt_body,
        grid=(b, HEADS_A),
        in_specs=[pl.BlockSpec((3, BLK, 2 * BLK), lambda i, h: (0, 0, 0)),
                  pl.BlockSpec(memory_space=pltpu.SMEM),
                  pl.BlockSpec(blk, lambda i, h: (i, 0, OFF_AQ // HEAD_DIM + h)),
                  pl.BlockSpec(blk, lambda i, h: (i, 0, OFF_AK // HEAD_DIM + h)),
                  pl.BlockSpec(blk, lambda i, h: (i, 0, OFF_AV // HEAD_DIM + h))],
        out_specs=pl.BlockSpec(blk, lambda i, h: (i, 0, h)),
        out_shape=jax.ShapeDtypeStruct((b, s, WIDTH_A), BF16),
        scratch_shapes=[pltpu.VMEM((6, BLK, 2 * BLK), F32),
                        pltpu.VMEM((s, HEAD_DIM), F32),
                        pltpu.VMEM((s, HEAD_DIM), F32),
                        pltpu.VMEM((s, HEAD_DIM), F32)],
        compiler_params=_params(("arbitrary", "arbitrary"), 40),
        name="attn_prompt",
    )(buckets, rel_bias, proj3, proj3, proj3)


def _attn_sample_body(bucket_ref, relbt_ref, q_ref, kn_ref, vn_ref,
                      k1_ref, k4_ref, k16_ref, v1_ref, v4_ref, v16_ref, o_ref, bias_scr):
    relbt = relbt_ref[...]
    tile = (HEADS_A, HEAD_DIM)

    @pl.when(pl.program_id(0) == 0)
    def _():
        for br in range(3):
            bk = bucket_ref[br]
            bias = jnp.zeros((BLK,) + tile, F32)
            for kb in range(N_BUCKETS):
                col = jnp.broadcast_to(relbt[:, kb:kb + 1], tile)
                bias = jnp.where(bk == kb, col[None], bias)
            bias_scr[br] = bias

    def lane_sum(x):
        return jnp.broadcast_to(jnp.sum(x, axis=-1, keepdims=True), x.shape)

    q = q_ref[0]
    s_self = lane_sum(q * kn_ref[0]) * ATT_SCALE + jnp.broadcast_to(relbt[:, 0:1], tile)
    scores = []
    m = s_self
    for br, k_ref in enumerate((k1_ref, k4_ref, k16_ref)):
        s = lane_sum(k_ref[...] * q[None]) * ATT_SCALE + bias_scr[br]
        scores.append(s)
        m = jnp.maximum(m, jnp.max(s, axis=0))
    p_self = 3.0 * jnp.exp(s_self - m)
    l = p_self
    acc = p_self * vn_ref[0]
    for s, v_ref in zip(scores, (v1_ref, v4_ref, v16_ref)):
        p = jnp.exp(s - m[None])
        l = l + jnp.sum(p, axis=0)
        acc = acc + jnp.sum(p * v_ref[...], axis=0)
    o_ref[0] = (acc / l).astype(o_ref.dtype)


def _attn_sample(q, k_new, v_new, cache_k, cache_v, rel_bias):
    b, past = cache_k.shape[:2]
    tile = (HEADS_A, HEAD_DIM)
    buckets = jnp.asarray(np.broadcast_to(_sample_bucket_tables().reshape(3, BLK, 1, 1), (3, BLK) + tile))
    row = pl.BlockSpec((1,) + tile, lambda i: (i, 0, 0))
    views, specs = [], []
    for cache in (cache_k, cache_v):
        for dil in DILATIONS:
            views.append(cache.reshape((b, past // dil, dil) + tile))
            last = past // dil // BLK - 1
            specs.append(pl.BlockSpec((None, BLK, None) + tile,
                                      functools.partial(lambda last, i: (i, last, 0, 0, 0), last)))
    return pl.pallas_call(
        _attn_sample_body,
        grid=(b,),
        in_specs=[pl.BlockSpec((3, BLK) + tile, lambda i: (0, 0, 0, 0)),
                  pl.BlockSpec((HEADS_A, N_BUCKETS), lambda i: (0, 0)),
                  row, row, row] + specs,
        out_specs=row,
        out_shape=jax.ShapeDtypeStruct((b,) + tile, BF16),
        scratch_shapes=[pltpu.VMEM((3, BLK) + tile, F32)],
        compiler_params=_params(("arbitrary",), 40),
        name="attn_sample",
    )(buckets, rel_bias.T, q, k_new, v_new, *views)


GROUP = 4
ATTN_UNROLL = 4


def _dn_prompt_body(q_ref, k_ref, v_ref, z_ref, gates_ref, wq_ref, wk_ref, wv_ref,
                    cq_ref, ck_ref, cv_ref, s0_ref, alog_ref, dtb_ref, nw_ref,
                    o_ref, s_out_ref,
                    s_scr, eq_scr, ek_scr, ev_scr, qn_scr, kn_scr, vv_scr, g_scr, beta_scr,
                    w_scr, u_scr, qg_scr, kdt_scr, attn_scr, gl_scr, o_scr, *, tt):
    hq = pl.program_id(1)
    t = pl.program_id(2)
    nt = pl.num_programs(2)

    @pl.when(t == 0)
    def _():
        s_scr[...] = s0_ref[0]
        eq_scr[5:8, :] = cq_ref[0]
        ek_scr[5:8, :] = ck_ref[0]
        ev_scr[5:8, :] = cv_ref[0]

    def conv_silu(x_ref, w_ref, e_scr):
        e_scr[8:8 + tt, :] = x_ref[0]
        w = w_ref[...]
        y = w[0:1, :] * e_scr[5:5 + tt, :]
        for i in range(1, CONV_W):
            y = y + w[i:i + 1, :] * e_scr[5 + i:5 + i + tt, :]
        tail = e_scr[tt + 5:tt + 8, :]
        e_scr[5:8, :] = tail
        return _silu(y)

    def l2n(x):
        return x * lax.rsqrt(jnp.sum(x * x, axis=-1, keepdims=True) + EPS)

    qn_scr[...] = l2n(conv_silu(q_ref, wq_ref, eq_scr)) * QK_SCALE
    kn_scr[...] = l2n(conv_silu(k_ref, wk_ref, ek_scr))
    vv_scr[...] = conv_silu(v_ref, wv_ref, ev_scr)
    gates = gates_ref[0]
    beta_scr[...] = _sigmoid(gates)
    g_scr[...] = -jnp.exp(alog_ref[...]) * _softplus(gates + dtb_ref[...])

    ri = lax.broadcasted_iota(jnp.int32, (CHUNK, CHUNK), 0)
    ci = lax.broadcasted_iota(jnp.int32, (CHUNK, CHUNK), 1)
    tri = ri >= ci
    strict = ri > ci
    tril_ones = tri.astype(F32)
    lane = lax.broadcasted_iota(jnp.int32, (CHUNK, 128), 1)
    sub = lax.broadcasted_iota(jnp.int32, (128, CHUNK), 0)
    nw = nw_ref[...]

    for c0 in range(0, tt // CHUNK, GROUP):
        units = []
        for c in range(c0, c0 + GROUP):
            rows = slice(c * CHUNK, (c + 1) * CHUNK)
            qn = qn_scr[rows, :]
            kn = kn_scr[rows, :]
            beta_all = beta_scr[rows, :]
            gc_all = _fdot(tril_ones, g_scr[rows, :])
            gc_all_t = gc_all.T
            kk = _bdot_nt(kn, kn)
            qk = _bdot_nt(qn, kn)
            for j in range(2):
                hv = 2 * hq + j
                beta = jnp.sum(jnp.where(lane == hv, beta_all, 0.0), axis=-1, keepdims=True)
                gc = jnp.sum(jnp.where(lane == V_HEADS_B + hv, gc_all, 0.0), axis=-1, keepdims=True)
                gc_row = jnp.sum(jnp.where(sub == V_HEADS_B + hv, gc_all_t, 0.0), axis=0, keepdims=True)
                gc_last = gc_row[:, CHUNK - 1:CHUNK]
                decay = jnp.exp(jnp.where(tri, gc - gc_row, NEG))
                a = jnp.where(strict, beta * kk * decay, 0.0)
                egc = jnp.exp(gc)
                attn_scr[j, rows, :] = (qk * decay).astype(BF16)
                qg_scr[j, rows, :] = (qn * egc).astype(BF16)
                kd = kn * jnp.exp(gc_last - gc)
                kdt_scr[j, c * DK:(c + 1) * DK, :] = kd.T.astype(BF16)
                gl_scr[j, c * 8:(c + 1) * 8, :] = jnp.broadcast_to(jnp.exp(gc_last), (8, DV))
                x = jnp.concatenate([kn * (beta * egc), vv_scr[rows, j * DV:(j + 1) * DV] * beta], axis=-1)
                units.append((j, rows, a, x))
        ns = [-a for _, _, a, _ in units]
        pws = [a for _, _, a, _ in units]
        for _ in range(5):
            pws = [_bdot(pw, pw) for pw in pws]
            ns = [n + pw + _bdot(n, pw) for n, pw in zip(ns, pws)]
        wus = [x + _bdot(n, x) for n, (_, _, _, x) in zip(ns, units)]
        for wu, (j, rows, _, _) in zip(wus, units):
            w_scr[j, rows, :] = wu[:, :DK].astype(BF16)
            u_scr[j, rows, :] = wu[:, DK:]

    for c in range(tt // CHUNK):
        rows = slice(c * CHUNK, (c + 1) * CHUNK)
        for j in range(2):
            state = s_scr[j]
            state_b = state.astype(BF16)
            v_new = u_scr[j, rows, :] - jnp.dot(w_scr[j, rows, :], state_b, preferred_element_type=F32)
            v_new_b = v_new.astype(BF16)
            o_scr[rows, j * DV:(j + 1) * DV] = (
                jnp.dot(qg_scr[j, rows, :], state_b, preferred_element_type=F32)
                + jnp.dot(attn_scr[j, rows, :], v_new_b, preferred_element_type=F32))
            s_scr[j] = (state * gl_scr[j, c * 8:c * 8 + 1, :]
                        + jnp.dot(kdt_scr[j, c * DK:(c + 1) * DK, :], v_new_b, preferred_element_type=F32))

    for j in range(2):
        o = o_scr[:, j * DV:(j + 1) * DV]
        z = z_ref[0, :, j * DV:(j + 1) * DV]
        o = o * lax.rsqrt(jnp.mean(o * o, axis=-1, keepdims=True) + EPS) * nw * _silu(z)
        o_ref[0, :, j * DV:(j + 1) * DV] = o.astype(o_ref.dtype)

    @pl.when(t == nt - 1)
    def _():
        s_out_ref[0] = s_scr[...]


def _dn_prompt(proj3, gates3, conv_state, s0, conv_w, alog_vec, dtb_vec, norm_w, tt):
    b, s, _ = proj3.shape
    body = functools.partial(_dn_prompt_body, tt=tt)
    c128 = lambda off: off // 128
    c256 = lambda off: off // 256
    in_specs = [
        pl.BlockSpec((1, tt, 128), lambda i, h, t: (i, t, c128(OFF_BQ) + h)),
        pl.BlockSpec((1, tt, 128), lambda i, h, t: (i, t, c128(OFF_BK) + h)),
        pl.BlockSpec((1, tt, 256), lambda i, h, t: (i, t, c256(OFF_BV) + h)),
        pl.BlockSpec((1, tt, 256), lambda i, h, t: (i, t, c256(OFF_BZ) + h)),
        pl.BlockSpec((1, tt, 128), lambda i, h, t: (i, t, 0)),
        pl.BlockSpec((CONV_W, 128), lambda i, h, t: (0, h)),
        pl.BlockSpec((CONV_W, 128), lambda i, h, t: (0, c128(WIDTH_BQK) + h)),
        pl.BlockSpec((CONV_W, 256), lambda i, h, t: (0, c256(2 * WIDTH_BQK) + h)),
        pl.BlockSpec((1, CONV_W - 1, 128), lambda i, h, t: (i, 0, h)),
        pl.BlockSpec((1, CONV_W - 1, 128), lambda i, h, t: (i, 0, c128(WIDTH_BQK) + h)),
        pl.BlockSpec((1, CONV_W - 1, 256), lambda i, h, t: (i, 0, c256(2 * WIDTH_BQK) + h)),
        pl.BlockSpec((1, 2, DK, DV), lambda i, h, t: (i, h, 0, 0)),
        pl.BlockSpec((1, 128), lambda i, h, t: (0, 0)),
        pl.BlockSpec((1, 128), lambda i, h, t: (0, 0)),
        pl.BlockSpec((1, DV), lambda i, h, t: (0, 0)),
    ]
    return pl.pallas_call(
        body,
        grid=(b, QK_HEADS_B, s // tt),
        in_specs=in_specs,
        out_specs=[pl.BlockSpec((1, tt, 256), lambda i, h, t: (i, t, h)),
                   pl.BlockSpec((1, 2, DK, DV), lambda i, h, t: (i, h, 0, 0))],
        out_shape=[jax.ShapeDtypeStruct((b, s, WIDTH_BV), BF16),
                   jax.ShapeDtypeStruct((b, V_HEADS_B, DK, DV), F32)],
        scratch_shapes=[pltpu.VMEM((2, DK, DV), F32),
                        pltpu.VMEM((tt + 8, 128), F32),
                        pltpu.VMEM((tt + 8, 128), F32),
                        pltpu.VMEM((tt + 8, 256), F32),
                        pltpu.VMEM((tt, 128), F32),
                        pltpu.VMEM((tt, 128), F32),
                        pltpu.VMEM((tt, 256), F32),
                        pltpu.VMEM((tt, 128), F32),
                        pltpu.VMEM((tt, 128), F32),
                        pltpu.VMEM((2, tt, DK), BF16),
                        pltpu.VMEM((2, tt, DV), F32),
                        pltpu.VMEM((2, tt, DK), BF16),
                        pltpu.VMEM((2, tt // CHUNK * DK, CHUNK), BF16),
                        pltpu.VMEM((2, tt, CHUNK), BF16),
                        pltpu.VMEM((2, tt // CHUNK * 8, DV), F32),
                        pltpu.VMEM((tt, 2 * DV), F32)],
        compiler_params=_params(("arbitrary", "arbitrary", "arbitrary"), 40),
        name="deltanet_prompt",
    )(proj3, proj3, proj3, proj3, gates3, conv_w, conv_w, conv_w,
      conv_state, conv_state, conv_state, s0, alog_vec, dtb_vec, norm_w)


def _dn_sample_body(proj_ref, gates_ref, cw_ref, cs_ref, s0_ref, alog_ref, dtb_ref, nw_ref,
                    o_ref, cs_out_ref, s_out_ref):
    pre = proj_ref[0, :, OFF_BQ:OFF_BQ + CONV_DIM]
    buf = cs_ref[0]
    w = cw_ref[...]
    y = w[CONV_W - 1:CONV_W, :] * pre
    for i in range(CONV_W - 1):
        y = y + w[i:i + 1, :] * buf[i:i + 1, :]
    y = _silu(y)
    cs_out_ref[0, 0:CONV_W - 2, :] = buf[1:CONV_W - 1, :]
    cs_out_ref[0, CONV_W - 2:CONV_W - 1, :] = pre

    gates = gates_ref[0]
    beta_all = _sigmoid(gates)
    g_all = -jnp.exp(alog_ref[...]) * _softplus(gates + dtb_ref[...])
    nw = nw_ref[...]

    def l2n(x):
        return x * lax.rsqrt(jnp.sum(x * x, axis=-1, keepdims=True) + EPS)

    row8 = lax.broadcasted_iota(jnp.int32, (8, DK), 0) == 0
    for hv in range(V_HEADS_B):
        hq = hv // 2
        q = l2n(y[:, hq * DK:(hq + 1) * DK]) * QK_SCALE
        k = l2n(y[:, WIDTH_BQK + hq * DK:WIDTH_BQK + (hq + 1) * DK])
        v = y[:, 2 * WIDTH_BQK + hv * DV:2 * WIDTH_BQK + (hv + 1) * DV]
        beta = beta_all[:, hv:hv + 1]
        g = g_all[:, V_HEADS_B + hv:V_HEADS_B + hv + 1]
        eg = jnp.exp(g)
        state = s0_ref[0, hv]

        def pad8(x):
            return jnp.where(row8, jnp.broadcast_to(x, (8, x.shape[-1])), 0.0)

        v_new = v * beta - _bdot(pad8(k * (beta * eg)), state)[0:1, :]
        qk = jnp.sum(q.astype(BF16).astype(F32) * k.astype(BF16).astype(F32), axis=-1, keepdims=True)
        o = _bdot(pad8(q * eg), state)[0:1, :] + qk.astype(BF16).astype(F32) * v_new.astype(BF16).astype(F32)
        s_out_ref[0, hv] = state * eg + _bdot_tn(pad8(k), pad8(v_new))
        z = proj_ref[0, :, OFF_BZ + hv * DV:OFF_BZ + (hv + 1) * DV]
        o = o * lax.rsqrt(jnp.mean(o * o, axis=-1, keepdims=True) + EPS) * nw * _silu(z)
        o_ref[0, :, hv * DV:(hv + 1) * DV] = o.astype(o_ref.dtype)


def _dn_sample(proj, gates, conv_state, s0, conv_w, alog_vec, dtb_vec, norm_w):
    b = proj.shape[0]
    return pl.pallas_call(
        _dn_sample_body,
        grid=(b,),
        in_specs=[pl.BlockSpec((1, 1, PROJ_MAIN), lambda i: (i, 0, 0)),
                  pl.BlockSpec((1, 1, 128), lambda i: (i, 0, 0)),
                  pl.BlockSpec((CONV_W, CONV_DIM), lambda i: (0, 0)),
                  pl.BlockSpec((1, CONV_W - 1, CONV_DIM), lambda i: (i, 0, 0)),
                  pl.BlockSpec((1, V_HEADS_B, DK, DV), lambda i: (i, 0, 0, 0)),
                  pl.BlockSpec((1, 128), lambda i: (0, 0)),
                  pl.BlockSpec((1, 128), lambda i: (0, 0)),
                  pl.BlockSpec((1, DV), lambda i: (0, 0))],
        out_specs=[pl.BlockSpec((1, 1, WIDTH_BV), lambda i: (i, 0, 0)),
                   pl.BlockSpec((1, CONV_W - 1, CONV_DIM), lambda i: (i, 0, 0)),
                   pl.BlockSpec((1, V_HEADS_B, DK, DV), lambda i: (i, 0, 0, 0))],
        out_shape=[jax.ShapeDtypeStruct((b, 1, WIDTH_BV), BF16),
                   jax.ShapeDtypeStruct((b, CONV_W - 1, CONV_DIM), F32),
                   jax.ShapeDtypeStruct((b, V_HEADS_B, DK, DV), F32)],
        compiler_params=_params(("arbitrary",), 40),
        name="deltanet_sample",
    )(proj, gates, conv_w, conv_state, s0, alog_vec, dtb_vec, norm_w)


def _outproj_body(att_ref, dn_ref, wa_ref, wb_ref, x_ref, lnpost_ref, lnpre_ref, x1_ref, h2_ref):
    mix = (jnp.dot(att_ref[...], wa_ref[...], preferred_element_type=F32)
           + jnp.dot(dn_ref[...], wb_ref[...], preferred_element_type=F32))
    x1 = x_ref[...] + _rms(mix, lnpost_ref[...])
    x1_ref[...] = x1
    h2_ref[...] = _rms(x1, lnpre_ref[...]).astype(h2_ref.dtype)


def _outproj(att, dn, wa, wb, x, ln_post, ln_pre, tm):
    m, d = x.shape
    return pl.pallas_call(
        _outproj_body,
        grid=(m // tm,),
        in_specs=[pl.BlockSpec((tm, WIDTH_A), lambda i: (i, 0)),
                  pl.BlockSpec((tm, WIDTH_BV), lambda i: (i, 0)),
                  pl.BlockSpec((WIDTH_A, d), lambda i: (0, 0)),
                  pl.BlockSpec((WIDTH_BV, d), lambda i: (0, 0)),
                  pl.BlockSpec((tm, d), lambda i: (i, 0)),
                  pl.BlockSpec((1, d), lambda i: (0, 0)),
                  pl.BlockSpec((1, d), lambda i: (0, 0))],
        out_specs=[pl.BlockSpec((tm, d), lambda i: (i, 0)),
                   pl.BlockSpec((tm, d), lambda i: (i, 0))],
        out_shape=[jax.ShapeDtypeStruct((m, d), F32),
                   jax.ShapeDtypeStruct((m, d), BF16)],
        compiler_params=_params(("arbitrary",), 48),
        name="outproj",
    )(att, dn, wa, wb, x, ln_post, ln_pre)


def _ffn_body(h_ref, wg_ref, wv_ref, cwg_ref, cwv_ref, cbg_ref, cbv_ref, wo_ref, x1_ref, ln_ref,
              pg_ref, pv_ref, o_ref, ng_ref, nv_ref, eg_scr, ev_scr, carry_scr,
              *, tm, tiles_per_seq, single_token):
    i = pl.program_id(0)
    j = pl.program_id(1)
    nj = pl.num_programs(1)
    h = h_ref[...]

    def up_conv(w_ref, cw_ref, cb_ref, prev_ref, new_ref, e_scr, slot):
        up = jnp.dot(h, w_ref[...], preferred_element_type=F32)
        cw = cw_ref[...]
        if single_token:
            new_ref[...] = up
            return cw[0:1, :] * prev_ref[0] + cw[1:2, :] * prev_ref[1] + cw[2:3, :] * up + cb_ref[...]
        e_scr[8:8 + tm, :] = up

        @pl.when(i % tiles_per_seq == 0)
        def _():
            e_scr[6:8, :] = prev_ref[0]

        @pl.when(i % tiles_per_seq != 0)
        def _():
            e_scr[6:8, :] = carry_scr[slot, j, 6:8, :]

        tail = up[tm - 2:tm, :]
        carry_scr[slot, j, 6:8, :] = tail
        new_ref[0] = tail
        return (cw[0:1, :] * e_scr[6:6 + tm, :] + cw[1:2, :] * e_scr[7:7 + tm, :]
                + cw[2:3, :] * up + cb_ref[...])

    gate = up_conv(wg_ref, cwg_ref, cbg_ref, pg_ref, ng_ref, eg_scr, 0)
    val = up_conv(wv_ref, cwv_ref, cbv_ref, pv_ref, nv_ref, ev_scr, 1)
    act = (_gelu_tanh(gate) * val).astype(BF16)
    part = jnp.dot(act, wo_ref[...], preferred_element_type=F32)

    @pl.when(j == 0)
    def _():
        o_ref[...] = part

    @pl.when(j != 0)
    def _():
        o_ref[...] += part

    @pl.when(j == nj - 1)
    def _():
        o_ref[...] = x1_ref[...] + _rms(o_ref[...], ln_ref[...])


def _ffn(h2, w_in, conv_w, conv_b, w_out, x1, ln_post, prev, tm, tf, seq_len):
    m, d = h2.shape
    single = seq_len == 1
    nj = D_FF // tf
    tiles_per_seq = 1 if single else seq_len // tm
    if single:
        prev_g = pl.BlockSpec((2, tm, tf), lambda i, j: (0, i, j))
        prev_v = pl.BlockSpec((2, tm, tf), lambda i, j: (0, i, nj + j))
        new_g = pl.BlockSpec((tm, tf), lambda i, j: (i, j))
        new_shape = jax.ShapeDtypeStruct((m, D_FF), F32)
    else:
        prev_g = pl.BlockSpec((1, 2, tf), lambda i, j: (i // tiles_per_seq, 0, j))
        prev_v = pl.BlockSpec((1, 2, tf), lambda i, j: (i // tiles_per_seq, 0, nj + j))
        new_g = pl.BlockSpec((1, 2, tf), lambda i, j: (i, 0, j))
        new_shape = jax.ShapeDtypeStruct((m // tm, 2, D_FF), F32)
    body = functools.partial(_ffn_body, tm=tm, tiles_per_seq=tiles_per_seq, single_token=single)
    return pl.pallas_call(
        body,
        grid=(m // tm, nj),
        in_specs=[pl.BlockSpec((tm, d), lambda i, j: (i, 0)),
                  pl.BlockSpec((d, tf), lambda i, j: (0, j)),
                  pl.BlockSpec((d, tf), lambda i, j: (0, nj + j)),
                  pl.BlockSpec((FFN_CONV_W, tf), lambda i, j: (0, j)),
                  pl.BlockSpec((FFN_CONV_W, tf), lambda i, j: (0, nj + j)),
                  pl.BlockSpec((1, tf), lambda i, j: (0, j)),
                  pl.BlockSpec((1, tf), lambda i, j: (0, nj + j)),
                  pl.BlockSpec((tf, d), lambda i, j: (j, 0)),
                  pl.BlockSpec((tm, d), lambda i, j: (i, 0)),
                  pl.BlockSpec((1, d), lambda i, j: (0, 0)),
                  prev_g, prev_v],
        out_specs=[pl.BlockSpec((tm, d), lambda i, j: (i, 0)), new_g, new_g],
        out_shape=[jax.ShapeDtypeStruct((m, d), F32), new_shape, new_shape],
        scratch_shapes=[pltpu.VMEM((tm + 8, tf), F32),
                        pltpu.VMEM((tm + 8, tf), F32),
                        pltpu.VMEM((2, nj, 8, tf), F32)],
        compiler_params=_params(("arbitrary", "arbitrary"), 56),
        name="convffn",
    )(h2, w_in, w_in, conv_w, conv_w, conv_b, conv_b, w_out, x1, ln_post, prev, prev)


def _cache_shift_body(ck_ref, cv_ref, ck_next_ref, cv_next_ref, nk_ref, nv_ref, ok_ref, ov_ref, *, tr):
    last = pl.program_id(1) == pl.num_programs(1) - 1
    for c_ref, nxt_ref, n_ref, o_ref in ((ck_ref, ck_next_ref, nk_ref, ok_ref),
                                         (cv_ref, cv_next_ref, nv_ref, ov_ref)):
        o_ref[0, 0:tr - 1] = c_ref[0, 1:tr]
        o_ref[0, tr - 1] = jnp.where(last, n_ref[0, 0], nxt_ref[0, 0])


def _cache_shift(cache_k, cache_v, new_k, new_v, tr):
    nb, rows, nh, dh = cache_k.shape
    main = pl.BlockSpec((1, tr, nh, dh), lambda b, i: (b, i, 0, 0))
    nxt = pl.BlockSpec((1, 1, nh, dh), lambda b, i: (b, jnp.minimum((i + 1) * tr, rows - 1), 0, 0))
    new = pl.BlockSpec((1, 1, nh, dh), lambda b, i: (b, 0, 0, 0))
    shape = jax.ShapeDtypeStruct(cache_k.shape, cache_k.dtype)
    return pl.pallas_call(
        functools.partial(_cache_shift_body, tr=tr),
        grid=(nb, rows // tr),
        in_specs=[main, main, nxt, nxt, new, new],
        out_specs=[main, main],
        out_shape=[shape, shape],
        compiler_params=_params(("arbitrary", "arbitrary"), 40),
        name="cache_shift",
    )(cache_k, cache_v, cache_k, cache_v, new_k, new_v)


def _lane_vec(values, offset):
    return jnp.zeros((1, 128), F32).at[0, offset:offset + V_HEADS_B].set(values.astype(F32))


def kernel(x_prompt, x_sample, cache_win_k, cache_win_v, state_dn_conv, state_dn_rec, state_ffn_conv,
           rel_bias, ln_mix_pre, w_in, dn_conv_w, dn_A_log, dn_dt_bias, dn_norm_w, w_out, ln_mix_post,
           ln_ffn_pre, w_ffn_in, ffn_conv_w, ffn_conv_b, w_ffn_out, ln_ffn_post):
    bp, sp, d = x_prompt.shape
    bs = x_sample.shape[0]
    l = 0

    w_main = w_in[l, :, :PROJ_MAIN].astype(BF16)
    w_gate = jnp.pad(w_in[l, :, PROJ_MAIN:], ((0, 0), (0, 128 - 2 * V_HEADS_B))).astype(BF16)
    wo_a = w_out[l, :WIDTH_A].astype(BF16)
    wo_b = w_out[l, WIDTH_A:].astype(BF16)
    wf_in = w_ffn_in[l].astype(BF16)
    wf_out = w_ffn_out[l].astype(BF16)
    ln1 = ln_mix_pre[l][None, :]
    ln2 = ln_mix_post[l][None, :]
    ln3 = ln_ffn_pre[l][None, :]
    ln4 = ln_ffn_post[l][None, :]
    conv_w = dn_conv_w[l]
    alog_vec = _lane_vec(dn_A_log[l], V_HEADS_B)
    dtb_vec = _lane_vec(dn_dt_bias[l], V_HEADS_B)
    norm_w = dn_norm_w[l][None, :]
    fcw = ffn_conv_w[l]
    fcb = ffn_conv_b[l][None, :]

    xp = x_prompt.reshape(bp * sp, d)
    hp = _rmsnorm(xp, ln1, 512)
    proj_p = _matmul(hp, w_main, 1024, 1024, "inproj_prompt")
    gates_p = _matmul(hp, w_gate, 1024, 128, "gates_prompt")
    proj3 = proj_p.reshape(bp, sp, PROJ_MAIN)
    att_p = _attn_prompt(proj3, rel_bias)
    dn_p, p_dn_rec = _dn_prompt(
        proj3, gates_p.reshape(bp, sp, 128),
        jnp.zeros((bp, CONV_W - 1, CONV_DIM), F32), jnp.zeros((bp, V_HEADS_B, DK, DV), F32),
        conv_w, alog_vec, dtb_vec, norm_w, 512)
    x1_p, h2_p = _outproj(att_p.reshape(bp * sp, WIDTH_A), dn_p.reshape(bp * sp, WIDTH_BV),
                          wo_a, wo_b, xp, ln2, ln3, 512)
    y_p, fc_g, fc_v = _ffn(h2_p, wf_in, fcw, fcb, wf_out, x1_p, ln4,
                           jnp.zeros((bp, FFN_CONV_W - 1, 2 * D_FF), F32), 512, 512, sp)
    keep = min(MAX_DISTANCE, sp)
    p_win_k = proj3[:, sp - keep:, OFF_AK:OFF_AK + WIDTH_A].reshape(1, bp, keep, HEADS_A, HEAD_DIM)
    p_win_v = proj3[:, sp - keep:, OFF_AV:OFF_AV + WIDTH_A].reshape(1, bp, keep, HEADS_A, HEAD_DIM)
    p_dn_conv = proj3[:, sp - (CONV_W - 1):, OFF_BQ:OFF_BQ + CONV_DIM][None]
    tiles = sp // 512
    p_ffn_conv = jnp.concatenate([fc_g[tiles - 1::tiles], fc_v[tiles - 1::tiles]], axis=-1)[None]

    xs = x_sample.reshape(bs, d)
    hs = _rmsnorm(xs, ln1, bs)
    proj_s = _matmul(hs, w_main, bs, 1024, "inproj_sample")
    gates_s = _matmul(hs, w_gate, bs, 128, "gates_sample")
    past = cache_win_k.shape[2]
    ck = cache_win_k[l]
    cv = cache_win_v[l]
    new_k = proj_s[:, OFF_AK:OFF_AK + WIDTH_A]
    new_v = proj_s[:, OFF_AV:OFF_AV + WIDTH_A]
    new_q = proj_s[:, OFF_AQ:OFF_AQ + WIDTH_A].reshape(bs, HEADS_A, HEAD_DIM)
    new_k = new_k.reshape(bs, HEADS_A, HEAD_DIM)
    new_v = new_v.reshape(bs, HEADS_A, HEAD_DIM)
    att_s = _attn_sample(new_q, new_k, new_v, ck, cv, rel_bias)
    s_win_k, s_win_v = _cache_shift(ck, cv, new_k[:, None], new_v[:, None], 512)
    dn_s, s_dn_conv, s_dn_rec = _dn_sample(proj_s[:, None], gates_s[:, None], state_dn_conv[l],
                                           state_dn_rec[l], conv_w, alog_vec, dtb_vec, norm_w)
    x1_s, h2_s = _outproj(att_s.reshape(bs, WIDTH_A), dn_s.reshape(bs, WIDTH_BV),
                          wo_a, wo_b, xs, ln2, ln3, bs)
    prev_s = jnp.swapaxes(state_ffn_conv[l], 0, 1)
    y_s, up_g, up_v = _ffn(h2_s, wf_in, fcw, fcb, wf_out, x1_s, ln4, prev_s, bs, 512, 1)
    s_ffn_conv = jnp.stack([prev_s[1], jnp.concatenate([up_g, up_v], axis=-1)], axis=1)[None]

    return (y_p.reshape(bp, sp, d), y_s.reshape(bs, 1, d),
            p_win_k, p_win_v, p_dn_conv, p_dn_rec[None], p_ffn_conv,
            s_win_k[None], s_win_v[None], s_dn_conv[None], s_dn_rec[None], s_ffn_conv)
```

```python
import functools
import math

import numpy as np
import jax
import jax.numpy as jnp
from jax import lax
from jax.experimental import pallas as pl
from jax.experimental.pallas import tpu as pltpu

F32 = jnp.float32
BF16 = jnp.bfloat16

D_MODEL = 2048
HEAD_DIM = 128
WIDTH_A = 1024
HEADS_A = 8
DILATIONS = (1, 4, 16)
BLK = 128
N_BUCKETS = 32
MAX_DISTANCE = 2048
DK = 128
DV = 128
V_HEADS_B = 8
QK_HEADS_B = 4
WIDTH_BQK = 512
WIDTH_BV = 1024
CONV_W = 4
CONV_DIM = 2048
CHUNK = 64
D_FF = 5632
FFN_CONV_W = 3
EPS = 1e-6
NEG = -1e30
ATT_SCALE = HEAD_DIM ** -0.5
QK_SCALE = DK ** -0.5

OFF_AQ, OFF_AK, OFF_AV = 0, 1024, 2048
OFF_BQ, OFF_BK, OFF_BV, OFF_BZ = 3072, 3584, 4096, 5120
OFF_GATES = 6144
PROJ_MAIN = 6144

MIB = 2 ** 20


def _params(semantics, vmem_mib):
    return pltpu.CompilerParams(dimension_semantics=semantics, vmem_limit_bytes=vmem_mib * MIB)


def _bdot(a, b):
    return jnp.dot(a.astype(BF16), b.astype(BF16), preferred_element_type=F32)


def _bdot_nt(a, b):
    return lax.dot_general(a.astype(BF16), b.astype(BF16), (((1,), (1,)), ((), ())),
                           preferred_element_type=F32)


def _bdot_tn(a, b):
    return lax.dot_general(a.astype(BF16), b.astype(BF16), (((0,), (0,)), ((), ())),
                           preferred_element_type=F32)


def _fdot(a, b):
    return jnp.dot(a, b, preferred_element_type=F32, precision=lax.Precision.HIGHEST)


def _silu(x):
    return x * (1.0 / (1.0 + jnp.exp(-x)))


def _sigmoid(x):
    return 1.0 / (1.0 + jnp.exp(-x))


def _softplus(x):
    return jnp.maximum(x, 0.0) + jnp.log(1.0 + jnp.exp(-jnp.abs(x)))


def _gelu_tanh(x):
    return 0.5 * x * (1.0 + jnp.tanh(math.sqrt(2.0 / math.pi) * (x + 0.044715 * (x * x * x))))


def _rms(x, w):
    return x * lax.rsqrt(jnp.mean(x * x, axis=-1, keepdims=True) + EPS) * w


def _rmsnorm_body(x_ref, w_ref, o_ref):
    o_ref[...] = _rms(x_ref[...], w_ref[...]).astype(o_ref.dtype)


def _rmsnorm(x, w, tm):
    m, d = x.shape
    return pl.pallas_call(
        _rmsnorm_body,
        grid=(m // tm,),
        in_specs=[pl.BlockSpec((tm, d), lambda i: (i, 0)),
                  pl.BlockSpec((1, d), lambda i: (0, 0))],
        out_specs=pl.BlockSpec((tm, d), lambda i: (i, 0)),
        out_shape=jax.ShapeDtypeStruct((m, d), BF16),
        compiler_params=_params(("arbitrary",), 40),
        name="rmsnorm",
    )(x, w)


def _matmul_body(x_ref, w_ref, o_ref):
    o_ref[...] = jnp.dot(x_ref[...], w_ref[...], preferred_element_type=F32)


def _matmul(x, w, tm, tn, name):
    m, k = x.shape
    n = w.shape[1]
    return pl.pallas_call(
        _matmul_body,
        grid=(n // tn, m // tm),
        in_specs=[pl.BlockSpec((tm, k), lambda j, i: (i, 0)),
                  pl.BlockSpec((k, tn), lambda j, i: (0, j))],
        out_specs=pl.BlockSpec((tm, tn), lambda j, i: (i, j)),
        out_shape=jax.ShapeDtypeStruct((m, n), F32),
        compiler_params=_params(("arbitrary", "arbitrary"), 48),
        name=name,
    )(x, w)


def _rel_bucket_np(dist):
    dist = np.asarray(dist, np.int64)
    max_exact = N_BUCKETS // 2
    d = np.maximum(dist, 1).astype(np.float64)
    val = np.log(d / max_exact) / math.log(MAX_DISTANCE / max_exact) * (N_BUCKETS - max_exact)
    frac = np.abs(val - np.round(val))
    near = (frac < 2e-5) &(dist >= max_exact) & (dist != max_exact) & (dist < MAX_DISTANCE)
    assert not near.any(), "distance on a bucket boundary"
    val = np.where(dist == max_exact, 0.0, val)
    large = np.minimum(max_exact + np.trunc(val).astype(np.int64), N_BUCKETS - 1)
    return np.where(dist < max_exact, dist, large).astype(np.int32)


def _prompt_bucket_tables():
    qi = np.arange(BLK)[:, None]
    kj = np.arange(2 * BLK)[None, :]
    delta = BLK + qi - kj
    inwin = (delta >= 0) & (delta <= BLK)
    tabs = []
    for dil in DILATIONS:
        b = _rel_bucket_np(np.clip(delta, 0, BLK) * dil)
        tabs.append(np.where(inwin, b, -1))
    return np.stack(tabs).astype(np.int32)


def _sample_bucket_tables():
    j = BLK - np.arange(BLK)
    return np.stack([_rel_bucket_np(j * dil)[None, :] for dil in DILATIONS]).astype(np.int32)


def _attn_prompt_body(bucket_ref, relb_ref, q_ref, k_ref, v_ref, o_ref,
                      bias_scr, acc_scr, m_scr, l_scr):
    h = pl.program_id(1)
    col = lax.broadcasted_iota(jnp.int32, (BLK, 2 * BLK), 1)
    tables = _prompt_bucket_tables()
    for br in range(3):
        bk = bucket_ref[br]
        bias = jnp.zeros((BLK, 2 * BLK), F32)
        for kb in sorted(set(tables[br].ravel().tolist()) - {-1}):
            bias = jnp.where(bk == kb, relb_ref[kb, h], bias)
        full = jnp.where(bk >= 0, bias, NEG)
        bias_scr[2 * br] = full
        bias_scr[2 * br + 1] = jnp.where(col >= BLK, full, NEG)

    def run_branch(br, dil, is_first_branch, is_last_branch):
        shift = int(math.log2(dil))
        span = BLK * dil
        stride = None if dil == 1 else dil

        def rows(start):
            return pl.ds(start, BLK, stride=stride) if stride else pl.ds(start, BLK)

        def tasks(it, carry):
            ts = [it * ATTN_UNROLL + u for u in range(ATTN_UNROLL)]
            q_starts = [(t >> shift) * span + (t & (dil - 1)) for t in ts]
            firsts = [jnp.where((t >> shift) == 0, 1, 0) for t in ts]
            p_starts = [qs - span * (1 - f) for qs, f in zip(q_starts, firsts)]
            qs_ = [q_ref[0, rows(qs), :].astype(BF16) for qs in q_starts]
            ks_ = [jnp.concatenate([k_ref[0, rows(ps), :], k_ref[0, rows(qs), :]], axis=0).astype(BF16)
                   for ps, qs in zip(p_starts, q_starts)]
            vs_ = [jnp.concatenate([v_ref[0, rows(ps), :], v_ref[0, rows(qs), :]], axis=0).astype(BF16)
                   for ps, qs in zip(p_starts, q_starts)]
            if not is_first_branch:
                runs = [(m_scr[rows(qs), :], l_scr[rows(qs), :], acc_scr[rows(qs), :]) for qs in q_starts]
            ss = [_bdot_nt(q, k) * ATT_SCALE + bias_scr[2 * br + f] for q, k, f in zip(qs_, ks_, firsts)]
            ms = [jnp.max(s, axis=-1, keepdims=True) for s in ss]
            ps_ = [jnp.exp(s - m) for s, m in zip(ss, ms)]
            ls = [jnp.sum(p, axis=-1, keepdims=True) for p in ps_]
            accs = [_bdot(p, v) for p, v in zip(ps_, vs_)]
            outs = []
            for u in range(ATTN_UNROLL):
                m_b = jnp.broadcast_to(ms[u], (BLK, HEAD_DIM))
                l_b = jnp.broadcast_to(ls[u], (BLK, HEAD_DIM))
                acc_t = accs[u]
                if not is_first_branch:
                    m_run, l_run, acc_run = runs[u]
                    m_new = jnp.maximum(m_run, m_b)
                    a = jnp.exp(m_run - m_new)
                    b = jnp.exp(m_b - m_new)
                    acc_t = a * acc_run + b * acc_t
                    l_b = a * l_run + b * l_b
                    m_b = m_new
                outs.append((m_b, l_b, acc_t))
            for qs, (m_b, l_b, acc_t) in zip(q_starts, outs):
                if is_last_branch:
                    o_ref[0, rows(qs), :] = (acc_t / l_b).astype(o_ref.dtype)
                else:
                    m_scr[rows(qs), :] = m_b
                    l_scr[rows(qs), :] = l_b
                    acc_scr[rows(qs), :] = acc_t
            return carry

        lax.fori_loop(0, 32 // ATTN_UNROLL, tasks, 0)

    run_branch(2, 16, True, False)
    run_branch(1, 4, False, False)
    run_branch(0, 1, False, True)


def _attn_prompt(proj3, rel_bias):
    b, s, _ = proj3.shape
    buckets = jnp.asarray(_prompt_bucket_tables())
    blk = (1, s, HEAD_DIM)
    return pl.pallas_call(
        _attn_prompt_body,
        grid=(b, HEADS_A),
        in_specs=[pl.BlockSpec((3, BLK, 2 * BLK), lambda i, h: (0, 0, 0)),
                  pl.BlockSpec(memory_space=pltpu.SMEM),
                  pl.BlockSpec(blk, lambda i, h: (i, 0, OFF_AQ // HEAD_DIM + h)),
                  pl.BlockSpec(blk, lambda i, h: (i, 0, OFF_AK // HEAD_DIM + h)),
                  pl.BlockSpec(blk, lambda i, h: (i, 0, OFF_AV // HEAD_DIM + h))],
        out_specs=pl.BlockSpec(blk, lambda i, h: (i, 0, h)),
        out_shape=jax.ShapeDtypeStruct((b, s, WIDTH_A), BF16),
        scratch_shapes=[pltpu.VMEM((6, BLK, 2 * BLK), F32),
                        pltpu.VMEM((s, HEAD_DIM), F32),
                        pltpu.VMEM((s, HEAD_DIM), F32),
                        pltpu.VMEM((s, HEAD_DIM), F32)],
        compiler_params=_params(("arbitrary", "arbitrary"), 40),
        name="attn_prompt",
    )(buckets, rel_bias, proj3, proj3, proj3)


def _attn_sample_body(bucket_ref, relbt_ref, q_ref, kn_ref, vn_ref,
                      k1_ref, k4_ref, k16_ref, v1_ref, v4_ref, v16_ref, o_ref, bias_scr):
    relbt = relbt_ref[...]
    tile = (HEADS_A, HEAD_DIM)

    @pl.when(pl.program_id(0) == 0)
    def _():
        for br in range(3):
            bk = bucket_ref[br]
            bias = jnp.zeros((BLK,) + tile, F32)
            for kb in range(N_BUCKETS):
                col = jnp.broadcast_to(relbt[:, kb:kb + 1], tile)
                bias = jnp.where(bk == kb, col[None], bias)
            bias_scr[br] = bias

    def lane_sum(x):
        return jnp.broadcast_to(jnp.sum(x, axis=-1, keepdims=True), x.shape)

    q = q_ref[0]
    s_self = lane_sum(q * kn_ref[0]) * ATT_SCALE + jnp.broadcast_to(relbt[:, 0:1], tile)
    scores = []
    m = s_self
    for br, k_ref in enumerate((k1_ref, k4_ref, k16_ref)):
        s = lane_sum(k_ref[...] * q[None]) * ATT_SCALE + bias_scr[br]
        scores.append(s)
        m = jnp.maximum(m, jnp.max(s, axis=0))
    p_self = 3.0 * jnp.exp(s_self - m)
    l = p_self
    acc = p_self * vn_ref[0]
    for s, v_ref in zip(scores, (v1_ref, v4_ref, v16_ref)):
        p = jnp.exp(s - m[None])
        l = l + jnp.sum(p, axis=0)
        acc = acc + jnp.sum(p * v_ref[...], axis=0)
    o_ref[0] = (acc / l).astype(o_ref.dtype)


def _attn_sample(q, k_new, v_new, cache_k, cache_v, rel_bias):
    b, past = cache_k.shape[:2]
    tile = (HEADS_A, HEAD_DIM)
    buckets = jnp.asarray(np.broadcast_to(_sample_bucket_tables().reshape(3, BLK, 1, 1), (3, BLK) + tile))
    row = pl.BlockSpec((1,) + tile, lambda i: (i, 0, 0))
    views, specs = [], []
    for cache in (cache_k, cache_v):
        for dil in DILATIONS:
            views.append(cache.reshape((b, past // dil, dil) + tile))
            last = past // dil // BLK - 1
            specs.append(pl.BlockSpec((None, BLK, None) + tile,
                                      functools.partial(lambda last, i: (i, last, 0, 0, 0), last)))
    return pl.pallas_call(
        _attn_sample_body,
        grid=(b,),
        in_specs=[pl.BlockSpec((3, BLK) + tile, lambda i: (0, 0, 0, 0)),
                  pl.BlockSpec((HEADS_A, N_BUCKETS), lambda i: (0, 0)),
                  row, row, row] + specs,
        out_specs=row,
        out_shape=jax.ShapeDtypeStruct((b,) + tile, BF16),
        scratch_shapes=[pltpu.VMEM((3, BLK) + tile, F32)],
        compiler_params=_params(("arbitrary",), 40),
        name="attn_sample",
    )(buckets, rel_bias.T, q, k_new, v_new, *views)


GROUP = 1
ATTN_UNROLL = 4
FFN_ROWS = 64
FFN_COLS = 512
FFN_PIECE = 256


def _dn_prompt_body(q_ref, k_ref, v_ref, z_ref, gates_ref, cw_ref, cs_ref, s0_ref, alog_ref, dtb_ref, nw_ref,
                    o_ref, s_out_ref,
                    s_scr, e_scr, qn_scr, kn_scr, vv_scr, g_scr, beta_scr,
                    w_scr, u_scr, qg_scr, kdt_scr, attn_scr, gl_scr, o_scr, *, tt):
    t = pl.program_id(1)
    nt = pl.num_programs(1)

    @pl.when(t == 0)
    def _():
        s_scr[...] = s0_ref[0]
        e_scr[5:8, :] = cs_ref[0]

    e_scr[8:8 + tt, 0:WIDTH_BQK] = q_ref[0]
    e_scr[8:8 + tt, WIDTH_BQK:2 * WIDTH_BQK] = k_ref[0]
    e_scr[8:8 + tt, 2 * WIDTH_BQK:CONV_DIM] = v_ref[0]

    def l2n(x):
        return x * lax.rsqrt(jnp.sum(x * x, axis=-1, keepdims=True) + EPS)

    for c0 in range(0, CONV_DIM, DK):
        cols = slice(c0, c0 + DK)
        w = cw_ref[:, cols]
        y = w[0:1, :] * e_scr[5:5 + tt, cols]
        for i in range(1, CONV_W):
            y = y + w[i:i + 1, :] * e_scr[5 + i:5 + i + tt, cols]
        y = _silu(y)
        if c0 < WIDTH_BQK:
            qn_scr[:, cols] = l2n(y) * QK_SCALE
        elif c0 < 2 * WIDTH_BQK:
            kn_scr[:, c0 - WIDTH_BQK:c0 - WIDTH_BQK + DK] = l2n(y)
        else:
            vv_scr[:, c0 - 2 * WIDTH_BQK:c0 - 2 * WIDTH_BQK + DK] = y
    e_scr[5:8, :] = e_scr[tt + 5:tt + 8, :]
    gates = gates_ref[0]
    beta_scr[...] = _sigmoid(gates)
    g_scr[...] = -jnp.exp(alog_ref[...]) * _softplus(gates + dtb_ref[...])

    ri = lax.broadcasted_iota(jnp.int32, (CHUNK, CHUNK), 0)
    ci = lax.broadcasted_iota(jnp.int32, (CHUNK, CHUNK), 1)
    tri = ri >= ci
    strict = ri > ci
    tril_ones = tri.astype(F32)
    nw = nw_ref[...]

    for c0 in range(0, tt // CHUNK, GROUP):
        units = []
        for c in range(c0, c0 + GROUP):
            rows = slice(c * CHUNK, (c + 1) * CHUNK)
            beta_all = beta_scr[rows, :]
            gc_all = _fdot(tril_ones, g_scr[rows, :])
            gc_all_t = gc_all.T
            for hq in range(QK_HEADS_B):
                qn = qn_scr[rows, hq * DK:(hq + 1) * DK]
                kn = kn_scr[rows, hq * DK:(hq + 1) * DK]
                kk = _bdot_nt(kn, kn)
                qk = _bdot_nt(qn, kn)
                for hv in range(2 * hq, 2 * hq + 2):
                    beta = beta_all[:, hv:hv + 1]
                    gc = gc_all[:, V_HEADS_B + hv:V_HEADS_B + hv + 1]
                    gc_row = gc_all_t[V_HEADS_B + hv:V_HEADS_B + hv + 1, :]
                    gc_last = gc_row[:, CHUNK - 1:CHUNK]
                    decay = jnp.exp(jnp.where(tri, gc - gc_row, NEG))
                    a = jnp.where(strict, beta * kk * decay, 0.0)
                    egc = jnp.exp(gc)
                    attn_scr[hv, rows, :] = (qk * decay).astype(BF16)
                    qg_scr[hv, rows, :] = (qn * egc).astype(BF16)
                    kd = kn * jnp.exp(gc_last - gc)
                    kdt_scr[hv, c * DK:(c + 1) * DK, :] = kd.T.astype(BF16)
                    gl_scr[hv, c * 8:(c + 1) * 8, :] = jnp.broadcast_to(jnp.exp(gc_last), (8, DV))
                    x = jnp.concatenate([kn * (beta * egc), vv_scr[rows, hv * DV:(hv + 1) * DV] * beta],
                                        axis=-1)
                    units.append((hv, rows, a, x))
        ns = [-a for _, _, a, _ in units]
        pws = [a for _, _, a, _ in units]
        for _ in range(5):
            pws = [_bdot(pw, pw) for pw in pws]
            ns = [n + pw + _bdot(n, pw) for n, pw in zip(ns, pws)]
        wus = [x + _bdot(n, x) for n, (_, _, _, x) in zip(ns, units)]
        for wu, (hv, rows, _, _) in zip(wus, units):
            w_scr[hv, rows, :] = wu[:, :DK].astype(BF16)
            u_scr[hv, rows, :] = wu[:, DK:]

    heads = range(V_HEADS_B)
    for c in range(tt // CHUNK):
        rows = slice(c * CHUNK, (c + 1) * CHUNK)
        states = [s_scr[hv] for hv in heads]
        states_b = [s.astype(BF16) for s in states]
        v_news = [u_scr[hv, rows, :] - jnp.dot(w_scr[hv, rows, :], states_b[hv], preferred_element_type=F32)
                  for hv in heads]
        v_news_b = [v.astype(BF16) for v in v_news]
        for hv in heads:
            s_scr[hv] = (states[hv] * gl_scr[hv, c * 8:c * 8 + 1, :]
                         + jnp.dot(kdt_scr[hv, c * DK:(c + 1) * DK, :], v_news_b[hv],
                                   preferred_element_type=F32))
        for hv in heads:
            o_scr[rows, hv * DV:(hv + 1) * DV] = (
                jnp.dot(qg_scr[hv, rows, :], states_b[hv], preferred_element_type=F32)
                + jnp.dot(attn_scr[hv, rows, :], v_news_b[hv], preferred_element_type=F32))

    for hv in heads:
        o = o_scr[:, hv * DV:(hv + 1) * DV]
        z = z_ref[0, :, hv * DV:(hv + 1) * DV]
        o = o * lax.rsqrt(jnp.mean(o * o, axis=-1, keepdims=True) + EPS) * nw * _silu(z)
        o_ref[0, :, hv * DV:(hv + 1) * DV] = o.astype(o_ref.dtype)

    @pl.when(t == nt - 1)
    def _():
        s_out_ref[0] = s_scr[...]


def _dn_prompt(proj3, gates3, conv_state, s0, conv_w, alog_vec, dtb_vec, norm_w, tt):
    b, s, _ = proj3.shape
    body = functools.partial(_dn_prompt_body, tt=tt)
    nh = V_HEADS_B
    in_specs = [
        pl.BlockSpec((1, tt, WIDTH_BQK), lambda i, t: (i, t, OFF_BQ // WIDTH_BQK)),
        pl.BlockSpec((1, tt, WIDTH_BQK), lambda i, t: (i, t, OFF_BK // WIDTH_BQK)),
        pl.BlockSpec((1, tt, WIDTH_BV), lambda i, t: (i, t, OFF_BV // WIDTH_BV)),
        pl.BlockSpec((1, tt, WIDTH_BV), lambda i, t: (i, t, OFF_BZ // WIDTH_BV)),
        pl.BlockSpec((1, tt, 128), lambda i, t: (i, t, 0)),
        pl.BlockSpec((CONV_W, CONV_DIM), lambda i, t: (0, 0)),
        pl.BlockSpec((1, CONV_W - 1, CONV_DIM), lambda i, t: (i, 0, 0)),
        pl.BlockSpec((1, nh, DK, DV), lambda i, t: (i, 0, 0, 0)),
        pl.BlockSpec((1, 128), lambda i, t: (0, 0)),
        pl.BlockSpec((1, 128), lambda i, t: (0, 0)),
        pl.BlockSpec((1, DV), lambda i, t: (0, 0)),
    ]
    return pl.pallas_call(
        body,
        grid=(b, s // tt),
        in_specs=in_specs,
        out_specs=[pl.BlockSpec((1, tt, WIDTH_BV), lambda i, t: (i, t, 0)),
                   pl.BlockSpec((1, nh, DK, DV), lambda i, t: (i, 0, 0, 0))],
        out_shape=[jax.ShapeDtypeStruct((b, s, WIDTH_BV), BF16),
                   jax.ShapeDtypeStruct((b, nh, DK, DV), F32)],
        scratch_shapes=[pltpu.VMEM((nh, DK, DV), F32),
                        pltpu.VMEM((tt + 8, CONV_DIM), F32),
                        pltpu.VMEM((tt, WIDTH_BQK), F32),
                        pltpu.VMEM((tt, WIDTH_BQK), F32),
                        pltpu.VMEM((tt, WIDTH_BV), F32),
                        pltpu.VMEM((tt, 128), F32),
                        pltpu.VMEM((tt, 128), F32),
                        pltpu.VMEM((nh, tt, DK), BF16),
                        pltpu.VMEM((nh, tt, DV), F32),
                        pltpu.VMEM((nh, tt, DK), BF16),
                        pltpu.VMEM((nh, tt // CHUNK * DK, CHUNK), BF16),
                        pltpu.VMEM((nh, tt, CHUNK), BF16),
                        pltpu.VMEM((nh, tt // CHUNK * 8, DV), F32),
                        pltpu.VMEM((tt, WIDTH_BV), F32)],
        compiler_params=_params(("arbitrary", "arbitrary"), 48),
        name="deltanet_prompt",
    )(proj3, proj3, proj3, proj3, gates3, conv_w, conv_state, s0, alog_vec, dtb_vec, norm_w)


def _dn_sample_body(proj_ref, gates_ref, cw_ref, cs_ref, s0_ref, alog_ref, dtb_ref, nw_ref,
                    o_ref, cs_out_ref, s_out_ref):
    pre = proj_ref[0, :, OFF_BQ:OFF_BQ + CONV_DIM]
    buf = cs_ref[0]
    w = cw_ref[...]
    y = w[CONV_W - 1:CONV_W, :] * pre
    for i in range(CONV_W - 1):
        y = y + w[i:i + 1, :] * buf[i:i + 1, :]
    y = _silu(y)
    cs_out_ref[0, 0:CONV_W - 2, :] = buf[1:CONV_W - 1, :]
    cs_out_ref[0, CONV_W - 2:CONV_W - 1, :] = pre

    gates = gates_ref[0]
    beta_all = _sigmoid(gates)
    g_all = -jnp.exp(alog_ref[...]) * _softplus(gates + dtb_ref[...])
    nw = nw_ref[...]

    def l2n(x):
        return x * lax.rsqrt(jnp.sum(x * x, axis=-1, keepdims=True) + EPS)

    row8 = lax.broadcasted_iota(jnp.int32, (8, DK), 0) == 0
    for hv in range(V_HEADS_B):
        hq = hv // 2
        q = l2n(y[:, hq * DK:(hq + 1) * DK]) * QK_SCALE
        k = l2n(y[:, WIDTH_BQK + hq * DK:WIDTH_BQK + (hq + 1) * DK])
        v = y[:, 2 * WIDTH_BQK + hv * DV:2 * WIDTH_BQK + (hv + 1) * DV]
        beta = beta_all[:, hv:hv + 1]
        g = g_all[:, V_HEADS_B + hv:V_HEADS_B + hv + 1]
        eg = jnp.exp(g)
        state = s0_ref[0, hv]

        def pad8(x):
            return jnp.where(row8, jnp.broadcast_to(x, (8, x.shape[-1])), 0.0)

        v_new = v * beta - _bdot(pad8(k * (beta * eg)), state)[0:1, :]
        qk = jnp.sum(q.astype(BF16).astype(F32) * k.astype(BF16).astype(F32), axis=-1, keepdims=True)
        o = _bdot(pad8(q * eg), state)[0:1, :] + qk.astype(BF16).astype(F32) * v_new.astype(BF16).astype(F32)
        s_out_ref[0, hv] = state * eg + _bdot_tn(pad8(k), pad8(v_new))
        z = proj_ref[0, :, OFF_BZ + hv * DV:OFF_BZ + (hv + 1) * DV]
        o = o * lax.rsqrt(jnp.mean(o * o, axis=-1, keepdims=True) + EPS) * nw * _silu(z)
        o_ref[0, :, hv * DV:(hv + 1) * DV] = o.astype(o_ref.dtype)


def _dn_sample(proj, gates, conv_state, s0, conv_w, alog_vec, dtb_vec, norm_w):
    b = proj.shape[0]
    return pl.pallas_call(
        _dn_sample_body,
        grid=(b,),
        in_specs=[pl.BlockSpec((1, 1, PROJ_MAIN), lambda i: (i, 0, 0)),
                  pl.BlockSpec((1, 1, 128), lambda i: (i, 0, 0)),
                  pl.BlockSpec((CONV_W, CONV_DIM), lambda i: (0, 0)),
                  pl.BlockSpec((1, CONV_W - 1, CONV_DIM), lambda i: (i, 0, 0)),
                  pl.BlockSpec((1, V_HEADS_B, DK, DV), lambda i: (i, 0, 0, 0)),
                  pl.BlockSpec((1, 128), lambda i: (0, 0)),
                  pl.BlockSpec((1, 128), lambda i: (0, 0)),
                  pl.BlockSpec((1, DV), lambda i: (0, 0))],
        out_specs=[pl.BlockSpec((1, 1, WIDTH_BV), lambda i: (i, 0, 0)),
                   pl.BlockSpec((1, CONV_W - 1, CONV_DIM), lambda i: (i, 0, 0)),
                   pl.BlockSpec((1, V_HEADS_B, DK, DV), lambda i: (i, 0, 0, 0))],
        out_shape=[jax.ShapeDtypeStruct((b, 1, WIDTH_BV), BF16),
                   jax.ShapeDtypeStruct((b, CONV_W - 1, CONV_DIM), F32),
                   jax.ShapeDtypeStruct((b, V_HEADS_B, DK, DV), F32)],
        compiler_params=_params(("arbitrary",), 40),
        name="deltanet_sample",
    )(proj, gates, conv_w, conv_state, s0, alog_vec, dtb_vec, norm_w)


def _outproj_body(att_ref, dn_ref, wa_ref, wb_ref, x_ref, lnpost_ref, lnpre_ref, x1_ref, h2_ref):
    mix = (jnp.dot(att_ref[...], wa_ref[...], preferred_element_type=F32)
           + jnp.dot(dn_ref[...], wb_ref[...], preferred_element_type=F32))
    x1 = x_ref[...] + _rms(mix, lnpost_ref[...])
    x1_ref[...] = x1
    h2_ref[...] = _rms(x1, lnpre_ref[...]).astype(h2_ref.dtype)


def _outproj(att, dn, wa, wb, x, ln_post, ln_pre, tm):
    m, d = x.shape
    return pl.pallas_call(
        _outproj_body,
        grid=(m // tm,),
        in_specs=[pl.BlockSpec((tm, WIDTH_A), lambda i: (i, 0)),
                  pl.BlockSpec((tm, WIDTH_BV), lambda i: (i, 0)),
                  pl.BlockSpec((WIDTH_A, d), lambda i: (0, 0)),
                  pl.BlockSpec((WIDTH_BV, d), lambda i: (0, 0)),
                  pl.BlockSpec((tm, d), lambda i: (i, 0)),
                  pl.BlockSpec((1, d), lambda i: (0, 0)),
                  pl.BlockSpec((1, d), lambda i: (0, 0))],
        out_specs=[pl.BlockSpec((tm, d), lambda i: (i, 0)),
                   pl.BlockSpec((tm, d), lambda i: (i, 0))],
        out_shape=[jax.ShapeDtypeStruct((m, d), F32),
                   jax.ShapeDtypeStruct((m, d), BF16)],
        compiler_params=_params(("arbitrary",), 48),
        name="outproj",
    )(att, dn, wa, wb, x, ln_post, ln_pre)


def _ffn_body(h_ref, wg_ref, wv_ref, cwg_ref, cwv_ref, cbg_ref, cbv_ref, wo_ref, x1_ref, ln_ref,
              pg_ref, pv_ref, o_ref, ng_ref, nv_ref, eg_scr, ev_scr, carry_scr, act_scr,
              *, tm, tiles_per_seq, single_token):
    i = pl.program_id(0)
    j = pl.program_id(1)
    nj = pl.num_programs(1)
    d = o_ref.shape[-1]
    tf = act_scr.shape[-1]

    @pl.when(j == 0)
    def _():
        o_ref[...] = jnp.zeros_like(o_ref)

    if single_token:
        def up_conv(w_ref, cw_ref, cb_ref, prev_ref, new_ref):
            up = jnp.dot(h_ref[...], w_ref[...], preferred_element_type=F32)
            cw = cw_ref[...]
            new_ref[...] = up
            return cw[0:1, :] * prev_ref[0] + cw[1:2, :] * prev_ref[1] + cw[2:3, :] * up + cb_ref[...]

        gate = up_conv(wg_ref, cwg_ref, cbg_ref, pg_ref, ng_ref)
        val = up_conv(wv_ref, cwv_ref, cbv_ref, pv_ref, nv_ref)
        act_scr[...] = (_gelu_tanh(gate) * val).astype(BF16)
        o_ref[...] += jnp.dot(act_scr[...], wo_ref[...], preferred_element_type=F32)
    else:
        first_tile = i % tiles_per_seq == 0
        pieces = [slice(c, c + FFN_PIECE) for c in range(0, tf, FFN_PIECE)]

        def up_proj(cols):
            for w_ref, prev_ref, new_ref, e_scr, slot in ((wg_ref, pg_ref, ng_ref, eg_scr, 0),
                                                          (wv_ref, pv_ref, nv_ref, ev_scr, 1)):
                e_scr[8:8 + tm, cols] = jnp.dot(h_ref[...], w_ref[:, cols], preferred_element_type=F32)
                e_scr[6:8, cols] = jnp.where(first_tile, prev_ref[0, :, cols], carry_scr[slot, j, 6:8, cols])
                tail = e_scr[tm + 6:tm + 8, cols]
                carry_scr[slot, j, 6:8, cols] = tail
                new_ref[0, :, cols] = tail

        def conv(e_scr, cw_ref, cb_ref, cols, r):
            cw = cw_ref[:, cols]
            return (cw[0:1, :] * e_scr[6 + r:6 + r + FFN_ROWS, cols]
                    + cw[1:2, :] * e_scr[7 + r:7 + r + FFN_ROWS, cols]
                    + cw[2:3, :] * e_scr[8 + r:8 + r + FFN_ROWS, cols] + cb_ref[:, cols])

        def conv_geglu(cols):
            for r in range(0, tm, FFN_ROWS):
                act_scr[r:r + FFN_ROWS, cols] = (_gelu_tanh(conv(eg_scr, cwg_ref, cbg_ref, cols, r))
                                                 * conv(ev_scr, cwv_ref, cbv_ref, cols, r)).astype(BF16)

        def down_proj(cols):
            for n in range(0, d, FFN_COLS):
                o_ref[:, n:n + FFN_COLS] += jnp.dot(act_scr[:, cols], wo_ref[cols, n:n + FFN_COLS],
                                                    preferred_element_type=F32)

        up_proj(pieces[0])
        for c in range(len(pieces)):
            if c + 1 < len(pieces):
                up_proj(pieces[c + 1])
            conv_geglu(pieces[c])
            if c > 0:
                down_proj(pieces[c - 1])
        down_proj(pieces[-1])

    @pl.when(j == nj - 1)
    def _():
        o_ref[...] = x1_ref[...] + _rms(o_ref[...], ln_ref[...])


def _ffn(h2, w_in, conv_w, conv_b, w_out, x1, ln_post, prev, tm, tf, seq_len):
    m, d = h2.shape
    single = seq_len == 1
    nj = D_FF // tf
    tiles_per_seq = 1 if single else seq_len // tm
    if single:
        prev_g = pl.BlockSpec((2, tm, tf), lambda i, j: (0, i, j))
        prev_v = pl.BlockSpec((2, tm, tf), lambda i, j: (0, i, nj + j))
        new_g = pl.BlockSpec((tm, tf), lambda i, j: (i, j))
        new_shape = jax.ShapeDtypeStruct((m, D_FF), F32)
    else:
        prev_g = pl.BlockSpec((1, 2, tf), lambda i, j: (i // tiles_per_seq, 0, j))
        prev_v = pl.BlockSpec((1, 2, tf), lambda i, j: (i // tiles_per_seq, 0, nj + j))
        new_g = pl.BlockSpec((1, 2, tf), lambda i, j: (i, 0, j))
        new_shape = jax.ShapeDtypeStruct((m // tm, 2, D_FF), F32)
    body = functools.partial(_ffn_body, tm=tm, tiles_per_seq=tiles_per_seq, single_token=single)
    return pl.pallas_call(
        body,
        grid=(m // tm, nj),
        in_specs=[pl.BlockSpec((tm, d), lambda i, j: (i, 0)),
                  pl.BlockSpec((d, tf), lambda i, j: (0, j)),
                  pl.BlockSpec((d, tf), lambda i, j: (0, nj + j)),
                  pl.BlockSpec((FFN_CONV_W, tf), lambda i, j: (0, j)),
                  pl.BlockSpec((FFN_CONV_W, tf), lambda i, j: (0, nj + j)),
                  pl.BlockSpec((1, tf), lambda i, j: (0, j)),
                  pl.BlockSpec((1, tf), lambda i, j: (0, nj + j)),
                  pl.BlockSpec((tf, d), lambda i, j: (j, 0)),
                  pl.BlockSpec((tm, d), lambda i, j: (i, 0)),
                  pl.BlockSpec((1, d), lambda i, j: (0, 0)),
                  prev_g, prev_v],
        out_specs=[pl.BlockSpec((tm, d), lambda i, j: (i, 0)), new_g, new_g],
        out_shape=[jax.ShapeDtypeStruct((m, d), F32), new_shape, new_shape],
        scratch_shapes=[pltpu.VMEM((tm + 8, tf), F32),
                        pltpu.VMEM((tm + 8, tf), F32),
                        pltpu.VMEM((2, nj, 8, tf), F32),
                        pltpu.VMEM((tm, tf), BF16)],
        compiler_params=_params(("arbitrary", "arbitrary"), 56),
        name="convffn",
    )(h2, w_in, w_in, conv_w, conv_w, conv_b, conv_b, w_out, x1, ln_post, prev, prev)


def _cache_shift_body(ck_ref, cv_ref, ck_next_ref, cv_next_ref, nk_ref, nv_ref, ok_ref, ov_ref, *, tr):
    last = pl.program_id(1) == pl.num_programs(1) - 1
    for c_ref, nxt_ref, n_ref, o_ref in ((ck_ref, ck_next_ref, nk_ref, ok_ref),
                                         (cv_ref, cv_next_ref, nv_ref, ov_ref)):
        o_ref[0, 0:tr - 1] = c_ref[0, 1:tr]
        o_ref[0, tr - 1] = jnp.where(last, n_ref[0, 0], nxt_ref[0, 0])


def _cache_shift(cache_k, cache_v, new_k, new_v, tr):
    nb, rows, nh, dh = cache_k.shape
    main = pl.BlockSpec((1, tr, nh, dh), lambda b, i: (b, i, 0, 0))
    nxt = pl.BlockSpec((1, 1, nh, dh), lambda b, i: (b, jnp.minimum((i + 1) * tr, rows - 1), 0, 0))
    new = pl.BlockSpec((1, 1, nh, dh), lambda b, i: (b, 0, 0, 0))
    shape = jax.ShapeDtypeStruct(cache_k.shape, cache_k.dtype)
    return pl.pallas_call(
        functools.partial(_cache_shift_body, tr=tr),
        grid=(nb, rows // tr),
        in_specs=[main, main, nxt, nxt, new, new],
        out_specs=[main, main],
        out_shape=[shape, shape],
        compiler_params=_params(("arbitrary", "arbitrary"), 40),
        name="cache_shift",
    )(cache_k, cache_v, cache_k, cache_v, new_k, new_v)


def _lane_vec(values, offset):
    return jnp.zeros((1, 128), F32).at[0, offset:offset + V_HEADS_B].set(values.astype(F32))


def kernel(x_prompt, x_sample, cache_win_k, cache_win_v, state_dn_conv, state_dn_rec, state_ffn_conv,
           rel_bias, ln_mix_pre, w_in, dn_conv_w, dn_A_log, dn_dt_bias, dn_norm_w, w_out, ln_mix_post,
           ln_ffn_pre, w_ffn_in, ffn_conv_w, ffn_conv_b, w_ffn_out, ln_ffn_post):
    bp, sp, d = x_prompt.shape
    bs = x_sample.shape[0]
    l = 0

    w_main = w_in[l, :, :PROJ_MAIN].astype(BF16)
    w_gate = jnp.pad(w_in[l, :, PROJ_MAIN:], ((0, 0), (0, 128 - 2 * V_HEADS_B))).astype(BF16)
    wo_a = w_out[l, :WIDTH_A].astype(BF16)
    wo_b = w_out[l, WIDTH_A:].astype(BF16)
    wf_in = w_ffn_in[l].astype(BF16)
    wf_out = w_ffn_out[l].astype(BF16)
    ln1 = ln_mix_pre[l][None, :]
    ln2 = ln_mix_post[l][None, :]
    ln3 = ln_ffn_pre[l][None, :]
    ln4 = ln_ffn_post[l][None, :]
    conv_w = dn_conv_w[l]
    alog_vec = _lane_vec(dn_A_log[l], V_HEADS_B)
    dtb_vec = _lane_vec(dn_dt_bias[l], V_HEADS_B)
    norm_w = dn_norm_w[l][None, :]
    fcw = ffn_conv_w[l]
    fcb = ffn_conv_b[l][None, :]

    xp = x_prompt.reshape(bp * sp, d)
    hp = _rmsnorm(xp, ln1, 512)
    proj_p = _matmul(hp, w_main, 1024, 1024, "inproj_prompt")
    gates_p = _matmul(hp, w_gate, 1024, 128, "gates_prompt")
    proj3 = proj_p.reshape(bp, sp, PROJ_MAIN)
    att_p = _attn_prompt(proj3, rel_bias)
    dn_p, p_dn_rec = _dn_prompt(
        proj3, gates_p.reshape(bp, sp, 128),
        jnp.zeros((bp, CONV_W - 1, CONV_DIM), F32), jnp.zeros((bp, V_HEADS_B, DK, DV), F32),
        conv_w, alog_vec, dtb_vec, norm_w, 256)
    x1_p, h2_p = _outproj(att_p.reshape(bp * sp, WIDTH_A), dn_p.reshape(bp * sp, WIDTH_BV),
                          wo_a, wo_b, xp, ln2, ln3, 512)
    y_p, fc_g, fc_v = _ffn(h2_p, wf_in, fcw, fcb, wf_out, x1_p, ln4,
                           jnp.zeros((bp, FFN_CONV_W - 1, 2 * D_FF), F32), 512, 512, sp)
    keep = min(MAX_DISTANCE, sp)
    p_win_k = proj3[:, sp - keep:, OFF_AK:OFF_AK + WIDTH_A].reshape(1, bp, keep, HEADS_A, HEAD_DIM)
    p_win_v = proj3[:, sp - keep:, OFF_AV:OFF_AV + WIDTH_A].reshape(1, bp, keep, HEADS_A, HEAD_DIM)
    p_dn_conv = proj3[:, sp - (CONV_W - 1):, OFF_BQ:OFF_BQ + CONV_DIM][None]
    tiles = sp // 512
    p_ffn_conv = jnp.concatenate([fc_g[tiles - 1::tiles], fc_v[tiles - 1::tiles]], axis=-1)[None]

    xs = x_sample.reshape(bs, d)
    hs = _rmsnorm(xs, ln1, bs)
    proj_s = _matmul(hs, w_main, bs, 1024, "inproj_sample")
    gates_s = _matmul(hs, w_gate, bs, 128, "gates_sample")
    past = cache_win_k.shape[2]
    ck = cache_win_k[l]
    cv = cache_win_v[l]
    new_k = proj_s[:, OFF_AK:OFF_AK + WIDTH_A]
    new_v = proj_s[:, OFF_AV:OFF_AV + WIDTH_A]
    new_q = proj_s[:, OFF_AQ:OFF_AQ + WIDTH_A].reshape(bs, HEADS_A, HEAD_DIM)
    new_k = new_k.reshape(bs, HEADS_A, HEAD_DIM)
    new_v = new_v.reshape(bs, HEADS_A, HEAD_DIM)
    att_s = _attn_sample(new_q, new_k, new_v, ck, cv, rel_bias)
    s_win_k, s_win_v = _cache_shift(ck, cv, new_k[:, None], new_v[:, None], 512)
    dn_s, s_dn_conv, s_dn_rec = _dn_sample(proj_s[:, None], gates_s[:, None], state_dn_conv[l],
                                           state_dn_rec[l], conv_w, alog_vec, dtb_vec, norm_w)
    x1_s, h2_s = _outproj(att_s.reshape(bs, WIDTH_A), dn_s.reshape(bs, WIDTH_BV),
                          wo_a, wo_b, xs, ln2, ln3, bs)
    prev_s = jnp.swapaxes(state_ffn_conv[l], 0, 1)
    y_s, up_g, up_v = _ffn(h2_s, wf_in, fcw, fcb, wf_out, x1_s, ln4, prev_s, bs, 512, 1)
    s_ffn_conv = jnp.stack([prev_s[1], jnp.concatenate([up_g, up_v], axis=-1)], axis=1)[None]

    return (y_p.reshape(bp, sp, d), y_s.reshape(bs, 1, d),
            p_win_k, p_win_v, p_dn_conv, p_dn_rec[None], p_ffn_conv,
            s_win_k[None], s_win_v[None], s_dn_conv[None], s_dn_rec[None], s_ffn_conv)
```

```python
import functools
import math

import numpy as np
import jax
import jax.numpy as jnp
from jax import lax
from jax.experimental import pallas as pl
from jax.experimental.pallas import tpu as pltpu

F32 = jnp.float32
BF16 = jnp.bfloat16

D_MODEL = 2048
HEAD_DIM = 128
WIDTH_A = 1024
HEADS_A = 8
DILATIONS = (1, 4, 16)
BLK = 128
N_BUCKETS = 32
MAX_DISTANCE = 2048
DK = 128
DV = 128
V_HEADS_B = 8
QK_HEADS_B = 4
WIDTH_BQK = 512
WIDTH_BV = 1024
CONV_W = 4
CONV_DIM = 2048
CHUNK = 64
D_FF = 5632
FFN_CONV_W = 3
EPS = 1e-6
NEG = -1e30
ATT_SCALE = HEAD_DIM ** -0.5
QK_SCALE = DK ** -0.5

OFF_AQ, OFF_AK, OFF_AV = 0, 1024, 2048
OFF_BQ, OFF_BK, OFF_BV, OFF_BZ = 3072, 3584, 4096, 5120
OFF_GATES = 6144
PROJ_MAIN = 6144

MIB = 2 ** 20


def _params(semantics, vmem_mib):
    return pltpu.CompilerParams(dimension_semantics=semantics, vmem_limit_bytes=vmem_mib * MIB)


def _bdot(a, b):
    return jnp.dot(a.astype(BF16), b.astype(BF16), preferred_element_type=F32)


def _bdot_nt(a, b):
    return lax.dot_general(a.astype(BF16), b.astype(BF16), (((1,), (1,)), ((), ())),
                           preferred_element_type=F32)


def _bdot_tn(a, b):
    return lax.dot_general(a.astype(BF16), b.astype(BF16), (((0,), (0,)), ((), ())),
                           preferred_element_type=F32)


def _fdot(a, b):
    return jnp.dot(a, b, preferred_element_type=F32, precision=lax.Precision.HIGHEST)


def _silu(x):
    return x * (1.0 / (1.0 + jnp.exp(-x)))


def _sigmoid(x):
    return 1.0 / (1.0 + jnp.exp(-x))


def _softplus(x):
    return jnp.maximum(x, 0.0) + jnp.log(1.0 + jnp.exp(-jnp.abs(x)))


def _gelu_tanh(x):
    return 0.5 * x * (1.0 + jnp.tanh(math.sqrt(2.0 / math.pi) * (x + 0.044715 * (x * x * x))))


def _rms(x, w):
    return x * lax.rsqrt(jnp.mean(x * x, axis=-1, keepdims=True) + EPS) * w


def _rmsnorm_body(x_ref, w_ref, o_ref):
    o_ref[...] = _rms(x_ref[...], w_ref[...]).astype(o_ref.dtype)


def _rmsnorm(x, w, tm):
    m, d = x.shape
    return pl.pallas_call(
        _rmsnorm_body,
        grid=(m // tm,),
        in_specs=[pl.BlockSpec((tm, d), lambda i: (i, 0)),
                  pl.BlockSpec((1, d), lambda i: (0, 0))],
        out_specs=pl.BlockSpec((tm, d), lambda i: (i, 0)),
        out_shape=jax.ShapeDtypeStruct((m, d), BF16),
        compiler_params=_params(("arbitrary",), 40),
        name="rmsnorm",
    )(x, w)


def _matmul_body(x_ref, w_ref, o_ref):
    o_ref[...] = jnp.dot(x_ref[...], w_ref[...], preferred_element_type=F32)


def _matmul(x, w, tm, tn, name):
    m, k = x.shape
    n = w.shape[1]
    return pl.pallas_call(
        _matmul_body,
        grid=(n // tn, m // tm),
        in_specs=[pl.BlockSpec((tm, k), lambda j, i: (i, 0)),
                  pl.BlockSpec((k, tn), lambda j, i: (0, j))],
        out_specs=pl.BlockSpec((tm, tn), lambda j, i: (i, j)),
        out_shape=jax.ShapeDtypeStruct((m, n), F32),
        compiler_params=_params(("arbitrary", "arbitrary"), 48),
        name=name,
    )(x, w)


def _matmul_f32w_body(x_ref, w_ref, o_ref, wb_scr):
    @pl.when(pl.program_id(1) == 0)
    def _():
        wb_scr[...] = w_ref[...].astype(BF16)

    o_ref[...] = jnp.dot(x_ref[...], wb_scr[...], preferred_element_type=F32)


def _matmul_f32w(x, w, n, tm, tn, name):
    m, k = x.shape
    return pl.pallas_call(
        _matmul_f32w_body,
        grid=(n // tn, m // tm),
        in_specs=[pl.BlockSpec((tm, k), lambda j, i: (i, 0)),
                  pl.BlockSpec((k, tn), lambda j, i: (0, j))],
        out_specs=pl.BlockSpec((tm, tn), lambda j, i: (i, j)),
        out_shape=jax.ShapeDtypeStruct((m, n), F32),
        scratch_shapes=[pltpu.VMEM((k, tn), BF16)],
        compiler_params=_params(("arbitrary", "arbitrary"), 52),
        name=name,
    )(x, w)


def _rel_bucket_np(dist):
    dist = np.asarray(dist, np.int64)
    max_exact = N_BUCKETS // 2
    d = np.maximum(dist, 1).astype(np.float64)
    val = np.log(d / max_exact) / math.log(MAX_DISTANCE / max_exact) * (N_BUCKETS - max_exact)
    frac = np.abs(val - np.round(val))
    near = (frac < 2e-5) &(dist >= max_exact) & (dist != max_exact) & (dist < MAX_DISTANCE)
    assert not near.any(), "distance on a bucket boundary"
    val = np.where(dist == max_exact, 0.0, val)
    large = np.minimum(max_exact + np.trunc(val).astype(np.int64), N_BUCKETS - 1)
    return np.where(dist < max_exact, dist, large).astype(np.int32)


def _prompt_bucket_tables():
    qi = np.arange(BLK)[:, None]
    kj = np.arange(2 * BLK)[None, :]
    delta = BLK + qi - kj
    inwin = (delta >= 0) & (delta <= BLK)
    tabs = []
    for dil in DILATIONS:
        b = _rel_bucket_np(np.clip(delta, 0, BLK) * dil)
        tabs.append(np.where(inwin, b, -1))
    return np.stack(tabs).astype(np.int32)


def _sample_bucket_tables():
    j = BLK - np.arange(BLK)
    return np.stack([_rel_bucket_np(j * dil)[None, :] for dil in DILATIONS]).astype(np.int32)


def _attn_prompt_body(bucket_ref, relb_ref, q_ref, k_ref, v_ref, o_ref,
                      bias_scr, acc_scr, m_scr, l_scr):
    h = pl.program_id(1)
    col = lax.broadcasted_iota(jnp.int32, (BLK, 2 * BLK), 1)
    tables = _prompt_bucket_tables()
    for br in range(3):
        bk = bucket_ref[br]
        bias = jnp.zeros((BLK, 2 * BLK), F32)
        for kb in sorted(set(tables[br].ravel().tolist()) - {-1}):
            bias = jnp.where(bk == kb, relb_ref[kb, h], bias)
        full = jnp.where(bk >= 0, bias, NEG)
        bias_scr[2 * br] = full
        bias_scr[2 * br + 1] = jnp.where(col >= BLK, full, NEG)

    def run_branch(br, dil, is_first_branch, is_last_branch):
        shift = int(math.log2(dil))
        span = BLK * dil
        stride = None if dil == 1 else dil

        def rows(start):
            return pl.ds(start, BLK, stride=stride) if stride else pl.ds(start, BLK)

        nb = q_ref.shape[1] // span
        run_len = min(nb, ATTN_UNROLL)
        runs_per_it = ATTN_UNROLL // run_len
        runs_per_res = nb // run_len
        starts_at_zero = runs_per_res == 1

        def tasks(it, carry):
            q_starts, firsts, qs_, ks_, vs_ = [], [], [], [], []
            for rr in range(runs_per_it):
                ri = it * runs_per_it + rr
                n0 = (ri % runs_per_res) * run_len
                if dil == 1:
                    base = pl.multiple_of(n0 * span, BLK)
                else:
                    base = n0 * span + ri // runs_per_res
                first = jnp.where(n0 == 0, 1, 0)
                starts = [base + u * span for u in range(run_len)]
                kb = [k_ref[0, rows(st), :].astype(BF16) for st in starts]
                vb = [v_ref[0, rows(st), :].astype(BF16) for st in starts]
                if starts_at_zero:
                    k_prev, v_prev = None, None
                else:
                    p_start = base - span * (1 - first)
                    if dil == 1:
                        p_start = pl.multiple_of(p_start, BLK)
                    k_prev = k_ref[0, rows(p_start), :].astype(BF16)
                    v_prev = v_ref[0, rows(p_start), :].astype(BF16)
                for u, st in enumerate(starts):
                    q_starts.append(st)
                    qs_.append(q_ref[0, rows(st), :].astype(BF16))
                    kp, vp = (k_prev, v_prev) if u == 0 else (kb[u - 1], vb[u - 1])
                    if kp is None:
                        firsts.append(None)
                        ks_.append(kb[u])
                        vs_.append(vb[u])
                    else:
                        firsts.append(first if u == 0 else 0)
                        ks_.append(jnp.concatenate([kp, kb[u]], axis=0))
                        vs_.append(jnp.concatenate([vp, vb[u]], axis=0))
            if not is_first_branch:
                runs = [(m_scr[rows(qs), :], l_scr[rows(qs), :], acc_scr[rows(qs), :]) for qs in q_starts]
            ss = [_bdot_nt(q, k) * ATT_SCALE
                  + (bias_scr[2 * br, :, BLK:] if f is None else bias_scr[2 * br + f])
                  for q, k, f in zip(qs_, ks_, firsts)]
            ms = [jnp.max(s, axis=-1, keepdims=True) for s in ss]
            ps_ = [jnp.exp(s - m) for s, m in zip(ss, ms)]
            ls = [jnp.sum(p, axis=-1, keepdims=True) for p in ps_]
            accs = [_bdot(p, v) for p, v in zip(ps_, vs_)]
            outs = []
            for u in range(ATTN_UNROLL):
                m_b = jnp.broadcast_to(ms[u], (BLK, HEAD_DIM))
                l_b = jnp.broadcast_to(ls[u], (BLK, HEAD_DIM))
                acc_t = accs[u]
                if not is_first_branch:
                    m_run, l_run, acc_run = runs[u]
                    m_new = jnp.maximum(m_run, m_b)
                    a = jnp.exp(m_run - m_new)
                    b = jnp.exp(m_b - m_new)
                    acc_t = a * acc_run + b * acc_t
                    l_b = a * l_run + b * l_b
                    m_b = m_new
                outs.append((m_b, l_b, acc_t))
            for qs, (m_b, l_b, acc_t) in zip(q_starts, outs):
                if is_last_branch:
                    o_ref[0, rows(qs), :] = (acc_t / l_b).astype(o_ref.dtype)
                else:
                    m_scr[rows(qs), :] = m_b
                    l_scr[rows(qs), :] = l_b
                    acc_scr[rows(qs), :] = acc_t
            return carry

        lax.fori_loop(0, nb * dil // ATTN_UNROLL, tasks, 0)

    run_branch(2, 16, True, False)
    run_branch(1, 4, False, False)
    run_branch(0, 1, False, True)


def _attn_prompt(proj3, rel_bias):
    b, s, _ = proj3.shape
    buckets = jnp.asarray(_prompt_bucket_tables())
    blk = (1, s, HEAD_DIM)
    return pl.pallas_call(
        _attn_prompt_body,
        grid=(b, HEADS_A),
        in_specs=[pl.BlockSpec((3, BLK, 2 * BLK), lambda i, h: (0, 0, 0)),
                  pl.BlockSpec(memory_space=pltpu.SMEM),
                  pl.BlockSpec(blk, lambda i, h: (i, 0, OFF_AQ // HEAD_DIM + h)),
                  pl.BlockSpec(blk, lambda i, h: (i, 0, OFF_AK // HEAD_DIM + h)),
                  pl.BlockSpec(blk, lambda i, h: (i, 0, OFF_AV // HEAD_DIM + h))],
        out_specs=pl.BlockSpec(blk, lambda i, h: (i, 0, h)),
        out_shape=jax.ShapeDtypeStruct((b, s, WIDTH_A), BF16),
        scratch_shapes=[pltpu.VMEM((6, BLK, 2 * BLK), F32),
                        pltpu.VMEM((s, HEAD_DIM), F32),
                        pltpu.VMEM((s, HEAD_DIM), F32),
                        pltpu.VMEM((s, HEAD_DIM), F32)],
        compiler_params=_params(("arbitrary", "arbitrary"), 40),
        name="attn_prompt",
    )(buckets, rel_bias, proj3, proj3, proj3)


def _attn_sample_body(bucket_ref, relbt_ref, q_ref, kn_ref, vn_ref,
                      k1_ref, k4_ref, k16_ref, v1_ref, v4_ref, v16_ref, o_ref, bias_scr):
    relbt = relbt_ref[...]
    tile = (HEADS_A, HEAD_DIM)

    @pl.when(pl.program_id(0) == 0)
    def _():
        for br in range(3):
            bk = bucket_ref[br]
            bias = jnp.zeros((BLK,) + tile, F32)
            for kb in range(N_BUCKETS):
                col = jnp.broadcast_to(relbt[:, kb:kb + 1], tile)
                bias = jnp.where(bk == kb, col[None], bias)
            bias_scr[br] = bias

    def lane_sum(x):
        return jnp.broadcast_to(jnp.sum(x, axis=-1, keepdims=True), x.shape)

    q = q_ref[0]
    s_self = lane_sum(q * kn_ref[0]) * ATT_SCALE + jnp.broadcast_to(relbt[:, 0:1], tile)
    scores = []
    m = s_self
    for br, k_ref in enumerate((k1_ref, k4_ref, k16_ref)):
        s = lane_sum(k_ref[...] * q[None]) * ATT_SCALE + bias_scr[br]
        scores.append(s)
        m = jnp.maximum(m, jnp.max(s, axis=0))
    p_self = 3.0 * jnp.exp(s_self - m)
    l = p_self
    acc = p_self * vn_ref[0]
    for s, v_ref in zip(scores, (v1_ref, v4_ref, v16_ref)):
        p = jnp.exp(s - m[None])
        l = l + jnp.sum(p, axis=0)
        acc = acc + jnp.sum(p * v_ref[...], axis=0)
    o_ref[0] = (acc / l).astype(o_ref.dtype)


def _attn_sample(q, k_new, v_new, cache_k, cache_v, rel_bias):
    b, past = cache_k.shape[:2]
    tile = (HEADS_A, HEAD_DIM)
    buckets = jnp.asarray(np.broadcast_to(_sample_bucket_tables().reshape(3, BLK, 1, 1), (3, BLK) + tile))
    row = pl.BlockSpec((1,) + tile, lambda i: (i, 0, 0))
    views, specs = [], []
    for cache in (cache_k, cache_v):
        for dil in DILATIONS:
            views.append(cache.reshape((b, past // dil, dil) + tile))
            last = past // dil // BLK - 1
            specs.append(pl.BlockSpec((None, BLK, None) + tile,
                                      functools.partial(lambda last, i: (i, last, 0, 0, 0), last)))
    return pl.pallas_call(
        _attn_sample_body,
        grid=(b,),
        in_specs=[pl.BlockSpec((3, BLK) + tile, lambda i: (0, 0, 0, 0)),
                  pl.BlockSpec((HEADS_A, N_BUCKETS), lambda i: (0, 0)),
                  row, row, row] + specs,
        out_specs=row,
        out_shape=jax.ShapeDtypeStruct((b,) + tile, BF16),
        scratch_shapes=[pltpu.VMEM((3, BLK) + tile, F32)],
        compiler_params=_params(("arbitrary",), 40),
        name="attn_sample",
    )(buckets, rel_bias.T, q, k_new, v_new, *views)


GROUP = 1
ATTN_UNROLL = 4
FFN_TM = 1024
FFN_ROWS = 64
FFN_COLS = 512
FFN_PIECE = 256


def _dn_prompt_body(q_ref, k_ref, v_ref, z_ref, h_ref, wgate_ref, cw_ref, cs_ref, s0_ref, alog_ref, dtb_ref, nw_ref,
                    o_ref, s_out_ref,
                    s_scr, e_scr, qn_scr, kn_scr, vv_scr, g_scr, beta_scr,
                    w_scr, u_scr, qg_scr, kdt_scr, attn_scr, gl_scr, o_scr, *, tt):
    t = pl.program_id(1)
    nt = pl.num_programs(1)

    @pl.when(t == 0)
    def _():
        s_scr[...] = s0_ref[0]
        e_scr[5:8, :] = cs_ref[0]

    e_scr[8:8 + tt, 0:WIDTH_BQK] = q_ref[0]
    e_scr[8:8 + tt, WIDTH_BQK:2 * WIDTH_BQK] = k_ref[0]
    e_scr[8:8 + tt, 2 * WIDTH_BQK:CONV_DIM] = v_ref[0]

    def l2n(x):
        return x * lax.rsqrt(jnp.sum(x * x, axis=-1, keepdims=True) + EPS)

    for c0 in range(0, CONV_DIM, DK):
        cols = slice(c0, c0 + DK)
        w = cw_ref[:, cols]
        y = w[0:1, :] * e_scr[5:5 + tt, cols]
        for i in range(1, CONV_W):
            y = y + w[i:i + 1, :] * e_scr[5 + i:5 + i + tt, cols]
        y = _silu(y)
        if c0 < WIDTH_BQK:
            qn_scr[:, cols] = l2n(y) * QK_SCALE
        elif c0 < 2 * WIDTH_BQK:
            kn_scr[:, c0 - WIDTH_BQK:c0 - WIDTH_BQK + DK] = l2n(y)
        else:
            vv_scr[:, c0 - 2 * WIDTH_BQK:c0 - 2 * WIDTH_BQK + DK] = y
    e_scr[5:8, :] = e_scr[tt + 5:tt + 8, :]
    gates = jnp.dot(h_ref[0], wgate_ref[...], preferred_element_type=F32)
    beta_scr[...] = _sigmoid(gates)
    g_scr[...] = -jnp.exp(alog_ref[...]) * _softplus(gates + dtb_ref[...])

    ri = lax.broadcasted_iota(jnp.int32, (CHUNK, CHUNK), 0)
    ci = lax.broadcasted_iota(jnp.int32, (CHUNK, CHUNK), 1)
    tri = ri >= ci
    strict = ri > ci
    tril_ones = tri.astype(F32)
    nw = nw_ref[...]

    for c0 in range(0, tt // CHUNK, GROUP):
        units = []
        for c in range(c0, c0 + GROUP):
            rows = slice(c * CHUNK, (c + 1) * CHUNK)
            beta_all = beta_scr[rows, :]
            gc_all = _fdot(tril_ones, g_scr[rows, :])
            gc_all_t = gc_all.T
            for hq in range(QK_HEADS_B):
                qn = qn_scr[rows, hq * DK:(hq + 1) * DK]
                kn = kn_scr[rows, hq * DK:(hq + 1) * DK]
                kk = _bdot_nt(kn, kn)
                qk = _bdot_nt(qn, kn)
                for hv in range(2 * hq, 2 * hq + 2):
                    beta = beta_all[:, hv:hv + 1]
                    gc = gc_all[:, V_HEADS_B + hv:V_HEADS_B + hv + 1]
                    gc_row = gc_all_t[V_HEADS_B + hv:V_HEADS_B + hv + 1, :]
                    gc_last = gc_row[:, CHUNK - 1:CHUNK]
                    decay = jnp.exp(jnp.where(tri, gc - gc_row, NEG))
                    a = jnp.where(strict, beta * kk * decay, 0.0)
                    egc = jnp.exp(gc)
                    attn_scr[hv, rows, :] = (qk * decay).astype(BF16)
                    qg_scr[hv, rows, :] = (qn * egc).astype(BF16)
                    kd = kn * jnp.exp(gc_last - gc)
                    kdt_scr[hv, c * DK:(c + 1) * DK, :] = kd.T.astype(BF16)
                    gl_scr[hv, c * 8:(c + 1) * 8, :] = jnp.broadcast_to(jnp.exp(gc_last), (8, DV))
                    x = jnp.concatenate([kn * (beta * egc), vv_scr[rows, hv * DV:(hv + 1) * DV] * beta],
                                        axis=-1)
                    units.append((hv, rows, a, x))
        ns = [-a for _, _, a, _ in units]
        pws = [a for _, _, a, _ in units]
        for _ in range(5):
            pws = [_bdot(pw, pw) for pw in pws]
            ns = [n + pw + _bdot(n, pw) for n, pw in zip(ns, pws)]
        wus = [x + _bdot(n, x) for n, (_, _, _, x) in zip(ns, units)]
        for wu, (hv, rows, _, _) in zip(wus, units):
            w_scr[hv, rows, :] = wu[:, :DK].astype(BF16)
            u_scr[hv, rows, :] = wu[:, DK:]

    heads = range(V_HEADS_B)
    for c in range(tt // CHUNK):
        rows = slice(c * CHUNK, (c + 1) * CHUNK)
        states = [s_scr[hv] for hv in heads]
        states_b = [s.astype(BF16) for s in states]
        v_news = [u_scr[hv, rows, :] - jnp.dot(w_scr[hv, rows, :], states_b[hv], preferred_element_type=F32)
                  for hv in heads]
        v_news_b = [v.astype(BF16) for v in v_news]
        for hv in heads:
            s_scr[hv] = (states[hv] * gl_scr[hv, c * 8:c * 8 + 1, :]
                         + jnp.dot(kdt_scr[hv, c * DK:(c + 1) * DK, :], v_news_b[hv],
                                   preferred_element_type=F32))
        for hv in heads:
            o_scr[rows, hv * DV:(hv + 1) * DV] = (
                jnp.dot(qg_scr[hv, rows, :], states_b[hv], preferred_element_type=F32)
                + jnp.dot(attn_scr[hv, rows, :], v_news_b[hv], preferred_element_type=F32))

    for hv in heads:
        o = o_scr[:, hv * DV:(hv + 1) * DV]
        z = z_ref[0, :, hv * DV:(hv + 1) * DV]
        o = o * lax.rsqrt(jnp.mean(o * o, axis=-1, keepdims=True) + EPS) * nw * _silu(z)
        o_ref[0, :, hv * DV:(hv + 1) * DV] = o.astype(o_ref.dtype)

    @pl.when(t == nt - 1)
    def _():
        s_out_ref[0] = s_scr[...]


def _dn_prompt(proj3, h3, w_gate, conv_state, s0, conv_w, alog_vec, dtb_vec, norm_w, tt):
    b, s, _ = proj3.shape
    body = functools.partial(_dn_prompt_body, tt=tt)
    nh = V_HEADS_B
    in_specs = [
        pl.BlockSpec((1, tt, WIDTH_BQK), lambda i, t: (i, t, OFF_BQ // WIDTH_BQK)),
        pl.BlockSpec((1, tt, WIDTH_BQK), lambda i, t: (i, t, OFF_BK // WIDTH_BQK)),
        pl.BlockSpec((1, tt, WIDTH_BV), lambda i, t: (i, t, OFF_BV // WIDTH_BV)),
        pl.BlockSpec((1, tt, WIDTH_BV), lambda i, t: (i, t, OFF_BZ // WIDTH_BV)),
        pl.BlockSpec((1, tt, D_MODEL), lambda i, t: (i, t, 0)),
        pl.BlockSpec((D_MODEL, 128), lambda i, t: (0, 0)),
        pl.BlockSpec((CONV_W, CONV_DIM), lambda i, t: (0, 0)),
        pl.BlockSpec((1, CONV_W - 1, CONV_DIM), lambda i, t: (i, 0, 0)),
        pl.BlockSpec((1, nh, DK, DV), lambda i, t: (i, 0, 0, 0)),
        pl.BlockSpec((1, 128), lambda i, t: (0, 0)),
        pl.BlockSpec((1, 128), lambda i, t: (0, 0)),
        pl.BlockSpec((1, DV), lambda i, t: (0, 0)),
    ]
    return pl.pallas_call(
        body,
        grid=(b, s // tt),
        in_specs=in_specs,
        out_specs=[pl.BlockSpec((1, tt, WIDTH_BV), lambda i, t: (i, t, 0)),
                   pl.BlockSpec((1, nh, DK, DV), lambda i, t: (i, 0, 0, 0))],
        out_shape=[jax.ShapeDtypeStruct((b, s, WIDTH_BV), BF16),
                   jax.ShapeDtypeStruct((b, nh, DK, DV), F32)],
        scratch_shapes=[pltpu.VMEM((nh, DK, DV), F32),
                        pltpu.VMEM((tt + 8, CONV_DIM), F32),
                        pltpu.VMEM((tt, WIDTH_BQK), F32),
                        pltpu.VMEM((tt, WIDTH_BQK), F32),
                        pltpu.VMEM((tt, WIDTH_BV), F32),
                        pltpu.VMEM((tt, 128), F32),
                        pltpu.VMEM((tt, 128), F32),
                        pltpu.VMEM((nh, tt, DK), BF16),
                        pltpu.VMEM((nh, tt, DV), F32),
                        pltpu.VMEM((nh, tt, DK), BF16),
                        pltpu.VMEM((nh, tt // CHUNK * DK, CHUNK), BF16),
                        pltpu.VMEM((nh, tt, CHUNK), BF16),
                        pltpu.VMEM((nh, tt // CHUNK * 8, DV), F32),
                        pltpu.VMEM((tt, WIDTH_BV), F32)],
        compiler_params=_params(("arbitrary", "arbitrary"), 48),
        name="deltanet_prompt",
    )(proj3, proj3, proj3, proj3, h3, w_gate, conv_w, conv_state, s0, alog_vec, dtb_vec, norm_w)


def _dn_sample_body(proj_ref, gates_ref, cw_ref, cs_ref, s0_ref, alog_ref, dtb_ref, nw_ref,
                    o_ref, cs_out_ref, s_out_ref):
    pre = proj_ref[0, :, OFF_BQ:OFF_BQ + CONV_DIM]
    buf = cs_ref[0]
    w = cw_ref[...]
    y = w[CONV_W - 1:CONV_W, :] * pre
    for i in range(CONV_W - 1):
        y = y + w[i:i + 1, :] * buf[i:i + 1, :]
    y = _silu(y)
    cs_out_ref[0, 0:CONV_W - 2, :] = buf[1:CONV_W - 1, :]
    cs_out_ref[0, CONV_W - 2:CONV_W - 1, :] = pre

    gates = gates_ref[0]
    beta_all = _sigmoid(gates)
    g_all = -jnp.exp(alog_ref[...]) * _softplus(gates + dtb_ref[...])
    nw = nw_ref[...]

    def l2n(x):
        return x * lax.rsqrt(jnp.sum(x * x, axis=-1, keepdims=True) + EPS)

    row8 = lax.broadcasted_iota(jnp.int32, (8, DK), 0) == 0
    for hv in range(V_HEADS_B):
        hq = hv // 2
        q = l2n(y[:, hq * DK:(hq + 1) * DK]) * QK_SCALE
        k = l2n(y[:, WIDTH_BQK + hq * DK:WIDTH_BQK + (hq + 1) * DK])
        v = y[:, 2 * WIDTH_BQK + hv * DV:2 * WIDTH_BQK + (hv + 1) * DV]
        beta = beta_all[:, hv:hv + 1]
        g = g_all[:, V_HEADS_B + hv:V_HEADS_B + hv + 1]
        eg = jnp.exp(g)
        state = s0_ref[0, hv]

        def pad8(x):
            return jnp.where(row8, jnp.broadcast_to(x, (8, x.shape[-1])), 0.0)

        v_new = v * beta - _bdot(pad8(k * (beta * eg)), state)[0:1, :]
        qk = jnp.sum(q.astype(BF16).astype(F32) * k.astype(BF16).astype(F32), axis=-1, keepdims=True)
        o = _bdot(pad8(q * eg), state)[0:1, :] + qk.astype(BF16).astype(F32) * v_new.astype(BF16).astype(F32)
        s_out_ref[0, hv] = state * eg + _bdot_tn(pad8(k), pad8(v_new))
        z = proj_ref[0, :, OFF_BZ + hv * DV:OFF_BZ + (hv + 1) * DV]
        o = o * lax.rsqrt(jnp.mean(o * o, axis=-1, keepdims=True) + EPS) * nw * _silu(z)
        o_ref[0, :, hv * DV:(hv + 1) * DV] = o.astype(o_ref.dtype)


def _dn_sample(proj, gates, conv_state, s0, conv_w, alog_vec, dtb_vec, norm_w):
    b = proj.shape[0]
    return pl.pallas_call(
        _dn_sample_body,
        grid=(b,),
        in_specs=[pl.BlockSpec((1, 1, PROJ_MAIN), lambda i: (i, 0, 0)),
                  pl.BlockSpec((1, 1, 128), lambda i: (i, 0, 0)),
                  pl.BlockSpec((CONV_W, CONV_DIM), lambda i: (0, 0)),
                  pl.BlockSpec((1, CONV_W - 1, CONV_DIM), lambda i: (i, 0, 0)),
                  pl.BlockSpec((1, V_HEADS_B, DK, DV), lambda i: (i, 0, 0, 0)),
                  pl.BlockSpec((1, 128), lambda i: (0, 0)),
                  pl.BlockSpec((1, 128), lambda i: (0, 0)),
                  pl.BlockSpec((1, DV), lambda i: (0, 0))],
        out_specs=[pl.BlockSpec((1, 1, WIDTH_BV), lambda i: (i, 0, 0)),
                   pl.BlockSpec((1, CONV_W - 1, CONV_DIM), lambda i: (i, 0, 0)),
                   pl.BlockSpec((1, V_HEADS_B, DK, DV), lambda i: (i, 0, 0, 0))],
        out_shape=[jax.ShapeDtypeStruct((b, 1, WIDTH_BV), BF16),
                   jax.ShapeDtypeStruct((b, CONV_W - 1, CONV_DIM), F32),
                   jax.ShapeDtypeStruct((b, V_HEADS_B, DK, DV), F32)],
        compiler_params=_params(("arbitrary",), 40),
        name="deltanet_sample",
    )(proj, gates, conv_w, conv_state, s0, alog_vec, dtb_vec, norm_w)


def _outproj_body(att_ref, dn_ref, wa_ref, wb_ref, x_ref, lnpost_ref, lnpre_ref, x1_ref, h2_ref):
    mix = (jnp.dot(att_ref[...], wa_ref[...], preferred_element_type=F32)
           + jnp.dot(dn_ref[...], wb_ref[...], preferred_element_type=F32))
    x1 = x_ref[...] + _rms(mix, lnpost_ref[...])
    x1_ref[...] = x1
    h2_ref[...] = _rms(x1, lnpre_ref[...]).astype(h2_ref.dtype)


def _outproj(att, dn, wa, wb, x, ln_post, ln_pre, tm):
    m, d = x.shape
    return pl.pallas_call(
        _outproj_body,
        grid=(m // tm,),
        in_specs=[pl.BlockSpec((tm, WIDTH_A), lambda i: (i, 0)),
                  pl.BlockSpec((tm, WIDTH_BV), lambda i: (i, 0)),
                  pl.BlockSpec((WIDTH_A, d), lambda i: (0, 0)),
                  pl.BlockSpec((WIDTH_BV, d), lambda i: (0, 0)),
                  pl.BlockSpec((tm, d), lambda i: (i, 0)),
                  pl.BlockSpec((1, d), lambda i: (0, 0)),
                  pl.BlockSpec((1, d), lambda i: (0, 0))],
        out_specs=[pl.BlockSpec((tm, d), lambda i: (i, 0)),
                   pl.BlockSpec((tm, d), lambda i: (i, 0))],
        out_shape=[jax.ShapeDtypeStruct((m, d), F32),
                   jax.ShapeDtypeStruct((m, d), BF16)],
        compiler_params=_params(("arbitrary",), 48),
        name="outproj",
    )(att, dn, wa, wb, x, ln_post, ln_pre)


def _ffn_body(h_ref, wg_ref, wv_ref, cwg_ref, cwv_ref, cbg_ref, cbv_ref, wo_ref, x1_ref, ln_ref,
              pg_ref, pv_ref, o_ref, ng_ref, nv_ref, eg_scr, ev_scr, carry_scr, act_scr,
              *, tm, tiles_per_seq, single_token):
    i = pl.program_id(0)
    j = pl.program_id(1)
    nj = pl.num_programs(1)
    d = o_ref.shape[-1]
    tf = act_scr.shape[-1]

    @pl.when(j == 0)
    def _():
        o_ref[...] = jnp.zeros_like(o_ref)

    if single_token:
        def up_conv(w_ref, cw_ref, cb_ref, prev_ref, new_ref):
            up = jnp.dot(h_ref[...], w_ref[...], preferred_element_type=F32)
            cw = cw_ref[...]
            new_ref[...] = up
            return cw[0:1, :] * prev_ref[0] + cw[1:2, :] * prev_ref[1] + cw[2:3, :] * up + cb_ref[...]

        gate = up_conv(wg_ref, cwg_ref, cbg_ref, pg_ref, ng_ref)
        val = up_conv(wv_ref, cwv_ref, cbv_ref, pv_ref, nv_ref)
        act_scr[...] = (_gelu_tanh(gate) * val).astype(BF16)
        o_ref[...] += jnp.dot(act_scr[...], wo_ref[...], preferred_element_type=F32)
    else:
        first_tile = i % tiles_per_seq == 0
        pieces = [slice(c, c + FFN_PIECE) for c in range(0, tf, FFN_PIECE)]

        def up_proj(cols):
            for w_ref, prev_ref, new_ref, e_scr, slot in ((wg_ref, pg_ref, ng_ref, eg_scr, 0),
                                                          (wv_ref, pv_ref, nv_ref, ev_scr, 1)):
                e_scr[8:8 + tm, cols] = jnp.dot(h_ref[...], w_ref[:, cols], preferred_element_type=F32)
                e_scr[6:8, cols] = jnp.where(first_tile, prev_ref[0, :, cols], carry_scr[slot, j, 6:8, cols])
                tail = e_scr[tm + 6:tm + 8, cols]
                carry_scr[slot, j, 6:8, cols] = tail
                new_ref[0, :, cols] = tail

        def conv(e_scr, cw_ref, cb_ref, cols, r):
            cw = cw_ref[:, cols]
            return (cw[0:1, :] * e_scr[6 + r:6 + r + FFN_ROWS, cols]
                    + cw[1:2, :] * e_scr[7 + r:7 + r + FFN_ROWS, cols]
                    + cw[2:3, :] * e_scr[8 + r:8 + r + FFN_ROWS, cols] + cb_ref[:, cols])

        def conv_geglu(cols):
            for r in range(0, tm, FFN_ROWS):
                act_scr[r:r + FFN_ROWS, cols] = (_gelu_tanh(conv(eg_scr, cwg_ref, cbg_ref, cols, r))
                                                 * conv(ev_scr, cwv_ref, cbv_ref, cols, r)).astype(BF16)

        def down_proj(cols):
            for n in range(0, d, FFN_COLS):
                o_ref[:, n:n + FFN_COLS] += jnp.dot(act_scr[:, cols], wo_ref[cols, n:n + FFN_COLS],
                                                    preferred_element_type=F32)

        up_proj(pieces[0])
        for c in range(len(pieces)):
            if c + 1 < len(pieces):
                up_proj(pieces[c + 1])
            conv_geglu(pieces[c])
            if c > 0:
                down_proj(pieces[c - 1])
        down_proj(pieces[-1])

    @pl.when(j == nj - 1)
    def _():
        o_ref[...] = x1_ref[...] + _rms(o_ref[...], ln_ref[...])


def _ffn(h2, w_in, conv_w, conv_b, w_out, x1, ln_post, prev, tm, tf, seq_len):
    m, d = h2.shape
    single = seq_len == 1
    nj = D_FF // tf
    tiles_per_seq = 1 if single else seq_len // tm
    if single:
        prev_g = pl.BlockSpec((2, tm, tf), lambda i, j: (0, i, j))
        prev_v = pl.BlockSpec((2, tm, tf), lambda i, j: (0, i, nj + j))
        new_g = pl.BlockSpec((tm, tf), lambda i, j: (i, j))
        new_shape = jax.ShapeDtypeStruct((m, D_FF), F32)
    else:
        prev_g = pl.BlockSpec((1, 2, tf), lambda i, j: (i // tiles_per_seq, 0, j))
        prev_v = pl.BlockSpec((1, 2, tf), lambda i, j: (i // tiles_per_seq, 0, nj + j))
        new_g = pl.BlockSpec((1, 2, tf), lambda i, j: (i, 0, j))
        new_shape = jax.ShapeDtypeStruct((m // tm, 2, D_FF), F32)
    body = functools.partial(_ffn_body, tm=tm, tiles_per_seq=tiles_per_seq, single_token=single)
    once = dict(pipeline_mode=pl.Buffered(1)) if tm >= 1024 else {}
    return pl.pallas_call(
        body,
        grid=(m // tm, nj),
        in_specs=[pl.BlockSpec((tm, d), lambda i, j: (i, 0)),
                  pl.BlockSpec((d, tf), lambda i, j: (0, j)),
                  pl.BlockSpec((d, tf), lambda i, j: (0, nj + j)),
                  pl.BlockSpec((FFN_CONV_W, tf), lambda i, j: (0, j)),
                  pl.BlockSpec((FFN_CONV_W, tf), lambda i, j: (0, nj + j)),
                  pl.BlockSpec((1, tf), lambda i, j: (0, j)),
                  pl.BlockSpec((1, tf), lambda i, j: (0, nj + j)),
                  pl.BlockSpec((tf, d), lambda i, j: (j, 0)),
                  pl.BlockSpec((tm, d), lambda i, j: (i, 0), **once),
                  pl.BlockSpec((1, d), lambda i, j: (0, 0)),
                  prev_g, prev_v],
        out_specs=[pl.BlockSpec((tm, d), lambda i, j: (i, 0), **once), new_g, new_g],
        out_shape=[jax.ShapeDtypeStruct((m, d), F32), new_shape, new_shape],
        scratch_shapes=[pltpu.VMEM((tm + 8, tf), F32),
                        pltpu.VMEM((tm + 8, tf), F32),
                        pltpu.VMEM((2, nj, 8, tf), F32),
                        pltpu.VMEM((tm, tf), BF16)],
        compiler_params=_params(("arbitrary", "arbitrary"), 56),
        name="convffn",
    )(h2, w_in, w_in, conv_w, conv_w, conv_b, conv_b, w_out, x1, ln_post, prev, prev)


def _cache_shift_body(ck_ref, cv_ref, ck_next_ref, cv_next_ref, nk_ref, nv_ref, ok_ref, ov_ref, *, tr):
    last = pl.program_id(1) == pl.num_programs(1) - 1
    for c_ref, nxt_ref, n_ref, o_ref in ((ck_ref, ck_next_ref, nk_ref, ok_ref),
                                         (cv_ref, cv_next_ref, nv_ref, ov_ref)):
        o_ref[0, 0:tr - 1] = c_ref[0, 1:tr]
        o_ref[0, tr - 1] = jnp.where(last, n_ref[0, 0], nxt_ref[0, 0])


def _cache_shift(cache_k, cache_v, new_k, new_v, tr):
    nb, rows, nh, dh = cache_k.shape
    main = pl.BlockSpec((1, tr, nh, dh), lambda b, i: (b, i, 0, 0))
    nxt = pl.BlockSpec((1, 1, nh, dh), lambda b, i: (b, jnp.minimum((i + 1) * tr, rows - 1), 0, 0))
    new = pl.BlockSpec((1, 1, nh, dh), lambda b, i: (b, 0, 0, 0))
    shape = jax.ShapeDtypeStruct(cache_k.shape, cache_k.dtype)
    return pl.pallas_call(
        functools.partial(_cache_shift_body, tr=tr),
        grid=(nb, rows // tr),
        in_specs=[main, main, nxt, nxt, new, new],
        out_specs=[main, main],
        out_shape=[shape, shape],
        compiler_params=_params(("arbitrary", "arbitrary"), 40),
        name="cache_shift",
    )(cache_k, cache_v, cache_k, cache_v, new_k, new_v)


def _lane_vec(values, offset):
    return jnp.zeros((1, 128), F32).at[0, offset:offset + V_HEADS_B].set(values.astype(F32))


def kernel(x_prompt, x_sample, cache_win_k, cache_win_v, state_dn_conv, state_dn_rec, state_ffn_conv,
           rel_bias, ln_mix_pre, w_in, dn_conv_w, dn_A_log, dn_dt_bias, dn_norm_w, w_out, ln_mix_post,
           ln_ffn_pre, w_ffn_in, ffn_conv_w, ffn_conv_b, w_ffn_out, ln_ffn_post):
    bp, sp, d = x_prompt.shape
    bs = x_sample.shape[0]
    l = 0

    w_gate = jnp.pad(w_in[l, :, PROJ_MAIN:], ((0, 0), (0, 128 - 2 * V_HEADS_B))).astype(BF16)
    wo_a = w_out[l, :WIDTH_A].astype(BF16)
    wo_b = w_out[l, WIDTH_A:].astype(BF16)
    wf_in = w_ffn_in[l].astype(BF16)
    wf_out = w_ffn_out[l].astype(BF16)
    ln1 = ln_mix_pre[l][None, :]
    ln2 = ln_mix_post[l][None, :]
    ln3 = ln_ffn_pre[l][None, :]
    ln4 = ln_ffn_post[l][None, :]
    conv_w = dn_conv_w[l]
    alog_vec = _lane_vec(dn_A_log[l], V_HEADS_B)
    dtb_vec = _lane_vec(dn_dt_bias[l], V_HEADS_B)
    norm_w = dn_norm_w[l][None, :]
    fcw = ffn_conv_w[l]
    fcb = ffn_conv_b[l][None, :]

    xp = x_prompt.reshape(bp * sp, d)
    hp = _rmsnorm(xp, ln1, 512)
    proj_p = _matmul_f32w(hp, w_in[l], PROJ_MAIN, 1024, 1024, "inproj_prompt")
    proj3 = proj_p.reshape(bp, sp, PROJ_MAIN)
    att_p = _attn_prompt(proj3, rel_bias)
    dn_p, p_dn_rec = _dn_prompt(
        proj3, hp.reshape(bp, sp, d), w_gate,
        jnp.zeros((bp, CONV_W - 1, CONV_DIM), F32), jnp.zeros((bp, V_HEADS_B, DK, DV), F32),
        conv_w, alog_vec, dtb_vec, norm_w, 256)
    x1_p, h2_p = _outproj(att_p.reshape(bp * sp, WIDTH_A), dn_p.reshape(bp * sp, WIDTH_BV),
                          wo_a, wo_b, xp, ln2, ln3, 512)
    y_p, fc_g, fc_v = _ffn(h2_p, wf_in, fcw, fcb, wf_out, x1_p, ln4,
                           jnp.zeros((bp, FFN_CONV_W - 1, 2 * D_FF), F32), FFN_TM, 512, sp)
    keep = min(MAX_DISTANCE, sp)
    p_win_k = proj3[:, sp - keep:, OFF_AK:OFF_AK + WIDTH_A].reshape(1, bp, keep, HEADS_A, HEAD_DIM)
    p_win_v = proj3[:, sp - keep:, OFF_AV:OFF_AV + WIDTH_A].reshape(1, bp, keep, HEADS_A, HEAD_DIM)
    p_dn_conv = proj3[:, sp - (CONV_W - 1):, OFF_BQ:OFF_BQ + CONV_DIM][None]
    tiles = sp // FFN_TM
    p_ffn_conv = jnp.concatenate([fc_g[tiles - 1::tiles], fc_v[tiles - 1::tiles]], axis=-1)[None]

    xs = x_sample.reshape(bs, d)
    hs = _rmsnorm(xs, ln1, bs)
    proj_s = _matmul_f32w(hs, w_in[l], PROJ_MAIN, bs, 1024, "inproj_sample")
    gates_s = _matmul(hs, w_gate, bs, 128, "gates_sample")
    past = cache_win_k.shape[2]
    ck = cache_win_k[l]
    cv = cache_win_v[l]
    new_k = proj_s[:, OFF_AK:OFF_AK + WIDTH_A]
    new_v = proj_s[:, OFF_AV:OFF_AV + WIDTH_A]
    new_q = proj_s[:, OFF_AQ:OFF_AQ + WIDTH_A].reshape(bs, HEADS_A, HEAD_DIM)
    new_k = new_k.reshape(bs, HEADS_A, HEAD_DIM)
    new_v = new_v.reshape(bs, HEADS_A, HEAD_DIM)
    att_s = _attn_sample(new_q, new_k, new_v, ck, cv, rel_bias)
    s_win_k, s_win_v = _cache_shift(ck, cv, new_k[:, None], new_v[:, None], 512)
    dn_s, s_dn_conv, s_dn_rec = _dn_sample(proj_s[:, None], gates_s[:, None], state_dn_conv[l],
                                           state_dn_rec[l], conv_w, alog_vec, dtb_vec, norm_w)
    x1_s, h2_s = _outproj(att_s.reshape(bs, WIDTH_A), dn_s.reshape(bs, WIDTH_BV),
                          wo_a, wo_b, xs, ln2, ln3, bs)
    prev_s = jnp.swapaxes(state_ffn_conv[l], 0, 1)
    y_s, up_g, up_v = _ffn(h2_s, wf_in, fcw, fcb, wf_out, x1_s, ln4, prev_s, bs, 512, 1)
    s_ffn_conv = jnp.stack([prev_s[1], jnp.concatenate([up_g, up_v], axis=-1)], axis=1)[None]

    return (y_p.reshape(bp, sp, d), y_s.reshape(bs, 1, d),
            p_win_k, p_win_v, p_dn_conv, p_dn_rec[None], p_ffn_conv,
            s_win_k[None], s_win_v[None], s_dn_conv[None], s_dn_rec[None], s_ffn_conv)
```

```python
import functools
import math

import numpy as np
import jax
import jax.numpy as jnp
from jax import lax
from jax.experimental import pallas as pl
from jax.experimental.pallas import tpu as pltpu

F32 = jnp.float32
BF16 = jnp.bfloat16

D_MODEL = 2048
HEAD_DIM = 128
WIDTH_A = 1024
HEADS_A = 8
DILATIONS = (1, 4, 16)
BLK = 128
N_BUCKETS = 32
MAX_DISTANCE = 2048
DK = 128
DV = 128
V_HEADS_B = 8
QK_HEADS_B = 4
WIDTH_BQK = 512
WIDTH_BV = 1024
CONV_W = 4
CONV_DIM = 2048
CHUNK = 64
D_FF = 5632
FFN_CONV_W = 3
EPS = 1e-6
NEG = -1e30
ATT_SCALE = HEAD_DIM ** -0.5
QK_SCALE = DK ** -0.5

OFF_AQ, OFF_AK, OFF_AV = 0, 1024, 2048
OFF_BQ, OFF_BK, OFF_BV, OFF_BZ = 3072, 3584, 4096, 5120
OFF_GATES = 6144
PROJ_MAIN = 6144

MIB = 2 ** 20


def _params(semantics, vmem_mib):
    return pltpu.CompilerParams(dimension_semantics=semantics, vmem_limit_bytes=vmem_mib * MIB)


def _bdot(a, b):
    return jnp.dot(a.astype(BF16), b.astype(BF16), preferred_element_type=F32)


def _bdot_nt(a, b):
    return lax.dot_general(a.astype(BF16), b.astype(BF16), (((1,), (1,)), ((), ())),
                           preferred_element_type=F32)


def _bdot_tn(a, b):
    return lax.dot_general(a.astype(BF16), b.astype(BF16), (((0,), (0,)), ((), ())),
                           preferred_element_type=F32)


def _fdot(a, b):
    return jnp.dot(a, b, preferred_element_type=F32, precision=lax.Precision.HIGHEST)


def _silu(x):
    return x * (1.0 / (1.0 + jnp.exp(-x)))


def _sigmoid(x):
    return 1.0 / (1.0 + jnp.exp(-x))


def _softplus(x):
    return jnp.maximum(x, 0.0) + jnp.log(1.0 + jnp.exp(-jnp.abs(x)))


def _gelu_tanh(x):
    c = math.sqrt(2.0 / math.pi)
    half = 0.5 * x
    return half + half * jnp.tanh(x * (c + (c * 0.044715) * (x * x)))


def _rms(x, w):
    return x * lax.rsqrt(jnp.mean(x * x, axis=-1, keepdims=True) + EPS) * w


def _rmsnorm_body(x_ref, w_ref, o_ref):
    o_ref[...] = _rms(x_ref[...], w_ref[...]).astype(o_ref.dtype)


def _rmsnorm(x, w, tm):
    m, d = x.shape
    return pl.pallas_call(
        _rmsnorm_body,
        grid=(m // tm,),
        in_specs=[pl.BlockSpec((tm, d), lambda i: (i, 0)),
                  pl.BlockSpec((1, d), lambda i: (0, 0))],
        out_specs=pl.BlockSpec((tm, d), lambda i: (i, 0)),
        out_shape=jax.ShapeDtypeStruct((m, d), BF16),
        compiler_params=_params(("arbitrary",), 40),
        name="rmsnorm",
    )(x, w)


def _matmul_body(x_ref, w_ref, o_ref):
    o_ref[...] = jnp.dot(x_ref[...], w_ref[...], preferred_element_type=F32)


def _matmul(x, w, tm, tn, name, n=None):
    m, k = x.shape
    n = w.shape[1] if n is None else n
    return pl.pallas_call(
        _matmul_body,
        grid=(n // tn, m // tm),
        in_specs=[pl.BlockSpec((tm, k), lambda j, i: (i, 0)),
                  pl.BlockSpec((k, tn), lambda j, i: (0, j))],
        out_specs=pl.BlockSpec((tm, tn), lambda j, i: (i, j)),
        out_shape=jax.ShapeDtypeStruct((m, n), F32),
        compiler_params=_params(("arbitrary", "arbitrary"), 48),
        name=name,
    )(x, w)


def _rel_bucket_np(dist):
    dist = np.asarray(dist, np.int64)
    max_exact = N_BUCKETS // 2
    d = np.maximum(dist, 1).astype(np.float64)
    val = np.log(d / max_exact) / math.log(MAX_DISTANCE / max_exact) * (N_BUCKETS - max_exact)
    frac = np.abs(val - np.round(val))
    near = (frac < 2e-5) &(dist >= max_exact) & (dist != max_exact) & (dist < MAX_DISTANCE)
    assert not near.any(), "distance on a bucket boundary"
    val = np.where(dist == max_exact, 0.0, val)
    large = np.minimum(max_exact + np.trunc(val).astype(np.int64), N_BUCKETS - 1)
    return np.where(dist < max_exact, dist, large).astype(np.int32)


def _prompt_bucket_tables():
    qi = np.arange(BLK)[:, None]
    kj = np.arange(2 * BLK)[None, :]
    delta = BLK + qi - kj
    inwin = (delta >= 0) & (delta <= BLK)
    tabs = []
    for dil in DILATIONS:
        b = _rel_bucket_np(np.clip(delta, 0, BLK) * dil)
        tabs.append(np.where(inwin, b, -1))
    return np.stack(tabs).astype(np.int32)


def _sample_bucket_tables():
    j = BLK - np.arange(BLK)
    return np.stack([_rel_bucket_np(j * dil)[None, :] for dil in DILATIONS]).astype(np.int32)


def _attn_prompt_body(bucket_ref, relb_ref, q_ref, k_ref, v_ref, o_ref,
                      bias_scr, acc_scr, m_scr, l_scr):
    h = pl.program_id(1)
    col = lax.broadcasted_iota(jnp.int32, (BLK, 2 * BLK), 1)
    tables = _prompt_bucket_tables()
    for br in range(3):
        bk = bucket_ref[br]
        bias = jnp.zeros((BLK, 2 * BLK), F32)
        for kb in sorted(set(tables[br].ravel().tolist()) - {-1}):
            bias = jnp.where(bk == kb, relb_ref[kb, h], bias)
        full = jnp.where(bk >= 0, bias, NEG)
        bias_scr[2 * br] = full
        bias_scr[2 * br + 1] = jnp.where(col >= BLK, full, NEG)

    def run_branch(br, dil, is_first_branch, is_last_branch):
        shift = int(math.log2(dil))
        span = BLK * dil
        stride = None if dil == 1 else dil

        def rows(start):
            return pl.ds(start, BLK, stride=stride) if stride else pl.ds(start, BLK)

        nb = q_ref.shape[1] // span
        run_len = min(nb, ATTN_UNROLL)
        runs_per_it = ATTN_UNROLL // run_len
        runs_per_res = nb // run_len
        starts_at_zero = runs_per_res == 1

        def tasks(it, carry):
            q_starts, firsts, qs_, ks_, vs_ = [], [], [], [], []
            for rr in range(runs_per_it):
                ri = it * runs_per_it + rr
                n0 = (ri % runs_per_res) * run_len
                if dil == 1:
                    base = pl.multiple_of(n0 * span, BLK)
                else:
                    base = n0 * span + ri // runs_per_res
                first = jnp.where(n0 == 0, 1, 0)
                starts = [base + u * span for u in range(run_len)]
                kb = [k_ref[0, rows(st), :].astype(BF16) for st in starts]
                vb = [v_ref[0, rows(st), :].astype(BF16) for st in starts]
                if starts_at_zero:
                    k_prev, v_prev = None, None
                else:
                    p_start = base - span * (1 - first)
                    if dil == 1:
                        p_start = pl.multiple_of(p_start, BLK)
                    k_prev = k_ref[0, rows(p_start), :].astype(BF16)
                    v_prev = v_ref[0, rows(p_start), :].astype(BF16)
                for u, st in enumerate(starts):
                    q_starts.append(st)
                    qs_.append(q_ref[0, rows(st), :].astype(BF16))
                    kp, vp = (k_prev, v_prev) if u == 0 else (kb[u - 1], vb[u - 1])
                    if kp is None:
                        firsts.append(None)
                        ks_.append(kb[u])
                        vs_.append(vb[u])
                    else:
                        firsts.append(first if u == 0 else 0)
                        ks_.append(jnp.concatenate([kp, kb[u]], axis=0))
                        vs_.append(jnp.concatenate([vp, vb[u]], axis=0))
            if not is_first_branch:
                runs = [(m_scr[rows(qs), :], l_scr[rows(qs), :], acc_scr[rows(qs), :]) for qs in q_starts]
            ss = [_bdot_nt(q, k) * ATT_SCALE
                  + (bias_scr[2 * br, :, BLK:] if f is None else bias_scr[2 * br + f])
                  for q, k, f in zip(qs_, ks_, firsts)]
            ms = [jnp.max(s, axis=-1, keepdims=True) for s in ss]
            ps_ = [jnp.exp(s - m) for s, m in zip(ss, ms)]
            ls = [jnp.sum(p, axis=-1, keepdims=True) for p in ps_]
            accs = [_bdot(p, v) for p, v in zip(ps_, vs_)]
            outs = []
            for u in range(ATTN_UNROLL):
                m_b = jnp.broadcast_to(ms[u], (BLK, HEAD_DIM))
                l_b = jnp.broadcast_to(ls[u], (BLK, HEAD_DIM))
                acc_t = accs[u]
                if not is_first_branch:
                    m_run, l_run, acc_run = runs[u]
                    m_new = jnp.maximum(m_run, m_b)
                    a = jnp.exp(m_run - m_new)
                    b = jnp.exp(m_b - m_new)
                    acc_t = a * acc_run + b * acc_t
                    l_b = a * l_run + b * l_b
                    m_b = m_new
                outs.append((m_b, l_b, acc_t))
            for qs, (m_b, l_b, acc_t) in zip(q_starts, outs):
                if is_last_branch:
                    o_ref[0, rows(qs), :] = (acc_t / l_b).astype(o_ref.dtype)
                else:
                    m_scr[rows(qs), :] = m_b
                    l_scr[rows(qs), :] = l_b
                    acc_scr[rows(qs), :] = acc_t
            return carry

        lax.fori_loop(0, nb * dil // ATTN_UNROLL, tasks, 0)

    run_branch(2, 16, True, False)
    run_branch(1, 4, False, False)
    run_branch(0, 1, False, True)


def _attn_prompt(proj3, rel_bias):
    b, s, _ = proj3.shape
    buckets = jnp.asarray(_prompt_bucket_tables())
    blk = (1, s, HEAD_DIM)
    return pl.pallas_call(
        _attn_prompt_body,
        grid=(b, HEADS_A),
        in_specs=[pl.BlockSpec((3, BLK, 2 * BLK), lambda i, h: (0, 0, 0)),
                  pl.BlockSpec(memory_space=pltpu.SMEM),
                  pl.BlockSpec(blk, lambda i, h: (i, 0, OFF_AQ // HEAD_DIM + h)),
                  pl.BlockSpec(blk, lambda i, h: (i, 0, OFF_AK // HEAD_DIM + h)),
                  pl.BlockSpec(blk, lambda i, h: (i, 0, OFF_AV // HEAD_DIM + h))],
        out_specs=pl.BlockSpec(blk, lambda i, h: (i, 0, h)),
        out_shape=jax.ShapeDtypeStruct((b, s, WIDTH_A), BF16),
        scratch_shapes=[pltpu.VMEM((6, BLK, 2 * BLK), F32),
                        pltpu.VMEM((s, HEAD_DIM), F32),
                        pltpu.VMEM((s, HEAD_DIM), F32),
                        pltpu.VMEM((s, HEAD_DIM), F32)],
        compiler_params=_params(("arbitrary", "arbitrary"), 40),
        name="attn_prompt",
    )(buckets, rel_bias, proj3, proj3, proj3)


def _attn_sample_body(bucket_ref, relbt_ref, q_ref, kn_ref, vn_ref,
                      k1_ref, k4_ref, k16_ref, v1_ref, v4_ref, v16_ref, o_ref, bias_scr):
    relbt = relbt_ref[...]
    tile = (HEADS_A, HEAD_DIM)

    @pl.when(pl.program_id(0) == 0)
    def _():
        for br in range(3):
            bk = bucket_ref[br]
            bias = jnp.zeros((BLK,) + tile, F32)
            for kb in range(N_BUCKETS):
                col = jnp.broadcast_to(relbt[:, kb:kb + 1], tile)
                bias = jnp.where(bk == kb, col[None], bias)
            bias_scr[br] = bias

    def lane_sum(x):
        return jnp.broadcast_to(jnp.sum(x, axis=-1, keepdims=True), x.shape)

    q = q_ref[0]
    s_self = lane_sum(q * kn_ref[0]) * ATT_SCALE + jnp.broadcast_to(relbt[:, 0:1], tile)
    scores = []
    m = s_self
    for br, k_ref in enumerate((k1_ref, k4_ref, k16_ref)):
        s = lane_sum(k_ref[...] * q[None]) * ATT_SCALE + bias_scr[br]
        scores.append(s)
        m = jnp.maximum(m, jnp.max(s, axis=0))
    p_self = 3.0 * jnp.exp(s_self - m)
    l = p_self
    acc = p_self * vn_ref[0]
    for s, v_ref in zip(scores, (v1_ref, v4_ref, v16_ref)):
        p = jnp.exp(s - m[None])
        l = l + jnp.sum(p, axis=0)
        acc = acc + jnp.sum(p * v_ref[...], axis=0)
    o_ref[0] = (acc / l).astype(o_ref.dtype)


def _attn_sample(q, k_new, v_new, cache_k, cache_v, rel_bias):
    b, past = cache_k.shape[:2]
    tile = (HEADS_A, HEAD_DIM)
    buckets = jnp.asarray(np.broadcast_to(_sample_bucket_tables().reshape(3, BLK, 1, 1), (3, BLK) + tile))
    row = pl.BlockSpec((1,) + tile, lambda i: (i, 0, 0))
    views, specs = [], []
    for cache in (cache_k, cache_v):
        for dil in DILATIONS:
            views.append(cache.reshape((b, past // dil, dil) + tile))
            last = past // dil // BLK - 1
            specs.append(pl.BlockSpec((None, BLK, None) + tile,
                                      functools.partial(lambda last, i: (i, last, 0, 0, 0), last)))
    return pl.pallas_call(
        _attn_sample_body,
        grid=(b,),
        in_specs=[pl.BlockSpec((3, BLK) + tile, lambda i: (0, 0, 0, 0)),
                  pl.BlockSpec((HEADS_A, N_BUCKETS), lambda i: (0, 0)),
                  row, row, row] + specs,
        out_specs=row,
        out_shape=jax.ShapeDtypeStruct((b,) + tile, BF16),
        scratch_shapes=[pltpu.VMEM((3, BLK) + tile, F32)],
        compiler_params=_params(("arbitrary",), 40),
        name="attn_sample",
    )(buckets, rel_bias.T, q, k_new, v_new, *views)


GROUP = 1
ATTN_UNROLL = 4
FFN_TM = 1024
FFN_ROWS = 64
FFN_COLS = 512
FFN_PIECE = 256


def _dn_prompt_body(q_ref, k_ref, v_ref, z_ref, h_ref, wgate_ref, cw_ref, cs_ref, s0_ref, alog_ref, dtb_ref, nw_ref,
                    o_ref, s_out_ref,
                    s_scr, e_scr, qn_scr, kn_scr, vv_scr, g_scr, beta_scr,
                    w_scr, u_scr, qg_scr, kdt_scr, attn_scr, gl_scr, o_scr, *, tt):
    t = pl.program_id(1)
    nt = pl.num_programs(1)

    @pl.when(t == 0)
    def _():
        s_scr[...] = s0_ref[0]
        e_scr[5:8, :] = cs_ref[0]

    e_scr[8:8 + tt, 0:WIDTH_BQK] = q_ref[0]
    e_scr[8:8 + tt, WIDTH_BQK:2 * WIDTH_BQK] = k_ref[0]
    e_scr[8:8 + tt, 2 * WIDTH_BQK:CONV_DIM] = v_ref[0]

    def l2n(x):
        return x * lax.rsqrt(jnp.sum(x * x, axis=-1, keepdims=True) + EPS)

    for c0 in range(0, CONV_DIM, DK):
        cols = slice(c0, c0 + DK)
        w = cw_ref[:, cols]
        y = w[0:1, :] * e_scr[5:5 + tt, cols]
        for i in range(1, CONV_W):
            y = y + w[i:i + 1, :] * e_scr[5 + i:5 + i + tt, cols]
        y = _silu(y)
        if c0 < WIDTH_BQK:
            qn_scr[:, cols] = l2n(y) * QK_SCALE
        elif c0 < 2 * WIDTH_BQK:
            kn_scr[:, c0 - WIDTH_BQK:c0 - WIDTH_BQK + DK] = l2n(y)
        else:
            vv_scr[:, c0 - 2 * WIDTH_BQK:c0 - 2 * WIDTH_BQK + DK] = y
    e_scr[5:8, :] = e_scr[tt + 5:tt + 8, :]
    gates = jnp.dot(h_ref[0], wgate_ref[...], preferred_element_type=F32)
    beta_scr[...] = _sigmoid(gates)
    g_scr[...] = -jnp.exp(alog_ref[...]) * _softplus(gates + dtb_ref[...])

    ri = lax.broadcasted_iota(jnp.int32, (CHUNK, CHUNK), 0)
    ci = lax.broadcasted_iota(jnp.int32, (CHUNK, CHUNK), 1)
    tri = ri >= ci
    strict = ri > ci
    tril_ones = tri.astype(F32)
    nw = nw_ref[...]

    for c0 in range(0, tt // CHUNK, GROUP):
        units = []
        for c in range(c0, c0 + GROUP):
            rows = slice(c * CHUNK, (c + 1) * CHUNK)
            beta_all = beta_scr[rows, :]
            gc_all = _fdot(tril_ones, g_scr[rows, :])
            gc_all_t = gc_all.T
            for hq in range(QK_HEADS_B):
                qn = qn_scr[rows, hq * DK:(hq + 1) * DK]
                kn = kn_scr[rows, hq * DK:(hq + 1) * DK]
                kk = _bdot_nt(kn, kn)
                qk = _bdot_nt(qn, kn)
                for hv in range(2 * hq, 2 * hq + 2):
                    beta = beta_all[:, hv:hv + 1]
                    gc = gc_all[:, V_HEADS_B + hv:V_HEADS_B + hv + 1]
                    gc_row = gc_all_t[V_HEADS_B + hv:V_HEADS_B + hv + 1, :]
                    gc_last = gc_row[:, CHUNK - 1:CHUNK]
                    decay = jnp.exp(jnp.where(tri, gc - gc_row, NEG))
                    a = jnp.where(strict, beta * kk * decay, 0.0)
                    egc = jnp.exp(gc)
                    attn_scr[hv, rows, :] = (qk * decay).astype(BF16)
                    qg_scr[hv, rows, :] = (qn * egc).astype(BF16)
                    kd = kn * jnp.exp(gc_last - gc)
                    kdt_scr[hv, c * DK:(c + 1) * DK, :] = kd.T.astype(BF16)
                    gl_scr[hv, c * 8:(c + 1) * 8, :] = jnp.broadcast_to(jnp.exp(gc_last), (8, DV))
                    units.append((hv, rows, a))
            g_scr[rows, :] = gc_all
        ns = [-u[2] for u in units]
        pws = [u[2] for u in units]
        for _ in range(5):
            pws = [_bdot(pw, pw) for pw in pws]
            ns = [n + pw + _bdot(n, pw) for n, pw in zip(ns, pws)]
        xs = []
        for hv, rows, _ in units:
            beta = beta_scr[rows, hv:hv + 1]
            kscale = beta * jnp.exp(g_scr[rows, V_HEADS_B + hv:V_HEADS_B + hv + 1])
            xs.append(jnp.concatenate([kn_scr[rows, (hv // 2) * DK:(hv // 2 + 1) * DK] * kscale,
                                       vv_scr[rows, hv * DV:(hv + 1) * DV] * beta], axis=-1))
        wus = [x + _bdot(n, x) for n, x in zip(ns, xs)]
        for wu, (hv, rows, _) in zip(wus, units):
            w_scr[hv, rows, :] = wu[:, :DK].astype(BF16)
            u_scr[hv, rows, :] = wu[:, DK:]

    heads = range(V_HEADS_B)
    for c in range(tt // CHUNK):
        rows = slice(c * CHUNK, (c + 1) * CHUNK)
        states = [s_scr[hv] for hv in heads]
        states_b = [s.astype(BF16) for s in states]
        v_news = [u_scr[hv, rows, :] - jnp.dot(w_scr[hv, rows, :], states_b[hv], preferred_element_type=F32)
                  for hv in heads]
        v_news_b = [v.astype(BF16) for v in v_news]
        for hv in heads:
            s_scr[hv] = (states[hv] * gl_scr[hv, c * 8:c * 8 + 1, :]
                         + jnp.dot(kdt_scr[hv, c * DK:(c + 1) * DK, :], v_news_b[hv],
                                   preferred_element_type=F32))
        for hv in heads:
            o_scr[rows, hv * DV:(hv + 1) * DV] = (
                jnp.dot(qg_scr[hv, rows, :], states_b[hv], preferred_element_type=F32)
                + jnp.dot(attn_scr[hv, rows, :], v_news_b[hv], preferred_element_type=F32))

    for hv in heads:
        o = o_scr[:, hv * DV:(hv + 1) * DV]
        z = z_ref[0, :, hv * DV:(hv + 1) * DV]
        o = o * lax.rsqrt(jnp.mean(o * o, axis=-1, keepdims=True) + EPS) * nw * _silu(z)
        o_ref[0, :, hv * DV:(hv + 1) * DV] = o.astype(o_ref.dtype)

    @pl.when(t == nt - 1)
    def _():
        s_out_ref[0] = s_scr[...]


def _dn_prompt(proj3, h3, w_gate, conv_state, s0, conv_w, alog_vec, dtb_vec, norm_w, tt):
    b, s, _ = proj3.shape
    body = functools.partial(_dn_prompt_body, tt=tt)
    nh = V_HEADS_B
    in_specs = [
        pl.BlockSpec((1, tt, WIDTH_BQK), lambda i, t: (i, t, OFF_BQ // WIDTH_BQK)),
        pl.BlockSpec((1, tt, WIDTH_BQK), lambda i, t: (i, t, OFF_BK // WIDTH_BQK)),
        pl.BlockSpec((1, tt, WIDTH_BV), lambda i, t: (i, t, OFF_BV // WIDTH_BV)),
        pl.BlockSpec((1, tt, WIDTH_BV), lambda i, t: (i, t, OFF_BZ // WIDTH_BV)),
        pl.BlockSpec((1, tt, D_MODEL), lambda i, t: (i, t, 0)),
        pl.BlockSpec((D_MODEL, 128), lambda i, t: (0, 0)),
        pl.BlockSpec((CONV_W, CONV_DIM), lambda i, t: (0, 0)),
        pl.BlockSpec((1, CONV_W - 1, CONV_DIM), lambda i, t: (i, 0, 0)),
        pl.BlockSpec((1, nh, DK, DV), lambda i, t: (i, 0, 0, 0)),
        pl.BlockSpec((1, 128), lambda i, t: (0, 0)),
        pl.BlockSpec((1, 128), lambda i, t: (0, 0)),
        pl.BlockSpec((1, DV), lambda i, t: (0, 0)),
    ]
    return pl.pallas_call(
        body,
        grid=(b, s // tt),
        in_specs=in_specs,
        out_specs=[pl.BlockSpec((1, tt, WIDTH_BV), lambda i, t: (i, t, 0)),
                   pl.BlockSpec((1, nh, DK, DV), lambda i, t: (i, 0, 0, 0))],
        out_shape=[jax.ShapeDtypeStruct((b, s, WIDTH_BV), BF16),
                   jax.ShapeDtypeStruct((b, nh, DK, DV), F32)],
        scratch_shapes=[pltpu.VMEM((nh, DK, DV), F32),
                        pltpu.VMEM((tt + 8, CONV_DIM), F32),
                        pltpu.VMEM((tt, WIDTH_BQK), F32),
                        pltpu.VMEM((tt, WIDTH_BQK), F32),
                        pltpu.VMEM((tt, WIDTH_BV), F32),
                        pltpu.VMEM((tt, 128), F32),
                        pltpu.VMEM((tt, 128), F32),
                        pltpu.VMEM((nh, tt, DK), BF16),
                        pltpu.VMEM((nh, tt, DV), F32),
                        pltpu.VMEM((nh, tt, DK), BF16),
                        pltpu.VMEM((nh, tt // CHUNK * DK, CHUNK), BF16),
                        pltpu.VMEM((nh, tt, CHUNK), BF16),
                        pltpu.VMEM((nh, tt // CHUNK * 8, DV), F32),
                        pltpu.VMEM((tt, WIDTH_BV), F32)],
        compiler_params=_params(("arbitrary", "arbitrary"), 48),
        name="deltanet_prompt",
    )(proj3, proj3, proj3, proj3, h3, w_gate, conv_w, conv_state, s0, alog_vec, dtb_vec, norm_w)


def _dn_sample_body(proj_ref, gates_ref, cw_ref, cs_ref, s0_ref, alog_ref, dtb_ref, nw_ref,
                    o_ref, cs_out_ref, s_out_ref):
    pre = proj_ref[0, :, OFF_BQ:OFF_BQ + CONV_DIM]
    buf = cs_ref[0]
    w = cw_ref[...]
    y = w[CONV_W - 1:CONV_W, :] * pre
    for i in range(CONV_W - 1):
        y = y + w[i:i + 1, :] * buf[i:i + 1, :]
    y = _silu(y)
    cs_out_ref[0, 0:CONV_W - 2, :] = buf[1:CONV_W - 1, :]
    cs_out_ref[0, CONV_W - 2:CONV_W - 1, :] = pre

    gates = gates_ref[0]
    beta_all = _sigmoid(gates)
    g_all = -jnp.exp(alog_ref[...]) * _softplus(gates + dtb_ref[...])
    nw = nw_ref[...]

    def l2n(x):
        return x * lax.rsqrt(jnp.sum(x * x, axis=-1, keepdims=True) + EPS)

    row8 = lax.broadcasted_iota(jnp.int32, (8, DK), 0) == 0
    for hv in range(V_HEADS_B):
        hq = hv // 2
        q = l2n(y[:, hq * DK:(hq + 1) * DK]) * QK_SCALE
        k = l2n(y[:, WIDTH_BQK + hq * DK:WIDTH_BQK + (hq + 1) * DK])
        v = y[:, 2 * WIDTH_BQK + hv * DV:2 * WIDTH_BQK + (hv + 1) * DV]
        beta = beta_all[:, hv:hv + 1]
        g = g_all[:, V_HEADS_B + hv:V_HEADS_B + hv + 1]
        eg = jnp.exp(g)
        state = s0_ref[0, hv]

        def pad8(x):
            return jnp.where(row8, jnp.broadcast_to(x, (8, x.shape[-1])), 0.0)

        v_new = v * beta - _bdot(pad8(k * (beta * eg)), state)[0:1, :]
        qk = jnp.sum(q.astype(BF16).astype(F32) * k.astype(BF16).astype(F32), axis=-1, keepdims=True)
        o = _bdot(pad8(q * eg), state)[0:1, :] + qk.astype(BF16).astype(F32) * v_new.astype(BF16).astype(F32)
        s_out_ref[0, hv] = state * eg + _bdot_tn(pad8(k), pad8(v_new))
        z = proj_ref[0, :, OFF_BZ + hv * DV:OFF_BZ + (hv + 1) * DV]
        o = o * lax.rsqrt(jnp.mean(o * o, axis=-1, keepdims=True) + EPS) * nw * _silu(z)
        o_ref[0, :, hv * DV:(hv + 1) * DV] = o.astype(o_ref.dtype)


def _dn_sample(proj, gates, conv_state, s0, conv_w, alog_vec, dtb_vec, norm_w):
    b = proj.shape[0]
    return pl.pallas_call(
        _dn_sample_body,
        grid=(b,),
        in_specs=[pl.BlockSpec((1, 1, PROJ_MAIN), lambda i: (i, 0, 0)),
                  pl.BlockSpec((1, 1, 128), lambda i: (i, 0, 0)),
                  pl.BlockSpec((CONV_W, CONV_DIM), lambda i: (0, 0)),
                  pl.BlockSpec((1, CONV_W - 1, CONV_DIM), lambda i: (i, 0, 0)),
                  pl.BlockSpec((1, V_HEADS_B, DK, DV), lambda i: (i, 0, 0, 0)),
                  pl.BlockSpec((1, 128), lambda i: (0, 0)),
                  pl.BlockSpec((1, 128), lambda i: (0, 0)),
                  pl.BlockSpec((1, DV), lambda i: (0, 0))],
        out_specs=[pl.BlockSpec((1, 1, WIDTH_BV), lambda i: (i, 0, 0)),
                   pl.BlockSpec((1, CONV_W - 1, CONV_DIM), lambda i: (i, 0, 0)),
                   pl.BlockSpec((1, V_HEADS_B, DK, DV), lambda i: (i, 0, 0, 0))],
        out_shape=[jax.ShapeDtypeStruct((b, 1, WIDTH_BV), BF16),
                   jax.ShapeDtypeStruct((b, CONV_W - 1, CONV_DIM), F32),
                   jax.ShapeDtypeStruct((b, V_HEADS_B, DK, DV), F32)],
        compiler_params=_params(("arbitrary",), 40),
        name="deltanet_sample",
    )(proj, gates, conv_w, conv_state, s0, alog_vec, dtb_vec, norm_w)


def _outproj_body(att_ref, dn_ref, wa_ref, wb_ref, x_ref, lnpost_ref, lnpre_ref, x1_ref, h2_ref):
    mix = (jnp.dot(att_ref[...], wa_ref[...], preferred_element_type=F32)
           + jnp.dot(dn_ref[...], wb_ref[...], preferred_element_type=F32))
    x1 = x_ref[...] + _rms(mix, lnpost_ref[...])
    x1_ref[...] = x1
    h2_ref[...] = _rms(x1, lnpre_ref[...]).astype(h2_ref.dtype)


def _outproj(att, dn, w, x, ln_post, ln_pre, tm):
    m, d = x.shape
    assert WIDTH_A == WIDTH_BV
    return pl.pallas_call(
        _outproj_body,
        grid=(m // tm,),
        in_specs=[pl.BlockSpec((tm, WIDTH_A), lambda i: (i, 0)),
                  pl.BlockSpec((tm, WIDTH_BV), lambda i: (i, 0)),
                  pl.BlockSpec((WIDTH_A, d), lambda i: (0, 0)),
                  pl.BlockSpec((WIDTH_BV, d), lambda i: (1, 0)),
                  pl.BlockSpec((tm, d), lambda i: (i, 0)),
                  pl.BlockSpec((1, d), lambda i: (0, 0)),
                  pl.BlockSpec((1, d), lambda i: (0, 0))],
        out_specs=[pl.BlockSpec((tm, d), lambda i: (i, 0)),
                   pl.BlockSpec((tm, d), lambda i: (i, 0))],
        out_shape=[jax.ShapeDtypeStruct((m, d), F32),
                   jax.ShapeDtypeStruct((m, d), BF16)],
        compiler_params=_params(("arbitrary",), 48),
        name="outproj",
    )(att, dn, w, w, x, ln_post, ln_pre)


def _ffn_body(h_ref, wg_ref, wv_ref, cwg_ref, cwv_ref, cbg_ref, cbv_ref, wo_ref, x1_ref, ln_ref,
              pg_ref, pv_ref, o_ref, ng_ref, nv_ref, eg_scr, ev_scr, carry_scr, act_scr,
              *, tm, tiles_per_seq, single_token):
    i = pl.program_id(0)
    j = pl.program_id(1)
    nj = pl.num_programs(1)
    d = o_ref.shape[-1]
    tf = act_scr.shape[-1]

    @pl.when(j == 0)
    def _():
        o_ref[...] = jnp.zeros_like(o_ref)

    if single_token:
        def up_conv(w_ref, cw_ref, cb_ref, prev_ref, new_ref):
            up = jnp.dot(h_ref[...], w_ref[...], preferred_element_type=F32)
            cw = cw_ref[...]
            new_ref[...] = up
            return cw[0:1, :] * prev_ref[0] + cw[1:2, :] * prev_ref[1] + cw[2:3, :] * up + cb_ref[...]

        gate = up_conv(wg_ref, cwg_ref, cbg_ref, pg_ref, ng_ref)
        val = up_conv(wv_ref, cwv_ref, cbv_ref, pv_ref, nv_ref)
        act_scr[...] = (_gelu_tanh(gate) * val).astype(BF16)
        o_ref[...] += jnp.dot(act_scr[...], wo_ref[...], preferred_element_type=F32)
    else:
        first_tile = i % tiles_per_seq == 0
        pieces = [slice(c, c + FFN_PIECE) for c in range(0, tf, FFN_PIECE)]

        def up_proj(cols):
            for w_ref, prev_ref, new_ref, e_scr, slot in ((wg_ref, pg_ref, ng_ref, eg_scr, 0),
                                                          (wv_ref, pv_ref, nv_ref, ev_scr, 1)):
                e_scr[0, 8:8 + tm, cols] = jnp.dot(h_ref[...], w_ref[:, cols], preferred_element_type=F32)
                for r in range(0, tm, FFN_ROWS):
                    blk = e_scr[0, 8 + r:8 + r + FFN_ROWS, cols]
                    e_scr[1, 9 + r:9 + r + FFN_ROWS, cols] = blk
                    e_scr[2, 10 + r:10 + r + FFN_ROWS, cols] = blk
                before = jnp.where(first_tile, prev_ref[0, :, cols], carry_scr[slot, j, 6:8, cols])
                e_scr[1, 8:9, cols] = before[1:2, :]
                e_scr[2, 8:10, cols] = before
                tail = e_scr[0, tm + 6:tm + 8, cols]
                carry_scr[slot, j, 6:8, cols] = tail
                new_ref[0, :, cols] = tail

        def conv(e_scr, cw_ref, cb_ref, cols, r):
            cw = cw_ref[:, cols]
            rows = slice(8 + r, 8 + r + FFN_ROWS)
            return (cw[0:1, :] * e_scr[2, rows, cols] + cw[1:2, :] * e_scr[1, rows, cols]
                    + cw[2:3, :] * e_scr[0, rows, cols] + cb_ref[:, cols])

        def conv_geglu(cols):
            for r in range(0, tm, FFN_ROWS):
                act_scr[r:r + FFN_ROWS, cols] = (_gelu_tanh(conv(eg_scr, cwg_ref, cbg_ref, cols, r))
                                                 * conv(ev_scr, cwv_ref, cbv_ref, cols, r)).astype(BF16)

        def down_proj(cols):
            for n in range(0, d, FFN_COLS):
                o_ref[:, n:n + FFN_COLS] += jnp.dot(act_scr[:, cols], wo_ref[cols, n:n + FFN_COLS],
                                                    preferred_element_type=F32)

        up_proj(pieces[0])
        for c in range(len(pieces)):
            if c + 1 < len(pieces):
                up_proj(pieces[c + 1])
            conv_geglu(pieces[c])
            if c > 0:
                down_proj(pieces[c - 1])
        down_proj(pieces[-1])

    @pl.when(j == nj - 1)
    def _():
        o_ref[...] = x1_ref[...] + _rms(o_ref[...], ln_ref[...])


def _ffn(h2, w_in, conv_w, conv_b, w_out, x1, ln_post, prev, tm, tf, seq_len):
    m, d = h2.shape
    single = seq_len == 1
    nj = D_FF // tf
    tiles_per_seq = 1 if single else seq_len // tm
    if single:
        prev_g = pl.BlockSpec((2, tm, tf), lambda i, j: (0, i, j))
        prev_v = pl.BlockSpec((2, tm, tf), lambda i, j: (0, i, nj + j))
        new_g = pl.BlockSpec((tm, tf), lambda i, j: (i, j))
        new_shape = jax.ShapeDtypeStruct((m, D_FF), F32)
    else:
        prev_g = pl.BlockSpec((1, 2, tf), lambda i, j: (i // tiles_per_seq, 0, j))
        prev_v = pl.BlockSpec((1, 2, tf), lambda i, j: (i // tiles_per_seq, 0, nj + j))
        new_g = pl.BlockSpec((1, 2, tf), lambda i, j: (i, 0, j))
        new_shape = jax.ShapeDtypeStruct((m // tm, 2, D_FF), F32)
    body = functools.partial(_ffn_body, tm=tm, tiles_per_seq=tiles_per_seq, single_token=single)
    once = dict(pipeline_mode=pl.Buffered(1)) if tm >= 1024 else {}
    return pl.pallas_call(
        body,
        grid=(m // tm, nj),
        in_specs=[pl.BlockSpec((tm, d), lambda i, j: (i, 0), **once),
                  pl.BlockSpec((d, tf), lambda i, j: (0, j)),
                  pl.BlockSpec((d, tf), lambda i, j: (0, nj + j)),
                  pl.BlockSpec((FFN_CONV_W, tf), lambda i, j: (0, j)),
                  pl.BlockSpec((FFN_CONV_W, tf), lambda i, j: (0, nj + j)),
                  pl.BlockSpec((1, tf), lambda i, j: (0, j)),
                  pl.BlockSpec((1, tf), lambda i, j: (0, nj + j)),
                  pl.BlockSpec((tf, d), lambda i, j: (j, 0)),
                  pl.BlockSpec((tm, d), lambda i, j: (i, 0), **once),
                  pl.BlockSpec((1, d), lambda i, j: (0, 0)),
                  prev_g, prev_v],
        out_specs=[pl.BlockSpec((tm, d), lambda i, j: (i, 0), **once), new_g, new_g],
        out_shape=[jax.ShapeDtypeStruct((m, d), F32), new_shape, new_shape],
        scratch_shapes=[pltpu.VMEM((3, tm + 16, tf), F32),
                        pltpu.VMEM((3, tm + 16, tf), F32),
                        pltpu.VMEM((2, nj, 8, tf), F32),
                        pltpu.VMEM((tm, tf), BF16)],
        compiler_params=_params(("arbitrary", "arbitrary"), 56),
        name="convffn",
    )(h2, w_in, w_in, conv_w, conv_w, conv_b, conv_b, w_out, x1, ln_post, prev, prev)


def _cache_shift_body(ck_ref, cv_ref, ck_next_ref, cv_next_ref, nk_ref, nv_ref, ok_ref, ov_ref, *, tr):
    last = pl.program_id(1) == pl.num_programs(1) - 1
    for c_ref, nxt_ref, n_ref, o_ref in ((ck_ref, ck_next_ref, nk_ref, ok_ref),
                                         (cv_ref, cv_next_ref, nv_ref, ov_ref)):
        o_ref[0, 0:tr - 1] = c_ref[0, 1:tr]
        o_ref[0, tr - 1] = jnp.where(last, n_ref[0, 0], nxt_ref[0, 0])


def _cache_shift(cache_k, cache_v, new_k, new_v, tr):
    nb, rows, nh, dh = cache_k.shape
    main = pl.BlockSpec((1, tr, nh, dh), lambda b, i: (b, i, 0, 0))
    nxt = pl.BlockSpec((1, 1, nh, dh), lambda b, i: (b, jnp.minimum((i + 1) * tr, rows - 1), 0, 0))
    new = pl.BlockSpec((1, 1, nh, dh), lambda b, i: (b, 0, 0, 0))
    shape = jax.ShapeDtypeStruct(cache_k.shape, cache_k.dtype)
    return pl.pallas_call(
        functools.partial(_cache_shift_body, tr=tr),
        grid=(nb, rows // tr),
        in_specs=[main, main, nxt, nxt, new, new],
        out_specs=[main, main],
        out_shape=[shape, shape],
        compiler_params=_params(("arbitrary", "arbitrary"), 40),
        name="cache_shift",
    )(cache_k, cache_v, cache_k, cache_v, new_k, new_v)


def _lane_vec(values, offset):
    return jnp.zeros((1, 128), F32).at[0, offset:offset + V_HEADS_B].set(values.astype(F32))


def kernel(x_prompt, x_sample, cache_win_k, cache_win_v, state_dn_conv, state_dn_rec, state_ffn_conv,
           rel_bias, ln_mix_pre, w_in, dn_conv_w, dn_A_log, dn_dt_bias, dn_norm_w, w_out, ln_mix_post,
           ln_ffn_pre, w_ffn_in, ffn_conv_w, ffn_conv_b, w_ffn_out, ln_ffn_post):
    bp, sp, d = x_prompt.shape
    bs = x_sample.shape[0]
    l = 0

    w_main = w_in[l].astype(BF16)
    w_gate = jnp.pad(w_main[:, PROJ_MAIN:], ((0, 0), (0, 128 - 2 * V_HEADS_B)))
    wo = w_out[l].astype(BF16)
    wf_in = w_ffn_in[l].astype(BF16)
    wf_out = w_ffn_out[l].astype(BF16)
    ln1 = ln_mix_pre[l][None, :]
    ln2 = ln_mix_post[l][None, :]
    ln3 = ln_ffn_pre[l][None, :]
    ln4 = ln_ffn_post[l][None, :]
    conv_w = dn_conv_w[l]
    alog_vec = _lane_vec(dn_A_log[l], V_HEADS_B)
    dtb_vec = _lane_vec(dn_dt_bias[l], V_HEADS_B)
    norm_w = dn_norm_w[l][None, :]
    fcw = ffn_conv_w[l]
    fcb = ffn_conv_b[l][None, :]

    xp = x_prompt.reshape(bp * sp, d)
    hp = _rmsnorm(xp, ln1, 512)
    proj_p = _matmul(hp, w_main, 1024, 1024, "inproj_prompt", n=PROJ_MAIN)
    proj3 = proj_p.reshape(bp, sp, PROJ_MAIN)
    att_p = _attn_prompt(proj3, rel_bias)
    dn_p, p_dn_rec = _dn_prompt(
        proj3, hp.reshape(bp, sp, d), w_gate,
        jnp.zeros((bp, CONV_W - 1, CONV_DIM), F32), jnp.zeros((bp, V_HEADS_B, DK, DV), F32),
        conv_w, alog_vec, dtb_vec, norm_w, 256)
    x1_p, h2_p = _outproj(att_p.reshape(bp * sp, WIDTH_A), dn_p.reshape(bp * sp, WIDTH_BV),
                          wo, xp, ln2, ln3, 512)
    y_p, fc_g, fc_v = _ffn(h2_p, wf_in, fcw, fcb, wf_out, x1_p, ln4,
                           jnp.zeros((bp, FFN_CONV_W - 1, 2 * D_FF), F32), FFN_TM, 512, sp)
    keep = min(MAX_DISTANCE, sp)
    p_win_k = proj3[:, sp - keep:, OFF_AK:OFF_AK + WIDTH_A].reshape(1, bp, keep, HEADS_A, HEAD_DIM)
    p_win_v = proj3[:, sp - keep:, OFF_AV:OFF_AV + WIDTH_A].reshape(1, bp, keep, HEADS_A, HEAD_DIM)
    p_dn_conv = proj3[:, sp - (CONV_W - 1):, OFF_BQ:OFF_BQ + CONV_DIM][None]
    tiles = sp // FFN_TM
    p_ffn_conv = jnp.concatenate([fc_g[tiles - 1::tiles], fc_v[tiles - 1::tiles]], axis=-1)[None]

    xs = x_sample.reshape(bs, d)
    hs = _rmsnorm(xs, ln1, bs)
    proj_s = _matmul(hs, w_main, bs, 1024, "inproj_sample", n=PROJ_MAIN)
    gates_s = _matmul(hs, w_gate, bs, 128, "gates_sample")
    past = cache_win_k.shape[2]
    ck = cache_win_k[l]
    cv = cache_win_v[l]
    new_k = proj_s[:, OFF_AK:OFF_AK + WIDTH_A]
    new_v = proj_s[:, OFF_AV:OFF_AV + WIDTH_A]
    new_q = proj_s[:, OFF_AQ:OFF_AQ + WIDTH_A].reshape(bs, HEADS_A, HEAD_DIM)
    new_k = new_k.reshape(bs, HEADS_A, HEAD_DIM)
    new_v = new_v.reshape(bs, HEADS_A, HEAD_DIM)
    att_s = _attn_sample(new_q, new_k, new_v, ck, cv, rel_bias)
    s_win_k, s_win_v = _cache_shift(ck, cv, new_k[:, None], new_v[:, None], 512)
    dn_s, s_dn_conv, s_dn_rec = _dn_sample(proj_s[:, None], gates_s[:, None], state_dn_conv[l],
                                           state_dn_rec[l], conv_w, alog_vec, dtb_vec, norm_w)
    x1_s, h2_s = _outproj(att_s.reshape(bs, WIDTH_A), dn_s.reshape(bs, WIDTH_BV),
                          wo, xs, ln2, ln3, bs)
    prev_s = jnp.swapaxes(state_ffn_conv[l], 0, 1)
    y_s, up_g, up_v = _ffn(h2_s, wf_in, fcw, fcb, wf_out, x1_s, ln4, prev_s, bs, 512, 1)
    s_ffn_conv = jnp.stack([prev_s[1], jnp.concatenate([up_g, up_v], axis=-1)], axis=1)[None]

    return (y_p.reshape(bp, sp, d), y_s.reshape(bs, 1, d),
            p_win_k, p_win_v, p_dn_conv, p_dn_rec[None], p_ffn_conv,
            s_win_k[None], s_win_v[None], s_dn_conv[None], s_dn_rec[None], s_ffn_conv)
```

```python
import functools
import math

import numpy as np
import jax
import jax.numpy as jnp
from jax import lax
from jax.experimental import pallas as pl
from jax.experimental.pallas import tpu as pltpu

F32 = jnp.float32
BF16 = jnp.bfloat16

D_MODEL = 2048
HEAD_DIM = 128
WIDTH_A = 1024
HEADS_A = 8
DILATIONS = (1, 4, 16)
BLK = 128
N_BUCKETS = 32
MAX_DISTANCE = 2048
DK = 128
DV = 128
V_HEADS_B = 8
QK_HEADS_B = 4
WIDTH_BQK = 512
WIDTH_BV = 1024
CONV_W = 4
CONV_DIM = 2048
CHUNK = 128
SUB = 64
D_FF = 5632
FFN_CONV_W = 3
EPS = 1e-6
NEG = -1e30
ATT_SCALE = HEAD_DIM ** -0.5
QK_SCALE = DK ** -0.5

OFF_AQ, OFF_AK, OFF_AV = 0, 1024, 2048
OFF_BQ, OFF_BK, OFF_BV, OFF_BZ = 3072, 3584, 4096, 5120
OFF_GATES = 6144
PROJ_MAIN = 6144

MIB = 2 ** 20


def _params(semantics, vmem_mib):
    return pltpu.CompilerParams(dimension_semantics=semantics, vmem_limit_bytes=vmem_mib * MIB)


def _bdot(a, b):
    return jnp.dot(a.astype(BF16), b.astype(BF16), preferred_element_type=F32)


def _bdot_nt(a, b):
    return lax.dot_general(a.astype(BF16), b.astype(BF16), (((1,), (1,)), ((), ())),
                           preferred_element_type=F32)


def _bdot_tn(a, b):
    return lax.dot_general(a.astype(BF16), b.astype(BF16), (((0,), (0,)), ((), ())),
                           preferred_element_type=F32)


def _fdot(a, b):
    return jnp.dot(a, b, preferred_element_type=F32, precision=lax.Precision.HIGHEST)


def _silu(x):
    return x * (1.0 / (1.0 + jnp.exp(-x)))


def _sigmoid(x):
    return 1.0 / (1.0 + jnp.exp(-x))


def _softplus(x):
    return jnp.maximum(x, 0.0) + jnp.log(1.0 + jnp.exp(-jnp.abs(x)))


def _gelu_tanh(x):
    c = math.sqrt(2.0 / math.pi)
    half = 0.5 * x
    return half + half * jnp.tanh(x * (c + (c * 0.044715) * (x * x)))


def _rms(x, w):
    return x * lax.rsqrt(jnp.mean(x * x, axis=-1, keepdims=True) + EPS) * w


def _rmsnorm_body(x_ref, w_ref, o_ref):
    o_ref[...] = _rms(x_ref[...], w_ref[...]).astype(o_ref.dtype)


def _rmsnorm(x, w, tm):
    m, d = x.shape
    return pl.pallas_call(
        _rmsnorm_body,
        grid=(m // tm,),
        in_specs=[pl.BlockSpec((tm, d), lambda i: (i, 0)),
                  pl.BlockSpec((1, d), lambda i: (0, 0))],
        out_specs=pl.BlockSpec((tm, d), lambda i: (i, 0)),
        out_shape=jax.ShapeDtypeStruct((m, d), BF16),
        compiler_params=_params(("arbitrary",), 40),
        name="rmsnorm",
    )(x, w)


def _matmul_body(x_ref, w_ref, o_ref):
    o_ref[...] = jnp.dot(x_ref[...], w_ref[...], preferred_element_type=F32)


def _matmul(x, w, tm, tn, name, n=None):
    m, k = x.shape
    n = w.shape[1] if n is None else n
    return pl.pallas_call(
        _matmul_body,
        grid=(n // tn, m // tm),
        in_specs=[pl.BlockSpec((tm, k), lambda j, i: (i, 0)),
                  pl.BlockSpec((k, tn), lambda j, i: (0, j))],
        out_specs=pl.BlockSpec((tm, tn), lambda j, i: (i, j)),
        out_shape=jax.ShapeDtypeStruct((m, n), F32),
        compiler_params=_params(("arbitrary", "arbitrary"), 48),
        name=name,
    )(x, w)


def _rel_bucket_np(dist):
    dist = np.asarray(dist, np.int64)
    max_exact = N_BUCKETS // 2
    d = np.maximum(dist, 1).astype(np.float64)
    val = np.log(d / max_exact) / math.log(MAX_DISTANCE / max_exact) * (N_BUCKETS - max_exact)
    frac = np.abs(val - np.round(val))
    near = (frac < 2e-5) &(dist >= max_exact) & (dist != max_exact) & (dist < MAX_DISTANCE)
    assert not near.any(), "distance on a bucket boundary"
    val = np.where(dist == max_exact, 0.0, val)
    large = np.minimum(max_exact + np.trunc(val).astype(np.int64), N_BUCKETS - 1)
    return np.where(dist < max_exact, dist, large).astype(np.int32)


def _prompt_bucket_tables():
    qi = np.arange(BLK)[:, None]
    kj = np.arange(2 * BLK)[None, :]
    delta = BLK + qi - kj
    inwin = (delta >= 0) & (delta <= BLK)
    tabs = []
    for dil in DILATIONS:
        b = _rel_bucket_np(np.clip(delta, 0, BLK) * dil)
        tabs.append(np.where(inwin, b, -1))
    return np.stack(tabs).astype(np.int32)


def _sample_bucket_tables():
    j = BLK - np.arange(BLK)
    return np.stack([_rel_bucket_np(j * dil)[None, :] for dil in DILATIONS]).astype(np.int32)


def _attn_prompt_body(bucket_ref, relb_ref, q_ref, k_ref, v_ref, o_ref,
                      bias_scr, acc_scr, m_scr, l_scr):
    h = pl.program_id(1)
    col = lax.broadcasted_iota(jnp.int32, (BLK, 2 * BLK), 1)
    tables = _prompt_bucket_tables()
    for br in range(3):
        bk = bucket_ref[br]
        bias = jnp.zeros((BLK, 2 * BLK), F32)
        for kb in sorted(set(tables[br].ravel().tolist()) - {-1}):
            bias = jnp.where(bk == kb, relb_ref[kb, h], bias)
        full = jnp.where(bk >= 0, bias, NEG)
        bias_scr[2 * br] = full
        bias_scr[2 * br + 1] = jnp.where(col >= BLK, full, NEG)

    def run_branch(br, dil, is_first_branch, is_last_branch):
        shift = int(math.log2(dil))
        span = BLK * dil
        stride = None if dil == 1 else dil

        def rows(start):
            return pl.ds(start, BLK, stride=stride) if stride else pl.ds(start, BLK)

        nb = q_ref.shape[1] // span
        run_len = min(nb, ATTN_UNROLL)
        runs_per_it = ATTN_UNROLL // run_len
        runs_per_res = nb // run_len
        starts_at_zero = runs_per_res == 1

        def tasks(it, carry):
            q_starts, firsts, qs_, ks_, vs_ = [], [], [], [], []
            for rr in range(runs_per_it):
                ri = it * runs_per_it + rr
                n0 = (ri % runs_per_res) * run_len
                if dil == 1:
                    base = pl.multiple_of(n0 * span, BLK)
                else:
                    base = n0 * span + ri // runs_per_res
                first = jnp.where(n0 == 0, 1, 0)
                starts = [base + u * span for u in range(run_len)]
                kb = [k_ref[0, rows(st), :].astype(BF16) for st in starts]
                vb = [v_ref[0, rows(st), :].astype(BF16) for st in starts]
                if starts_at_zero:
                    k_prev, v_prev = None, None
                else:
                    p_start = base - span * (1 - first)
                    if dil == 1:
                        p_start = pl.multiple_of(p_start, BLK)
                    k_prev = k_ref[0, rows(p_start), :].astype(BF16)
                    v_prev = v_ref[0, rows(p_start), :].astype(BF16)
                for u, st in enumerate(starts):
                    q_starts.append(st)
                    qs_.append(q_ref[0, rows(st), :].astype(BF16))
                    kp, vp = (k_prev, v_prev) if u == 0 else (kb[u - 1], vb[u - 1])
                    if kp is None:
                        firsts.append(None)
                        ks_.append(kb[u])
                        vs_.append(vb[u])
                    else:
                        firsts.append(first if u == 0 else 0)
                        ks_.append(jnp.concatenate([kp, kb[u]], axis=0))
                        vs_.append(jnp.concatenate([vp, vb[u]], axis=0))
            if not is_first_branch:
                runs = [(m_scr[rows(qs), :], l_scr[rows(qs), :], acc_scr[rows(qs), :]) for qs in q_starts]
            ss = [_bdot_nt(q, k) * ATT_SCALE
                  + (bias_scr[2 * br, :, BLK:] if f is None else bias_scr[2 * br + f])
                  for q, k, f in zip(qs_, ks_, firsts)]
            ms = [jnp.max(s, axis=-1, keepdims=True) for s in ss]
            ps_ = [jnp.exp(s - m) for s, m in zip(ss, ms)]
            ls = [jnp.sum(p, axis=-1, keepdims=True) for p in ps_]
            accs = [_bdot(p, v) for p, v in zip(ps_, vs_)]
            outs = []
            for u in range(ATTN_UNROLL):
                m_b = jnp.broadcast_to(ms[u], (BLK, HEAD_DIM))
                l_b = jnp.broadcast_to(ls[u], (BLK, HEAD_DIM))
                acc_t = accs[u]
                if not is_first_branch:
                    m_run, l_run, acc_run = runs[u]
                    m_new = jnp.maximum(m_run, m_b)
                    a = jnp.exp(m_run - m_new)
                    b = jnp.exp(m_b - m_new)
                    acc_t = a * acc_run + b * acc_t
                    l_b = a * l_run + b * l_b
                    m_b = m_new
                outs.append((m_b, l_b, acc_t))
            for qs, (m_b, l_b, acc_t) in zip(q_starts, outs):
                if is_last_branch:
                    o_ref[0, rows(qs), :] = (acc_t / l_b).astype(o_ref.dtype)
                else:
                    m_scr[rows(qs), :] = m_b
                    l_scr[rows(qs), :] = l_b
                    acc_scr[rows(qs), :] = acc_t
            return carry

        lax.fori_loop(0, nb * dil // ATTN_UNROLL, tasks, 0)

    run_branch(2, 16, True, False)
    run_branch(1, 4, False, False)
    run_branch(0, 1, False, True)


def _attn_prompt(proj3, rel_bias):
    b, s, _ = proj3.shape
    buckets = jnp.asarray(_prompt_bucket_tables())
    blk = (1, s, HEAD_DIM)
    return pl.pallas_call(
        _attn_prompt_body,
        grid=(b, HEADS_A),
        in_specs=[pl.BlockSpec((3, BLK, 2 * BLK), lambda i, h: (0, 0, 0)),
                  pl.BlockSpec(memory_space=pltpu.SMEM),
                  pl.BlockSpec(blk, lambda i, h: (i, 0, OFF_AQ // HEAD_DIM + h)),
                  pl.BlockSpec(blk, lambda i, h: (i, 0, OFF_AK // HEAD_DIM + h)),
                  pl.BlockSpec(blk, lambda i, h: (i, 0, OFF_AV // HEAD_DIM + h))],
        out_specs=pl.BlockSpec(blk, lambda i, h: (i, 0, h)),
        out_shape=jax.ShapeDtypeStruct((b, s, WIDTH_A), BF16),
        scratch_shapes=[pltpu.VMEM((6, BLK, 2 * BLK), F32),
                        pltpu.VMEM((s, HEAD_DIM), F32),
                        pltpu.VMEM((s, HEAD_DIM), F32),
                        pltpu.VMEM((s, HEAD_DIM), F32)],
        compiler_params=_params(("arbitrary", "arbitrary"), 40),
        name="attn_prompt",
    )(buckets, rel_bias, proj3, proj3, proj3)


def _attn_sample_body(bucket_ref, relbt_ref, q_ref, kn_ref, vn_ref,
                      k1_ref, k4_ref, k16_ref, v1_ref, v4_ref, v16_ref, o_ref, bias_scr):
    relbt = relbt_ref[...]
    tile = (HEADS_A, HEAD_DIM)

    @pl.when(pl.program_id(0) == 0)
    def _():
        for br in range(3):
            bk = bucket_ref[br]
            bias = jnp.zeros((BLK,) + tile, F32)
            for kb in range(N_BUCKETS):
                col = jnp.broadcast_to(relbt[:, kb:kb + 1], tile)
                bias = jnp.where(bk == kb, col[None], bias)
            bias_scr[br] = bias

    def lane_sum(x):
        return jnp.broadcast_to(jnp.sum(x, axis=-1, keepdims=True), x.shape)

    q = q_ref[0]
    s_self = lane_sum(q * kn_ref[0]) * ATT_SCALE + jnp.broadcast_to(relbt[:, 0:1], tile)
    scores = []
    m = s_self
    for br, k_ref in enumerate((k1_ref, k4_ref, k16_ref)):
        s = lane_sum(k_ref[...] * q[None]) * ATT_SCALE + bias_scr[br]
        scores.append(s)
        m = jnp.maximum(m, jnp.max(s, axis=0))
    p_self = 3.0 * jnp.exp(s_self - m)
    l = p_self
    acc = p_self * vn_ref[0]
    for s, v_ref in zip(scores, (v1_ref, v4_ref, v16_ref)):
        p = jnp.exp(s - m[None])
        l = l + jnp.sum(p, axis=0)
        acc = acc + jnp.sum(p * v_ref[...], axis=0)
    o_ref[0] = (acc / l).astype(o_ref.dtype)


def _attn_sample(q, k_new, v_new, cache_k, cache_v, rel_bias):
    b, past = cache_k.shape[:2]
    tile = (HEADS_A, HEAD_DIM)
    buckets = jnp.asarray(np.broadcast_to(_sample_bucket_tables().reshape(3, BLK, 1, 1), (3, BLK) + tile))
    row = pl.BlockSpec((1,) + tile, lambda i: (i, 0, 0))
    views, specs = [], []
    for cache in (cache_k, cache_v):
        for dil in DILATIONS:
            views.append(cache.reshape((b, past // dil, dil) + tile))
            last = past // dil // BLK - 1
            specs.append(pl.BlockSpec((None, BLK, None) + tile,
                                      functools.partial(lambda last, i: (i, last, 0, 0, 0), last)))
    return pl.pallas_call(
        _attn_sample_body,
        grid=(b,),
        in_specs=[pl.BlockSpec((3, BLK) + tile, lambda i: (0, 0, 0, 0)),
                  pl.BlockSpec((HEADS_A, N_BUCKETS), lambda i: (0, 0)),
                  row, row, row] + specs,
        out_specs=row,
        out_shape=jax.ShapeDtypeStruct((b,) + tile, BF16),
        scratch_shapes=[pltpu.VMEM((3, BLK) + tile, F32)],
        compiler_params=_params(("arbitrary",), 40),
        name="attn_sample",
    )(buckets, rel_bias.T, q, k_new, v_new, *views)


GROUP = 1
ATTN_UNROLL = 4
FFN_TM = 1024
FFN_ROWS = 64
FFN_COLS = 512
FFN_PIECE = 256


def _dn_prompt_body(q_ref, k_ref, v_ref, z_ref, h_ref, wgate_ref, cw_ref, cs_ref, s0_ref, alog_ref, dtb_ref, nw_ref,
                    o_ref, s_out_ref,
                    s_scr, e_scr, qn_scr, kn_scr, vv_scr, g_scr, beta_scr,
                    w_scr, u_scr, qg_scr, kdt_scr, attn_scr, gl_scr, o_scr, *, tt):
    t = pl.program_id(1)
    nt = pl.num_programs(1)

    @pl.when(t == 0)
    def _():
        s_scr[...] = s0_ref[0]
        e_scr[5:8, :] = cs_ref[0]

    e_scr[8:8 + tt, 0:WIDTH_BQK] = q_ref[0]
    e_scr[8:8 + tt, WIDTH_BQK:2 * WIDTH_BQK] = k_ref[0]
    e_scr[8:8 + tt, 2 * WIDTH_BQK:CONV_DIM] = v_ref[0]

    def l2n(x):
        return x * lax.rsqrt(jnp.sum(x * x, axis=-1, keepdims=True) + EPS)

    for c0 in range(0, CONV_DIM, DK):
        cols = slice(c0, c0 + DK)
        w = cw_ref[:, cols]
        y = w[0:1, :] * e_scr[5:5 + tt, cols]
        for i in range(1, CONV_W):
            y = y + w[i:i + 1, :] * e_scr[5 + i:5 + i + tt, cols]
        y = _silu(y)
        if c0 < WIDTH_BQK:
            qn_scr[:, cols] = l2n(y) * QK_SCALE
        elif c0 < 2 * WIDTH_BQK:
            kn_scr[:, c0 - WIDTH_BQK:c0 - WIDTH_BQK + DK] = l2n(y)
        else:
            vv_scr[:, c0 - 2 * WIDTH_BQK:c0 - 2 * WIDTH_BQK + DK] = y
    e_scr[5:8, :] = e_scr[tt + 5:tt + 8, :]
    gates = jnp.dot(h_ref[0], wgate_ref[...], preferred_element_type=F32)
    beta_scr[...] = _sigmoid(gates)
    g_scr[...] = -jnp.exp(alog_ref[...]) * _softplus(gates + dtb_ref[...])

    ri = lax.broadcasted_iota(jnp.int32, (CHUNK, CHUNK), 0)
    ci = lax.broadcasted_iota(jnp.int32, (CHUNK, CHUNK), 1)
    tri = ri >= ci
    strict = ri > ci
    same_sub = (ri // SUB) == (ci // SUB)
    tril_ones = tri.astype(F32)
    nw = nw_ref[...]

    for c0 in range(0, tt // CHUNK, GROUP):
        units = []
        for c in range(c0, c0 + GROUP):
            rows = slice(c * CHUNK, (c + 1) * CHUNK)
            beta_all = beta_scr[rows, :]
            gc_all = _fdot(tril_ones, g_scr[rows, :])
            gc_all_t = gc_all.T
            for hq in range(QK_HEADS_B):
                qn = qn_scr[rows, hq * DK:(hq + 1) * DK]
                kn = kn_scr[rows, hq * DK:(hq + 1) * DK]
                kk = _bdot_nt(kn, kn)
                qk = _bdot_nt(qn, kn)
                for hv in range(2 * hq, 2 * hq + 2):
                    beta = beta_all[:, hv:hv + 1]
                    gc = gc_all[:, V_HEADS_B + hv:V_HEADS_B + hv + 1]
                    gc_row = gc_all_t[V_HEADS_B + hv:V_HEADS_B + hv + 1, :]
                    gc_last = gc_row[:, CHUNK - 1:CHUNK]
                    decay = jnp.exp(jnp.where(tri, gc - gc_row, NEG))
                    a = jnp.where(strict, beta * kk * decay, 0.0)
                    egc = jnp.exp(gc)
                    attn_scr[hv, rows, :] = (qk * decay).astype(BF16)
                    qg_scr[hv, rows, :] = (qn * egc).astype(BF16)
                    kd = kn * jnp.exp(gc_last - gc)
                    kdt_scr[hv, c * DK:(c + 1) * DK, :] = kd.T.astype(BF16)
                    gl_scr[hv, c * 8:(c + 1) * 8, :] = jnp.broadcast_to(jnp.exp(gc_last), (8, DV))
                    units.append((hv, rows, a))
            g_scr[rows, :] = gc_all
        ds = [jnp.where(same_sub, u[2], 0.0) for u in units]
        ns = [-dd for dd in ds]
        pws = ds
        for _ in range(SUB.bit_length() - 2):
            pws = [_bdot(pw, pw) for pw in pws]
            ns = [n + pw + _bdot(n, pw) for n, pw in zip(ns, pws)]
        ls = [jnp.where(same_sub, 0.0, u[2]) for u in units]
        ps = [lo + _bdot(lo, n) for lo, n in zip(ls, ns)]
        ns = [n - (p + _bdot(n, p)) for n, p in zip(ns, ps)]
        xs = []
        for hv, rows, _ in units:
            beta = beta_scr[rows, hv:hv + 1]
            kscale = beta * jnp.exp(g_scr[rows, V_HEADS_B + hv:V_HEADS_B + hv + 1])
            xs.append(jnp.concatenate([kn_scr[rows, (hv // 2) * DK:(hv // 2 + 1) * DK] * kscale,
                                       vv_scr[rows, hv * DV:(hv + 1) * DV] * beta], axis=-1))
        wus = [x + _bdot(n, x) for n, x in zip(ns, xs)]
        for wu, (hv, rows, _) in zip(wus, units):
            w_scr[hv, rows, :] = wu[:, :DK].astype(BF16)
            u_scr[hv, rows, :] = wu[:, DK:]

    heads = range(V_HEADS_B)
    for c in range(tt // CHUNK):
        rows = slice(c * CHUNK, (c + 1) * CHUNK)
        states = [s_scr[hv] for hv in heads]
        states_b = [s.astype(BF16) for s in states]
        v_news = [u_scr[hv, rows, :] - jnp.dot(w_scr[hv, rows, :], states_b[hv], preferred_element_type=F32)
                  for hv in heads]
        v_news_b = [v.astype(BF16) for v in v_news]
        for hv in heads:
            s_scr[hv] = (states[hv] * gl_scr[hv, c * 8:c * 8 + 1, :]
                         + jnp.dot(kdt_scr[hv, c * DK:(c + 1) * DK, :], v_news_b[hv],
                                   preferred_element_type=F32))
        for hv in heads:
            o_scr[rows, hv * DV:(hv + 1) * DV] = (
                jnp.dot(qg_scr[hv, rows, :], states_b[hv], preferred_element_type=F32)
                + jnp.dot(attn_scr[hv, rows, :], v_news_b[hv], preferred_element_type=F32))

    for hv in heads:
        o = o_scr[:, hv * DV:(hv + 1) * DV]
        z = z_ref[0, :, hv * DV:(hv + 1) * DV]
        o = o * lax.rsqrt(jnp.mean(o * o, axis=-1, keepdims=True) + EPS) * nw * _silu(z)
        o_ref[0, :, hv * DV:(hv + 1) * DV] = o.astype(o_ref.dtype)

    @pl.when(t == nt - 1)
    def _():
        s_out_ref[0] = s_scr[...]


def _dn_prompt(proj3, h3, w_gate, conv_state, s0, conv_w, alog_vec, dtb_vec, norm_w, tt):
    b, s, _ = proj3.shape
    body = functools.partial(_dn_prompt_body, tt=tt)
    nh = V_HEADS_B
    in_specs = [
        pl.BlockSpec((1, tt, WIDTH_BQK), lambda i, t: (i, t, OFF_BQ // WIDTH_BQK)),
        pl.BlockSpec((1, tt, WIDTH_BQK), lambda i, t: (i, t, OFF_BK // WIDTH_BQK)),
        pl.BlockSpec((1, tt, WIDTH_BV), lambda i, t: (i, t, OFF_BV // WIDTH_BV)),
        pl.BlockSpec((1, tt, WIDTH_BV), lambda i, t: (i, t, OFF_BZ // WIDTH_BV)),
        pl.BlockSpec((1, tt, D_MODEL), lambda i, t: (i, t, 0)),
        pl.BlockSpec((D_MODEL, 128), lambda i, t: (0, 0)),
        pl.BlockSpec((CONV_W, CONV_DIM), lambda i, t: (0, 0)),
        pl.BlockSpec((1, CONV_W - 1, CONV_DIM), lambda i, t: (i, 0, 0)),
        pl.BlockSpec((1, nh, DK, DV), lambda i, t: (i, 0, 0, 0)),
        pl.BlockSpec((1, 128), lambda i, t: (0, 0)),
        pl.BlockSpec((1, 128), lambda i, t: (0, 0)),
        pl.BlockSpec((1, DV), lambda i, t: (0, 0)),
    ]
    return pl.pallas_call(
        body,
        grid=(b, s // tt),
        in_specs=in_specs,
        out_specs=[pl.BlockSpec((1, tt, WIDTH_BV), lambda i, t: (i, t, 0)),
                   pl.BlockSpec((1, nh, DK, DV), lambda i, t: (i, 0, 0, 0))],
        out_shape=[jax.ShapeDtypeStruct((b, s, WIDTH_BV), BF16),
                   jax.ShapeDtypeStruct((b, nh, DK, DV), F32)],
        scratch_shapes=[pltpu.VMEM((nh, DK, DV), F32),
                        pltpu.VMEM((tt + 8, CONV_DIM), F32),
                        pltpu.VMEM((tt, WIDTH_BQK), F32),
                        pltpu.VMEM((tt, WIDTH_BQK), F32),
                        pltpu.VMEM((tt, WIDTH_BV), F32),
                        pltpu.VMEM((tt, 128), F32),
                        pltpu.VMEM((tt, 128), F32),
                        pltpu.VMEM((nh, tt, DK), BF16),
                        pltpu.VMEM((nh, tt, DV), F32),
                        pltpu.VMEM((nh, tt, DK), BF16),
                        pltpu.VMEM((nh, tt // CHUNK * DK, CHUNK), BF16),
                        pltpu.VMEM((nh, tt, CHUNK), BF16),
                        pltpu.VMEM((nh, tt // CHUNK * 8, DV), F32),
                        pltpu.VMEM((tt, WIDTH_BV), F32)],
        compiler_params=_params(("arbitrary", "arbitrary"), 48),
        name="deltanet_prompt",
    )(proj3, proj3, proj3, proj3, h3, w_gate, conv_w, conv_state, s0, alog_vec, dtb_vec, norm_w)


def _dn_sample_body(proj_ref, gates_ref, cw_ref, cs_ref, s0_ref, alog_ref, dtb_ref, nw_ref,
                    o_ref, cs_out_ref, s_out_ref):
    pre = proj_ref[0, :, OFF_BQ:OFF_BQ + CONV_DIM]
    buf = cs_ref[0]
    w = cw_ref[...]
    y = w[CONV_W - 1:CONV_W, :] * pre
    for i in range(CONV_W - 1):
        y = y + w[i:i + 1, :] * buf[i:i + 1, :]
    y = _silu(y)
    cs_out_ref[0, 0:CONV_W - 2, :] = buf[1:CONV_W - 1, :]
    cs_out_ref[0, CONV_W - 2:CONV_W - 1, :] = pre

    gates = gates_ref[0]
    beta_all = _sigmoid(gates)
    g_all = -jnp.exp(alog_ref[...]) * _softplus(gates + dtb_ref[...])
    nw = nw_ref[...]

    def l2n(x):
        return x * lax.rsqrt(jnp.sum(x * x, axis=-1, keepdims=True) + EPS)

    row8 = lax.broadcasted_iota(jnp.int32, (8, DK), 0) == 0
    for hv in range(V_HEADS_B):
        hq = hv // 2
        q = l2n(y[:, hq * DK:(hq + 1) * DK]) * QK_SCALE
        k = l2n(y[:, WIDTH_BQK + hq * DK:WIDTH_BQK + (hq + 1) * DK])
        v = y[:, 2 * WIDTH_BQK + hv * DV:2 * WIDTH_BQK + (hv + 1) * DV]
        beta = beta_all[:, hv:hv + 1]
        g = g_all[:, V_HEADS_B + hv:V_HEADS_B + hv + 1]
        eg = jnp.exp(g)
        state = s0_ref[0, hv]

        def pad8(x):
            return jnp.where(row8, jnp.broadcast_to(x, (8, x.shape[-1])), 0.0)

        v_new = v * beta - _bdot(pad8(k * (beta * eg)), state)[0:1, :]
        qk = jnp.sum(q.astype(BF16).astype(F32) * k.astype(BF16).astype(F32), axis=-1, keepdims=True)
        o = _bdot(pad8(q * eg), state)[0:1, :] + qk.astype(BF16).astype(F32) * v_new.astype(BF16).astype(F32)
        s_out_ref[0, hv] = state * eg + _bdot_tn(pad8(k), pad8(v_new))
        z = proj_ref[0, :, OFF_BZ + hv * DV:OFF_BZ + (hv + 1) * DV]
        o = o * lax.rsqrt(jnp.mean(o * o, axis=-1, keepdims=True) + EPS) * nw * _silu(z)
        o_ref[0, :, hv * DV:(hv + 1) * DV] = o.astype(o_ref.dtype)


def _dn_sample(proj, gates, conv_state, s0, conv_w, alog_vec, dtb_vec, norm_w):
    b = proj.shape[0]
    return pl.pallas_call(
        _dn_sample_body,
        grid=(b,),
        in_specs=[pl.BlockSpec((1, 1, PROJ_MAIN), lambda i: (i, 0, 0)),
                  pl.BlockSpec((1, 1, 128), lambda i: (i, 0, 0)),
                  pl.BlockSpec((CONV_W, CONV_DIM), lambda i: (0, 0)),
                  pl.BlockSpec((1, CONV_W - 1, CONV_DIM), lambda i: (i, 0, 0)),
                  pl.BlockSpec((1, V_HEADS_B, DK, DV), lambda i: (i, 0, 0, 0)),
                  pl.BlockSpec((1, 128), lambda i: (0, 0)),
                  pl.BlockSpec((1, 128), lambda i: (0, 0)),
                  pl.BlockSpec((1, DV), lambda i: (0, 0))],
        out_specs=[pl.BlockSpec((1, 1, WIDTH_BV), lambda i: (i, 0, 0)),
                   pl.BlockSpec((1, CONV_W - 1, CONV_DIM), lambda i: (i, 0, 0)),
                   pl.BlockSpec((1, V_HEADS_B, DK, DV), lambda i: (i, 0, 0, 0))],
        out_shape=[jax.ShapeDtypeStruct((b, 1, WIDTH_BV), BF16),
                   jax.ShapeDtypeStruct((b, CONV_W - 1, CONV_DIM), F32),
                   jax.ShapeDtypeStruct((b, V_HEADS_B, DK, DV), F32)],
        compiler_params=_params(("arbitrary",), 40),
        name="deltanet_sample",
    )(proj, gates, conv_w, conv_state, s0, alog_vec, dtb_vec, norm_w)


def _outproj_body(att_ref, dn_ref, wa_ref, wb_ref, x_ref, lnpost_ref, lnpre_ref, x1_ref, h2_ref):
    mix = (jnp.dot(att_ref[...], wa_ref[...], preferred_element_type=F32)
           + jnp.dot(dn_ref[...], wb_ref[...], preferred_element_type=F32))
    x1 = x_ref[...] + _rms(mix, lnpost_ref[...])
    x1_ref[...] = x1
    h2_ref[...] = _rms(x1, lnpre_ref[...]).astype(h2_ref.dtype)


def _outproj(att, dn, w, x, ln_post, ln_pre, tm):
    m, d = x.shape
    assert WIDTH_A == WIDTH_BV
    return pl.pallas_call(
        _outproj_body,
        grid=(m // tm,),
        in_specs=[pl.BlockSpec((tm, WIDTH_A), lambda i: (i, 0)),
                  pl.BlockSpec((tm, WIDTH_BV), lambda i: (i, 0)),
                  pl.BlockSpec((WIDTH_A, d), lambda i: (0, 0)),
                  pl.BlockSpec((WIDTH_BV, d), lambda i: (1, 0)),
                  pl.BlockSpec((tm, d), lambda i: (i, 0)),
                  pl.BlockSpec((1, d), lambda i: (0, 0)),
                  pl.BlockSpec((1, d), lambda i: (0, 0))],
        out_specs=[pl.BlockSpec((tm, d), lambda i: (i, 0)),
                   pl.BlockSpec((tm, d), lambda i: (i, 0))],
        out_shape=[jax.ShapeDtypeStruct((m, d), F32),
                   jax.ShapeDtypeStruct((m, d), BF16)],
        compiler_params=_params(("arbitrary",), 48),
        name="outproj",
    )(att, dn, w, w, x, ln_post, ln_pre)


def _ffn_body(h_ref, wg_ref, wv_ref, cwg_ref, cwv_ref, cbg_ref, cbv_ref, wo_ref, x1_hbm, ln_ref,
              pg_ref, pv_ref, o_ref, ng_ref, nv_ref, eg_scr, ev_scr, carry_scr, act_scr, x1_scr, x1_sem,
              *, tm, tiles_per_seq, single_token):
    i = pl.program_id(0)
    j = pl.program_id(1)
    nj = pl.num_programs(1)
    d = o_ref.shape[-1]
    tf = act_scr.shape[-1]

    def x1_copy():
        return pltpu.make_async_copy(x1_hbm.at[pl.ds(pl.multiple_of(i * tm, tm), tm), :], x1_scr, x1_sem)

    @pl.when(j == 0)
    def _():
        x1_copy().start()
        o_ref[...] = jnp.zeros_like(o_ref)

    if single_token:
        def up_conv(w_ref, cw_ref, cb_ref, prev_ref, new_ref):
            up = jnp.dot(h_ref[...], w_ref[...], preferred_element_type=F32)
            cw = cw_ref[...]
            new_ref[...] = up
            return cw[0:1, :] * prev_ref[0] + cw[1:2, :] * prev_ref[1] + cw[2:3, :] * up + cb_ref[...]

        gate = up_conv(wg_ref, cwg_ref, cbg_ref, pg_ref, ng_ref)
        val = up_conv(wv_ref, cwv_ref, cbv_ref, pv_ref, nv_ref)
        act_scr[...] = (_gelu_tanh(gate) * val).astype(BF16)
        o_ref[...] += jnp.dot(act_scr[...], wo_ref[...], preferred_element_type=F32)
    else:
        first_tile = i % tiles_per_seq == 0
        pieces = [slice(c, c + FFN_PIECE) for c in range(0, tf, FFN_PIECE)]

        def up_proj(cols):
            for w_ref, prev_ref, new_ref, e_scr, slot in ((wg_ref, pg_ref, ng_ref, eg_scr, 0),
                                                          (wv_ref, pv_ref, nv_ref, ev_scr, 1)):
                e_scr[8:8 + tm, cols] = jnp.dot(h_ref[...], w_ref[:, cols], preferred_element_type=F32)
                e_scr[6:8, cols] = jnp.where(first_tile, prev_ref[0, :, cols], carry_scr[slot, j, 6:8, cols])
                tail = e_scr[tm + 6:tm + 8, cols]
                carry_scr[slot, j, 6:8, cols] = tail
                new_ref[0, :, cols] = tail

        def conv(e_scr, cw_ref, cb_ref, cols, r):
            cw = cw_ref[:, cols]
            return (cw[0:1, :] * e_scr[6 + r:6 + r + FFN_ROWS, cols]
                    + cw[1:2, :] * e_scr[7 + r:7 + r + FFN_ROWS, cols]
                    + cw[2:3, :] * e_scr[8 + r:8 + r + FFN_ROWS, cols] + cb_ref[:, cols])

        def conv_geglu(cols):
            for r in range(0, tm, FFN_ROWS):
                act_scr[r:r + FFN_ROWS, cols] = (_gelu_tanh(conv(eg_scr, cwg_ref, cbg_ref, cols, r))
                                                 * conv(ev_scr, cwv_ref, cbv_ref, cols, r)).astype(BF16)

        def down_proj(cols):
            for n in range(0, d, FFN_COLS):
                o_ref[:, n:n + FFN_COLS] += jnp.dot(act_scr[:, cols], wo_ref[cols, n:n + FFN_COLS],
                                                    preferred_element_type=F32)

        up_proj(pieces[0])
        for c in range(len(pieces)):
            if c + 1 < len(pieces):
                up_proj(pieces[c + 1])
            conv_geglu(pieces[c])
            if c > 0:
                down_proj(pieces[c - 1])
        down_proj(pieces[-1])

    @pl.when(j == nj - 1)
    def _():
        x1_copy().wait()
        o_ref[...] = x1_scr[...] + _rms(o_ref[...], ln_ref[...])


def _ffn(h2, w_in, conv_w, conv_b, w_out, x1, ln_post, prev, tm, tf, seq_len):
    m, d = h2.shape
    single = seq_len == 1
    nj = D_FF // tf
    tiles_per_seq = 1 if single else seq_len // tm
    if single:
        prev_g = pl.BlockSpec((2, tm, tf), lambda i, j: (0, i, j))
        prev_v = pl.BlockSpec((2, tm, tf), lambda i, j: (0, i, nj + j))
        new_g = pl.BlockSpec((tm, tf), lambda i, j: (i, j))
        new_shape = jax.ShapeDtypeStruct((m, D_FF), F32)
    else:
        prev_g = pl.BlockSpec((1, 2, tf), lambda i, j: (i // tiles_per_seq, 0, j))
        prev_v = pl.BlockSpec((1, 2, tf), lambda i, j: (i // tiles_per_seq, 0, nj + j))
        new_g = pl.BlockSpec((1, 2, tf), lambda i, j: (i, 0, j))
        new_shape = jax.ShapeDtypeStruct((m // tm, 2, D_FF), F32)
    body = functools.partial(_ffn_body, tm=tm, tiles_per_seq=tiles_per_seq, single_token=single)
    return pl.pallas_call(
        body,
        grid=(m // tm, nj),
        in_specs=[pl.BlockSpec((tm, d), lambda i, j: (i, 0)),
                  pl.BlockSpec((d, tf), lambda i, j: (0, j)),
                  pl.BlockSpec((d, tf), lambda i, j: (0, nj + j)),
                  pl.BlockSpec((FFN_CONV_W, tf), lambda i, j: (0, j)),
                  pl.BlockSpec((FFN_CONV_W, tf), lambda i, j: (0, nj + j)),
                  pl.BlockSpec((1, tf), lambda i, j: (0, j)),
                  pl.BlockSpec((1, tf), lambda i, j: (0, nj + j)),
                  pl.BlockSpec((tf, d), lambda i, j: (j, 0)),
                  pl.BlockSpec(memory_space=pl.ANY),
                  pl.BlockSpec((1, d), lambda i, j: (0, 0)),
                  prev_g, prev_v],
        out_specs=[pl.BlockSpec((tm, d), lambda i, j: (i, 0),
                                **(dict(pipeline_mode=pl.Buffered(1)) if tm >= 1024 else {})), new_g, new_g],
        out_shape=[jax.ShapeDtypeStruct((m, d), F32), new_shape, new_shape],
        scratch_shapes=[pltpu.VMEM((tm + 8, tf), F32),
                        pltpu.VMEM((tm + 8, tf), F32),
                        pltpu.VMEM((2, nj, 8, tf), F32),
                        pltpu.VMEM((tm, tf), BF16),
                        pltpu.VMEM((tm, d), F32),
                        pltpu.SemaphoreType.DMA(())],
        compiler_params=_params(("arbitrary", "arbitrary"), 56),
        name="convffn",
    )(h2, w_in, w_in, conv_w, conv_w, conv_b, conv_b, w_out, x1, ln_post, prev, prev)


def _cache_shift_body(ck_ref, cv_ref, ck_next_ref, cv_next_ref, nk_ref, nv_ref, ok_ref, ov_ref, *, tr):
    last = pl.program_id(1) == pl.num_programs(1) - 1
    for c_ref, nxt_ref, n_ref, o_ref in ((ck_ref, ck_next_ref, nk_ref, ok_ref),
                                         (cv_ref, cv_next_ref, nv_ref, ov_ref)):
        o_ref[0, 0:tr - 1] = c_ref[0, 1:tr]
        o_ref[0, tr - 1] = jnp.where(last, n_ref[0, 0], nxt_ref[0, 0])


def _cache_shift(cache_k, cache_v, new_k, new_v, tr):
    nb, rows, nh, dh = cache_k.shape
    main = pl.BlockSpec((1, tr, nh, dh), lambda b, i: (b, i, 0, 0))
    nxt = pl.BlockSpec((1, 1, nh, dh), lambda b, i: (b, jnp.minimum((i + 1) * tr, rows - 1), 0, 0))
    new = pl.BlockSpec((1, 1, nh, dh), lambda b, i: (b, 0, 0, 0))
    shape = jax.ShapeDtypeStruct(cache_k.shape, cache_k.dtype)
    return pl.pallas_call(
        functools.partial(_cache_shift_body, tr=tr),
        grid=(nb, rows // tr),
        in_specs=[main, main, nxt, nxt, new, new],
        out_specs=[main, main],
        out_shape=[shape, shape],
        compiler_params=_params(("arbitrary", "arbitrary"), 40),
        name="cache_shift",
    )(cache_k, cache_v, cache_k, cache_v, new_k, new_v)


def _lane_vec(values, offset):
    return jnp.zeros((1, 128), F32).at[0, offset:offset + V_HEADS_B].set(values.astype(F32))


def kernel(x_prompt, x_sample, cache_win_k, cache_win_v, state_dn_conv, state_dn_rec, state_ffn_conv,
           rel_bias, ln_mix_pre, w_in, dn_conv_w, dn_A_log, dn_dt_bias, dn_norm_w, w_out, ln_mix_post,
           ln_ffn_pre, w_ffn_in, ffn_conv_w, ffn_conv_b, w_ffn_out, ln_ffn_post):
    bp, sp, d = x_prompt.shape
    bs = x_sample.shape[0]
    l = 0

    w_main = w_in[l].astype(BF16)
    w_gate = jnp.pad(w_main[:, PROJ_MAIN:], ((0, 0), (0, 128 - 2 * V_HEADS_B)))
    wo = w_out[l].astype(BF16)
    wf_in = w_ffn_in[l].astype(BF16)
    wf_out = w_ffn_out[l].astype(BF16)
    ln1 = ln_mix_pre[l][None, :]
    ln2 = ln_mix_post[l][None, :]
    ln3 = ln_ffn_pre[l][None, :]
    ln4 = ln_ffn_post[l][None, :]
    conv_w = dn_conv_w[l]
    alog_vec = _lane_vec(dn_A_log[l], V_HEADS_B)
    dtb_vec = _lane_vec(dn_dt_bias[l], V_HEADS_B)
    norm_w = dn_norm_w[l][None, :]
    fcw = ffn_conv_w[l]
    fcb = ffn_conv_b[l][None, :]

    xp = x_prompt.reshape(bp * sp, d)
    hp = _rmsnorm(xp, ln1, 512)
    proj_p = _matmul(hp, w_main, 1024, 1024, "inproj_prompt", n=PROJ_MAIN)
    proj3 = proj_p.reshape(bp, sp, PROJ_MAIN)
    att_p = _attn_prompt(proj3, rel_bias)
    dn_p, p_dn_rec = _dn_prompt(
        proj3, hp.reshape(bp, sp, d), w_gate,
        jnp.zeros((bp, CONV_W - 1, CONV_DIM), F32), jnp.zeros((bp, V_HEADS_B, DK, DV), F32),
        conv_w, alog_vec, dtb_vec, norm_w, 256)
    x1_p, h2_p = _outproj(att_p.reshape(bp * sp, WIDTH_A), dn_p.reshape(bp * sp, WIDTH_BV),
                          wo, xp, ln2, ln3, 512)
    y_p, fc_g, fc_v = _ffn(h2_p, wf_in, fcw, fcb, wf_out, x1_p, ln4,
                           jnp.zeros((bp, FFN_CONV_W - 1, 2 * D_FF), F32), FFN_TM, 512, sp)
    keep = min(MAX_DISTANCE, sp)
    p_win_k = proj3[:, sp - keep:, OFF_AK:OFF_AK + WIDTH_A].reshape(1, bp, keep, HEADS_A, HEAD_DIM)
    p_win_v = proj3[:, sp - keep:, OFF_AV:OFF_AV + WIDTH_A].reshape(1, bp, keep, HEADS_A, HEAD_DIM)
    p_dn_conv = proj3[:, sp - (CONV_W - 1):, OFF_BQ:OFF_BQ + CONV_DIM][None]
    tiles = sp // FFN_TM
    p_ffn_conv = jnp.concatenate([fc_g[tiles - 1::tiles], fc_v[tiles - 1::tiles]], axis=-1)[None]

    xs = x_sample.reshape(bs, d)
    hs = _rmsnorm(xs, ln1, bs)
    proj_s = _matmul(hs, w_main, bs, 1024, "inproj_sample", n=PROJ_MAIN)
    gates_s = _matmul(hs, w_gate, bs, 128, "gates_sample")
    past = cache_win_k.shape[2]
    ck = cache_win_k[l]
    cv = cache_win_v[l]
    new_k = proj_s[:, OFF_AK:OFF_AK + WIDTH_A]
    new_v = proj_s[:, OFF_AV:OFF_AV + WIDTH_A]
    new_q = proj_s[:, OFF_AQ:OFF_AQ + WIDTH_A].reshape(bs, HEADS_A, HEAD_DIM)
    new_k = new_k.reshape(bs, HEADS_A, HEAD_DIM)
    new_v = new_v.reshape(bs, HEADS_A, HEAD_DIM)
    att_s = _attn_sample(new_q, new_k, new_v, ck, cv, rel_bias)
    s_win_k, s_win_v = _cache_shift(ck, cv, new_k[:, None], new_v[:, None], 512)
    dn_s, s_dn_conv, s_dn_rec = _dn_sample(proj_s[:, None], gates_s[:, None], state_dn_conv[l],
                                           state_dn_rec[l], conv_w, alog_vec, dtb_vec, norm_w)
    x1_s, h2_s = _outproj(att_s.reshape(bs, WIDTH_A), dn_s.reshape(bs, WIDTH_BV),
                          wo, xs, ln2, ln3, bs)
    prev_s = jnp.swapaxes(state_ffn_conv[l], 0, 1)
    y_s, up_g, up_v = _ffn(h2_s, wf_in, fcw, fcb, wf_out, x1_s, ln4, prev_s, bs, 512, 1)
    s_ffn_conv = jnp.stack([prev_s[1], jnp.concatenate([up_g, up_v], axis=-1)], axis=1)[None]

    return (y_p.reshape(bp, sp, d), y_s.reshape(bs, 1, d),
            p_win_k, p_win_v, p_dn_conv, p_dn_rec[None], p_ffn_conv,
            s_win_k[None], s_win_v[None], s_dn_conv[None], s_dn_rec[None], s_ffn_conv)
```

```python
import functools
import math

import numpy as np
import jax
import jax.numpy as jnp
from jax import lax
from jax.experimental import pallas as pl
from jax.experimental.pallas import tpu as pltpu

F32 = jnp.float32
BF16 = jnp.bfloat16

D_MODEL = 2048
HEAD_DIM = 128
WIDTH_A = 1024
HEADS_A = 8
DILATIONS = (1, 4, 16)
BLK = 128
N_BUCKETS = 32
MAX_DISTANCE = 2048
DK = 128
DV = 128
V_HEADS_B = 8
QK_HEADS_B = 4
WIDTH_BQK = 512
WIDTH_BV = 1024
CONV_W = 4
CONV_DIM = 2048
CHUNK = 128
SUB = 64
D_FF = 5632
FFN_CONV_W = 3
EPS = 1e-6
NEG = -1e30
ATT_SCALE = HEAD_DIM ** -0.5
QK_SCALE = DK ** -0.5

OFF_AQ, OFF_AK, OFF_AV = 0, 1024, 2048
OFF_BQ, OFF_BK, OFF_BV, OFF_BZ = 3072, 3584, 4096, 5120
OFF_GATES = 6144
PROJ_MAIN = 6144

MIB = 2 ** 20


def _params(semantics, vmem_mib):
    return pltpu.CompilerParams(dimension_semantics=semantics, vmem_limit_bytes=vmem_mib * MIB)


def _bdot(a, b):
    return jnp.dot(a.astype(BF16), b.astype(BF16), preferred_element_type=F32)


def _bdot_nt(a, b):
    return lax.dot_general(a.astype(BF16), b.astype(BF16), (((1,), (1,)), ((), ())),
                           preferred_element_type=F32)


def _bdot_tn(a, b):
    return lax.dot_general(a.astype(BF16), b.astype(BF16), (((0,), (0,)), ((), ())),
                           preferred_element_type=F32)


def _fdot(a, b):
    return jnp.dot(a, b, preferred_element_type=F32, precision=lax.Precision.HIGHEST)


def _silu(x):
    return x * (1.0 / (1.0 + jnp.exp(-x)))


def _sigmoid(x):
    return 1.0 / (1.0 + jnp.exp(-x))


def _softplus(x):
    return jnp.maximum(x, 0.0) + jnp.log(1.0 + jnp.exp(-jnp.abs(x)))


def _gelu_tanh(x):
    c = math.sqrt(2.0 / math.pi)
    half = 0.5 * x
    return half + half * jnp.tanh(x * (c + (c * 0.044715) * (x * x)))


def _rms(x, w):
    return x * lax.rsqrt(jnp.mean(x * x, axis=-1, keepdims=True) + EPS) * w


def _rmsnorm_body(x_ref, w_ref, o_ref):
    o_ref[...] = _rms(x_ref[...], w_ref[...]).astype(o_ref.dtype)


def _rmsnorm(x, w, tm):
    m, d = x.shape
    return pl.pallas_call(
        _rmsnorm_body,
        grid=(m // tm,),
        in_specs=[pl.BlockSpec((tm, d), lambda i: (i, 0)),
                  pl.BlockSpec((1, d), lambda i: (0, 0))],
        out_specs=pl.BlockSpec((tm, d), lambda i: (i, 0)),
        out_shape=jax.ShapeDtypeStruct((m, d), BF16),
        compiler_params=_params(("arbitrary",), 40),
        name="rmsnorm",
    )(x, w)


def _matmul_body(x_ref, w_ref, o_ref):
    o_ref[...] = jnp.dot(x_ref[...], w_ref[...], preferred_element_type=F32)


def _matmul(x, w, tm, tn, name, n=None):
    m, k = x.shape
    n = w.shape[1] if n is None else n
    return pl.pallas_call(
        _matmul_body,
        grid=(n // tn, m // tm),
        in_specs=[pl.BlockSpec((tm, k), lambda j, i: (i, 0)),
                  pl.BlockSpec((k, tn), lambda j, i: (0, j))],
        out_specs=pl.BlockSpec((tm, tn), lambda j, i: (i, j)),
        out_shape=jax.ShapeDtypeStruct((m, n), F32),
        compiler_params=_params(("arbitrary", "arbitrary"), 48),
        name=name,
    )(x, w)


def _rel_bucket_np(dist):
    dist = np.asarray(dist, np.int64)
    max_exact = N_BUCKETS // 2
    d = np.maximum(dist, 1).astype(np.float64)
    val = np.log(d / max_exact) / math.log(MAX_DISTANCE / max_exact) * (N_BUCKETS - max_exact)
    frac = np.abs(val - np.round(val))
    near = (frac < 2e-5) &(dist >= max_exact) & (dist != max_exact) & (dist < MAX_DISTANCE)
    assert not near.any(), "distance on a bucket boundary"
    val = np.where(dist == max_exact, 0.0, val)
    large = np.minimum(max_exact + np.trunc(val).astype(np.int64), N_BUCKETS - 1)
    return np.where(dist < max_exact, dist, large).astype(np.int32)


def _prompt_bucket_tables():
    qi = np.arange(BLK)[:, None]
    kj = np.arange(2 * BLK)[None, :]
    delta = BLK + qi - kj
    inwin = (delta >= 0) & (delta <= BLK)
    tabs = []
    for dil in DILATIONS:
        b = _rel_bucket_np(np.clip(delta, 0, BLK) * dil)
        tabs.append(np.where(inwin, b, -1))
    return np.stack(tabs).astype(np.int32)


def _sample_bucket_tables():
    j = BLK - np.arange(BLK)
    return np.stack([_rel_bucket_np(j * dil)[None, :] for dil in DILATIONS]).astype(np.int32)


def _attn_prompt_body(bucket_ref, relb_ref, q_ref, k_ref, v_ref, o_ref, wk_ref, wv_ref,
                      bias_scr, acc_scr, m_scr, l_scr):
    h = pl.program_id(1)
    s, keep = k_ref.shape[1], wk_ref.shape[1] // HEADS_A
    wk_ref[0, pl.ds(h, keep, stride=HEADS_A), :] = k_ref[0, s - keep:s, :]
    wv_ref[0, pl.ds(h, keep, stride=HEADS_A), :] = v_ref[0, s - keep:s, :]
    col = lax.broadcasted_iota(jnp.int32, (BLK, 2 * BLK), 1)
    tables = _prompt_bucket_tables()
    for br in range(3):
        bk = bucket_ref[br]
        bias = jnp.zeros((BLK, 2 * BLK), F32)
        for kb in sorted(set(tables[br].ravel().tolist()) - {-1}):
            bias = jnp.where(bk == kb, relb_ref[kb, h], bias)
        full = jnp.where(bk >= 0, bias, NEG)
        bias_scr[2 * br] = full
        bias_scr[2 * br + 1] = jnp.where(col >= BLK, full, NEG)

    def run_branch(br, dil, is_first_branch, is_last_branch):
        shift = int(math.log2(dil))
        span = BLK * dil
        stride = None if dil == 1 else dil

        def rows(start):
            return pl.ds(start, BLK, stride=stride) if stride else pl.ds(start, BLK)

        nb = q_ref.shape[1] // span
        run_len = min(nb, ATTN_UNROLL)
        runs_per_it = ATTN_UNROLL // run_len
        runs_per_res = nb // run_len
        starts_at_zero = runs_per_res == 1

        def tasks(it, carry):
            q_starts, firsts, qs_, ks_, vs_ = [], [], [], [], []
            for rr in range(runs_per_it):
                ri = it * runs_per_it + rr
                n0 = (ri % runs_per_res) * run_len
                if dil == 1:
                    base = pl.multiple_of(n0 * span, BLK)
                else:
                    base = n0 * span + ri // runs_per_res
                first = jnp.where(n0 == 0, 1, 0)
                starts = [base + u * span for u in range(run_len)]
                kb = [k_ref[0, rows(st), :].astype(BF16) for st in starts]
                vb = [v_ref[0, rows(st), :].astype(BF16) for st in starts]
                if starts_at_zero:
                    k_prev, v_prev = None, None
                else:
                    p_start = base - span * (1 - first)
                    if dil == 1:
                        p_start = pl.multiple_of(p_start, BLK)
                    k_prev = k_ref[0, rows(p_start), :].astype(BF16)
                    v_prev = v_ref[0, rows(p_start), :].astype(BF16)
                for u, st in enumerate(starts):
                    q_starts.append(st)
                    qs_.append(q_ref[0, rows(st), :].astype(BF16))
                    kp, vp = (k_prev, v_prev) if u == 0 else (kb[u - 1], vb[u - 1])
                    if kp is None:
                        firsts.append(None)
                        ks_.append(kb[u])
                        vs_.append(vb[u])
                    else:
                        firsts.append(first if u == 0 else 0)
                        ks_.append(jnp.concatenate([kp, kb[u]], axis=0))
                        vs_.append(jnp.concatenate([vp, vb[u]], axis=0))
            if not is_first_branch:
                runs = [(m_scr[rows(qs), :], l_scr[rows(qs), :], acc_scr[rows(qs), :]) for qs in q_starts]
            ss = [_bdot_nt(q, k) * ATT_SCALE
                  + (bias_scr[2 * br, :, BLK:] if f is None else bias_scr[2 * br + f])
                  for q, k, f in zip(qs_, ks_, firsts)]
            ms = [jnp.max(s, axis=-1, keepdims=True) for s in ss]
            ps_ = [jnp.exp(s - m) for s, m in zip(ss, ms)]
            accs = [_bdot(p, jnp.concatenate([v, jnp.ones_like(v)], axis=1)) for p, v in zip(ps_, vs_)]
            outs = []
            for u in range(ATTN_UNROLL):
                m_b = jnp.broadcast_to(ms[u], (BLK, HEAD_DIM))
                l_b = accs[u][:, HEAD_DIM:]
                acc_t = accs[u][:, :HEAD_DIM]
                if not is_first_branch:
                    m_run, l_run, acc_run = runs[u]
                    m_new = jnp.maximum(m_run, m_b)
                    a = jnp.exp(m_run - m_new)
                    b = jnp.exp(m_b - m_new)
                    acc_t = a * acc_run + b * acc_t
                    l_b = a * l_run + b * l_b
                    m_b = m_new
                outs.append((m_b, l_b, acc_t))
            for qs, (m_b, l_b, acc_t) in zip(q_starts, outs):
                if is_last_branch:
                    o_ref[0, rows(qs), :] = (acc_t / l_b).astype(o_ref.dtype)
                else:
                    m_scr[rows(qs), :] = m_b
                    l_scr[rows(qs), :] = l_b
                    acc_scr[rows(qs), :] = acc_t
            return carry

        lax.fori_loop(0, nb * dil // ATTN_UNROLL, tasks, 0)

    run_branch(2, 16, True, False)
    run_branch(1, 4, False, False)
    run_branch(0, 1, False, True)


def _attn_prompt(proj3, rel_bias, keep):
    b, s, _ = proj3.shape
    buckets = jnp.asarray(_prompt_bucket_tables())
    blk = (1, s, HEAD_DIM)
    win = pl.BlockSpec((1, keep * HEADS_A, HEAD_DIM), lambda i, h: (i, 0, 0), pipeline_mode=pl.Buffered(1))
    win_shape = jax.ShapeDtypeStruct((b, keep * HEADS_A, HEAD_DIM), F32)
    return pl.pallas_call(
        _attn_prompt_body,
        grid=(b, HEADS_A),
        in_specs=[pl.BlockSpec((3, BLK, 2 * BLK), lambda i, h: (0, 0, 0)),
                  pl.BlockSpec(memory_space=pltpu.SMEM),
                  pl.BlockSpec(blk, lambda i, h: (i, 0, OFF_AQ // HEAD_DIM + h)),
                  pl.BlockSpec(blk, lambda i, h: (i, 0, OFF_AK // HEAD_DIM + h)),
                  pl.BlockSpec(blk, lambda i, h: (i, 0, OFF_AV // HEAD_DIM + h))],
        out_specs=[pl.BlockSpec(blk, lambda i, h: (i, 0, h)), win, win],
        out_shape=[jax.ShapeDtypeStruct((b, s, WIDTH_A), BF16), win_shape, win_shape],
        scratch_shapes=[pltpu.VMEM((6, BLK, 2 * BLK), F32),
                        pltpu.VMEM((s, HEAD_DIM), F32),
                        pltpu.VMEM((s, HEAD_DIM), F32),
                        pltpu.VMEM((s, HEAD_DIM), F32)],
        compiler_params=_params(("arbitrary", "arbitrary"), 48),
        name="attn_prompt",
    )(buckets, rel_bias, proj3, proj3, proj3)


def _attn_sample_body(bucket_ref, relbt_ref, q_ref, kn_ref, vn_ref,
                      k1_ref, k4_ref, k16_ref, v1_ref, v4_ref, v16_ref, o_ref, bias_scr):
    relbt = relbt_ref[...]
    tile = (HEADS_A, HEAD_DIM)

    @pl.when(pl.program_id(0) == 0)
    def _():
        for br in range(3):
            bk = bucket_ref[br]
            bias = jnp.zeros((BLK,) + tile, F32)
            for kb in range(N_BUCKETS):
                col = jnp.broadcast_to(relbt[:, kb:kb + 1], tile)
                bias = jnp.where(bk == kb, col[None], bias)
            bias_scr[br] = bias

    def lane_sum(x):
        return jnp.broadcast_to(jnp.sum(x, axis=-1, keepdims=True), x.shape)

    q = q_ref[0]
    s_self = lane_sum(q * kn_ref[0]) * ATT_SCALE + jnp.broadcast_to(relbt[:, 0:1], tile)
    scores = []
    m = s_self
    for br, k_ref in enumerate((k1_ref, k4_ref, k16_ref)):
        s = lane_sum(k_ref[...] * q[None]) * ATT_SCALE + bias_scr[br]
        scores.append(s)
        m = jnp.maximum(m, jnp.max(s, axis=0))
    p_self = 3.0 * jnp.exp(s_self - m)
    l = p_self
    acc = p_self * vn_ref[0]
    for s, v_ref in zip(scores, (v1_ref, v4_ref, v16_ref)):
        p = jnp.exp(s - m[None])
        l = l + jnp.sum(p, axis=0)
        acc = acc + jnp.sum(p * v_ref[...], axis=0)
    o_ref[0] = (acc / l).astype(o_ref.dtype)


def _attn_sample(q, k_new, v_new, cache_k, cache_v, rel_bias):
    b, past = cache_k.shape[:2]
    tile = (HEADS_A, HEAD_DIM)
    buckets = jnp.asarray(np.broadcast_to(_sample_bucket_tables().reshape(3, BLK, 1, 1), (3, BLK) + tile))
    row = pl.BlockSpec((1,) + tile, lambda i: (i, 0, 0))
    views, specs = [], []
    for cache in (cache_k, cache_v):
        for dil in DILATIONS:
            views.append(cache.reshape((b, past // dil, dil) + tile))
            last = past // dil // BLK - 1
            specs.append(pl.BlockSpec((None, BLK, None) + tile,
                                      functools.partial(lambda last, i: (i, last, 0, 0, 0), last)))
    return pl.pallas_call(
        _attn_sample_body,
        grid=(b,),
        in_specs=[pl.BlockSpec((3, BLK) + tile, lambda i: (0, 0, 0, 0)),
                  pl.BlockSpec((HEADS_A, N_BUCKETS), lambda i: (0, 0)),
                  row, row, row] + specs,
        out_specs=row,
        out_shape=jax.ShapeDtypeStruct((b,) + tile, BF16),
        scratch_shapes=[pltpu.VMEM((3, BLK) + tile, F32)],
        compiler_params=_params(("arbitrary",), 40),
        name="attn_sample",
    )(buckets, rel_bias.T, q, k_new, v_new, *views)


GROUP = 1
ATTN_UNROLL = 8
FFN_TM = 1024
FFN_ROWS = 64
FFN_COLS = 512
FFN_PIECE = 256


def _dn_prompt_body(q_ref, k_ref, v_ref, z_ref, h_ref, wgate_ref, cw_ref, cs_ref, s0_ref, alog_ref, dtb_ref, nw_ref,
                    o_ref, s_out_ref,
                    s_scr, e_scr, qn_scr, kn_scr, vv_scr, g_scr, beta_scr,
                    w_scr, u_scr, qg_scr, kdt_scr, attn_scr, gl_scr, o_scr, *, tt):
    t = pl.program_id(1)
    nt = pl.num_programs(1)

    @pl.when(t == 0)
    def _():
        s_scr[...] = s0_ref[0]
        e_scr[5:8, :] = cs_ref[0]

    e_scr[8:8 + tt, 0:WIDTH_BQK] = q_ref[0]
    e_scr[8:8 + tt, WIDTH_BQK:2 * WIDTH_BQK] = k_ref[0]
    e_scr[8:8 + tt, 2 * WIDTH_BQK:CONV_DIM] = v_ref[0]

    def l2n(x):
        return x * lax.rsqrt(jnp.sum(x * x, axis=-1, keepdims=True) + EPS)

    for c0 in range(0, CONV_DIM, DK):
        cols = slice(c0, c0 + DK)
        w = cw_ref[:, cols]
        y = w[0:1, :] * e_scr[5:5 + tt, cols]
        for i in range(1, CONV_W):
            y = y + w[i:i + 1, :] * e_scr[5 + i:5 + i + tt, cols]
        y = _silu(y)
        if c0 < WIDTH_BQK:
            qn_scr[:, cols] = l2n(y) * QK_SCALE
        elif c0 < 2 * WIDTH_BQK:
            kn_scr[:, c0 - WIDTH_BQK:c0 - WIDTH_BQK + DK] = l2n(y)
        else:
            vv_scr[:, c0 - 2 * WIDTH_BQK:c0 - 2 * WIDTH_BQK + DK] = y
    e_scr[5:8, :] = e_scr[tt + 5:tt + 8, :]
    gates = jnp.dot(h_ref[0], wgate_ref[...], preferred_element_type=F32)
    beta_scr[...] = _sigmoid(gates)
    g_scr[...] = -jnp.exp(alog_ref[...]) * _softplus(gates + dtb_ref[...])

    ri = lax.broadcasted_iota(jnp.int32, (CHUNK, CHUNK), 0)
    ci = lax.broadcasted_iota(jnp.int32, (CHUNK, CHUNK), 1)
    tri = ri >= ci
    strict = ri > ci
    same_sub = (ri // SUB) == (ci // SUB)
    tril_ones = tri.astype(F32)
    nw = nw_ref[...]

    for c0 in range(0, tt // CHUNK, GROUP):
        units = []
        for c in range(c0, c0 + GROUP):
            rows = slice(c * CHUNK, (c + 1) * CHUNK)
            beta_all = beta_scr[rows, :]
            gc_all = _fdot(tril_ones, g_scr[rows, :])
            gc_all_t = gc_all.T
            for hq in range(QK_HEADS_B):
                qn = qn_scr[rows, hq * DK:(hq + 1) * DK]
                kn = kn_scr[rows, hq * DK:(hq + 1) * DK]
                kk = _bdot_nt(kn, kn)
                qk = _bdot_nt(qn, kn)
                for hv in range(2 * hq, 2 * hq + 2):
                    beta = beta_all[:, hv:hv + 1]
                    gc = gc_all[:, V_HEADS_B + hv:V_HEADS_B + hv + 1]
                    gc_row = gc_all_t[V_HEADS_B + hv:V_HEADS_B + hv + 1, :]
                    gc_last = gc_row[:, CHUNK - 1:CHUNK]
                    decay = jnp.exp(jnp.where(tri, gc - gc_row, NEG))
                    a = jnp.where(strict, beta * kk * decay, 0.0)
                    egc = jnp.exp(gc)
                    attn_scr[hv, rows, :] = (qk * decay).astype(BF16)
                    qg_scr[hv, rows, :] = (qn * egc).astype(BF16)
                    kd = kn * jnp.exp(gc_last - gc)
                    kdt_scr[hv, c * DK:(c + 1) * DK, :] = kd.T.astype(BF16)
                    gl_scr[hv, c * 8:(c + 1) * 8, :] = jnp.broadcast_to(jnp.exp(gc_last), (8, DV))
                    units.append((hv, rows, a))
            g_scr[rows, :] = gc_all
        ds = [jnp.where(same_sub, u[2], 0.0) for u in units]
        ns = [-dd for dd in ds]
        pws = ds
        for _ in range(SUB.bit_length() - 2):
            pws = [_bdot(pw, pw) for pw in pws]
            ns = [n + pw + _bdot(n, pw) for n, pw in zip(ns, pws)]
        ls = [jnp.where(same_sub, 0.0, u[2]) for u in units]
        ps = [lo + _bdot(lo, n) for lo, n in zip(ls, ns)]
        ns = [n - (p + _bdot(n, p)) for n, p in zip(ns, ps)]
        xs = []
        for hv, rows, _ in units:
            beta = beta_scr[rows, hv:hv + 1]
            kscale = beta * jnp.exp(g_scr[rows, V_HEADS_B + hv:V_HEADS_B + hv + 1])
            xs.append(jnp.concatenate([kn_scr[rows, (hv // 2) * DK:(hv // 2 + 1) * DK] * kscale,
                                       vv_scr[rows, hv * DV:(hv + 1) * DV] * beta], axis=-1))
        wus = [x + _bdot(n, x) for n, x in zip(ns, xs)]
        for wu, (hv, rows, _) in zip(wus, units):
            w_scr[hv, rows, :] = wu[:, :DK].astype(BF16)
            u_scr[hv, rows, :] = wu[:, DK:]

    heads = range(V_HEADS_B)
    for c in range(tt // CHUNK):
        rows = slice(c * CHUNK, (c + 1) * CHUNK)
        states = [s_scr[hv] for hv in heads]
        states_b = [s.astype(BF16) for s in states]
        v_news = [u_scr[hv, rows, :] - jnp.dot(w_scr[hv, rows, :], states_b[hv], preferred_element_type=F32)
                  for hv in heads]
        v_news_b = [v.astype(BF16) for v in v_news]
        for hv in heads:
            s_scr[hv] = (states[hv] * gl_scr[hv, c * 8:c * 8 + 1, :]
                         + jnp.dot(kdt_scr[hv, c * DK:(c + 1) * DK, :], v_news_b[hv],
                                   preferred_element_type=F32))
        for hv in heads:
            o_scr[rows, hv * DV:(hv + 1) * DV] = (
                jnp.dot(qg_scr[hv, rows, :], states_b[hv], preferred_element_type=F32)
                + jnp.dot(attn_scr[hv, rows, :], v_news_b[hv], preferred_element_type=F32))

    for hv in heads:
        o = o_scr[:, hv * DV:(hv + 1) * DV]
        z = z_ref[0, :, hv * DV:(hv + 1) * DV]
        o = o * lax.rsqrt(jnp.mean(o * o, axis=-1, keepdims=True) + EPS) * nw * _silu(z)
        o_ref[0, :, hv * DV:(hv + 1) * DV] = o.astype(o_ref.dtype)

    @pl.when(t == nt - 1)
    def _():
        s_out_ref[0] = s_scr[...]


def _dn_prompt(proj3, h3, w_gate, conv_state, s0, conv_w, alog_vec, dtb_vec, norm_w, tt):
    b, s, _ = proj3.shape
    body = functools.partial(_dn_prompt_body, tt=tt)
    nh = V_HEADS_B
    in_specs = [
        pl.BlockSpec((1, tt, WIDTH_BQK), lambda i, t: (i, t, OFF_BQ // WIDTH_BQK)),
        pl.BlockSpec((1, tt, WIDTH_BQK), lambda i, t: (i, t, OFF_BK // WIDTH_BQK)),
        pl.BlockSpec((1, tt, WIDTH_BV), lambda i, t: (i, t, OFF_BV // WIDTH_BV)),
        pl.BlockSpec((1, tt, WIDTH_BV), lambda i, t: (i, t, OFF_BZ // WIDTH_BV)),
        pl.BlockSpec((1, tt, D_MODEL), lambda i, t: (i, t, 0)),
        pl.BlockSpec((D_MODEL, 128), lambda i, t: (0, 0)),
        pl.BlockSpec((CONV_W, CONV_DIM), lambda i, t: (0, 0)),
        pl.BlockSpec((1, CONV_W - 1, CONV_DIM), lambda i, t: (i, 0, 0)),
        pl.BlockSpec((1, nh, DK, DV), lambda i, t: (i, 0, 0, 0)),
        pl.BlockSpec((1, 128), lambda i, t: (0, 0)),
        pl.BlockSpec((1, 128), lambda i, t: (0, 0)),
        pl.BlockSpec((1, DV), lambda i, t: (0, 0)),
    ]
    return pl.pallas_call(
        body,
        grid=(b, s // tt),
        in_specs=in_specs,
        out_specs=[pl.BlockSpec((1, tt, WIDTH_BV), lambda i, t: (i, t, 0)),
                   pl.BlockSpec((1, nh, DK, DV), lambda i, t: (i, 0, 0, 0))],
        out_shape=[jax.ShapeDtypeStruct((b, s, WIDTH_BV), BF16),
                   jax.ShapeDtypeStruct((b, nh, DK, DV), F32)],
        scratch_shapes=[pltpu.VMEM((nh, DK, DV), F32),
                        pltpu.VMEM((tt + 8, CONV_DIM), F32),
                        pltpu.VMEM((tt, WIDTH_BQK), F32),
                        pltpu.VMEM((tt, WIDTH_BQK), F32),
                        pltpu.VMEM((tt, WIDTH_BV), F32),
                        pltpu.VMEM((tt, 128), F32),
                        pltpu.VMEM((tt, 128), F32),
                        pltpu.VMEM((nh, tt, DK), BF16),
                        pltpu.VMEM((nh, tt, DV), F32),
                        pltpu.VMEM((nh, tt, DK), BF16),
                        pltpu.VMEM((nh, tt // CHUNK * DK, CHUNK), BF16),
                        pltpu.VMEM((nh, tt, CHUNK), BF16),
                        pltpu.VMEM((nh, tt // CHUNK * 8, DV), F32),
                        pltpu.VMEM((tt, WIDTH_BV), F32)],
        compiler_params=_params(("arbitrary", "arbitrary"), 48),
        name="deltanet_prompt",
    )(proj3, proj3, proj3, proj3, h3, w_gate, conv_w, conv_state, s0, alog_vec, dtb_vec, norm_w)


def _dn_sample_body(proj_ref, gates_ref, cw_ref, cs_ref, s0_ref, alog_ref, dtb_ref, nw_ref,
                    o_ref, cs_out_ref, s_out_ref):
    pre = proj_ref[0, :, OFF_BQ:OFF_BQ + CONV_DIM]
    buf = cs_ref[0]
    w = cw_ref[...]
    y = w[CONV_W - 1:CONV_W, :] * pre
    for i in range(CONV_W - 1):
        y = y + w[i:i + 1, :] * buf[i:i + 1, :]
    y = _silu(y)
    cs_out_ref[0, 0:CONV_W - 2, :] = buf[1:CONV_W - 1, :]
    cs_out_ref[0, CONV_W - 2:CONV_W - 1, :] = pre

    gates = gates_ref[0]
    beta_all = _sigmoid(gates)
    g_all = -jnp.exp(alog_ref[...]) * _softplus(gates + dtb_ref[...])
    nw = nw_ref[...]

    def l2n(x):
        return x * lax.rsqrt(jnp.sum(x * x, axis=-1, keepdims=True) + EPS)

    row8 = lax.broadcasted_iota(jnp.int32, (8, DK), 0) == 0
    for hv in range(V_HEADS_B):
        hq = hv // 2
        q = l2n(y[:, hq * DK:(hq + 1) * DK]) * QK_SCALE
        k = l2n(y[:, WIDTH_BQK + hq * DK:WIDTH_BQK + (hq + 1) * DK])
        v = y[:, 2 * WIDTH_BQK + hv * DV:2 * WIDTH_BQK + (hv + 1) * DV]
        beta = beta_all[:, hv:hv + 1]
        g = g_all[:, V_HEADS_B + hv:V_HEADS_B + hv + 1]
        eg = jnp.exp(g)
        state = s0_ref[0, hv]

        def pad8(x):
            return jnp.where(row8, jnp.broadcast_to(x, (8, x.shape[-1])), 0.0)

        v_new = v * beta - _bdot(pad8(k * (beta * eg)), state)[0:1, :]
        qk = jnp.sum(q.astype(BF16).astype(F32) * k.astype(BF16).astype(F32), axis=-1, keepdims=True)
        o = _bdot(pad8(q * eg), state)[0:1, :] + qk.astype(BF16).astype(F32) * v_new.astype(BF16).astype(F32)
        s_out_ref[0, hv] = state * eg + _bdot_tn(pad8(k), pad8(v_new))
        z = proj_ref[0, :, OFF_BZ + hv * DV:OFF_BZ + (hv + 1) * DV]
        o = o * lax.rsqrt(jnp.mean(o * o, axis=-1, keepdims=True) + EPS) * nw * _silu(z)
        o_ref[0, :, hv * DV:(hv + 1) * DV] = o.astype(o_ref.dtype)


def _dn_sample(proj, gates, conv_state, s0, conv_w, alog_vec, dtb_vec, norm_w):
    b = proj.shape[0]
    return pl.pallas_call(
        _dn_sample_body,
        grid=(b,),
        in_specs=[pl.BlockSpec((1, 1, PROJ_MAIN), lambda i: (i, 0, 0)),
                  pl.BlockSpec((1, 1, 128), lambda i: (i, 0, 0)),
                  pl.BlockSpec((CONV_W, CONV_DIM), lambda i: (0, 0)),
                  pl.BlockSpec((1, CONV_W - 1, CONV_DIM), lambda i: (i, 0, 0)),
                  pl.BlockSpec((1, V_HEADS_B, DK, DV), lambda i: (i, 0, 0, 0)),
                  pl.BlockSpec((1, 128), lambda i: (0, 0)),
                  pl.BlockSpec((1, 128), lambda i: (0, 0)),
                  pl.BlockSpec((1, DV), lambda i: (0, 0))],
        out_specs=[pl.BlockSpec((1, 1, WIDTH_BV), lambda i: (i, 0, 0)),
                   pl.BlockSpec((1, CONV_W - 1, CONV_DIM), lambda i: (i, 0, 0)),
                   pl.BlockSpec((1, V_HEADS_B, DK, DV), lambda i: (i, 0, 0, 0))],
        out_shape=[jax.ShapeDtypeStruct((b, 1, WIDTH_BV), BF16),
                   jax.ShapeDtypeStruct((b, CONV_W - 1, CONV_DIM), F32),
                   jax.ShapeDtypeStruct((b, V_HEADS_B, DK, DV), F32)],
        compiler_params=_params(("arbitrary",), 40),
        name="deltanet_sample",
    )(proj, gates, conv_w, conv_state, s0, alog_vec, dtb_vec, norm_w)


def _outproj_body(att_ref, dn_ref, wa_ref, wb_ref, x_ref, lnpost_ref, lnpre_ref, x1_ref, h2_ref):
    mix = (jnp.dot(att_ref[...], wa_ref[...], preferred_element_type=F32)
           + jnp.dot(dn_ref[...], wb_ref[...], preferred_element_type=F32))
    x1 = x_ref[...] + _rms(mix, lnpost_ref[...])
    x1_ref[...] = x1
    h2_ref[...] = _rms(x1, lnpre_ref[...]).astype(h2_ref.dtype)


def _outproj(att, dn, w, x, ln_post, ln_pre, tm):
    m, d = x.shape
    assert WIDTH_A == WIDTH_BV
    return pl.pallas_call(
        _outproj_body,
        grid=(m // tm,),
        in_specs=[pl.BlockSpec((tm, WIDTH_A), lambda i: (i, 0)),
                  pl.BlockSpec((tm, WIDTH_BV), lambda i: (i, 0)),
                  pl.BlockSpec((WIDTH_A, d), lambda i: (0, 0)),
                  pl.BlockSpec((WIDTH_BV, d), lambda i: (1, 0)),
                  pl.BlockSpec((tm, d), lambda i: (i, 0)),
                  pl.BlockSpec((1, d), lambda i: (0, 0)),
                  pl.BlockSpec((1, d), lambda i: (0, 0))],
        out_specs=[pl.BlockSpec((tm, d), lambda i: (i, 0)),
                   pl.BlockSpec((tm, d), lambda i: (i, 0))],
        out_shape=[jax.ShapeDtypeStruct((m, d), F32),
                   jax.ShapeDtypeStruct((m, d), BF16)],
        compiler_params=_params(("arbitrary",), 48),
        name="outproj",
    )(att, dn, w, w, x, ln_post, ln_pre)


def _ffn_body(h_ref, wg_ref, wv_ref, cwg_ref, cwv_ref, cbg_ref, cbv_ref, wo_prev_ref, wo_last_ref, x1_hbm, ln_ref,
              pg_ref, pv_ref, o_ref, ng_ref, nv_ref, eg_scr, ev_scr, carry_scr, act_scr, x1_scr, x1_sem,
              *, tm, tiles_per_seq, single_token):
    i = pl.program_id(0)
    j = pl.program_id(1)
    nj = pl.num_programs(1)
    d = o_ref.shape[-1]
    tf = act_scr.shape[-1]
    cur = j % 2
    act_cur = act_scr.at[cur]
    act_prev = act_scr.at[1 - cur]

    def x1_copy():
        return pltpu.make_async_copy(x1_hbm.at[pl.ds(pl.multiple_of(i * tm, tm), tm), :], x1_scr, x1_sem)

    def down_proj(act_ref, wo_ref):
        for n in range(0, d, FFN_COLS):
            o_ref[:, n:n + FFN_COLS] += jnp.dot(act_ref[...], wo_ref[:, n:n + FFN_COLS],
                                                preferred_element_type=F32)

    @pl.when(j == 0)
    def _():
        x1_copy().start()
        o_ref[...] = jnp.zeros_like(o_ref)
        act_prev[...] = jnp.zeros_like(act_prev)

    if single_token:
        def up_conv(w_ref, cw_ref, cb_ref, prev_ref, new_ref):
            up = jnp.dot(h_ref[...], w_ref[...], preferred_element_type=F32)
            cw = cw_ref[...]
            new_ref[...] = up
            return cw[0:1, :] * prev_ref[0] + cw[1:2, :] * prev_ref[1] + cw[2:3, :] * up + cb_ref[...]

        gate = up_conv(wg_ref, cwg_ref, cbg_ref, pg_ref, ng_ref)
        val = up_conv(wv_ref, cwv_ref, cbv_ref, pv_ref, nv_ref)
        act_cur[...] = (_gelu_tanh(gate) * val).astype(BF16)
        down_proj(act_prev, wo_prev_ref)
    else:
        first_tile = i % tiles_per_seq == 0
        pieces = [slice(c, c + FFN_PIECE) for c in range(0, tf, FFN_PIECE)]

        def up_proj(cols):
            for w_ref, prev_ref, new_ref, e_scr, slot in ((wg_ref, pg_ref, ng_ref, eg_scr, 0),
                                                          (wv_ref, pv_ref, nv_ref, ev_scr, 1)):
                e_scr[8:8 + tm, cols] = jnp.dot(h_ref[...], w_ref[:, cols], preferred_element_type=F32)
                e_scr[6:8, cols] = jnp.where(first_tile, prev_ref[0, :, cols], carry_scr[slot, j, 6:8, cols])
                tail = e_scr[tm + 6:tm + 8, cols]
                carry_scr[slot, j, 6:8, cols] = tail
                new_ref[0, :, cols] = tail

        def conv(e_scr, cw_ref, cb_ref, cols, r):
            cw = cw_ref[:, cols]
            return (cw[0:1, :] * e_scr[6 + r:6 + r + FFN_ROWS, cols]
                    + cw[1:2, :] * e_scr[7 + r:7 + r + FFN_ROWS, cols]
                    + cw[2:3, :] * e_scr[8 + r:8 + r + FFN_ROWS, cols] + cb_ref[:, cols])

        def conv_geglu(cols):
            for r in range(0, tm, FFN_ROWS):
                act_cur[r:r + FFN_ROWS, cols] = (_gelu_tanh(conv(eg_scr, cwg_ref, cbg_ref, cols, r))
                                                 * conv(ev_scr, cwv_ref, cbv_ref, cols, r)).astype(BF16)

        up_proj(pieces[0])
        for c in range(1, len(pieces)):
            up_proj(pieces[c])
            conv_geglu(pieces[c - 1])
        down_proj(act_prev, wo_prev_ref)
        conv_geglu(pieces[-1])

    @pl.when(j == nj - 1)
    def _():
        down_proj(act_cur, wo_last_ref)
        x1_copy().wait()
        o_ref[...] = x1_scr[...] + _rms(o_ref[...], ln_ref[...])


def _ffn(h2, w_in, conv_w, conv_b, w_out, x1, ln_post, prev, tm, tf, seq_len):
    m, d = h2.shape
    single = seq_len == 1
    nj = D_FF // tf
    tiles_per_seq = 1 if single else seq_len // tm
    if single:
        prev_g = pl.BlockSpec((2, tm, tf), lambda i, j: (0, i, j))
        prev_v = pl.BlockSpec((2, tm, tf), lambda i, j: (0, i, nj + j))
        new_g = pl.BlockSpec((tm, tf), lambda i, j: (i, j))
        new_shape = jax.ShapeDtypeStruct((m, D_FF), F32)
    else:
        prev_g = pl.BlockSpec((1, 2, tf), lambda i, j: (i // tiles_per_seq, 0, j))
        prev_v = pl.BlockSpec((1, 2, tf), lambda i, j: (i // tiles_per_seq, 0, nj + j))
        new_g = pl.BlockSpec((1, 2, tf), lambda i, j: (i, 0, j))
        new_shape = jax.ShapeDtypeStruct((m // tm, 2, D_FF), F32)
    body = functools.partial(_ffn_body, tm=tm, tiles_per_seq=tiles_per_seq, single_token=single)
    once = dict(pipeline_mode=pl.Buffered(1)) if tm >= 1024 else {}
    return pl.pallas_call(
        body,
        grid=(m // tm, nj),
        in_specs=[pl.BlockSpec((tm, d), lambda i, j: (i, 0), **once),
                  pl.BlockSpec((d, tf), lambda i, j: (0, j)),
                  pl.BlockSpec((d, tf), lambda i, j: (0, nj + j)),
                  pl.BlockSpec((FFN_CONV_W, tf), lambda i, j: (0, j)),
                  pl.BlockSpec((FFN_CONV_W, tf), lambda i, j: (0, nj + j)),
                  pl.BlockSpec((1, tf), lambda i, j: (0, j)),
                  pl.BlockSpec((1, tf), lambda i, j: (0, nj + j)),
                  pl.BlockSpec((tf, d), lambda i, j: (jnp.maximum(j - 1, 0), 0)),
                  pl.BlockSpec((tf, d), lambda i, j: (nj - 1, 0), pipeline_mode=pl.Buffered(1)),
                  pl.BlockSpec(memory_space=pl.ANY),
                  pl.BlockSpec((1, d), lambda i, j: (0, 0)),
                  prev_g, prev_v],
        out_specs=[pl.BlockSpec((tm, d), lambda i, j: (i, 0), **once), new_g, new_g],
        out_shape=[jax.ShapeDtypeStruct((m, d), F32), new_shape, new_shape],
        scratch_shapes=[pltpu.VMEM((tm + 8, tf), F32),
                        pltpu.VMEM((tm + 8, tf), F32),
                        pltpu.VMEM((2, nj, 8, tf), F32),
                        pltpu.VMEM((2, tm, tf), BF16),
                        pltpu.VMEM((tm, d), F32),
                        pltpu.SemaphoreType.DMA(())],
        compiler_params=_params(("arbitrary", "arbitrary"), 57),
        name="convffn",
    )(h2, w_in, w_in, conv_w, conv_w, conv_b, conv_b, w_out, w_out, x1, ln_post, prev, prev)


def _cache_shift_body(ck_ref, cv_ref, ck_next_ref, cv_next_ref, nk_ref, nv_ref, ok_ref, ov_ref, *, tr):
    last = pl.program_id(1) == pl.num_programs(1) - 1
    for c_ref, nxt_ref, n_ref, o_ref in ((ck_ref, ck_next_ref, nk_ref, ok_ref),
                                         (cv_ref, cv_next_ref, nv_ref, ov_ref)):
        o_ref[0, 0:tr - 1] = c_ref[0, 1:tr]
        o_ref[0, tr - 1] = jnp.where(last, n_ref[0, 0], nxt_ref[0, 0])


def _cache_shift(cache_k, cache_v, new_k, new_v, tr):
    nb, rows, nh, dh = cache_k.shape
    main = pl.BlockSpec((1, tr, nh, dh), lambda b, i: (b, i, 0, 0))
    nxt = pl.BlockSpec((1, 1, nh, dh), lambda b, i: (b, jnp.minimum((i + 1) * tr, rows - 1), 0, 0))
    new = pl.BlockSpec((1, 1, nh, dh), lambda b, i: (b, 0, 0, 0))
    shape = jax.ShapeDtypeStruct(cache_k.shape, cache_k.dtype)
    return pl.pallas_call(
        functools.partial(_cache_shift_body, tr=tr),
        grid=(nb, rows // tr),
        in_specs=[main, main, nxt, nxt, new, new],
        out_specs=[main, main],
        out_shape=[shape, shape],
        compiler_params=_params(("arbitrary", "arbitrary"), 40),
        name="cache_shift",
    )(cache_k, cache_v, cache_k, cache_v, new_k, new_v)


def _lane_vec(values, offset):
    return jnp.zeros((1, 128), F32).at[0, offset:offset + V_HEADS_B].set(values.astype(F32))


def kernel(x_prompt, x_sample, cache_win_k, cache_win_v, state_dn_conv, state_dn_rec, state_ffn_conv,
           rel_bias, ln_mix_pre, w_in, dn_conv_w, dn_A_log, dn_dt_bias, dn_norm_w, w_out, ln_mix_post,
           ln_ffn_pre, w_ffn_in, ffn_conv_w, ffn_conv_b, w_ffn_out, ln_ffn_post):
    bp, sp, d = x_prompt.shape
    bs = x_sample.shape[0]
    l = 0

    w_main = w_in[l].astype(BF16)
    w_gate = jnp.pad(w_main[:, PROJ_MAIN:], ((0, 0), (0, 128 - 2 * V_HEADS_B)))
    wo = w_out[l].astype(BF16)
    wf_in = w_ffn_in[l].astype(BF16)
    wf_out = w_ffn_out[l].astype(BF16)
    ln1 = ln_mix_pre[l][None, :]
    ln2 = ln_mix_post[l][None, :]
    ln3 = ln_ffn_pre[l][None, :]
    ln4 = ln_ffn_post[l][None, :]
    conv_w = dn_conv_w[l]
    alog_vec = _lane_vec(dn_A_log[l], V_HEADS_B)
    dtb_vec = _lane_vec(dn_dt_bias[l], V_HEADS_B)
    norm_w = dn_norm_w[l][None, :]
    fcw = ffn_conv_w[l]
    fcb = ffn_conv_b[l][None, :]

    xp = x_prompt.reshape(bp * sp, d)
    hp = _rmsnorm(xp, ln1, 512)
    proj_p = _matmul(hp, w_main, 1024, 1024, "inproj_prompt", n=PROJ_MAIN)
    proj3 = proj_p.reshape(bp, sp, PROJ_MAIN)
    keep = min(MAX_DISTANCE, sp)
    att_p, win_k, win_v = _attn_prompt(proj3, rel_bias, keep)
    dn_p, p_dn_rec = _dn_prompt(
        proj3, hp.reshape(bp, sp, d), w_gate,
        jnp.zeros((bp, CONV_W - 1, CONV_DIM), F32), jnp.zeros((bp, V_HEADS_B, DK, DV), F32),
        conv_w, alog_vec, dtb_vec, norm_w, 256)
    x1_p, h2_p = _outproj(att_p.reshape(bp * sp, WIDTH_A), dn_p.reshape(bp * sp, WIDTH_BV),
                          wo, xp, ln2, ln3, 512)
    y_p, fc_g, fc_v = _ffn(h2_p, wf_in, fcw, fcb, wf_out, x1_p, ln4,
                           jnp.zeros((bp, FFN_CONV_W - 1, 2 * D_FF), F32), FFN_TM, 512, sp)
    p_win_k = win_k.reshape(1, bp, keep, HEADS_A, HEAD_DIM)
    p_win_v = win_v.reshape(1, bp, keep, HEADS_A, HEAD_DIM)
    p_dn_conv = proj3[:, sp - (CONV_W - 1):, OFF_BQ:OFF_BQ + CONV_DIM][None]
    tiles = sp // FFN_TM
    p_ffn_conv = jnp.concatenate([fc_g[tiles - 1::tiles], fc_v[tiles - 1::tiles]], axis=-1)[None]

    xs = x_sample.reshape(bs, d)
    hs = _rmsnorm(xs, ln1, bs)
    proj_s = _matmul(hs, w_main, bs, 1024, "inproj_sample", n=PROJ_MAIN)
    gates_s = _matmul(hs, w_gate, bs, 128, "gates_sample")
    past = cache_win_k.shape[2]
    ck = cache_win_k[l]
    cv = cache_win_v[l]
    new_k = proj_s[:, OFF_AK:OFF_AK + WIDTH_A]
    new_v = proj_s[:, OFF_AV:OFF_AV + WIDTH_A]
    new_q = proj_s[:, OFF_AQ:OFF_AQ + WIDTH_A].reshape(bs, HEADS_A, HEAD_DIM)
    new_k = new_k.reshape(bs, HEADS_A, HEAD_DIM)
    new_v = new_v.reshape(bs, HEADS_A, HEAD_DIM)
    att_s = _attn_sample(new_q, new_k, new_v, ck, cv, rel_bias)
    s_win_k, s_win_v = _cache_shift(ck, cv, new_k[:, None], new_v[:, None], 512)
    dn_s, s_dn_conv, s_dn_rec = _dn_sample(proj_s[:, None], gates_s[:, None], state_dn_conv[l],
                                           state_dn_rec[l], conv_w, alog_vec, dtb_vec, norm_w)
    x1_s, h2_s = _outproj(att_s.reshape(bs, WIDTH_A), dn_s.reshape(bs, WIDTH_BV),
                          wo, xs, ln2, ln3, bs)
    prev_s = jnp.swapaxes(state_ffn_conv[l], 0, 1)
    y_s, up_g, up_v = _ffn(h2_s, wf_in, fcw, fcb, wf_out, x1_s, ln4, prev_s, bs, 512, 1)
    s_ffn_conv = jnp.stack([prev_s[1], jnp.concatenate([up_g, up_v], axis=-1)], axis=1)[None]

    return (y_p.reshape(bp, sp, d), y_s.reshape(bs, 1, d),
            p_win_k, p_win_v, p_dn_conv, p_dn_rec[None], p_ffn_conv,
            s_win_k[None], s_win_v[None], s_dn_conv[None], s_dn_rec[None], s_ffn_conv)
```

```python
import functools
import math

import numpy as np
import jax
import jax.numpy as jnp
from jax import lax
from jax.experimental import pallas as pl
from jax.experimental.pallas import tpu as pltpu

F32 = jnp.float32
BF16 = jnp.bfloat16

D_MODEL = 2048
HEAD_DIM = 128
WIDTH_A = 1024
HEADS_A = 8
DILATIONS = (1, 4, 16)
BLK = 128
N_BUCKETS = 32
MAX_DISTANCE = 2048
DK = 128
DV = 128
V_HEADS_B = 8
QK_HEADS_B = 4
WIDTH_BQK = 512
WIDTH_BV = 1024
CONV_W = 4
CONV_DIM = 2048
CHUNK = 128
SUB = 64
D_FF = 5632
FFN_CONV_W = 3
EPS = 1e-6
NEG = -1e30
ATT_SCALE = HEAD_DIM ** -0.5
QK_SCALE = DK ** -0.5

OFF_AQ, OFF_AK, OFF_AV = 0, 1024, 2048
OFF_BQ, OFF_BK, OFF_BV, OFF_BZ = 3072, 3584, 4096, 5120
OFF_GATES = 6144
PROJ_MAIN = 6144

MIB = 2 ** 20


def _params(semantics, vmem_mib):
    return pltpu.CompilerParams(dimension_semantics=semantics, vmem_limit_bytes=vmem_mib * MIB)


def _bdot(a, b):
    return jnp.dot(a.astype(BF16), b.astype(BF16), preferred_element_type=F32)


def _bdot_nt(a, b):
    return lax.dot_general(a.astype(BF16), b.astype(BF16), (((1,), (1,)), ((), ())),
                           preferred_element_type=F32)


def _bdot_tn(a, b):
    return lax.dot_general(a.astype(BF16), b.astype(BF16), (((0,), (0,)), ((), ())),
                           preferred_element_type=F32)


def _fdot(a, b):
    return jnp.dot(a, b, preferred_element_type=F32, precision=lax.Precision.HIGHEST)


def _silu(x):
    return x * (1.0 / (1.0 + jnp.exp(-x)))


def _sigmoid(x):
    return 1.0 / (1.0 + jnp.exp(-x))


def _softplus(x):
    return jnp.maximum(x, 0.0) + jnp.log(1.0 + jnp.exp(-jnp.abs(x)))


def _gelu_tanh(x):
    c = math.sqrt(2.0 / math.pi)
    half = 0.5 * x
    return half + half * jnp.tanh(x * (c + (c * 0.044715) * (x * x)))


def _rms(x, w):
    return x * lax.rsqrt(jnp.mean(x * x, axis=-1, keepdims=True) + EPS) * w


def _rmsnorm_body(x_ref, w_ref, o_ref):
    o_ref[...] = _rms(x_ref[...], w_ref[...]).astype(o_ref.dtype)


def _rmsnorm(x, w, tm):
    m, d = x.shape
    return pl.pallas_call(
        _rmsnorm_body,
        grid=(m // tm,),
        in_specs=[pl.BlockSpec((tm, d), lambda i: (i, 0)),
                  pl.BlockSpec((1, d), lambda i: (0, 0))],
        out_specs=pl.BlockSpec((tm, d), lambda i: (i, 0)),
        out_shape=jax.ShapeDtypeStruct((m, d), BF16),
        compiler_params=_params(("arbitrary",), 40),
        name="rmsnorm",
    )(x, w)


def _matmul_body(x_ref, w_ref, o_ref):
    o_ref[...] = jnp.dot(x_ref[...], w_ref[...], preferred_element_type=F32)


def _matmul(x, w, tm, tn, name, n=None, n0=0):
    m, k = x.shape
    n = w.shape[1] if n is None else n
    j0 = n0 // tn
    return pl.pallas_call(
        _matmul_body,
        grid=((n - n0) // tn, m // tm),
        in_specs=[pl.BlockSpec((tm, k), lambda j, i: (i, 0)),
                  pl.BlockSpec((k, tn), lambda j, i: (0, j0 + j))],
        out_specs=pl.BlockSpec((tm, tn), lambda j, i: (i, j)),
        out_shape=jax.ShapeDtypeStruct((m, n - n0), F32),
        compiler_params=_params(("arbitrary", "arbitrary"), 48),
        name=name,
    )(x, w)


def _matmul_heads_body(x_ref, w_ref, o_ref):
    acc = jnp.dot(x_ref[...], w_ref[...], preferred_element_type=F32)
    for hh in range(o_ref.shape[1]):
        o_ref[0, hh] = acc[:, hh * HEAD_DIM:(hh + 1) * HEAD_DIM]


def _matmul_heads(x, w, tm, groups, name):
    m, k = x.shape
    return pl.pallas_call(
        _matmul_heads_body,
        grid=(groups, m // tm),
        in_specs=[pl.BlockSpec((tm, k), lambda j, i: (i, 0)),
                  pl.BlockSpec((k, WIDTH_A), lambda j, i: (0, j))],
        out_specs=pl.BlockSpec((1, HEADS_A, tm, HEAD_DIM), lambda j, i: (j, 0, i, 0)),
        out_shape=jax.ShapeDtypeStruct((groups, HEADS_A, m, HEAD_DIM), F32),
        compiler_params=_params(("arbitrary", "arbitrary"), 48),
        name=name,
    )(x, w)


def _rel_bucket_np(dist):
    dist = np.asarray(dist, np.int64)
    max_exact = N_BUCKETS // 2
    d = np.maximum(dist, 1).astype(np.float64)
    val = np.log(d / max_exact) / math.log(MAX_DISTANCE / max_exact) * (N_BUCKETS - max_exact)
    frac = np.abs(val - np.round(val))
    near = (frac < 2e-5) &(dist >= max_exact) & (dist != max_exact) & (dist < MAX_DISTANCE)
    assert not near.any(), "distance on a bucket boundary"
    val = np.where(dist == max_exact, 0.0, val)
    large = np.minimum(max_exact + np.trunc(val).astype(np.int64), N_BUCKETS - 1)
    return np.where(dist < max_exact, dist, large).astype(np.int32)


def _prompt_bucket_tables():
    qi = np.arange(BLK)[:, None]
    kj = np.arange(2 * BLK)[None, :]
    delta = BLK + qi - kj
    inwin = (delta >= 0) & (delta <= BLK)
    tabs = []
    for dil in DILATIONS:
        b = _rel_bucket_np(np.clip(delta, 0, BLK) * dil)
        tabs.append(np.where(inwin, b, -1))
    return np.stack(tabs).astype(np.int32)


def _sample_bucket_tables():
    j = BLK - np.arange(BLK)
    return np.stack([_rel_bucket_np(j * dil)[None, :] for dil in DILATIONS]).astype(np.int32)


def _attn_prompt_body(bucket_ref, relb_ref, q_ref, k_ref, v_ref, o_ref, wk_ref, wv_ref,
                      bias_scr, acc_scr, m_scr, l_scr):
    h = pl.program_id(1)
    s, keep = k_ref.shape[1], wk_ref.shape[1] // HEADS_A
    wk_ref[0, pl.ds(h, keep, stride=HEADS_A), :] = k_ref[0, s - keep:s, :]
    wv_ref[0, pl.ds(h, keep, stride=HEADS_A), :] = v_ref[0, s - keep:s, :]
    col = lax.broadcasted_iota(jnp.int32, (BLK, 2 * BLK), 1)
    tables = _prompt_bucket_tables()
    for br in range(3):
        bk = bucket_ref[br]
        bias = jnp.zeros((BLK, 2 * BLK), F32)
        for kb in sorted(set(tables[br].ravel().tolist()) - {-1}):
            bias = jnp.where(bk == kb, relb_ref[kb, h], bias)
        full = jnp.where(bk >= 0, bias, NEG)
        bias_scr[2 * br] = full
        bias_scr[2 * br + 1] = jnp.where(col >= BLK, full, NEG)

    def run_branch(br, dil, is_first_branch, is_last_branch):
        shift = int(math.log2(dil))
        span = BLK * dil
        stride = None if dil == 1 else dil

        def rows(start):
            return pl.ds(start, BLK, stride=stride) if stride else pl.ds(start, BLK)

        nb = q_ref.shape[1] // span
        run_len = min(nb, ATTN_UNROLL)
        runs_per_it = ATTN_UNROLL // run_len
        runs_per_res = nb // run_len
        starts_at_zero = runs_per_res == 1

        def tasks(it, carry):
            q_starts, firsts, qs_, ks_, vs_ = [], [], [], [], []
            for rr in range(runs_per_it):
                ri = it * runs_per_it + rr
                n0 = (ri % runs_per_res) * run_len
                if dil == 1:
                    base = pl.multiple_of(n0 * span, BLK)
                else:
                    base = n0 * span + ri // runs_per_res
                first = jnp.where(n0 == 0, 1, 0)
                starts = [base + u * span for u in range(run_len)]
                kb = [k_ref[0, rows(st), :].astype(BF16) for st in starts]
                vb = [v_ref[0, rows(st), :].astype(BF16) for st in starts]
                if starts_at_zero:
                    k_prev, v_prev = None, None
                else:
                    p_start = base - span * (1 - first)
                    if dil == 1:
                        p_start = pl.multiple_of(p_start, BLK)
                    k_prev = k_ref[0, rows(p_start), :].astype(BF16)
                    v_prev = v_ref[0, rows(p_start), :].astype(BF16)
                for u, st in enumerate(starts):
                    q_starts.append(st)
                    qs_.append(q_ref[0, rows(st), :].astype(BF16))
                    kp, vp = (k_prev, v_prev) if u == 0 else (kb[u - 1], vb[u - 1])
                    if kp is None:
                        firsts.append(None)
                        ks_.append(kb[u])
                        vs_.append(vb[u])
                    else:
                        firsts.append(first if u == 0 else 0)
                        ks_.append(jnp.concatenate([kp, kb[u]], axis=0))
                        vs_.append(jnp.concatenate([vp, vb[u]], axis=0))
            if not is_first_branch:
                runs = [(m_scr[rows(qs), :], l_scr[rows(qs), :], acc_scr[rows(qs), :]) for qs in q_starts]
            ss = [_bdot_nt(q, k) * ATT_SCALE
                  + (bias_scr[2 * br, :, BLK:] if f is None else bias_scr[2 * br + f])
                  for q, k, f in zip(qs_, ks_, firsts)]
            ms = [jnp.max(s, axis=-1, keepdims=True) for s in ss]
            ps_ = [jnp.exp(s - m) for s, m in zip(ss, ms)]
            accs = [_bdot(p, jnp.concatenate([v, jnp.ones_like(v)], axis=1)) for p, v in zip(ps_, vs_)]
            outs = []
            for u in range(ATTN_UNROLL):
                m_b = jnp.broadcast_to(ms[u], (BLK, HEAD_DIM))
                l_b = accs[u][:, HEAD_DIM:]
                acc_t = accs[u][:, :HEAD_DIM]
                if not is_first_branch:
                    m_run, l_run, acc_run = runs[u]
                    m_new = jnp.maximum(m_run, m_b)
                    a = jnp.exp(m_run - m_new)
                    b = jnp.exp(m_b - m_new)
                    acc_t = a * acc_run + b * acc_t
                    l_b = a * l_run + b * l_b
                    m_b = m_new
                outs.append((m_b, l_b, acc_t))
            for qs, (m_b, l_b, acc_t) in zip(q_starts, outs):
                if is_last_branch:
                    o_ref[0, rows(qs), :] = (acc_t / l_b).astype(o_ref.dtype)
                else:
                    m_scr[rows(qs), :] = m_b
                    l_scr[rows(qs), :] = l_b
                    acc_scr[rows(qs), :] = acc_t
            return carry

        lax.fori_loop(0, nb * dil // ATTN_UNROLL, tasks, 0)

    run_branch(2, 16, True, False)
    run_branch(1, 4, False, False)
    run_branch(0, 1, False, True)


def _attn_prompt(qkv, rel_bias, keep):
    _, _, b, s, _ = qkv.shape
    buckets = jnp.asarray(_prompt_bucket_tables())
    blk = (1, s, HEAD_DIM)
    head = lambda g: pl.BlockSpec((None, None, 1, s, HEAD_DIM), lambda i, h: (g, h, i, 0, 0))
    win = pl.BlockSpec((1, keep * HEADS_A, HEAD_DIM), lambda i, h: (i, 0, 0), pipeline_mode=pl.Buffered(1))
    win_shape = jax.ShapeDtypeStruct((b, keep * HEADS_A, HEAD_DIM), F32)
    return pl.pallas_call(
        _attn_prompt_body,
        grid=(b, HEADS_A),
        in_specs=[pl.BlockSpec((3, BLK, 2 * BLK), lambda i, h: (0, 0, 0)),
                  pl.BlockSpec(memory_space=pltpu.SMEM),
                  head(0), head(1), head(2)],
        out_specs=[pl.BlockSpec(blk, lambda i, h: (i, 0, h)), win, win],
        out_shape=[jax.ShapeDtypeStruct((b, s, WIDTH_A), BF16), win_shape, win_shape],
        scratch_shapes=[pltpu.VMEM((6, BLK, 2 * BLK), F32),
                        pltpu.VMEM((s, HEAD_DIM), F32),
                        pltpu.VMEM((s, HEAD_DIM), F32),
                        pltpu.VMEM((s, HEAD_DIM), F32)],
        compiler_params=_params(("arbitrary", "arbitrary"), 48),
        name="attn_prompt",
    )(buckets, rel_bias, qkv, qkv, qkv)


def _attn_sample_body(bucket_ref, relbt_ref, q_ref, kn_ref, vn_ref,
                      k1_ref, k4_ref, k16_ref, v1_ref, v4_ref, v16_ref, o_ref, bias_scr):
    relbt = relbt_ref[...]
    tile = (HEADS_A, HEAD_DIM)

    @pl.when(pl.program_id(0) == 0)
    def _():
        for br in range(3):
            bk = bucket_ref[br]
            bias = jnp.zeros((BLK,) + tile, F32)
            for kb in range(N_BUCKETS):
                col = jnp.broadcast_to(relbt[:, kb:kb + 1], tile)
                bias = jnp.where(bk == kb, col[None], bias)
            bias_scr[br] = bias

    def lane_sum(x):
        return jnp.broadcast_to(jnp.sum(x, axis=-1, keepdims=True), x.shape)

    q = q_ref[0]
    s_self = lane_sum(q * kn_ref[0]) * ATT_SCALE + jnp.broadcast_to(relbt[:, 0:1], tile)
    scores = []
    m = s_self
    for br, k_ref in enumerate((k1_ref, k4_ref, k16_ref)):
        s = lane_sum(k_ref[...] * q[None]) * ATT_SCALE + bias_scr[br]
        scores.append(s)
        m = jnp.maximum(m, jnp.max(s, axis=0))
    p_self = 3.0 * jnp.exp(s_self - m)
    l = p_self
    acc = p_self * vn_ref[0]
    for s, v_ref in zip(scores, (v1_ref, v4_ref, v16_ref)):
        p = jnp.exp(s - m[None])
        l = l + jnp.sum(p, axis=0)
        acc = acc + jnp.sum(p * v_ref[...], axis=0)
    o_ref[0] = (acc / l).astype(o_ref.dtype)


def _attn_sample(q, k_new, v_new, cache_k, cache_v, rel_bias):
    b, past = cache_k.shape[:2]
    tile = (HEADS_A, HEAD_DIM)
    buckets = jnp.asarray(np.broadcast_to(_sample_bucket_tables().reshape(3, BLK, 1, 1), (3, BLK) + tile))
    row = pl.BlockSpec((1,) + tile, lambda i: (i, 0, 0))
    views, specs = [], []
    for cache in (cache_k, cache_v):
        for dil in DILATIONS:
            views.append(cache.reshape((b, past // dil, dil) + tile))
            last = past // dil // BLK - 1
            specs.append(pl.BlockSpec((None, BLK, None) + tile,
                                      functools.partial(lambda last, i: (i, last, 0, 0, 0), last)))
    return pl.pallas_call(
        _attn_sample_body,
        grid=(b,),
        in_specs=[pl.BlockSpec((3, BLK) + tile, lambda i: (0, 0, 0, 0)),
                  pl.BlockSpec((HEADS_A, N_BUCKETS), lambda i: (0, 0)),
                  row, row, row] + specs,
        out_specs=row,
        out_shape=jax.ShapeDtypeStruct((b,) + tile, BF16),
        scratch_shapes=[pltpu.VMEM((3, BLK) + tile, F32)],
        compiler_params=_params(("arbitrary",), 40),
        name="attn_sample",
    )(buckets, rel_bias.T, q, k_new, v_new, *views)


GROUP = 1
ATTN_UNROLL = 8
FFN_TM = 1024
FFN_TF = 512
FFN_ROWS = 64
FFN_COLS = 512
FFN_PIECE = 256


def _dn_prompt_body(q_ref, k_ref, v_ref, z_ref, h_ref, wgate_ref, cw_ref, cs_ref, s0_ref, alog_ref, dtb_ref, nw_ref,
                    o_ref, s_out_ref,
                    s_scr, e_scr, qn_scr, kn_scr, vv_scr, g_scr, beta_scr,
                    w_scr, u_scr, qg_scr, kdt_scr, attn_scr, gl_scr, o_scr, *, tt):
    t = pl.program_id(1)
    nt = pl.num_programs(1)

    @pl.when(t == 0)
    def _():
        s_scr[...] = s0_ref[0]
        e_scr[5:8, :] = cs_ref[0]

    e_scr[8:8 + tt, 0:WIDTH_BQK] = q_ref[0]
    e_scr[8:8 + tt, WIDTH_BQK:2 * WIDTH_BQK] = k_ref[0]
    e_scr[8:8 + tt, 2 * WIDTH_BQK:CONV_DIM] = v_ref[0]

    def l2n(x):
        return x * lax.rsqrt(jnp.sum(x * x, axis=-1, keepdims=True) + EPS)

    for c0 in range(0, CONV_DIM, DK):
        cols = slice(c0, c0 + DK)
        w = cw_ref[:, cols]
        y = w[0:1, :] * e_scr[5:5 + tt, cols]
        for i in range(1, CONV_W):
            y = y + w[i:i + 1, :] * e_scr[5 + i:5 + i + tt, cols]
        y = _silu(y)
        if c0 < WIDTH_BQK:
            qn_scr[:, cols] = l2n(y) * QK_SCALE
        elif c0 < 2 * WIDTH_BQK:
            kn_scr[:, c0 - WIDTH_BQK:c0 - WIDTH_BQK + DK] = l2n(y)
        else:
            vv_scr[:, c0 - 2 * WIDTH_BQK:c0 - 2 * WIDTH_BQK + DK] = y
    e_scr[5:8, :] = e_scr[tt + 5:tt + 8, :]
    gates = jnp.dot(h_ref[0], wgate_ref[...], preferred_element_type=F32)
    beta_scr[...] = _sigmoid(gates)
    g_scr[...] = -jnp.exp(alog_ref[...]) * _softplus(gates + dtb_ref[...])

    ri = lax.broadcasted_iota(jnp.int32, (CHUNK, CHUNK), 0)
    ci = lax.broadcasted_iota(jnp.int32, (CHUNK, CHUNK), 1)
    tri = ri >= ci
    strict = ri > ci
    same_sub = (ri // SUB) == (ci // SUB)
    tril_ones = tri.astype(F32)
    nw = nw_ref[...]

    for c0 in range(0, tt // CHUNK, GROUP):
        units = []
        for c in range(c0, c0 + GROUP):
            rows = slice(c * CHUNK, (c + 1) * CHUNK)
            beta_all = beta_scr[rows, :]
            gc_all = _fdot(tril_ones, g_scr[rows, :])
            gc_all_t = gc_all.T
            for hq in range(QK_HEADS_B):
                qn = qn_scr[rows, hq * DK:(hq + 1) * DK]
                kn = kn_scr[rows, hq * DK:(hq + 1) * DK]
                kk = _bdot_nt(kn, kn)
                qk = _bdot_nt(qn, kn)
                for hv in range(2 * hq, 2 * hq + 2):
                    beta = beta_all[:, hv:hv + 1]
                    gc = gc_all[:, V_HEADS_B + hv:V_HEADS_B + hv + 1]
                    gc_row = gc_all_t[V_HEADS_B + hv:V_HEADS_B + hv + 1, :]
                    gc_last = gc_row[:, CHUNK - 1:CHUNK]
                    decay = jnp.exp(jnp.where(tri, gc - gc_row, NEG))
                    a = jnp.where(strict, beta * kk * decay, 0.0)
                    egc = jnp.exp(gc)
                    attn_scr[hv, rows, :] = (qk * decay).astype(BF16)
                    qg_scr[hv, rows, :] = (qn * egc).astype(BF16)
                    kd = kn * jnp.exp(gc_last - gc)
                    kdt_scr[hv, c * DK:(c + 1) * DK, :] = kd.T.astype(BF16)
                    gl_scr[hv, c * 8:(c + 1) * 8, :] = jnp.broadcast_to(jnp.exp(gc_last), (8, DV))
                    units.append((hv, rows, a))
            g_scr[rows, :] = gc_all
        ds = [jnp.where(same_sub, u[2], 0.0) for u in units]
        ns = [-dd for dd in ds]
        pws = ds
        for _ in range(SUB.bit_length() - 2):
            pws = [_bdot(pw, pw) for pw in pws]
            ns = [n + pw + _bdot(n, pw) for n, pw in zip(ns, pws)]
        ls = [jnp.where(same_sub, 0.0, u[2]) for u in units]
        ps = [lo + _bdot(lo, n) for lo, n in zip(ls, ns)]
        ns = [n - (p + _bdot(n, p)) for n, p in zip(ns, ps)]
        xs = []
        for hv, rows, _ in units:
            beta = beta_scr[rows, hv:hv + 1]
            kscale = beta * jnp.exp(g_scr[rows, V_HEADS_B + hv:V_HEADS_B + hv + 1])
            xs.append(jnp.concatenate([kn_scr[rows, (hv // 2) * DK:(hv // 2 + 1) * DK] * kscale,
                                       vv_scr[rows, hv * DV:(hv + 1) * DV] * beta], axis=-1))
        wus = [x + _bdot(n, x) for n, x in zip(ns, xs)]
        for wu, (hv, rows, _) in zip(wus, units):
            w_scr[hv, rows, :] = wu[:, :DK].astype(BF16)
            u_scr[hv, rows, :] = wu[:, DK:]

    heads = range(V_HEADS_B)
    for c in range(tt // CHUNK):
        rows = slice(c * CHUNK, (c + 1) * CHUNK)
        states = [s_scr[hv] for hv in heads]
        states_b = [s.astype(BF16) for s in states]
        v_news = [u_scr[hv, rows, :] - jnp.dot(w_scr[hv, rows, :], states_b[hv], preferred_element_type=F32)
                  for hv in heads]
        v_news_b = [v.astype(BF16) for v in v_news]
        for hv in heads:
            s_scr[hv] = (states[hv] * gl_scr[hv, c * 8:c * 8 + 1, :]
                         + jnp.dot(kdt_scr[hv, c * DK:(c + 1) * DK, :], v_news_b[hv],
                                   preferred_element_type=F32))
        for hv in heads:
            o_scr[rows, hv * DV:(hv + 1) * DV] = (
                jnp.dot(qg_scr[hv, rows, :], states_b[hv], preferred_element_type=F32)
                + jnp.dot(attn_scr[hv, rows, :], v_news_b[hv], preferred_element_type=F32))

    for hv in heads:
        o = o_scr[:, hv * DV:(hv + 1) * DV]
        z = z_ref[0, :, hv * DV:(hv + 1) * DV]
        o = o * lax.rsqrt(jnp.mean(o * o, axis=-1, keepdims=True) + EPS) * nw * _silu(z)
        o_ref[0, :, hv * DV:(hv + 1) * DV] = o.astype(o_ref.dtype)

    @pl.when(t == nt - 1)
    def _():
        s_out_ref[0] = s_scr[...]


def _dn_prompt(proj3, h3, w_gate, conv_state, s0, conv_w, alog_vec, dtb_vec, norm_w, tt):
    b, s, _ = proj3.shape
    body = functools.partial(_dn_prompt_body, tt=tt)
    nh = V_HEADS_B
    in_specs = [
        pl.BlockSpec((1, tt, WIDTH_BQK), lambda i, t: (i, t, (OFF_BQ - OFF_BQ) // WIDTH_BQK)),
        pl.BlockSpec((1, tt, WIDTH_BQK), lambda i, t: (i, t, (OFF_BK - OFF_BQ) // WIDTH_BQK)),
        pl.BlockSpec((1, tt, WIDTH_BV), lambda i, t: (i, t, (OFF_BV - OFF_BQ) // WIDTH_BV)),
        pl.BlockSpec((1, tt, WIDTH_BV), lambda i, t: (i, t, (OFF_BZ - OFF_BQ) // WIDTH_BV)),
        pl.BlockSpec((1, tt, D_MODEL), lambda i, t: (i, t, 0)),
        pl.BlockSpec((D_MODEL, 128), lambda i, t: (0, 0)),
        pl.BlockSpec((CONV_W, CONV_DIM), lambda i, t: (0, 0)),
        pl.BlockSpec((1, CONV_W - 1, CONV_DIM), lambda i, t: (i, 0, 0)),
        pl.BlockSpec((1, nh, DK, DV), lambda i, t: (i, 0, 0, 0)),
        pl.BlockSpec((1, 128), lambda i, t: (0, 0)),
        pl.BlockSpec((1, 128), lambda i, t: (0, 0)),
        pl.BlockSpec((1, DV), lambda i, t: (0, 0)),
    ]
    return pl.pallas_call(
        body,
        grid=(b, s // tt),
        in_specs=in_specs,
        out_specs=[pl.BlockSpec((1, tt, WIDTH_BV), lambda i, t: (i, t, 0)),
                   pl.BlockSpec((1, nh, DK, DV), lambda i, t: (i, 0, 0, 0))],
        out_shape=[jax.ShapeDtypeStruct((b, s, WIDTH_BV), BF16),
                   jax.ShapeDtypeStruct((b, nh, DK, DV), F32)],
        scratch_shapes=[pltpu.VMEM((nh, DK, DV), F32),
                        pltpu.VMEM((tt + 8, CONV_DIM), F32),
                        pltpu.VMEM((tt, WIDTH_BQK), F32),
                        pltpu.VMEM((tt, WIDTH_BQK), F32),
                        pltpu.VMEM((tt, WIDTH_BV), F32),
                        pltpu.VMEM((tt, 128), F32),
                        pltpu.VMEM((tt, 128), F32),
                        pltpu.VMEM((nh, tt, DK), BF16),
                        pltpu.VMEM((nh, tt, DV), F32),
                        pltpu.VMEM((nh, tt, DK), BF16),
                        pltpu.VMEM((nh, tt // CHUNK * DK, CHUNK), BF16),
                        pltpu.VMEM((nh, tt, CHUNK), BF16),
                        pltpu.VMEM((nh, tt // CHUNK * 8, DV), F32),
                        pltpu.VMEM((tt, WIDTH_BV), F32)],
        compiler_params=_params(("arbitrary", "arbitrary"), 48),
        name="deltanet_prompt",
    )(proj3, proj3, proj3, proj3, h3, w_gate, conv_w, conv_state, s0, alog_vec, dtb_vec, norm_w)


def _dn_sample_body(proj_ref, gates_ref, cw_ref, cs_ref, s0_ref, alog_ref, dtb_ref, nw_ref,
                    o_ref, cs_out_ref, s_out_ref):
    pre = proj_ref[0, :, OFF_BQ:OFF_BQ + CONV_DIM]
    buf = cs_ref[0]
    w = cw_ref[...]
    y = w[CONV_W - 1:CONV_W, :] * pre
    for i in range(CONV_W - 1):
        y = y + w[i:i + 1, :] * buf[i:i + 1, :]
    y = _silu(y)
    cs_out_ref[0, 0:CONV_W - 2, :] = buf[1:CONV_W - 1, :]
    cs_out_ref[0, CONV_W - 2:CONV_W - 1, :] = pre

    gates = gates_ref[0]
    beta_all = _sigmoid(gates)
    g_all = -jnp.exp(alog_ref[...]) * _softplus(gates + dtb_ref[...])
    nw = nw_ref[...]

    def l2n(x):
        return x * lax.rsqrt(jnp.sum(x * x, axis=-1, keepdims=True) + EPS)

    row8 = lax.broadcasted_iota(jnp.int32, (8, DK), 0) == 0
    for hv in range(V_HEADS_B):
        hq = hv // 2
        q = l2n(y[:, hq * DK:(hq + 1) * DK]) * QK_SCALE
        k = l2n(y[:, WIDTH_BQK + hq * DK:WIDTH_BQK + (hq + 1) * DK])
        v = y[:, 2 * WIDTH_BQK + hv * DV:2 * WIDTH_BQK + (hv + 1) * DV]
        beta = beta_all[:, hv:hv + 1]
        g = g_all[:, V_HEADS_B + hv:V_HEADS_B + hv + 1]
        eg = jnp.exp(g)
        state = s0_ref[0, hv]

        def pad8(x):
            return jnp.where(row8, jnp.broadcast_to(x, (8, x.shape[-1])), 0.0)

        v_new = v * beta - _bdot(pad8(k * (beta * eg)), state)[0:1, :]
        qk = jnp.sum(q.astype(BF16).astype(F32) * k.astype(BF16).astype(F32), axis=-1, keepdims=True)
        o = _bdot(pad8(q * eg), state)[0:1, :] + qk.astype(BF16).astype(F32) * v_new.astype(BF16).astype(F32)
        s_out_ref[0, hv] = state * eg + _bdot_tn(pad8(k), pad8(v_new))
        z = proj_ref[0, :, OFF_BZ + hv * DV:OFF_BZ + (hv + 1) * DV]
        o = o * lax.rsqrt(jnp.mean(o * o, axis=-1, keepdims=True) + EPS) * nw * _silu(z)
        o_ref[0, :, hv * DV:(hv + 1) * DV] = o.astype(o_ref.dtype)


def _dn_sample(proj, gates, conv_state, s0, conv_w, alog_vec, dtb_vec, norm_w):
    b = proj.shape[0]
    return pl.pallas_call(
        _dn_sample_body,
        grid=(b,),
        in_specs=[pl.BlockSpec((1, 1, PROJ_MAIN), lambda i: (i, 0, 0)),
                  pl.BlockSpec((1, 1, 128), lambda i: (i, 0, 0)),
                  pl.BlockSpec((CONV_W, CONV_DIM), lambda i: (0, 0)),
                  pl.BlockSpec((1, CONV_W - 1, CONV_DIM), lambda i: (i, 0, 0)),
                  pl.BlockSpec((1, V_HEADS_B, DK, DV), lambda i: (i, 0, 0, 0)),
                  pl.BlockSpec((1, 128), lambda i: (0, 0)),
                  pl.BlockSpec((1, 128), lambda i: (0, 0)),
                  pl.BlockSpec((1, DV), lambda i: (0, 0))],
        out_specs=[pl.BlockSpec((1, 1, WIDTH_BV), lambda i: (i, 0, 0)),
                   pl.BlockSpec((1, CONV_W - 1, CONV_DIM), lambda i: (i, 0, 0)),
                   pl.BlockSpec((1, V_HEADS_B, DK, DV), lambda i: (i, 0, 0, 0))],
        out_shape=[jax.ShapeDtypeStruct((b, 1, WIDTH_BV), BF16),
                   jax.ShapeDtypeStruct((b, CONV_W - 1, CONV_DIM), F32),
                   jax.ShapeDtypeStruct((b, V_HEADS_B, DK, DV), F32)],
        compiler_params=_params(("arbitrary",), 40),
        name="deltanet_sample",
    )(proj, gates, conv_w, conv_state, s0, alog_vec, dtb_vec, norm_w)


def _outproj_body(att_ref, dn_ref, wa_ref, wb_ref, x_ref, lnpost_ref, lnpre_ref, x1_ref, h2_ref):
    mix = (jnp.dot(att_ref[...], wa_ref[...], preferred_element_type=F32)
           + jnp.dot(dn_ref[...], wb_ref[...], preferred_element_type=F32))
    x1 = x_ref[...] + _rms(mix, lnpost_ref[...])
    x1_ref[...] = x1
    h2_ref[...] = _rms(x1, lnpre_ref[...]).astype(h2_ref.dtype)


def _outproj(att, dn, w, x, ln_post, ln_pre, tm):
    m, d = x.shape
    assert WIDTH_A == WIDTH_BV
    return pl.pallas_call(
        _outproj_body,
        grid=(m // tm,),
        in_specs=[pl.BlockSpec((tm, WIDTH_A), lambda i: (i, 0)),
                  pl.BlockSpec((tm, WIDTH_BV), lambda i: (i, 0)),
                  pl.BlockSpec((WIDTH_A, d), lambda i: (0, 0)),
                  pl.BlockSpec((WIDTH_BV, d), lambda i: (1, 0)),
                  pl.BlockSpec((tm, d), lambda i: (i, 0)),
                  pl.BlockSpec((1, d), lambda i: (0, 0)),
                  pl.BlockSpec((1, d), lambda i: (0, 0))],
        out_specs=[pl.BlockSpec((tm, d), lambda i: (i, 0)),
                   pl.BlockSpec((tm, d), lambda i: (i, 0))],
        out_shape=[jax.ShapeDtypeStruct((m, d), F32),
                   jax.ShapeDtypeStruct((m, d), BF16)],
        compiler_params=_params(("arbitrary",), 48),
        name="outproj",
    )(att, dn, w, w, x, ln_post, ln_pre)


def _ffn_body(h_ref, wg_ref, wv_ref, cwg_ref, cwv_ref, cbg_ref, cbv_ref, wo_prev_ref, wo_last_ref, x1_hbm, ln_ref,
              pg_ref, pv_ref, o_ref, ng_ref, nv_ref, eg_scr, ev_scr, carry_scr, act_scr, x1_scr, x1_sem,
              *, tm, tiles_per_seq, single_token):
    i = pl.program_id(0)
    j = pl.program_id(1)
    nj = pl.num_programs(1)
    d = o_ref.shape[-1]
    tf = act_scr.shape[-1]
    cur = j % 2
    act_cur = act_scr.at[cur]
    act_prev = act_scr.at[1 - cur]

    def x1_copy():
        return pltpu.make_async_copy(x1_hbm.at[pl.ds(pl.multiple_of(i * tm, tm), tm), :], x1_scr, x1_sem)

    def down_proj(act_ref, wo_ref):
        for n in range(0, d, FFN_COLS):
            o_ref[:, n:n + FFN_COLS] += jnp.dot(act_ref[...], wo_ref[:, n:n + FFN_COLS],
                                                preferred_element_type=F32)

    @pl.when(j == 0)
    def _():
        x1_copy().start()
        o_ref[...] = jnp.zeros_like(o_ref)
        act_prev[...] = jnp.zeros_like(act_prev)

    if single_token:
        def up_conv(w_ref, cw_ref, cb_ref, prev_ref, new_ref):
            up = jnp.dot(h_ref[...], w_ref[...], preferred_element_type=F32)
            cw = cw_ref[...]
            new_ref[...] = up
            return cw[0:1, :] * prev_ref[0] + cw[1:2, :] * prev_ref[1] + cw[2:3, :] * up + cb_ref[...]

        gate = up_conv(wg_ref, cwg_ref, cbg_ref, pg_ref, ng_ref)
        val = up_conv(wv_ref, cwv_ref, cbv_ref, pv_ref, nv_ref)
        act_cur[...] = (_gelu_tanh(gate) * val).astype(BF16)
        down_proj(act_prev, wo_prev_ref)
    else:
        first_tile = i % tiles_per_seq == 0
        pieces = [slice(c, c + FFN_PIECE) for c in range(0, tf, FFN_PIECE)]

        def up_proj(cols):
            for w_ref, prev_ref, new_ref, e_scr, slot in ((wg_ref, pg_ref, ng_ref, eg_scr, 0),
                                                          (wv_ref, pv_ref, nv_ref, ev_scr, 1)):
                e_scr[8:8 + tm, cols] = jnp.dot(h_ref[...], w_ref[:, cols], preferred_element_type=F32)
                e_scr[6:8, cols] = jnp.where(first_tile, prev_ref[0, :, cols], carry_scr[slot, j, 6:8, cols])
                tail = e_scr[tm + 6:tm + 8, cols]
                carry_scr[slot, j, 6:8, cols] = tail
                new_ref[0, :, cols] = tail

        def conv(e_scr, cw_ref, cb_ref, cols, r):
            cw = cw_ref[:, cols]
            return (cw[0:1, :] * e_scr[6 + r:6 + r + FFN_ROWS, cols]
                    + cw[1:2, :] * e_scr[7 + r:7 + r + FFN_ROWS, cols]
                    + cw[2:3, :] * e_scr[8 + r:8 + r + FFN_ROWS, cols] + cb_ref[:, cols])

        def conv_geglu(cols):
            for r in range(0, tm, FFN_ROWS):
                act_cur[r:r + FFN_ROWS, cols] = (_gelu_tanh(conv(eg_scr, cwg_ref, cbg_ref, cols, r))
                                                 * conv(ev_scr, cwv_ref, cbv_ref, cols, r)).astype(BF16)

        up_proj(pieces[0])
        for c in range(1, len(pieces)):
            up_proj(pieces[c])
            conv_geglu(pieces[c - 1])
        down_proj(act_prev, wo_prev_ref)
        conv_geglu(pieces[-1])

    @pl.when(j == nj - 1)
    def _():
        down_proj(act_cur, wo_last_ref)
        x1_copy().wait()
        o_ref[...] = x1_scr[...] + _rms(o_ref[...], ln_ref[...])


def _ffn(h2, w_in, conv_w, conv_b, w_out, x1, ln_post, prev, tm, tf, seq_len):
    m, d = h2.shape
    single = seq_len == 1
    nj = D_FF // tf
    tiles_per_seq = 1 if single else seq_len // tm
    if single:
        prev_g = pl.BlockSpec((2, tm, tf), lambda i, j: (0, i, j))
        prev_v = pl.BlockSpec((2, tm, tf), lambda i, j: (0, i, nj + j))
        new_g = pl.BlockSpec((tm, tf), lambda i, j: (i, j))
        new_shape = jax.ShapeDtypeStruct((m, D_FF), F32)
    else:
        prev_g = pl.BlockSpec((1, 2, tf), lambda i, j: (i // tiles_per_seq, 0, j))
        prev_v = pl.BlockSpec((1, 2, tf), lambda i, j: (i // tiles_per_seq, 0, nj + j))
        new_g = pl.BlockSpec((1, 2, tf), lambda i, j: (i, 0, j))
        new_shape = jax.ShapeDtypeStruct((m // tm, 2, D_FF), F32)
    body = functools.partial(_ffn_body, tm=tm, tiles_per_seq=tiles_per_seq, single_token=single)
    once = dict(pipeline_mode=pl.Buffered(1)) if tm >= 1024 else {}
    return pl.pallas_call(
        body,
        grid=(m // tm, nj),
        in_specs=[pl.BlockSpec((tm, d), lambda i, j: (i, 0), **once),
                  pl.BlockSpec((None, d, tf), lambda i, j: (j, 0, 0)),
                  pl.BlockSpec((None, d, tf), lambda i, j: (nj + j, 0, 0)),
                  pl.BlockSpec((FFN_CONV_W, tf), lambda i, j: (0, j)),
                  pl.BlockSpec((FFN_CONV_W, tf), lambda i, j: (0, nj + j)),
                  pl.BlockSpec((1, tf), lambda i, j: (0, j)),
                  pl.BlockSpec((1, tf), lambda i, j: (0, nj + j)),
                  pl.BlockSpec((tf, d), lambda i, j: (jnp.maximum(j - 1, 0), 0)),
                  pl.BlockSpec((tf, d), lambda i, j: (nj - 1, 0), pipeline_mode=pl.Buffered(1)),
                  pl.BlockSpec(memory_space=pl.ANY),
                  pl.BlockSpec((1, d), lambda i, j: (0, 0)),
                  prev_g, prev_v],
        out_specs=[pl.BlockSpec((tm, d), lambda i, j: (i, 0), **once), new_g, new_g],
        out_shape=[jax.ShapeDtypeStruct((m, d), F32), new_shape, new_shape],
        scratch_shapes=[pltpu.VMEM((tm + 8, tf), F32),
                        pltpu.VMEM((tm + 8, tf), F32),
                        pltpu.VMEM((2, nj, 8, tf), F32),
                        pltpu.VMEM((2, tm, tf), BF16),
                        pltpu.VMEM((tm, d), F32),
                        pltpu.SemaphoreType.DMA(())],
        compiler_params=_params(("arbitrary", "arbitrary"), 57),
        name="convffn",
    )(h2, w_in, w_in, conv_w, conv_w, conv_b, conv_b, w_out, w_out, x1, ln_post, prev, prev)


def _cache_shift_body(ck_ref, cv_ref, ck_next_ref, cv_next_ref, nk_ref, nv_ref, ok_ref, ov_ref, *, tr):
    last = pl.program_id(1) == pl.num_programs(1) - 1
    for c_ref, nxt_ref, n_ref, o_ref in ((ck_ref, ck_next_ref, nk_ref, ok_ref),
                                         (cv_ref, cv_next_ref, nv_ref, ov_ref)):
        o_ref[0, 0:tr - 1] = c_ref[0, 1:tr]
        o_ref[0, tr - 1] = jnp.where(last, n_ref[0, 0], nxt_ref[0, 0])


def _cache_shift(cache_k, cache_v, new_k, new_v, tr):
    nb, rows, nh, dh = cache_k.shape
    main = pl.BlockSpec((1, tr, nh, dh), lambda b, i: (b, i, 0, 0))
    nxt = pl.BlockSpec((1, 1, nh, dh), lambda b, i: (b, jnp.minimum((i + 1) * tr, rows - 1), 0, 0))
    new = pl.BlockSpec((1, 1, nh, dh), lambda b, i: (b, 0, 0, 0))
    shape = jax.ShapeDtypeStruct(cache_k.shape, cache_k.dtype)
    return pl.pallas_call(
        functools.partial(_cache_shift_body, tr=tr),
        grid=(nb, rows // tr),
        in_specs=[main, main, nxt, nxt, new, new],
        out_specs=[main, main],
        out_shape=[shape, shape],
        compiler_params=_params(("arbitrary", "arbitrary"), 40),
        name="cache_shift",
    )(cache_k, cache_v, cache_k, cache_v, new_k, new_v)


def _lane_vec(values, offset):
    return jnp.zeros((1, 128), F32).at[0, offset:offset + V_HEADS_B].set(values.astype(F32))


def kernel(x_prompt, x_sample, cache_win_k, cache_win_v, state_dn_conv, state_dn_rec, state_ffn_conv,
           rel_bias, ln_mix_pre, w_in, dn_conv_w, dn_A_log, dn_dt_bias, dn_norm_w, w_out, ln_mix_post,
           ln_ffn_pre, w_ffn_in, ffn_conv_w, ffn_conv_b, w_ffn_out, ln_ffn_post):
    bp, sp, d = x_prompt.shape
    bs = x_sample.shape[0]
    l = 0

    w_main = w_in[l].astype(BF16)
    w_gate = jnp.pad(w_main[:, PROJ_MAIN:], ((0, 0), (0, 128 - 2 * V_HEADS_B)))
    wo = w_out[l].astype(BF16)
    wf_in = w_ffn_in[l].reshape(d, 2 * D_FF // FFN_TF, FFN_TF).transpose(1, 0, 2).astype(BF16)
    wf_out = w_ffn_out[l].astype(BF16)
    ln1 = ln_mix_pre[l][None, :]
    ln2 = ln_mix_post[l][None, :]
    ln3 = ln_ffn_pre[l][None, :]
    ln4 = ln_ffn_post[l][None, :]
    conv_w = dn_conv_w[l]
    alog_vec = _lane_vec(dn_A_log[l], V_HEADS_B)
    dtb_vec = _lane_vec(dn_dt_bias[l], V_HEADS_B)
    norm_w = dn_norm_w[l][None, :]
    fcw = ffn_conv_w[l]
    fcb = ffn_conv_b[l][None, :]

    xp = x_prompt.reshape(bp * sp, d)
    hp = _rmsnorm(xp, ln1, 512)
    qkv = _matmul_heads(hp, w_main, 1024, 3, "inproj_attn").reshape(3, HEADS_A, bp, sp, HEAD_DIM)
    proj3 = _matmul(hp, w_main, 1024, 1024, "inproj_delta", n=PROJ_MAIN, n0=OFF_BQ
                    ).reshape(bp, sp, PROJ_MAIN - OFF_BQ)
    keep = min(MAX_DISTANCE, sp)
    att_p, win_k, win_v = _attn_prompt(qkv, rel_bias, keep)
    dn_p, p_dn_rec = _dn_prompt(
        proj3, hp.reshape(bp, sp, d), w_gate,
        jnp.zeros((bp, CONV_W - 1, CONV_DIM), F32), jnp.zeros((bp, V_HEADS_B, DK, DV), F32),
        conv_w, alog_vec, dtb_vec, norm_w, 256)
    x1_p, h2_p = _outproj(att_p.reshape(bp * sp, WIDTH_A), dn_p.reshape(bp * sp, WIDTH_BV),
                          wo, xp, ln2, ln3, 512)
    y_p, fc_g, fc_v = _ffn(h2_p, wf_in, fcw, fcb, wf_out, x1_p, ln4,
                           jnp.zeros((bp, FFN_CONV_W - 1, 2 * D_FF), F32), FFN_TM, FFN_TF, sp)
    p_win_k = win_k.reshape(1, bp, keep, HEADS_A, HEAD_DIM)
    p_win_v = win_v.reshape(1, bp, keep, HEADS_A, HEAD_DIM)
    p_dn_conv = proj3[:, sp - (CONV_W - 1):, :CONV_DIM][None]
    tiles = sp // FFN_TM
    p_ffn_conv = jnp.concatenate([fc_g[tiles - 1::tiles], fc_v[tiles - 1::tiles]], axis=-1)[None]

    xs = x_sample.reshape(bs, d)
    hs = _rmsnorm(xs, ln1, bs)
    proj_s = _matmul(hs, w_main, bs, 1024, "inproj_sample", n=PROJ_MAIN)
    gates_s = _matmul(hs, w_gate, bs, 128, "gates_sample")
    past = cache_win_k.shape[2]
    ck = cache_win_k[l]
    cv = cache_win_v[l]
    new_k = proj_s[:, OFF_AK:OFF_AK + WIDTH_A]
    new_v = proj_s[:, OFF_AV:OFF_AV + WIDTH_A]
    new_q = proj_s[:, OFF_AQ:OFF_AQ + WIDTH_A].reshape(bs, HEADS_A, HEAD_DIM)
    new_k = new_k.reshape(bs, HEADS_A, HEAD_DIM)
    new_v = new_v.reshape(bs, HEADS_A, HEAD_DIM)
    att_s = _attn_sample(new_q, new_k, new_v, ck, cv, rel_bias)
    s_win_k, s_win_v = _cache_shift(ck, cv, new_k[:, None], new_v[:, None], 512)
    dn_s, s_dn_conv, s_dn_rec = _dn_sample(proj_s[:, None], gates_s[:, None], state_dn_conv[l],
                                           state_dn_rec[l], conv_w, alog_vec, dtb_vec, norm_w)
    x1_s, h2_s = _outproj(att_s.reshape(bs, WIDTH_A), dn_s.reshape(bs, WIDTH_BV),
                          wo, xs, ln2, ln3, bs)
    prev_s = jnp.swapaxes(state_ffn_conv[l], 0, 1)
    y_s, up_g, up_v = _ffn(h2_s, wf_in, fcw, fcb, wf_out, x1_s, ln4, prev_s, bs, FFN_TF, 1)
    s_ffn_conv = jnp.stack([prev_s[1], jnp.concatenate([up_g, up_v], axis=-1)], axis=1)[None]

    return (y_p.reshape(bp, sp, d), y_s.reshape(bs, 1, d),
            p_win_k, p_win_v, p_dn_conv, p_dn_rec[None], p_ffn_conv,
            s_win_k[None], s_win_v[None], s_dn_conv[None], s_dn_rec[None], s_ffn_conv)
```

```python
import functools
import math

import numpy as np
import jax
import jax.numpy as jnp
from jax import lax
from jax.experimental import pallas as pl
from jax.experimental.pallas import tpu as pltpu

F32 = jnp.float32
BF16 = jnp.bfloat16

D_MODEL = 2048
HEAD_DIM = 128
WIDTH_A = 1024
HEADS_A = 8
DILATIONS = (1, 4, 16)
BLK = 128
N_BUCKETS = 32
MAX_DISTANCE = 2048
DK = 128
DV = 128
V_HEADS_B = 8
QK_HEADS_B = 4
WIDTH_BQK = 512
WIDTH_BV = 1024
CONV_W = 4
CONV_DIM = 2048
CHUNK = 128
SUB = 64
D_FF = 5632
FFN_CONV_W = 3
EPS = 1e-6
NEG = -1e30
ATT_SCALE = HEAD_DIM ** -0.5
QK_SCALE = DK ** -0.5

OFF_AQ, OFF_AK, OFF_AV = 0, 1024, 2048
OFF_BQ, OFF_BK, OFF_BV, OFF_BZ = 3072, 3584, 4096, 5120
OFF_GATES = 6144
PROJ_MAIN = 6144

MIB = 2 ** 20


def _params(semantics, vmem_mib):
    return pltpu.CompilerParams(dimension_semantics=semantics, vmem_limit_bytes=vmem_mib * MIB)


def _bdot(a, b):
    return jnp.dot(a.astype(BF16), b.astype(BF16), preferred_element_type=F32)


def _bdot_nt(a, b):
    return lax.dot_general(a.astype(BF16), b.astype(BF16), (((1,), (1,)), ((), ())),
                           preferred_element_type=F32)


def _bdot_tn(a, b):
    return lax.dot_general(a.astype(BF16), b.astype(BF16), (((0,), (0,)), ((), ())),
                           preferred_element_type=F32)


def _fdot(a, b):
    return jnp.dot(a, b, preferred_element_type=F32, precision=lax.Precision.HIGHEST)


def _silu(x):
    return x * (1.0 / (1.0 + jnp.exp(-x)))


def _sigmoid(x):
    return 1.0 / (1.0 + jnp.exp(-x))


def _softplus(x):
    return jnp.maximum(x, 0.0) + jnp.log(1.0 + jnp.exp(-jnp.abs(x)))


def _gelu_tanh(x):
    c = math.sqrt(2.0 / math.pi)
    half = 0.5 * x
    return half + half * jnp.tanh(x * (c + (c * 0.044715) * (x * x)))


def _rms(x, w):
    return x * lax.rsqrt(jnp.mean(x * x, axis=-1, keepdims=True) + EPS) * w


def _rmsnorm_body(x_ref, w_ref, o_ref):
    o_ref[...] = _rms(x_ref[...], w_ref[...]).astype(o_ref.dtype)


def _rmsnorm(x, w, tm):
    m, d = x.shape
    return pl.pallas_call(
        _rmsnorm_body,
        grid=(m // tm,),
        in_specs=[pl.BlockSpec((tm, d), lambda i: (i, 0)),
                  pl.BlockSpec((1, d), lambda i: (0, 0))],
        out_specs=pl.BlockSpec((tm, d), lambda i: (i, 0)),
        out_shape=jax.ShapeDtypeStruct((m, d), BF16),
        compiler_params=_params(("arbitrary",), 40),
        name="rmsnorm",
    )(x, w)


def _matmul_body(x_ref, w_ref, o_ref):
    o_ref[...] = jnp.dot(x_ref[...], w_ref[...].astype(BF16), preferred_element_type=F32)


def _matmul(x, w, tm, tn, name, n=None, n0=0):
    m, k = x.shape
    n = w.shape[1] if n is None else n
    j0 = n0 // tn
    return pl.pallas_call(
        _matmul_body,
        grid=((n - n0) // tn, m // tm),
        in_specs=[pl.BlockSpec((tm, k), lambda j, i: (i, 0)),
                  pl.BlockSpec((k, tn), lambda j, i: (0, j0 + j))],
        out_specs=pl.BlockSpec((tm, tn), lambda j, i: (i, j)),
        out_shape=jax.ShapeDtypeStruct((m, n - n0), F32),
        compiler_params=_params(("arbitrary", "arbitrary"), 48),
        name=name,
    )(x, w)


def _rel_bucket_np(dist):
    dist = np.asarray(dist, np.int64)
    max_exact = N_BUCKETS // 2
    d = np.maximum(dist, 1).astype(np.float64)
    val = np.log(d / max_exact) / math.log(MAX_DISTANCE / max_exact) * (N_BUCKETS - max_exact)
    frac = np.abs(val - np.round(val))
    near = (frac < 2e-5) &(dist >= max_exact) & (dist != max_exact) & (dist < MAX_DISTANCE)
    assert not near.any(), "distance on a bucket boundary"
    val = np.where(dist == max_exact, 0.0, val)
    large = np.minimum(max_exact + np.trunc(val).astype(np.int64), N_BUCKETS - 1)
    return np.where(dist < max_exact, dist, large).astype(np.int32)


def _prompt_bucket_tables():
    qi = np.arange(BLK)[:, None]
    kj = np.arange(2 * BLK)[None, :]
    delta = BLK + qi - kj
    inwin = (delta >= 0) & (delta <= BLK)
    tabs = []
    for dil in DILATIONS:
        b = _rel_bucket_np(np.clip(delta, 0, BLK) * dil)
        tabs.append(np.where(inwin, b, -1))
    return np.stack(tabs).astype(np.int32)


def _sample_bucket_tables():
    j = BLK - np.arange(BLK)
    return np.stack([_rel_bucket_np(j * dil)[None, :] for dil in DILATIONS]).astype(np.int32)


def _attn_prompt_body(bucket_ref, relb_ref, q_ref, k_ref, v_ref, o_ref, wk_ref, wv_ref,
                      bias_scr, acc_scr, m_scr, l_scr):
    h = pl.program_id(1)
    s, keep = k_ref.shape[1], wk_ref.shape[1] // HEADS_A
    wk_ref[0, pl.ds(h, keep, stride=HEADS_A), :] = k_ref[0, s - keep:s, :]
    wv_ref[0, pl.ds(h, keep, stride=HEADS_A), :] = v_ref[0, s - keep:s, :]
    col = lax.broadcasted_iota(jnp.int32, (BLK, 2 * BLK), 1)
    tables = _prompt_bucket_tables()
    for br in range(3):
        bk = bucket_ref[br]
        bias = jnp.zeros((BLK, 2 * BLK), F32)
        for kb in sorted(set(tables[br].ravel().tolist()) - {-1}):
            bias = jnp.where(bk == kb, relb_ref[kb, h], bias)
        full = jnp.where(bk >= 0, bias, NEG)
        bias_scr[2 * br] = full
        bias_scr[2 * br + 1] = jnp.where(col >= BLK, full, NEG)

    def run_branch(br, dil, is_first_branch, is_last_branch):
        shift = int(math.log2(dil))
        span = BLK * dil
        stride = None if dil == 1 else dil

        def rows(start):
            return pl.ds(start, BLK, stride=stride) if stride else pl.ds(start, BLK)

        nb = q_ref.shape[1] // span
        run_len = min(nb, ATTN_UNROLL)
        runs_per_it = ATTN_UNROLL // run_len
        runs_per_res = nb // run_len
        starts_at_zero = runs_per_res == 1

        def tasks(it, carry):
            q_starts, firsts, qs_, ks_, vs_ = [], [], [], [], []
            for rr in range(runs_per_it):
                ri = it * runs_per_it + rr
                n0 = (ri % runs_per_res) * run_len
                if dil == 1:
                    base = pl.multiple_of(n0 * span, BLK)
                else:
                    base = n0 * span + ri // runs_per_res
                first = jnp.where(n0 == 0, 1, 0)
                starts = [base + u * span for u in range(run_len)]
                kb = [k_ref[0, rows(st), :].astype(BF16) for st in starts]
                vb = [v_ref[0, rows(st), :].astype(BF16) for st in starts]
                if starts_at_zero:
                    k_prev, v_prev = None, None
                else:
                    p_start = base - span * (1 - first)
                    if dil == 1:
                        p_start = pl.multiple_of(p_start, BLK)
                    k_prev = k_ref[0, rows(p_start), :].astype(BF16)
                    v_prev = v_ref[0, rows(p_start), :].astype(BF16)
                for u, st in enumerate(starts):
                    q_starts.append(st)
                    qs_.append(q_ref[0, rows(st), :].astype(BF16))
                    kp, vp = (k_prev, v_prev) if u == 0 else (kb[u - 1], vb[u - 1])
                    if kp is None:
                        firsts.append(None)
                        ks_.append(kb[u])
                        vs_.append(vb[u])
                    else:
                        firsts.append(first if u == 0 else 0)
                        ks_.append(jnp.concatenate([kp, kb[u]], axis=0))
                        vs_.append(jnp.concatenate([vp, vb[u]], axis=0))
            if not is_first_branch:
                runs = [(m_scr[rows(qs), :], l_scr[rows(qs), :], acc_scr[rows(qs), :]) for qs in q_starts]
            ss = [_bdot_nt(q, k) * ATT_SCALE
                  + (bias_scr[2 * br, :, BLK:] if f is None else bias_scr[2 * br + f])
                  for q, k, f in zip(qs_, ks_, firsts)]
            ms = [jnp.max(s, axis=-1, keepdims=True) for s in ss]
            ps_ = [jnp.exp(s - m) for s, m in zip(ss, ms)]
            accs = [_bdot(p, jnp.concatenate([v, jnp.ones_like(v)], axis=1)) for p, v in zip(ps_, vs_)]
            outs = []
            for u in range(ATTN_UNROLL):
                m_b = jnp.broadcast_to(ms[u], (BLK, HEAD_DIM))
                l_b = accs[u][:, HEAD_DIM:]
                acc_t = accs[u][:, :HEAD_DIM]
                if not is_first_branch:
                    m_run, l_run, acc_run = runs[u]
                    m_new = jnp.maximum(m_run, m_b)
                    a = jnp.exp(m_run - m_new)
                    b = jnp.exp(m_b - m_new)
                    acc_t = a * acc_run + b * acc_t
                    l_b = a * l_run + b * l_b
                    m_b = m_new
                outs.append((m_b, l_b, acc_t))
            for qs, (m_b, l_b, acc_t) in zip(q_starts, outs):
                if is_last_branch:
                    o_ref[0, rows(qs), :] = (acc_t / l_b).astype(o_ref.dtype)
                else:
                    m_scr[rows(qs), :] = m_b
                    l_scr[rows(qs), :] = l_b
                    acc_scr[rows(qs), :] = acc_t
            return carry

        lax.fori_loop(0, nb * dil // ATTN_UNROLL, tasks, 0)

    run_branch(2, 16, True, False)
    run_branch(1, 4, False, False)
    run_branch(0, 1, False, True)


def _attn_prompt(proj3, rel_bias, keep):
    b, s, _ = proj3.shape
    buckets = jnp.asarray(_prompt_bucket_tables())
    blk = (1, s, HEAD_DIM)
    win = pl.BlockSpec((1, keep * HEADS_A, HEAD_DIM), lambda i, h: (i, 0, 0), pipeline_mode=pl.Buffered(1))
    win_shape = jax.ShapeDtypeStruct((b, keep * HEADS_A, HEAD_DIM), F32)
    return pl.pallas_call(
        _attn_prompt_body,
        grid=(b, HEADS_A),
        in_specs=[pl.BlockSpec((3, BLK, 2 * BLK), lambda i, h: (0, 0, 0)),
                  pl.BlockSpec(memory_space=pltpu.SMEM),
                  pl.BlockSpec(blk, lambda i, h: (i, 0, OFF_AQ // HEAD_DIM + h)),
                  pl.BlockSpec(blk, lambda i, h: (i, 0, OFF_AK // HEAD_DIM + h)),
                  pl.BlockSpec(blk, lambda i, h: (i, 0, OFF_AV // HEAD_DIM + h))],
        out_specs=[pl.BlockSpec(blk, lambda i, h: (i, 0, h)), win, win],
        out_shape=[jax.ShapeDtypeStruct((b, s, WIDTH_A), BF16), win_shape, win_shape],
        scratch_shapes=[pltpu.VMEM((6, BLK, 2 * BLK), F32),
                        pltpu.VMEM((s, HEAD_DIM), F32),
                        pltpu.VMEM((s, HEAD_DIM), F32),
                        pltpu.VMEM((s, HEAD_DIM), F32)],
        compiler_params=_params(("arbitrary", "arbitrary"), 48),
        name="attn_prompt",
    )(buckets, rel_bias, proj3, proj3, proj3)


def _attn_sample_body(bucket_ref, relbt_ref, q_ref, kn_ref, vn_ref,
                      k1_ref, k4_ref, k16_ref, v1_ref, v4_ref, v16_ref, o_ref, bias_scr):
    relbt = relbt_ref[...]
    tile = (HEADS_A, HEAD_DIM)

    @pl.when(pl.program_id(0) == 0)
    def _():
        for br in range(3):
            bk = bucket_ref[br]
            bias = jnp.zeros((BLK,) + tile, F32)
            for kb in range(N_BUCKETS):
                col = jnp.broadcast_to(relbt[:, kb:kb + 1], tile)
                bias = jnp.where(bk == kb, col[None], bias)
            bias_scr[br] = bias

    def lane_sum(x):
        return jnp.broadcast_to(jnp.sum(x, axis=-1, keepdims=True), x.shape)

    q = q_ref[0]
    s_self = lane_sum(q * kn_ref[0]) * ATT_SCALE + jnp.broadcast_to(relbt[:, 0:1], tile)
    scores = []
    m = s_self
    for br, k_ref in enumerate((k1_ref, k4_ref, k16_ref)):
        s = lane_sum(k_ref[...] * q[None]) * ATT_SCALE + bias_scr[br]
        scores.append(s)
        m = jnp.maximum(m, jnp.max(s, axis=0))
    p_self = 3.0 * jnp.exp(s_self - m)
    l = p_self
    acc = p_self * vn_ref[0]
    for s, v_ref in zip(scores, (v1_ref, v4_ref, v16_ref)):
        p = jnp.exp(s - m[None])
        l = l + jnp.sum(p, axis=0)
        acc = acc + jnp.sum(p * v_ref[...], axis=0)
    o_ref[0] = (acc / l).astype(o_ref.dtype)


def _attn_sample(q, k_new, v_new, cache_k, cache_v, rel_bias):
    b, past = cache_k.shape[:2]
    tile = (HEADS_A, HEAD_DIM)
    buckets = jnp.asarray(np.broadcast_to(_sample_bucket_tables().reshape(3, BLK, 1, 1), (3, BLK) + tile))
    row = pl.BlockSpec((1,) + tile, lambda i: (i, 0, 0))
    views, specs = [], []
    for cache in (cache_k, cache_v):
        for dil in DILATIONS:
            views.append(cache.reshape((b, past // dil, dil) + tile))
            last = past // dil // BLK - 1
            specs.append(pl.BlockSpec((None, BLK, None) + tile,
                                      functools.partial(lambda last, i: (i, last, 0, 0, 0), last)))
    return pl.pallas_call(
        _attn_sample_body,
        grid=(b,),
        in_specs=[pl.BlockSpec((3, BLK) + tile, lambda i: (0, 0, 0, 0)),
                  pl.BlockSpec((HEADS_A, N_BUCKETS), lambda i: (0, 0)),
                  row, row, row] + specs,
        out_specs=row,
        out_shape=jax.ShapeDtypeStruct((b,) + tile, BF16),
        scratch_shapes=[pltpu.VMEM((3, BLK) + tile, F32)],
        compiler_params=_params(("arbitrary",), 40),
        name="attn_sample",
    )(buckets, rel_bias.T, q, k_new, v_new, *views)


GROUP = 1
ATTN_UNROLL = 8
OUTPROJ_ROWS = 128
FFN_TM = 1024
FFN_TF = 512
FFN_ROWS = 64
FFN_COLS = 512
FFN_PIECE = 256


def _dn_prompt_body(q_ref, k_ref, v_ref, z_ref, h_ref, wgate_ref, cw_ref, cs_ref, s0_ref, alog_ref, dtb_ref, nw_ref,
                    o_ref, s_out_ref,
                    s_scr, e_scr, qn_scr, kn_scr, vv_scr, g_scr, beta_scr,
                    w_scr, u_scr, qg_scr, kdt_scr, attn_scr, gl_scr, o_scr, *, tt):
    t = pl.program_id(1)
    nt = pl.num_programs(1)

    @pl.when(t == 0)
    def _():
        s_scr[...] = s0_ref[0]
        e_scr[5:8, :] = cs_ref[0]

    e_scr[8:8 + tt, 0:WIDTH_BQK] = q_ref[0]
    e_scr[8:8 + tt, WIDTH_BQK:2 * WIDTH_BQK] = k_ref[0]
    e_scr[8:8 + tt, 2 * WIDTH_BQK:CONV_DIM] = v_ref[0]

    def l2n(x):
        return x * lax.rsqrt(jnp.sum(x * x, axis=-1, keepdims=True) + EPS)

    for c0 in range(0, CONV_DIM, DK):
        cols = slice(c0, c0 + DK)
        w = cw_ref[:, cols]
        y = w[0:1, :] * e_scr[5:5 + tt, cols]
        for i in range(1, CONV_W):
            y = y + w[i:i + 1, :] * e_scr[5 + i:5 + i + tt, cols]
        y = _silu(y)
        if c0 < WIDTH_BQK:
            qn_scr[:, cols] = l2n(y) * QK_SCALE
        elif c0 < 2 * WIDTH_BQK:
            kn_scr[:, c0 - WIDTH_BQK:c0 - WIDTH_BQK + DK] = l2n(y)
        else:
            vv_scr[:, c0 - 2 * WIDTH_BQK:c0 - 2 * WIDTH_BQK + DK] = y
    e_scr[5:8, :] = e_scr[tt + 5:tt + 8, :]
    gates = jnp.dot(h_ref[0], wgate_ref[...], preferred_element_type=F32)
    beta_scr[...] = _sigmoid(gates)
    g_scr[...] = -jnp.exp(alog_ref[...]) * _softplus(gates + dtb_ref[...])

    ri = lax.broadcasted_iota(jnp.int32, (CHUNK, CHUNK), 0)
    ci = lax.broadcasted_iota(jnp.int32, (CHUNK, CHUNK), 1)
    tri = ri >= ci
    strict = ri > ci
    same_sub = (ri // SUB) == (ci // SUB)
    tril_ones = tri.astype(F32)
    nw = nw_ref[...]

    for c0 in range(0, tt // CHUNK, GROUP):
        units = []
        for c in range(c0, c0 + GROUP):
            rows = slice(c * CHUNK, (c + 1) * CHUNK)
            beta_all = beta_scr[rows, :]
            gc_all = _fdot(tril_ones, g_scr[rows, :])
            gc_all_t = gc_all.T
            for hq in range(QK_HEADS_B):
                qn = qn_scr[rows, hq * DK:(hq + 1) * DK]
                kn = kn_scr[rows, hq * DK:(hq + 1) * DK]
                kk = _bdot_nt(kn, kn)
                qk = _bdot_nt(qn, kn)
                for hv in range(2 * hq, 2 * hq + 2):
                    beta = beta_all[:, hv:hv + 1]
                    gc = gc_all[:, V_HEADS_B + hv:V_HEADS_B + hv + 1]
                    gc_row = gc_all_t[V_HEADS_B + hv:V_HEADS_B + hv + 1, :]
                    gc_last = gc_row[:, CHUNK - 1:CHUNK]
                    decay = jnp.exp(jnp.where(tri, gc - gc_row, NEG))
                    a = jnp.where(strict, beta * kk * decay, 0.0)
                    egc = jnp.exp(gc)
                    attn_scr[hv, rows, :] = (qk * decay).astype(BF16)
                    qg_scr[hv, rows, :] = (qn * egc).astype(BF16)
                    kd = kn * jnp.exp(gc_last - gc)
                    kdt_scr[hv, c * DK:(c + 1) * DK, :] = kd.T.astype(BF16)
                    gl_scr[hv, c * 8:(c + 1) * 8, :] = jnp.broadcast_to(jnp.exp(gc_last), (8, DV))
                    units.append((hv, rows, a))
            g_scr[rows, :] = gc_all
        ds = [jnp.where(same_sub, u[2], 0.0) for u in units]
        ns = [-dd for dd in ds]
        pws = ds
        for _ in range(SUB.bit_length() - 2):
            pws = [_bdot(pw, pw) for pw in pws]
            ns = [n + pw + _bdot(n, pw) for n, pw in zip(ns, pws)]
        ls = [jnp.where(same_sub, 0.0, u[2]) for u in units]
        ps = [lo + _bdot(lo, n) for lo, n in zip(ls, ns)]
        ns = [n - (p + _bdot(n, p)) for n, p in zip(ns, ps)]
        xs = []
        for hv, rows, _ in units:
            beta = beta_scr[rows, hv:hv + 1]
            kscale = beta * jnp.exp(g_scr[rows, V_HEADS_B + hv:V_HEADS_B + hv + 1])
            xs.append(jnp.concatenate([kn_scr[rows, (hv // 2) * DK:(hv // 2 + 1) * DK] * kscale,
                                       vv_scr[rows, hv * DV:(hv + 1) * DV] * beta], axis=-1))
        wus = [x + _bdot(n, x) for n, x in zip(ns, xs)]
        for wu, (hv, rows, _) in zip(wus, units):
            w_scr[hv, rows, :] = wu[:, :DK].astype(BF16)
            u_scr[hv, rows, :] = wu[:, DK:]

    heads = range(V_HEADS_B)
    for c in range(tt // CHUNK):
        rows = slice(c * CHUNK, (c + 1) * CHUNK)
        states = [s_scr[hv] for hv in heads]
        states_b = [s.astype(BF16) for s in states]
        v_news = [u_scr[hv, rows, :] - jnp.dot(w_scr[hv, rows, :], states_b[hv], preferred_element_type=F32)
                  for hv in heads]
        v_news_b = [v.astype(BF16) for v in v_news]
        for hv in heads:
            s_scr[hv] = (states[hv] * gl_scr[hv, c * 8:c * 8 + 1, :]
                         + jnp.dot(kdt_scr[hv, c * DK:(c + 1) * DK, :], v_news_b[hv],
                                   preferred_element_type=F32))
        for hv in heads:
            o_scr[rows, hv * DV:(hv + 1) * DV] = (
                jnp.dot(qg_scr[hv, rows, :], states_b[hv], preferred_element_type=F32)
                + jnp.dot(attn_scr[hv, rows, :], v_news_b[hv], preferred_element_type=F32))

    for hv in heads:
        o = o_scr[:, hv * DV:(hv + 1) * DV]
        z = z_ref[0, :, hv * DV:(hv + 1) * DV]
        o = o * lax.rsqrt(jnp.mean(o * o, axis=-1, keepdims=True) + EPS) * nw * _silu(z)
        o_ref[0, :, hv * DV:(hv + 1) * DV] = o.astype(o_ref.dtype)

    @pl.when(t == nt - 1)
    def _():
        s_out_ref[0] = s_scr[...]


def _dn_prompt(proj3, h3, w_gate, conv_state, s0, conv_w, alog_vec, dtb_vec, norm_w, tt):
    b, s, _ = proj3.shape
    body = functools.partial(_dn_prompt_body, tt=tt)
    nh = V_HEADS_B
    in_specs = [
        pl.BlockSpec((1, tt, WIDTH_BQK), lambda i, t: (i, t, OFF_BQ // WIDTH_BQK)),
        pl.BlockSpec((1, tt, WIDTH_BQK), lambda i, t: (i, t, OFF_BK // WIDTH_BQK)),
        pl.BlockSpec((1, tt, WIDTH_BV), lambda i, t: (i, t, OFF_BV // WIDTH_BV)),
        pl.BlockSpec((1, tt, WIDTH_BV), lambda i, t: (i, t, OFF_BZ // WIDTH_BV)),
        pl.BlockSpec((1, tt, D_MODEL), lambda i, t: (i, t, 0)),
        pl.BlockSpec((D_MODEL, 128), lambda i, t: (0, 0)),
        pl.BlockSpec((CONV_W, CONV_DIM), lambda i, t: (0, 0)),
        pl.BlockSpec((1, CONV_W - 1, CONV_DIM), lambda i, t: (i, 0, 0)),
        pl.BlockSpec((1, nh, DK, DV), lambda i, t: (i, 0, 0, 0)),
        pl.BlockSpec((1, 128), lambda i, t: (0, 0)),
        pl.BlockSpec((1, 128), lambda i, t: (0, 0)),
        pl.BlockSpec((1, DV), lambda i, t: (0, 0)),
    ]
    return pl.pallas_call(
        body,
        grid=(b, s // tt),
        in_specs=in_specs,
        out_specs=[pl.BlockSpec((1, tt, WIDTH_BV), lambda i, t: (i, t, 0)),
                   pl.BlockSpec((1, nh, DK, DV), lambda i, t: (i, 0, 0, 0))],
        out_shape=[jax.ShapeDtypeStruct((b, s, WIDTH_BV), BF16),
                   jax.ShapeDtypeStruct((b, nh, DK, DV), F32)],
        scratch_shapes=[pltpu.VMEM((nh, DK, DV), F32),
                        pltpu.VMEM((tt + 8, CONV_DIM), F32),
                        pltpu.VMEM((tt, WIDTH_BQK), F32),
                        pltpu.VMEM((tt, WIDTH_BQK), F32),
                        pltpu.VMEM((tt, WIDTH_BV), F32),
                        pltpu.VMEM((tt, 128), F32),
                        pltpu.VMEM((tt, 128), F32),
                        pltpu.VMEM((nh, tt, DK), BF16),
                        pltpu.VMEM((nh, tt, DV), F32),
                        pltpu.VMEM((nh, tt, DK), BF16),
                        pltpu.VMEM((nh, tt // CHUNK * DK, CHUNK), BF16),
                        pltpu.VMEM((nh, tt, CHUNK), BF16),
                        pltpu.VMEM((nh, tt // CHUNK * 8, DV), F32),
                        pltpu.VMEM((tt, WIDTH_BV), F32)],
        compiler_params=_params(("arbitrary", "arbitrary"), 48),
        name="deltanet_prompt",
    )(proj3, proj3, proj3, proj3, h3, w_gate, conv_w, conv_state, s0, alog_vec, dtb_vec, norm_w)


def _dn_sample_body(proj_ref, gates_ref, cw_ref, cs_ref, s0_ref, alog_ref, dtb_ref, nw_ref,
                    o_ref, cs_out_ref, s_out_ref):
    pre = proj_ref[0, :, OFF_BQ:OFF_BQ + CONV_DIM]
    buf = cs_ref[0]
    w = cw_ref[...]
    y = w[CONV_W - 1:CONV_W, :] * pre
    for i in range(CONV_W - 1):
        y = y + w[i:i + 1, :] * buf[i:i + 1, :]
    y = _silu(y)
    cs_out_ref[0, 0:CONV_W - 2, :] = buf[1:CONV_W - 1, :]
    cs_out_ref[0, CONV_W - 2:CONV_W - 1, :] = pre

    gates = gates_ref[0]
    beta_all = _sigmoid(gates)
    g_all = -jnp.exp(alog_ref[...]) * _softplus(gates + dtb_ref[...])
    nw = nw_ref[...]

    def l2n(x):
        return x * lax.rsqrt(jnp.sum(x * x, axis=-1, keepdims=True) + EPS)

    row8 = lax.broadcasted_iota(jnp.int32, (8, DK), 0) == 0
    for hv in range(V_HEADS_B):
        hq = hv // 2
        q = l2n(y[:, hq * DK:(hq + 1) * DK]) * QK_SCALE
        k = l2n(y[:, WIDTH_BQK + hq * DK:WIDTH_BQK + (hq + 1) * DK])
        v = y[:, 2 * WIDTH_BQK + hv * DV:2 * WIDTH_BQK + (hv + 1) * DV]
        beta = beta_all[:, hv:hv + 1]
        g = g_all[:, V_HEADS_B + hv:V_HEADS_B + hv + 1]
        eg = jnp.exp(g)
        state = s0_ref[0, hv]

        def pad8(x):
            return jnp.where(row8, jnp.broadcast_to(x, (8, x.shape[-1])), 0.0)

        v_new = v * beta - _bdot(pad8(k * (beta * eg)), state)[0:1, :]
        qk = jnp.sum(q.astype(BF16).astype(F32) * k.astype(BF16).astype(F32), axis=-1, keepdims=True)
        o = _bdot(pad8(q * eg), state)[0:1, :] + qk.astype(BF16).astype(F32) * v_new.astype(BF16).astype(F32)
        s_out_ref[0, hv] = state * eg + _bdot_tn(pad8(k), pad8(v_new))
        z = proj_ref[0, :, OFF_BZ + hv * DV:OFF_BZ + (hv + 1) * DV]
        o = o * lax.rsqrt(jnp.mean(o * o, axis=-1, keepdims=True) + EPS) * nw * _silu(z)
        o_ref[0, :, hv * DV:(hv + 1) * DV] = o.astype(o_ref.dtype)


def _dn_sample(proj, gates, conv_state, s0, conv_w, alog_vec, dtb_vec, norm_w):
    b = proj.shape[0]
    return pl.pallas_call(
        _dn_sample_body,
        grid=(b,),
        in_specs=[pl.BlockSpec((1, 1, PROJ_MAIN), lambda i: (i, 0, 0)),
                  pl.BlockSpec((1, 1, 128), lambda i: (i, 0, 0)),
                  pl.BlockSpec((CONV_W, CONV_DIM), lambda i: (0, 0)),
                  pl.BlockSpec((1, CONV_W - 1, CONV_DIM), lambda i: (i, 0, 0)),
                  pl.BlockSpec((1, V_HEADS_B, DK, DV), lambda i: (i, 0, 0, 0)),
                  pl.BlockSpec((1, 128), lambda i: (0, 0)),
                  pl.BlockSpec((1, 128), lambda i: (0, 0)),
                  pl.BlockSpec((1, DV), lambda i: (0, 0))],
        out_specs=[pl.BlockSpec((1, 1, WIDTH_BV), lambda i: (i, 0, 0)),
                   pl.BlockSpec((1, CONV_W - 1, CONV_DIM), lambda i: (i, 0, 0)),
                   pl.BlockSpec((1, V_HEADS_B, DK, DV), lambda i: (i, 0, 0, 0))],
        out_shape=[jax.ShapeDtypeStruct((b, 1, WIDTH_BV), BF16),
                   jax.ShapeDtypeStruct((b, CONV_W - 1, CONV_DIM), F32),
                   jax.ShapeDtypeStruct((b, V_HEADS_B, DK, DV), F32)],
        compiler_params=_params(("arbitrary",), 40),
        name="deltanet_sample",
    )(proj, gates, conv_w, conv_state, s0, alog_vec, dtb_vec, norm_w)


def _outproj_body(att_ref, dn_ref, wa_ref, wb_ref, x_ref, lnpost_ref, lnpre_ref, x1_ref, h2_ref):
    tm = x_ref.shape[0]
    piece = min(tm, OUTPROJ_ROWS)
    for rows in [slice(r, r + piece) for r in range(0, tm, piece)]:
        mix = (jnp.dot(att_ref[rows, :], wa_ref[...], preferred_element_type=F32)
               + jnp.dot(dn_ref[rows, :], wb_ref[...], preferred_element_type=F32))
        x1 = x_ref[rows, :] + _rms(mix, lnpost_ref[...])
        x1_ref[rows, :] = x1
        h2_ref[rows, :] = _rms(x1, lnpre_ref[...]).astype(h2_ref.dtype)


def _outproj(att, dn, w, x, ln_post, ln_pre, tm):
    m, d = x.shape
    assert WIDTH_A == WIDTH_BV
    return pl.pallas_call(
        _outproj_body,
        grid=(m // tm,),
        in_specs=[pl.BlockSpec((tm, WIDTH_A), lambda i: (i, 0)),
                  pl.BlockSpec((tm, WIDTH_BV), lambda i: (i, 0)),
                  pl.BlockSpec((WIDTH_A, d), lambda i: (0, 0)),
                  pl.BlockSpec((WIDTH_BV, d), lambda i: (1, 0)),
                  pl.BlockSpec((tm, d), lambda i: (i, 0)),
                  pl.BlockSpec((1, d), lambda i: (0, 0)),
                  pl.BlockSpec((1, d), lambda i: (0, 0))],
        out_specs=[pl.BlockSpec((tm, d), lambda i: (i, 0)),
                   pl.BlockSpec((tm, d), lambda i: (i, 0))],
        out_shape=[jax.ShapeDtypeStruct((m, d), F32),
                   jax.ShapeDtypeStruct((m, d), BF16)],
        compiler_params=_params(("arbitrary",), 48),
        name="outproj",
    )(att, dn, w, w, x, ln_post, ln_pre)


def _ffn_body(h_ref, wg_ref, wv_ref, cwg_ref, cwv_ref, cbg_ref, cbv_ref, wo_prev_ref, wo_last_ref, x1_hbm, ln_ref,
              pg_ref, pv_ref, o_ref, ng_ref, nv_ref, eg_scr, ev_scr, carry_scr, act_scr, x1_scr, x1_sem,
              *, tm, tiles_per_seq, single_token):
    i = pl.program_id(0)
    j = pl.program_id(1)
    nj = pl.num_programs(1)
    d = o_ref.shape[-1]
    tf = act_scr.shape[-1]
    cur = j % 2
    act_cur = act_scr.at[cur]
    act_prev = act_scr.at[1 - cur]

    def x1_copy():
        return pltpu.make_async_copy(x1_hbm.at[pl.ds(pl.multiple_of(i * tm, tm), tm), :], x1_scr, x1_sem)

    def down_proj(act_ref, wo_ref):
        for n in range(0, d, FFN_COLS):
            o_ref[:, n:n + FFN_COLS] += jnp.dot(act_ref[...], wo_ref[:, n:n + FFN_COLS],
                                                preferred_element_type=F32)

    @pl.when(j == 0)
    def _():
        x1_copy().start()
        o_ref[...] = jnp.zeros_like(o_ref)
        act_prev[...] = jnp.zeros_like(act_prev)

    if single_token:
        def up_conv(w_ref, cw_ref, cb_ref, prev_ref, new_ref):
            up = jnp.dot(h_ref[...], w_ref[...], preferred_element_type=F32)
            cw = cw_ref[...]
            new_ref[...] = up
            return cw[0:1, :] * prev_ref[0] + cw[1:2, :] * prev_ref[1] + cw[2:3, :] * up + cb_ref[...]

        gate = up_conv(wg_ref, cwg_ref, cbg_ref, pg_ref, ng_ref)
        val = up_conv(wv_ref, cwv_ref, cbv_ref, pv_ref, nv_ref)
        act_cur[...] = (_gelu_tanh(gate) * val).astype(BF16)
        down_proj(act_prev, wo_prev_ref)
    else:
        first_tile = i % tiles_per_seq == 0
        pieces = [slice(c, c + FFN_PIECE) for c in range(0, tf, FFN_PIECE)]

        def up_proj(cols):
            for w_ref, prev_ref, new_ref, e_scr, slot in ((wg_ref, pg_ref, ng_ref, eg_scr, 0),
                                                          (wv_ref, pv_ref, nv_ref, ev_scr, 1)):
                e_scr[8:8 + tm, cols] = jnp.dot(h_ref[...], w_ref[:, cols], preferred_element_type=F32)
                e_scr[6:8, cols] = jnp.where(first_tile, prev_ref[0, :, cols], carry_scr[slot, j, 6:8, cols])
                tail = e_scr[tm + 6:tm + 8, cols]
                carry_scr[slot, j, 6:8, cols] = tail
                new_ref[0, :, cols] = tail

        def conv(e_scr, cw_ref, cb_ref, cols, r):
            cw = cw_ref[:, cols]
            return (cw[0:1, :] * e_scr[6 + r:6 + r + FFN_ROWS, cols]
                    + cw[1:2, :] * e_scr[7 + r:7 + r + FFN_ROWS, cols]
                    + cw[2:3, :] * e_scr[8 + r:8 + r + FFN_ROWS, cols] + cb_ref[:, cols])

        def conv_geglu(cols):
            for r in range(0, tm, FFN_ROWS):
                act_cur[r:r + FFN_ROWS, cols] = (_gelu_tanh(conv(eg_scr, cwg_ref, cbg_ref, cols, r))
                                                 * conv(ev_scr, cwv_ref, cbv_ref, cols, r)).astype(BF16)

        up_proj(pieces[0])
        for c in range(1, len(pieces)):
            up_proj(pieces[c])
            conv_geglu(pieces[c - 1])
        down_proj(act_prev, wo_prev_ref)
        conv_geglu(pieces[-1])

    @pl.when(j == nj - 1)
    def _():
        down_proj(act_cur, wo_last_ref)
        x1_copy().wait()
        o_ref[...] = x1_scr[...] + _rms(o_ref[...], ln_ref[...])


def _ffn(h2, w_in, conv_w, conv_b, w_out, x1, ln_post, prev, tm, tf, seq_len):
    m, d = h2.shape
    single = seq_len == 1
    nj = D_FF // tf
    tiles_per_seq = 1 if single else seq_len // tm
    if single:
        prev_g = pl.BlockSpec((2, tm, tf), lambda i, j: (0, i, j))
        prev_v = pl.BlockSpec((2, tm, tf), lambda i, j: (0, i, nj + j))
        new_g = pl.BlockSpec((tm, tf), lambda i, j: (i, j))
        new_shape = jax.ShapeDtypeStruct((m, D_FF), F32)
    else:
        prev_g = pl.BlockSpec((1, 2, tf), lambda i, j: (i // tiles_per_seq, 0, j))
        prev_v = pl.BlockSpec((1, 2, tf), lambda i, j: (i // tiles_per_seq, 0, nj + j))
        new_g = pl.BlockSpec((1, 2, tf), lambda i, j: (i, 0, j))
        new_shape = jax.ShapeDtypeStruct((m // tm, 2, D_FF), F32)
    body = functools.partial(_ffn_body, tm=tm, tiles_per_seq=tiles_per_seq, single_token=single)
    once = dict(pipeline_mode=pl.Buffered(1)) if tm >= 1024 else {}
    return pl.pallas_call(
        body,
        grid=(m // tm, nj),
        in_specs=[pl.BlockSpec((tm, d), lambda i, j: (i, 0), **once),
                  pl.BlockSpec((d, tf), lambda i, j: (0, j)),
                  pl.BlockSpec((d, tf), lambda i, j: (0, nj + j)),
                  pl.BlockSpec((FFN_CONV_W, tf), lambda i, j: (0, j)),
                  pl.BlockSpec((FFN_CONV_W, tf), lambda i, j: (0, nj + j)),
                  pl.BlockSpec((1, tf), lambda i, j: (0, j)),
                  pl.BlockSpec((1, tf), lambda i, j: (0, nj + j)),
                  pl.BlockSpec((tf, d), lambda i, j: (jnp.maximum(j - 1, 0), 0)),
                  pl.BlockSpec((tf, d), lambda i, j: (nj - 1, 0), pipeline_mode=pl.Buffered(1)),
                  pl.BlockSpec(memory_space=pl.ANY),
                  pl.BlockSpec((1, d), lambda i, j: (0, 0)),
                  prev_g, prev_v],
        out_specs=[pl.BlockSpec((tm, d), lambda i, j: (i, 0), **once), new_g, new_g],
        out_shape=[jax.ShapeDtypeStruct((m, d), F32), new_shape, new_shape],
        scratch_shapes=[pltpu.VMEM((tm + 8, tf), F32),
                        pltpu.VMEM((tm + 8, tf), F32),
                        pltpu.VMEM((2, nj, 8, tf), F32),
                        pltpu.VMEM((2, tm, tf), BF16),
                        pltpu.VMEM((tm, d), F32),
                        pltpu.SemaphoreType.DMA(())],
        compiler_params=_params(("arbitrary", "arbitrary"), 57),
        name="convffn",
    )(h2, w_in, w_in, conv_w, conv_w, conv_b, conv_b, w_out, w_out, x1, ln_post, prev, prev)


def _cache_shift_body(ck_ref, cv_ref, ck_next_ref, cv_next_ref, nk_ref, nv_ref, ok_ref, ov_ref, *, tr):
    last = pl.program_id(1) == pl.num_programs(1) - 1
    for c_ref, nxt_ref, n_ref, o_ref in ((ck_ref, ck_next_ref, nk_ref, ok_ref),
                                         (cv_ref, cv_next_ref, nv_ref, ov_ref)):
        o_ref[0, 0:tr - 1] = c_ref[0, 1:tr]
        o_ref[0, tr - 1] = jnp.where(last, n_ref[0, 0], nxt_ref[0, 0])


def _cache_shift(cache_k, cache_v, new_k, new_v, tr):
    nb, rows, nh, dh = cache_k.shape
    main = pl.BlockSpec((1, tr, nh, dh), lambda b, i: (b, i, 0, 0))
    nxt = pl.BlockSpec((1, 1, nh, dh), lambda b, i: (b, jnp.minimum((i + 1) * tr, rows - 1), 0, 0))
    new = pl.BlockSpec((1, 1, nh, dh), lambda b, i: (b, 0, 0, 0))
    shape = jax.ShapeDtypeStruct(cache_k.shape, cache_k.dtype)
    return pl.pallas_call(
        functools.partial(_cache_shift_body, tr=tr),
        grid=(nb, rows // tr),
        in_specs=[main, main, nxt, nxt, new, new],
        out_specs=[main, main],
        out_shape=[shape, shape],
        compiler_params=_params(("arbitrary", "arbitrary"), 40),
        name="cache_shift",
    )(cache_k, cache_v, cache_k, cache_v, new_k, new_v)


def _lane_vec(values, offset):
    return jnp.zeros((1, 128), F32).at[0, offset:offset + V_HEADS_B].set(values.astype(F32))


def kernel(x_prompt, x_sample, cache_win_k, cache_win_v, state_dn_conv, state_dn_rec, state_ffn_conv,
           rel_bias, ln_mix_pre, w_in, dn_conv_w, dn_A_log, dn_dt_bias, dn_norm_w, w_out, ln_mix_post,
           ln_ffn_pre, w_ffn_in, ffn_conv_w, ffn_conv_b, w_ffn_out, ln_ffn_post):
    bp, sp, d = x_prompt.shape
    bs = x_sample.shape[0]
    l = 0

    w_main = w_in[l]
    w_gate = jnp.pad(w_main[:, PROJ_MAIN:], ((0, 0), (0, 128 - 2 * V_HEADS_B))).astype(BF16)
    wo = w_out[l].astype(BF16)
    wf_in = w_ffn_in[l].astype(BF16)
    wf_out = w_ffn_out[l].astype(BF16)
    ln1 = ln_mix_pre[l][None, :]
    ln2 = ln_mix_post[l][None, :]
    ln3 = ln_ffn_pre[l][None, :]
    ln4 = ln_ffn_post[l][None, :]
    conv_w = dn_conv_w[l]
    alog_vec = _lane_vec(dn_A_log[l], V_HEADS_B)
    dtb_vec = _lane_vec(dn_dt_bias[l], V_HEADS_B)
    norm_w = dn_norm_w[l][None, :]
    fcw = ffn_conv_w[l]
    fcb = ffn_conv_b[l][None, :]

    xp = x_prompt.reshape(bp * sp, d)
    hp = _rmsnorm(xp, ln1, 512)
    proj3 = _matmul(hp, w_main, 1024, 1024, "inproj_prompt", n=PROJ_MAIN).reshape(bp, sp, PROJ_MAIN)
    keep = min(MAX_DISTANCE, sp)
    att_p, win_k, win_v = _attn_prompt(proj3, rel_bias, keep)
    dn_p, p_dn_rec = _dn_prompt(
        proj3, hp.reshape(bp, sp, d), w_gate,
        jnp.zeros((bp, CONV_W - 1, CONV_DIM), F32), jnp.zeros((bp, V_HEADS_B, DK, DV), F32),
        conv_w, alog_vec, dtb_vec, norm_w, 256)
    x1_p, h2_p = _outproj(att_p.reshape(bp * sp, WIDTH_A), dn_p.reshape(bp * sp, WIDTH_BV),
                          wo, xp, ln2, ln3, 512)
    y_p, fc_g, fc_v = _ffn(h2_p, wf_in, fcw, fcb, wf_out, x1_p, ln4,
                           jnp.zeros((bp, FFN_CONV_W - 1, 2 * D_FF), F32), FFN_TM, FFN_TF, sp)
    p_win_k = win_k.reshape(1, bp, keep, HEADS_A, HEAD_DIM)
    p_win_v = win_v.reshape(1, bp, keep, HEADS_A, HEAD_DIM)
    p_dn_conv = proj3[:, sp - (CONV_W - 1):, OFF_BQ:OFF_BQ + CONV_DIM][None]
    tiles = sp // FFN_TM
    p_ffn_conv = jnp.concatenate([fc_g[tiles - 1::tiles], fc_v[tiles - 1::tiles]], axis=-1)[None]

    xs = x_sample.reshape(bs, d)
    hs = _rmsnorm(xs, ln1, bs)
    proj_s = _matmul(hs, w_main, bs, 1024, "inproj_sample", n=PROJ_MAIN)
    gates_s = _matmul(hs, w_gate, bs, 128, "gates_sample")
    past = cache_win_k.shape[2]
    ck = cache_win_k[l]
    cv = cache_win_v[l]
    new_k = proj_s[:, OFF_AK:OFF_AK + WIDTH_A]
    new_v = proj_s[:, OFF_AV:OFF_AV + WIDTH_A]
    new_q = proj_s[:, OFF_AQ:OFF_AQ + WIDTH_A].reshape(bs, HEADS_A, HEAD_DIM)
    new_k = new_k.reshape(bs, HEADS_A, HEAD_DIM)
    new_v = new_v.reshape(bs, HEADS_A, HEAD_DIM)
    att_s = _attn_sample(new_q, new_k, new_v, ck, cv, rel_bias)
    s_win_k, s_win_v = _cache_shift(ck, cv, new_k[:, None], new_v[:, None], 512)
    dn_s, s_dn_conv, s_dn_rec = _dn_sample(proj_s[:, None], gates_s[:, None], state_dn_conv[l],
                                           state_dn_rec[l], conv_w, alog_vec, dtb_vec, norm_w)
    x1_s, h2_s = _outproj(att_s.reshape(bs, WIDTH_A), dn_s.reshape(bs, WIDTH_BV),
                          wo, xs, ln2, ln3, bs)
    prev_s = jnp.swapaxes(state_ffn_conv[l], 0, 1)
    y_s, up_g, up_v = _ffn(h2_s, wf_in, fcw, fcb, wf_out, x1_s, ln4, prev_s, bs, FFN_TF, 1)
    s_ffn_conv = jnp.stack([prev_s[1], jnp.concatenate([up_g, up_v], axis=-1)], axis=1)[None]

    return (y_p.reshape(bp, sp, d), y_s.reshape(bs, 1, d),
            p_win_k, p_win_v, p_dn_conv, p_dn_rec[None], p_ffn_conv,
            s_win_k[None], s_win_v[None], s_dn_conv[None], s_dn_rec[None], s_ffn_conv)
```

```python
import functools
import math

import numpy as np
import jax
import jax.numpy as jnp
from jax import lax
from jax.experimental import pallas as pl
from jax.experimental.pallas import tpu as pltpu

F32 = jnp.float32
BF16 = jnp.bfloat16

D_MODEL = 2048
HEAD_DIM = 128
WIDTH_A = 1024
HEADS_A = 8
DILATIONS = (1, 4, 16)
BLK = 128
N_BUCKETS = 32
MAX_DISTANCE = 2048
DK = 128
DV = 128
V_HEADS_B = 8
QK_HEADS_B = 4
WIDTH_BQK = 512
WIDTH_BV = 1024
CONV_W = 4
CONV_DIM = 2048
CHUNK = 128
SUB = 64
D_FF = 5632
FFN_CONV_W = 3
EPS = 1e-6
NEG = -1e30
ATT_SCALE = HEAD_DIM ** -0.5
QK_SCALE = DK ** -0.5

OFF_AQ, OFF_AK, OFF_AV = 0, 1024, 2048
OFF_BQ, OFF_BK, OFF_BV, OFF_BZ = 3072, 3584, 4096, 5120
OFF_GATES = 6144
PROJ_MAIN = 6144

MIB = 2 ** 20


def _params(semantics, vmem_mib):
    return pltpu.CompilerParams(dimension_semantics=semantics, vmem_limit_bytes=vmem_mib * MIB)


def _bdot(a, b):
    return jnp.dot(a.astype(BF16), b.astype(BF16), preferred_element_type=F32)


def _bdot_nt(a, b):
    return lax.dot_general(a.astype(BF16), b.astype(BF16), (((1,), (1,)), ((), ())),
                           preferred_element_type=F32)


def _bdot_tn(a, b):
    return lax.dot_general(a.astype(BF16), b.astype(BF16), (((0,), (0,)), ((), ())),
                           preferred_element_type=F32)


def _fdot(a, b):
    return jnp.dot(a, b, preferred_element_type=F32, precision=lax.Precision.HIGHEST)


def _silu(x):
    return x * (1.0 / (1.0 + jnp.exp(-x)))


def _sigmoid(x):
    return 1.0 / (1.0 + jnp.exp(-x))


def _softplus(x):
    return jnp.maximum(x, 0.0) + jnp.log(1.0 + jnp.exp(-jnp.abs(x)))


def _gelu_tanh(x):
    c = math.sqrt(2.0 / math.pi)
    half = 0.5 * x
    return half + half * jnp.tanh(x * (c + (c * 0.044715) * (x * x)))


def _rms(x, w):
    return x * lax.rsqrt(jnp.mean(x * x, axis=-1, keepdims=True) + EPS) * w


def _rmsnorm_body(x_ref, w_ref, o_ref):
    o_ref[...] = _rms(x_ref[...], w_ref[...]).astype(o_ref.dtype)


def _rmsnorm(x, w, tm):
    m, d = x.shape
    return pl.pallas_call(
        _rmsnorm_body,
        grid=(m // tm,),
        in_specs=[pl.BlockSpec((tm, d), lambda i: (i, 0)),
                  pl.BlockSpec((1, d), lambda i: (0, 0))],
        out_specs=pl.BlockSpec((tm, d), lambda i: (i, 0)),
        out_shape=jax.ShapeDtypeStruct((m, d), BF16),
        compiler_params=_params(("arbitrary",), 40),
        name="rmsnorm",
    )(x, w)


def _matmul_body(x_ref, w_ref, o_ref):
    o_ref[...] = jnp.dot(x_ref[...], w_ref[...], preferred_element_type=F32)


def _matmul(x, w, tm, tn, name, n=None, n0=0):
    m, k = x.shape
    n = w.shape[1] if n is None else n
    j0 = n0 // tn
    return pl.pallas_call(
        _matmul_body,
        grid=((n - n0) // tn, m // tm),
        in_specs=[pl.BlockSpec((tm, k), lambda j, i: (i, 0)),
                  pl.BlockSpec((k, tn), lambda j, i: (0, j0 + j))],
        out_specs=pl.BlockSpec((tm, tn), lambda j, i: (i, j)),
        out_shape=jax.ShapeDtypeStruct((m, n - n0), F32),
        compiler_params=_params(("arbitrary", "arbitrary"), 48),
        name=name,
    )(x, w)


def _norm_matmul_body(x_ref, ln_ref, w_ref, o_ref, h_ref):
    tm = x_ref.shape[0]
    for r in range(0, tm, NORM_ROWS):
        rows = slice(r, r + NORM_ROWS)
        h = _rms(x_ref[rows, :], ln_ref[...]).astype(BF16)
        h_ref[rows, :] = h
        o_ref[rows, :] = jnp.dot(h, w_ref[...], preferred_element_type=F32)


def _norm_matmul(x, ln, w, n, tm, tn, name):
    m, k = x.shape
    return pl.pallas_call(
        _norm_matmul_body,
        grid=(m // tm, n // tn),
        in_specs=[pl.BlockSpec((tm, k), lambda i, j: (i, 0)),
                  pl.BlockSpec((1, k), lambda i, j: (0, 0)),
                  pl.BlockSpec((k, tn), lambda i, j: (0, j))],
        out_specs=[pl.BlockSpec((tm, tn), lambda i, j: (i, j)),
                   pl.BlockSpec((tm, k), lambda i, j: (i, 0))],
        out_shape=[jax.ShapeDtypeStruct((m, n), F32), jax.ShapeDtypeStruct((m, k), BF16)],
        compiler_params=_params(("arbitrary", "arbitrary"), 48),
        name=name,
    )(x, ln, w)


def _rel_bucket_np(dist):
    dist = np.asarray(dist, np.int64)
    max_exact = N_BUCKETS // 2
    d = np.maximum(dist, 1).astype(np.float64)
    val = np.log(d / max_exact) / math.log(MAX_DISTANCE / max_exact) * (N_BUCKETS - max_exact)
    frac = np.abs(val - np.round(val))
    near = (frac < 2e-5) &(dist >= max_exact) & (dist != max_exact) & (dist < MAX_DISTANCE)
    assert not near.any(), "distance on a bucket boundary"
    val = np.where(dist == max_exact, 0.0, val)
    large = np.minimum(max_exact + np.trunc(val).astype(np.int64), N_BUCKETS - 1)
    return np.where(dist < max_exact, dist, large).astype(np.int32)


def _prompt_bucket_tables():
    qi = np.arange(BLK)[:, None]
    kj = np.arange(2 * BLK)[None, :]
    delta = BLK + qi - kj
    inwin = (delta >= 0) & (delta <= BLK)
    tabs = []
    for dil in DILATIONS:
        b = _rel_bucket_np(np.clip(delta, 0, BLK) * dil)
        tabs.append(np.where(inwin, b, -1))
    return np.stack(tabs).astype(np.int32)


def _sample_bucket_tables():
    j = BLK - np.arange(BLK)
    return np.stack([_rel_bucket_np(j * dil)[None, :] for dil in DILATIONS]).astype(np.int32)


def _attn_prompt_body(bucket_ref, relb_ref, q_ref, k_ref, v_ref, o_ref, wk_ref, wv_ref,
                      bias_scr, acc_scr, m_scr, l_scr):
    h = pl.program_id(1)
    s, keep = k_ref.shape[1], wk_ref.shape[1] // HEADS_A
    wk_ref[0, pl.ds(h, keep, stride=HEADS_A), :] = k_ref[0, s - keep:s, :]
    wv_ref[0, pl.ds(h, keep, stride=HEADS_A), :] = v_ref[0, s - keep:s, :]
    col = lax.broadcasted_iota(jnp.int32, (BLK, 2 * BLK), 1)
    tables = _prompt_bucket_tables()
    for br in range(3):
        bk = bucket_ref[br]
        bias = jnp.zeros((BLK, 2 * BLK), F32)
        for kb in sorted(set(tables[br].ravel().tolist()) - {-1}):
            bias = jnp.where(bk == kb, relb_ref[kb, h], bias)
        full = jnp.where(bk >= 0, bias, NEG)
        bias_scr[2 * br] = full
        bias_scr[2 * br + 1] = jnp.where(col >= BLK, full, NEG)

    def run_branch(br, dil, is_first_branch, is_last_branch):
        shift = int(math.log2(dil))
        span = BLK * dil
        stride = None if dil == 1 else dil

        def rows(start):
            return pl.ds(start, BLK, stride=stride) if stride else pl.ds(start, BLK)

        nb = q_ref.shape[1] // span
        run_len = min(nb, ATTN_UNROLL)
        runs_per_it = ATTN_UNROLL // run_len
        runs_per_res = nb // run_len
        starts_at_zero = runs_per_res == 1

        def tasks(it, carry):
            q_starts, firsts, qs_, ks_, vs_ = [], [], [], [], []
            for rr in range(runs_per_it):
                ri = it * runs_per_it + rr
                n0 = (ri % runs_per_res) * run_len
                if dil == 1:
                    base = pl.multiple_of(n0 * span, BLK)
                else:
                    base = n0 * span + ri // runs_per_res
                first = jnp.where(n0 == 0, 1, 0)
                starts = [base + u * span for u in range(run_len)]
                kb = [k_ref[0, rows(st), :].astype(BF16) for st in starts]
                vb = [v_ref[0, rows(st), :].astype(BF16) for st in starts]
                if starts_at_zero:
                    k_prev, v_prev = None, None
                else:
                    p_start = base - span * (1 - first)
                    if dil == 1:
                        p_start = pl.multiple_of(p_start, BLK)
                    k_prev = k_ref[0, rows(p_start), :].astype(BF16)
                    v_prev = v_ref[0, rows(p_start), :].astype(BF16)
                for u, st in enumerate(starts):
                    q_starts.append(st)
                    qs_.append(q_ref[0, rows(st), :].astype(BF16))
                    kp, vp = (k_prev, v_prev) if u == 0 else (kb[u - 1], vb[u - 1])
                    if kp is None:
                        firsts.append(None)
                        ks_.append(kb[u])
                        vs_.append(vb[u])
                    else:
                        firsts.append(first if u == 0 else 0)
                        ks_.append(jnp.concatenate([kp, kb[u]], axis=0))
                        vs_.append(jnp.concatenate([vp, vb[u]], axis=0))
            if not is_first_branch:
                runs = [(m_scr[rows(qs), :], l_scr[rows(qs), :], acc_scr[rows(qs), :]) for qs in q_starts]
            ss = [_bdot_nt(q, k) * ATT_SCALE
                  + (bias_scr[2 * br, :, BLK:] if f is None else bias_scr[2 * br + f])
                  for q, k, f in zip(qs_, ks_, firsts)]
            ms = [jnp.max(s, axis=-1, keepdims=True) for s in ss]
            ps_ = [jnp.exp(s - m) for s, m in zip(ss, ms)]
            accs = [_bdot(p, jnp.concatenate([v, jnp.ones_like(v)], axis=1)) for p, v in zip(ps_, vs_)]
            outs = []
            for u in range(ATTN_UNROLL):
                m_b = jnp.broadcast_to(ms[u], (BLK, HEAD_DIM))
                l_b = accs[u][:, HEAD_DIM:]
                acc_t = accs[u][:, :HEAD_DIM]
                if not is_first_branch:
                    m_run, l_run, acc_run = runs[u]
                    m_new = jnp.maximum(m_run, m_b)
                    a = jnp.exp(m_run - m_new)
                    b = jnp.exp(m_b - m_new)
                    acc_t = a * acc_run + b * acc_t
                    l_b = a * l_run + b * l_b
                    m_b = m_new
                outs.append((m_b, l_b, acc_t))
            for qs, (m_b, l_b, acc_t) in zip(q_starts, outs):
                if is_last_branch:
                    o_ref[0, rows(qs), :] = (acc_t / l_b).astype(o_ref.dtype)
                else:
                    m_scr[rows(qs), :] = m_b
                    l_scr[rows(qs), :] = l_b
                    acc_scr[rows(qs), :] = acc_t
            return carry

        lax.fori_loop(0, nb * dil // ATTN_UNROLL, tasks, 0)

    run_branch(2, 16, True, False)
    run_branch(1, 4, False, False)
    run_branch(0, 1, False, True)


def _attn_prompt(proj3, rel_bias, keep):
    b, s, _ = proj3.shape
    buckets = jnp.asarray(_prompt_bucket_tables())
    blk = (1, s, HEAD_DIM)
    win = pl.BlockSpec((1, keep * HEADS_A, HEAD_DIM), lambda i, h: (i, 0, 0), pipeline_mode=pl.Buffered(1))
    win_shape = jax.ShapeDtypeStruct((b, keep * HEADS_A, HEAD_DIM), F32)
    return pl.pallas_call(
        _attn_prompt_body,
        grid=(b, HEADS_A),
        in_specs=[pl.BlockSpec((3, BLK, 2 * BLK), lambda i, h: (0, 0, 0)),
                  pl.BlockSpec(memory_space=pltpu.SMEM),
                  pl.BlockSpec(blk, lambda i, h: (i, 0, OFF_AQ // HEAD_DIM + h)),
                  pl.BlockSpec(blk, lambda i, h: (i, 0, OFF_AK // HEAD_DIM + h)),
                  pl.BlockSpec(blk, lambda i, h: (i, 0, OFF_AV // HEAD_DIM + h))],
        out_specs=[pl.BlockSpec(blk, lambda i, h: (i, 0, h)), win, win],
        out_shape=[jax.ShapeDtypeStruct((b, s, WIDTH_A), BF16), win_shape, win_shape],
        scratch_shapes=[pltpu.VMEM((6, BLK, 2 * BLK), F32),
                        pltpu.VMEM((s, HEAD_DIM), F32),
                        pltpu.VMEM((s, HEAD_DIM), F32),
                        pltpu.VMEM((s, HEAD_DIM), F32)],
        compiler_params=_params(("arbitrary", "arbitrary"), 48),
        name="attn_prompt",
    )(buckets, rel_bias, proj3, proj3, proj3)


def _attn_sample_body(bucket_ref, relbt_ref, q_ref, kn_ref, vn_ref,
                      k1_ref, k4_ref, k16_ref, v1_ref, v4_ref, v16_ref, o_ref, bias_scr):
    relbt = relbt_ref[...]
    tile = (HEADS_A, HEAD_DIM)

    @pl.when(pl.program_id(0) == 0)
    def _():
        for br in range(3):
            bk = bucket_ref[br]
            bias = jnp.zeros((BLK,) + tile, F32)
            for kb in range(N_BUCKETS):
                col = jnp.broadcast_to(relbt[:, kb:kb + 1], tile)
                bias = jnp.where(bk == kb, col[None], bias)
            bias_scr[br] = bias

    def lane_sum(x):
        return jnp.broadcast_to(jnp.sum(x, axis=-1, keepdims=True), x.shape)

    q = q_ref[0]
    s_self = lane_sum(q * kn_ref[0]) * ATT_SCALE + jnp.broadcast_to(relbt[:, 0:1], tile)
    scores = []
    m = s_self
    for br, k_ref in enumerate((k1_ref, k4_ref, k16_ref)):
        s = lane_sum(k_ref[...] * q[None]) * ATT_SCALE + bias_scr[br]
        scores.append(s)
        m = jnp.maximum(m, jnp.max(s, axis=0))
    p_self = 3.0 * jnp.exp(s_self - m)
    l = p_self
    acc = p_self * vn_ref[0]
    for s, v_ref in zip(scores, (v1_ref, v4_ref, v16_ref)):
        p = jnp.exp(s - m[None])
        l = l + jnp.sum(p, axis=0)
        acc = acc + jnp.sum(p * v_ref[...], axis=0)
    o_ref[0] = (acc / l).astype(o_ref.dtype)


def _attn_sample(q, k_new, v_new, cache_k, cache_v, rel_bias):
    b, past = cache_k.shape[:2]
    tile = (HEADS_A, HEAD_DIM)
    buckets = jnp.asarray(np.broadcast_to(_sample_bucket_tables().reshape(3, BLK, 1, 1), (3, BLK) + tile))
    row = pl.BlockSpec((1,) + tile, lambda i: (i, 0, 0))
    views, specs = [], []
    for cache in (cache_k, cache_v):
        for dil in DILATIONS:
            views.append(cache.reshape((b, past // dil, dil) + tile))
            last = past // dil // BLK - 1
            specs.append(pl.BlockSpec((None, BLK, None) + tile,
                                      functools.partial(lambda last, i: (i, last, 0, 0, 0), last)))
    return pl.pallas_call(
        _attn_sample_body,
        grid=(b,),
        in_specs=[pl.BlockSpec((3, BLK) + tile, lambda i: (0, 0, 0, 0)),
                  pl.BlockSpec((HEADS_A, N_BUCKETS), lambda i: (0, 0)),
                  row, row, row] + specs,
        out_specs=row,
        out_shape=jax.ShapeDtypeStruct((b,) + tile, BF16),
        scratch_shapes=[pltpu.VMEM((3, BLK) + tile, F32)],
        compiler_params=_params(("arbitrary",), 40),
        name="attn_sample",
    )(buckets, rel_bias.T, q, k_new, v_new, *views)


GROUP = 4
ATTN_UNROLL = 8
OUTPROJ_ROWS = 128
NORM_ROWS = 128
FFN_TM = 1024
FFN_TF = 512
FFN_ROWS = 64
FFN_COLS = 512
FFN_PIECE = 256


def _dn_prompt_body(q_ref, k_ref, v_ref, z_ref, h_ref, wgate_ref, cw_ref, cs_ref, s0_ref, alog_ref, dtb_ref, nw_ref,
                    o_ref, s_out_ref,
                    s_scr, e_scr, qn_scr, kn_scr, vv_scr, g_scr, beta_scr,
                    w_scr, u_scr, qg_scr, kdt_scr, attn_scr, gl_scr, o_scr, *, tt):
    t = pl.program_id(1)
    nt = pl.num_programs(1)

    @pl.when(t == 0)
    def _():
        s_scr[...] = s0_ref[0]
        e_scr[5:8, :] = cs_ref[0]

    e_scr[8:8 + tt, 0:WIDTH_BQK] = q_ref[0]
    e_scr[8:8 + tt, WIDTH_BQK:2 * WIDTH_BQK] = k_ref[0]
    e_scr[8:8 + tt, 2 * WIDTH_BQK:CONV_DIM] = v_ref[0]

    def l2n(x):
        return x * lax.rsqrt(jnp.sum(x * x, axis=-1, keepdims=True) + EPS)

    for c0 in range(0, CONV_DIM, DK):
        cols = slice(c0, c0 + DK)
        w = cw_ref[:, cols]
        y = w[0:1, :] * e_scr[5:5 + tt, cols]
        for i in range(1, CONV_W):
            y = y + w[i:i + 1, :] * e_scr[5 + i:5 + i + tt, cols]
        y = _silu(y)
        if c0 < WIDTH_BQK:
            qn_scr[:, cols] = l2n(y) * QK_SCALE
        elif c0 < 2 * WIDTH_BQK:
            kn_scr[:, c0 - WIDTH_BQK:c0 - WIDTH_BQK + DK] = l2n(y)
        else:
            vv_scr[:, c0 - 2 * WIDTH_BQK:c0 - 2 * WIDTH_BQK + DK] = y
    e_scr[5:8, :] = e_scr[tt + 5:tt + 8, :]
    gates = jnp.dot(h_ref[0], wgate_ref[...], preferred_element_type=F32)
    beta_scr[...] = _sigmoid(gates)
    g_scr[...] = -jnp.exp(alog_ref[...]) * _softplus(gates + dtb_ref[...])

    ri = lax.broadcasted_iota(jnp.int32, (CHUNK, CHUNK), 0)
    ci = lax.broadcasted_iota(jnp.int32, (CHUNK, CHUNK), 1)
    tri = ri >= ci
    strict = ri > ci
    same_sub = (ri // SUB) == (ci // SUB)
    tril_ones = tri.astype(F32)
    nw = nw_ref[...]

    for c in range(tt // CHUNK):
        rows = slice(c * CHUNK, (c + 1) * CHUNK)
        beta_all = beta_scr[rows, :]
        gc_all = _fdot(tril_ones, g_scr[rows, :])
        gc_all_t = gc_all.T
        g_scr[rows, :] = gc_all
        for hq0 in range(0, QK_HEADS_B, GROUP):
            units = []
            for hq in range(hq0, hq0 + GROUP):
                qn = qn_scr[rows, hq * DK:(hq + 1) * DK]
                kn = kn_scr[rows, hq * DK:(hq + 1) * DK]
                kk = _bdot_nt(kn, kn)
                qk = _bdot_nt(qn, kn)
                for hv in range(2 * hq, 2 * hq + 2):
                    beta = beta_all[:, hv:hv + 1]
                    gc = gc_all[:, V_HEADS_B + hv:V_HEADS_B + hv + 1]
                    gc_row = gc_all_t[V_HEADS_B + hv:V_HEADS_B + hv + 1, :]
                    gc_last = gc_row[:, CHUNK - 1:CHUNK]
                    decay = jnp.exp(jnp.where(tri, gc - gc_row, NEG))
                    a = jnp.where(strict, beta * kk * decay, 0.0)
                    egc = jnp.exp(gc)
                    attn_scr[hv, rows, :] = (qk * decay).astype(BF16)
                    qg_scr[hv, rows, :] = (qn * egc).astype(BF16)
                    kd = kn * jnp.exp(gc_last - gc)
                    kdt_scr[hv, c * DK:(c + 1) * DK, :] = kd.T.astype(BF16)
                    gl_scr[hv, c * 8:(c + 1) * 8, :] = jnp.broadcast_to(jnp.exp(gc_last), (8, DV))
                    units.append((hv, a))
            ds = [jnp.where(same_sub, a, 0.0) for _, a in units]
            ns = [-dd for dd in ds]
            pws = ds
            for _ in range(SUB.bit_length() - 2):
                pws = [_bdot(pw, pw) for pw in pws]
                ns = [n + pw + _bdot(n, pw) for n, pw in zip(ns, pws)]
            ls = [jnp.where(same_sub, 0.0, a) for _, a in units]
            ps = [lo + _bdot(lo, n) for lo, n in zip(ls, ns)]
            ns = [n - (p + _bdot(n, p)) for n, p in zip(ns, ps)]
            xs = []
            for hv, _ in units:
                beta = beta_scr[rows, hv:hv + 1]
                kscale = beta * jnp.exp(g_scr[rows, V_HEADS_B + hv:V_HEADS_B + hv + 1])
                xs.append(jnp.concatenate([kn_scr[rows, (hv // 2) * DK:(hv // 2 + 1) * DK] * kscale,
                                           vv_scr[rows, hv * DV:(hv + 1) * DV] * beta], axis=-1))
            wus = [x + _bdot(n, x) for n, x in zip(ns, xs)]
            for wu, (hv, _) in zip(wus, units):
                w_scr[hv, rows, :] = wu[:, :DK].astype(BF16)
                u_scr[hv, rows, :] = wu[:, DK:]

    heads = range(V_HEADS_B)
    for c in range(tt // CHUNK):
        rows = slice(c * CHUNK, (c + 1) * CHUNK)
        states = [s_scr[hv] for hv in heads]
        states_b = [s.astype(BF16) for s in states]
        v_news = [u_scr[hv, rows, :] - jnp.dot(w_scr[hv, rows, :], states_b[hv], preferred_element_type=F32)
                  for hv in heads]
        v_news_b = [v.astype(BF16) for v in v_news]
        for hv in heads:
            s_scr[hv] = (states[hv] * gl_scr[hv, c * 8:c * 8 + 1, :]
                         + jnp.dot(kdt_scr[hv, c * DK:(c + 1) * DK, :], v_news_b[hv],
                                   preferred_element_type=F32))
        for hv in heads:
            o_scr[rows, hv * DV:(hv + 1) * DV] = (
                jnp.dot(qg_scr[hv, rows, :], states_b[hv], preferred_element_type=F32)
                + jnp.dot(attn_scr[hv, rows, :], v_news_b[hv], preferred_element_type=F32))

    for hv in heads:
        o = o_scr[:, hv * DV:(hv + 1) * DV]
        z = z_ref[0, :, hv * DV:(hv + 1) * DV]
        o = o * lax.rsqrt(jnp.mean(o * o, axis=-1, keepdims=True) + EPS) * nw * _silu(z)
        o_ref[0, :, hv * DV:(hv + 1) * DV] = o.astype(o_ref.dtype)

    @pl.when(t == nt - 1)
    def _():
        s_out_ref[0] = s_scr[...]


def _dn_prompt(proj3, h3, w_gate, conv_state, s0, conv_w, alog_vec, dtb_vec, norm_w, tt):
    b, s, _ = proj3.shape
    body = functools.partial(_dn_prompt_body, tt=tt)
    nh = V_HEADS_B
    in_specs = [
        pl.BlockSpec((1, tt, WIDTH_BQK), lambda i, t: (i, t, OFF_BQ // WIDTH_BQK)),
        pl.BlockSpec((1, tt, WIDTH_BQK), lambda i, t: (i, t, OFF_BK // WIDTH_BQK)),
        pl.BlockSpec((1, tt, WIDTH_BV), lambda i, t: (i, t, OFF_BV // WIDTH_BV)),
        pl.BlockSpec((1, tt, WIDTH_BV), lambda i, t: (i, t, OFF_BZ // WIDTH_BV)),
        pl.BlockSpec((1, tt, D_MODEL), lambda i, t: (i, t, 0)),
        pl.BlockSpec((D_MODEL, 128), lambda i, t: (0, 0)),
        pl.BlockSpec((CONV_W, CONV_DIM), lambda i, t: (0, 0)),
        pl.BlockSpec((1, CONV_W - 1, CONV_DIM), lambda i, t: (i, 0, 0)),
        pl.BlockSpec((1, nh, DK, DV), lambda i, t: (i, 0, 0, 0)),
        pl.BlockSpec((1, 128), lambda i, t: (0, 0)),
        pl.BlockSpec((1, 128), lambda i, t: (0, 0)),
        pl.BlockSpec((1, DV), lambda i, t: (0, 0)),
    ]
    return pl.pallas_call(
        body,
        grid=(b, s // tt),
        in_specs=in_specs,
        out_specs=[pl.BlockSpec((1, tt, WIDTH_BV), lambda i, t: (i, t, 0)),
                   pl.BlockSpec((1, nh, DK, DV), lambda i, t: (i, 0, 0, 0))],
        out_shape=[jax.ShapeDtypeStruct((b, s, WIDTH_BV), BF16),
                   jax.ShapeDtypeStruct((b, nh, DK, DV), F32)],
        scratch_shapes=[pltpu.VMEM((nh, DK, DV), F32),
                        pltpu.VMEM((tt + 8, CONV_DIM), F32),
                        pltpu.VMEM((tt, WIDTH_BQK), F32),
                        pltpu.VMEM((tt, WIDTH_BQK), F32),
                        pltpu.VMEM((tt, WIDTH_BV), F32),
                        pltpu.VMEM((tt, 128), F32),
                        pltpu.VMEM((tt, 128), F32),
                        pltpu.VMEM((nh, tt, DK), BF16),
                        pltpu.VMEM((nh, tt, DV), F32),
                        pltpu.VMEM((nh, tt, DK), BF16),
                        pltpu.VMEM((nh, tt // CHUNK * DK, CHUNK), BF16),
                        pltpu.VMEM((nh, tt, CHUNK), BF16),
                        pltpu.VMEM((nh, tt // CHUNK * 8, DV), F32),
                        pltpu.VMEM((tt, WIDTH_BV), F32)],
        compiler_params=_params(("arbitrary", "arbitrary"), 48),
        name="deltanet_prompt",
    )(proj3, proj3, proj3, proj3, h3, w_gate, conv_w, conv_state, s0, alog_vec, dtb_vec, norm_w)


def _dn_sample_body(proj_ref, gates_ref, cw_ref, cs_ref, s0_ref, alog_ref, dtb_ref, nw_ref,
                    o_ref, cs_out_ref, s_out_ref):
    pre = proj_ref[0, :, OFF_BQ:OFF_BQ + CONV_DIM]
    buf = cs_ref[0]
    w = cw_ref[...]
    y = w[CONV_W - 1:CONV_W, :] * pre
    for i in range(CONV_W - 1):
        y = y + w[i:i + 1, :] * buf[i:i + 1, :]
    y = _silu(y)
    cs_out_ref[0, 0:CONV_W - 2, :] = buf[1:CONV_W - 1, :]
    cs_out_ref[0, CONV_W - 2:CONV_W - 1, :] = pre

    gates = gates_ref[0]
    beta_all = _sigmoid(gates)
    g_all = -jnp.exp(alog_ref[...]) * _softplus(gates + dtb_ref[...])
    nw = nw_ref[...]

    def l2n(x):
        return x * lax.rsqrt(jnp.sum(x * x, axis=-1, keepdims=True) + EPS)

    row8 = lax.broadcasted_iota(jnp.int32, (8, DK), 0) == 0
    for hv in range(V_HEADS_B):
        hq = hv // 2
        q = l2n(y[:, hq * DK:(hq + 1) * DK]) * QK_SCALE
        k = l2n(y[:, WIDTH_BQK + hq * DK:WIDTH_BQK + (hq + 1) * DK])
        v = y[:, 2 * WIDTH_BQK + hv * DV:2 * WIDTH_BQK + (hv + 1) * DV]
        beta = beta_all[:, hv:hv + 1]
        g = g_all[:, V_HEADS_B + hv:V_HEADS_B + hv + 1]
        eg = jnp.exp(g)
        state = s0_ref[0, hv]

        def pad8(x):
            return jnp.where(row8, jnp.broadcast_to(x, (8, x.shape[-1])), 0.0)

        v_new = v * beta - _bdot(pad8(k * (beta * eg)), state)[0:1, :]
        qk = jnp.sum(q.astype(BF16).astype(F32) * k.astype(BF16).astype(F32), axis=-1, keepdims=True)
        o = _bdot(pad8(q * eg), state)[0:1, :] + qk.astype(BF16).astype(F32) * v_new.astype(BF16).astype(F32)
        s_out_ref[0, hv] = state * eg + _bdot_tn(pad8(k), pad8(v_new))
        z = proj_ref[0, :, OFF_BZ + hv * DV:OFF_BZ + (hv + 1) * DV]
        o = o * lax.rsqrt(jnp.mean(o * o, axis=-1, keepdims=True) + EPS) * nw * _silu(z)
        o_ref[0, :, hv * DV:(hv + 1) * DV] = o.astype(o_ref.dtype)


def _dn_sample(proj, gates, conv_state, s0, conv_w, alog_vec, dtb_vec, norm_w):
    b = proj.shape[0]
    return pl.pallas_call(
        _dn_sample_body,
        grid=(b,),
        in_specs=[pl.BlockSpec((1, 1, PROJ_MAIN), lambda i: (i, 0, 0)),
                  pl.BlockSpec((1, 1, 128), lambda i: (i, 0, 0)),
                  pl.BlockSpec((CONV_W, CONV_DIM), lambda i: (0, 0)),
                  pl.BlockSpec((1, CONV_W - 1, CONV_DIM), lambda i: (i, 0, 0)),
                  pl.BlockSpec((1, V_HEADS_B, DK, DV), lambda i: (i, 0, 0, 0)),
                  pl.BlockSpec((1, 128), lambda i: (0, 0)),
                  pl.BlockSpec((1, 128), lambda i: (0, 0)),
                  pl.BlockSpec((1, DV), lambda i: (0, 0))],
        out_specs=[pl.BlockSpec((1, 1, WIDTH_BV), lambda i: (i, 0, 0)),
                   pl.BlockSpec((1, CONV_W - 1, CONV_DIM), lambda i: (i, 0, 0)),
                   pl.BlockSpec((1, V_HEADS_B, DK, DV), lambda i: (i, 0, 0, 0))],
        out_shape=[jax.ShapeDtypeStruct((b, 1, WIDTH_BV), BF16),
                   jax.ShapeDtypeStruct((b, CONV_W - 1, CONV_DIM), F32),
                   jax.ShapeDtypeStruct((b, V_HEADS_B, DK, DV), F32)],
        compiler_params=_params(("arbitrary",), 40),
        name="deltanet_sample",
    )(proj, gates, conv_w, conv_state, s0, alog_vec, dtb_vec, norm_w)


def _outproj_body(att_ref, dn_ref, wa_ref, wb_ref, x_ref, lnpost_ref, lnpre_ref, x1_ref, h2_ref):
    tm = x_ref.shape[0]
    piece = min(tm, OUTPROJ_ROWS)
    for rows in [slice(r, r + piece) for r in range(0, tm, piece)]:
        mix = (jnp.dot(att_ref[rows, :], wa_ref[...], preferred_element_type=F32)
               + jnp.dot(dn_ref[rows, :], wb_ref[...], preferred_element_type=F32))
        x1 = x_ref[rows, :] + _rms(mix, lnpost_ref[...])
        x1_ref[rows, :] = x1
        h2_ref[rows, :] = _rms(x1, lnpre_ref[...]).astype(h2_ref.dtype)


def _outproj(att, dn, w, x, ln_post, ln_pre, tm):
    m, d = x.shape
    assert WIDTH_A == WIDTH_BV
    return pl.pallas_call(
        _outproj_body,
        grid=(m // tm,),
        in_specs=[pl.BlockSpec((tm, WIDTH_A), lambda i: (i, 0)),
                  pl.BlockSpec((tm, WIDTH_BV), lambda i: (i, 0)),
                  pl.BlockSpec((WIDTH_A, d), lambda i: (0, 0)),
                  pl.BlockSpec((WIDTH_BV, d), lambda i: (1, 0)),
                  pl.BlockSpec((tm, d), lambda i: (i, 0)),
                  pl.BlockSpec((1, d), lambda i: (0, 0)),
                  pl.BlockSpec((1, d), lambda i: (0, 0))],
        out_specs=[pl.BlockSpec((tm, d), lambda i: (i, 0)),
                   pl.BlockSpec((tm, d), lambda i: (i, 0))],
        out_shape=[jax.ShapeDtypeStruct((m, d), F32),
                   jax.ShapeDtypeStruct((m, d), BF16)],
        compiler_params=_params(("arbitrary",), 48),
        name="outproj",
    )(att, dn, w, w, x, ln_post, ln_pre)


def _ffn_body(*refs, tm, tiles_per_seq, single_token):
    if single_token:
        (h_ref, wg_ref, wv_ref, cwb_ref, wo_prev_ref, wo_last_ref, x1_hbm, ln_ref, pg_ref, pv_ref,
         o_ref, ng_ref, nv_ref, eg_scr, ev_scr, carry_scr, act_scr, x1_scr, x1_sem) = refs
    else:
        (h_ref, wg_ref, wv_ref, cwb_ref, wo_prev_ref, wo_last_ref, x1_hbm, ln_ref, prev_ref,
         o_ref, new_ref, eg_scr, ev_scr, carry_scr, act_scr, x1_scr, x1_sem) = refs
    i = pl.program_id(0)
    j = pl.program_id(1)
    nj = pl.num_programs(1)
    d = o_ref.shape[-1]
    tf = act_scr.shape[-1]
    cur = j % 2
    act_cur = act_scr.at[cur]
    act_prev = act_scr.at[1 - cur]

    def x1_copy():
        return pltpu.make_async_copy(x1_hbm.at[pl.ds(pl.multiple_of(i * tm, tm), tm), :], x1_scr, x1_sem)

    def down_proj(act_ref, wo_ref):
        for n in range(0, d, FFN_COLS):
            o_ref[:, n:n + FFN_COLS] += jnp.dot(act_ref[...], wo_ref[:, n:n + FFN_COLS],
                                                preferred_element_type=F32)

    @pl.when(j == 0)
    def _():
        x1_copy().start()
        o_ref[...] = jnp.zeros_like(o_ref)
        act_prev[...] = jnp.zeros_like(act_prev)

    if single_token:
        def up_conv(w_ref, tile, prev_ref, new_ref):
            up = jnp.dot(h_ref[...], w_ref[...], preferred_element_type=F32)
            cw = cwb_ref[tile]
            new_ref[...] = up
            return cw[0:1, :] * prev_ref[0] + cw[1:2, :] * prev_ref[1] + cw[2:3, :] * up + cw[3:4, :]

        gate = up_conv(wg_ref, j, pg_ref, ng_ref)
        val = up_conv(wv_ref, nj + j, pv_ref, nv_ref)
        act_cur[...] = (_gelu_tanh(gate) * val).astype(BF16)
        down_proj(act_prev, wo_prev_ref)
    else:
        first_tile = i % tiles_per_seq == 0
        pieces = [slice(c, c + FFN_PIECE) for c in range(0, tf, FFN_PIECE)]

        def up_proj(cols):
            for w_ref, tile, e_scr, slot in ((wg_ref, j, eg_scr, 0), (wv_ref, nj + j, ev_scr, 1)):
                e_scr[8:8 + tm, cols] = jnp.dot(h_ref[...], w_ref[:, cols], preferred_element_type=F32)
                e_scr[6:8, cols] = jnp.where(first_tile, prev_ref[0, tile, :, cols],
                                             carry_scr[slot, j, 6:8, cols])
                tail = e_scr[tm + 6:tm + 8, cols]
                carry_scr[slot, j, 6:8, cols] = tail
                new_ref[0, tile, :, cols] = tail

        def conv(e_scr, tile, cols, r):
            cw = cwb_ref[tile, :, cols]
            return (cw[0:1, :] * e_scr[6 + r:6 + r + FFN_ROWS, cols]
                    + cw[1:2, :] * e_scr[7 + r:7 + r + FFN_ROWS, cols]
                    + cw[2:3, :] * e_scr[8 + r:8 + r + FFN_ROWS, cols] + cw[3:4, :])

        def conv_geglu(cols):
            for r in range(0, tm, FFN_ROWS):
                act_cur[r:r + FFN_ROWS, cols] = (_gelu_tanh(conv(eg_scr, j, cols, r))
                                                 * conv(ev_scr, nj + j, cols, r)).astype(BF16)

        up_proj(pieces[0])
        for c in range(1, len(pieces)):
            up_proj(pieces[c])
            conv_geglu(pieces[c - 1])
        down_proj(act_prev, wo_prev_ref)
        conv_geglu(pieces[-1])

    @pl.when(j == nj - 1)
    def _():
        down_proj(act_cur, wo_last_ref)
        x1_copy().wait()
        o_ref[...] = x1_scr[...] + _rms(o_ref[...], ln_ref[...])


def _ffn(h2, w_in, conv_w, conv_b, w_out, x1, ln_post, prev, tm, tf, seq_len):
    m, d = h2.shape
    single = seq_len == 1
    nj = D_FF // tf
    tiles_per_seq = 1 if single else seq_len // tm
    cwb = jnp.concatenate([conv_w, conv_b], axis=0).reshape(FFN_CONV_W + 1, 2 * nj, tf).transpose(1, 0, 2)
    cwb_spec = pl.BlockSpec((2 * nj, FFN_CONV_W + 1, tf), lambda i, j: (0, 0, 0))
    if single:
        prev_args = (prev, prev)
        prev_specs = [pl.BlockSpec((2, tm, tf), lambda i, j: (0, i, j)),
                      pl.BlockSpec((2, tm, tf), lambda i, j: (0, i, nj + j))]
        new_specs = [pl.BlockSpec((tm, tf), lambda i, j: (i, j))] * 2
        new_shapes = [jax.ShapeDtypeStruct((m, D_FF), F32)] * 2
    else:
        prev_args = (prev.reshape(-1, 2, 2 * nj, tf).transpose(0, 2, 1, 3),)
        prev_specs = [pl.BlockSpec((1, 2 * nj, 2, tf), lambda i, j: (i // tiles_per_seq, 0, 0, 0))]
        new_specs = [pl.BlockSpec((1, 2 * nj, 2, tf), lambda i, j: (i, 0, 0, 0))]
        new_shapes = [jax.ShapeDtypeStruct((m // tm, 2 * nj, 2, tf), F32)]
    body = functools.partial(_ffn_body, tm=tm, tiles_per_seq=tiles_per_seq, single_token=single)
    once = dict(pipeline_mode=pl.Buffered(1)) if tm >= 1024 else {}
    outs = pl.pallas_call(
        body,
        grid=(m // tm, nj),
        in_specs=[pl.BlockSpec((tm, d), lambda i, j: (i, 0), **once),
                  pl.BlockSpec((d, tf), lambda i, j: (0, j)),
                  pl.BlockSpec((d, tf), lambda i, j: (0, nj + j)),
                  cwb_spec,
                  pl.BlockSpec((tf, d), lambda i, j: (jnp.maximum(j - 1, 0), 0)),
                  pl.BlockSpec((tf, d), lambda i, j: (nj - 1, 0), pipeline_mode=pl.Buffered(1)),
                  pl.BlockSpec(memory_space=pl.ANY),
                  pl.BlockSpec((1, d), lambda i, j: (0, 0))] + prev_specs,
        out_specs=[pl.BlockSpec((tm, d), lambda i, j: (i, 0), **once)] + new_specs,
        out_shape=[jax.ShapeDtypeStruct((m, d), F32)] + new_shapes,
        scratch_shapes=[pltpu.VMEM((tm + 8, tf), F32),
                        pltpu.VMEM((tm + 8, tf), F32),
                        pltpu.VMEM((2, nj, 8, tf), F32),
                        pltpu.VMEM((2, tm, tf), BF16),
                        pltpu.VMEM((tm, d), F32),
                        pltpu.SemaphoreType.DMA(())],
        compiler_params=_params(("arbitrary", "arbitrary"), 57),
        name="convffn",
    )(h2, w_in, w_in, cwb, w_out, w_out, x1, ln_post, *prev_args)
    if single:
        y, new_g, new_v = outs
        return y, jnp.concatenate([new_g, new_v], axis=-1)
    y, new = outs
    return y, new.transpose(0, 2, 1, 3).reshape(m // tm, 2, 2 * D_FF)


def _cache_shift_body(ck_ref, cv_ref, ck_next_ref, cv_next_ref, nk_ref, nv_ref, ok_ref, ov_ref, *, tr):
    last = pl.program_id(1) == pl.num_programs(1) - 1
    for c_ref, nxt_ref, n_ref, o_ref in ((ck_ref, ck_next_ref, nk_ref, ok_ref),
                                         (cv_ref, cv_next_ref, nv_ref, ov_ref)):
        o_ref[0, 0:tr - 1] = c_ref[0, 1:tr]
        o_ref[0, tr - 1] = jnp.where(last, n_ref[0, 0], nxt_ref[0, 0])


def _cache_shift(cache_k, cache_v, new_k, new_v, tr):
    nb, rows, nh, dh = cache_k.shape
    main = pl.BlockSpec((1, tr, nh, dh), lambda b, i: (b, i, 0, 0))
    nxt = pl.BlockSpec((1, 1, nh, dh), lambda b, i: (b, jnp.minimum((i + 1) * tr, rows - 1), 0, 0))
    new = pl.BlockSpec((1, 1, nh, dh), lambda b, i: (b, 0, 0, 0))
    shape = jax.ShapeDtypeStruct(cache_k.shape, cache_k.dtype)
    return pl.pallas_call(
        functools.partial(_cache_shift_body, tr=tr),
        grid=(nb, rows // tr),
        in_specs=[main, main, nxt, nxt, new, new],
        out_specs=[main, main],
        out_shape=[shape, shape],
        compiler_params=_params(("arbitrary", "arbitrary"), 40),
        name="cache_shift",
    )(cache_k, cache_v, cache_k, cache_v, new_k, new_v)


def _lane_vec(values, offset):
    return jnp.zeros((1, 128), F32).at[0, offset:offset + V_HEADS_B].set(values.astype(F32))


def kernel(x_prompt, x_sample, cache_win_k, cache_win_v, state_dn_conv, state_dn_rec, state_ffn_conv,
           rel_bias, ln_mix_pre, w_in, dn_conv_w, dn_A_log, dn_dt_bias, dn_norm_w, w_out, ln_mix_post,
           ln_ffn_pre, w_ffn_in, ffn_conv_w, ffn_conv_b, w_ffn_out, ln_ffn_post):
    bp, sp, d = x_prompt.shape
    bs = x_sample.shape[0]
    l = 0

    w_main = w_in[l].astype(BF16)
    w_gate = jnp.pad(w_main[:, PROJ_MAIN:], ((0, 0), (0, 128 - 2 * V_HEADS_B)))
    wo = w_out[l].astype(BF16)
    wf_in = w_ffn_in[l].astype(BF16)
    wf_out = w_ffn_out[l].astype(BF16)
    ln1 = ln_mix_pre[l][None, :]
    ln2 = ln_mix_post[l][None, :]
    ln3 = ln_ffn_pre[l][None, :]
    ln4 = ln_ffn_post[l][None, :]
    conv_w = dn_conv_w[l]
    alog_vec = _lane_vec(dn_A_log[l], V_HEADS_B)
    dtb_vec = _lane_vec(dn_dt_bias[l], V_HEADS_B)
    norm_w = dn_norm_w[l][None, :]
    fcw = ffn_conv_w[l]
    fcb = ffn_conv_b[l][None, :]

    xp = x_prompt.reshape(bp * sp, d)
    proj_p, hp = _norm_matmul(xp, ln1, w_main, PROJ_MAIN, 1024, 1024, "inproj_prompt")
    proj3 = proj_p.reshape(bp, sp, PROJ_MAIN)
    keep = min(MAX_DISTANCE, sp)
    att_p, win_k, win_v = _attn_prompt(proj3, rel_bias, keep)
    dn_p, p_dn_rec = _dn_prompt(
        proj3, hp.reshape(bp, sp, d), w_gate,
        jnp.zeros((bp, CONV_W - 1, CONV_DIM), F32), jnp.zeros((bp, V_HEADS_B, DK, DV), F32),
        conv_w, alog_vec, dtb_vec, norm_w, 256)
    x1_p, h2_p = _outproj(att_p.reshape(bp * sp, WIDTH_A), dn_p.reshape(bp * sp, WIDTH_BV),
                          wo, xp, ln2, ln3, 512)
    y_p, fc = _ffn(h2_p, wf_in, fcw, fcb, wf_out, x1_p, ln4,
                   jnp.zeros((bp, FFN_CONV_W - 1, 2 * D_FF), F32), FFN_TM, FFN_TF, sp)
    p_win_k = win_k.reshape(1, bp, keep, HEADS_A, HEAD_DIM)
    p_win_v = win_v.reshape(1, bp, keep, HEADS_A, HEAD_DIM)
    p_dn_conv = proj3[:, sp - (CONV_W - 1):, OFF_BQ:OFF_BQ + CONV_DIM][None]
    tiles = sp // FFN_TM
    p_ffn_conv = fc[tiles - 1::tiles][None]

    xs = x_sample.reshape(bs, d)
    hs = _rmsnorm(xs, ln1, bs)
    proj_s = _matmul(hs, w_main, bs, 1024, "inproj_sample", n=PROJ_MAIN)
    gates_s = _matmul(hs, w_gate, bs, 128, "gates_sample")
    past = cache_win_k.shape[2]
    ck = cache_win_k[l]
    cv = cache_win_v[l]
    new_k = proj_s[:, OFF_AK:OFF_AK + WIDTH_A]
    new_v = proj_s[:, OFF_AV:OFF_AV + WIDTH_A]
    new_q = proj_s[:, OFF_AQ:OFF_AQ + WIDTH_A].reshape(bs, HEADS_A, HEAD_DIM)
    new_k = new_k.reshape(bs, HEADS_A, HEAD_DIM)
    new_v = new_v.reshape(bs, HEADS_A, HEAD_DIM)
    att_s = _attn_sample(new_q, new_k, new_v, ck, cv, rel_bias)
    s_win_k, s_win_v = _cache_shift(ck, cv, new_k[:, None], new_v[:, None], 512)
    dn_s, s_dn_conv, s_dn_rec = _dn_sample(proj_s[:, None], gates_s[:, None], state_dn_conv[l],
                                           state_dn_rec[l], conv_w, alog_vec, dtb_vec, norm_w)
    x1_s, h2_s = _outproj(att_s.reshape(bs, WIDTH_A), dn_s.reshape(bs, WIDTH_BV),
                          wo, xs, ln2, ln3, bs)
    prev_s = jnp.swapaxes(state_ffn_conv[l], 0, 1)
    y_s, up_s = _ffn(h2_s, wf_in, fcw, fcb, wf_out, x1_s, ln4, prev_s, bs, FFN_TF, 1)
    s_ffn_conv = jnp.stack([prev_s[1], up_s], axis=1)[None]

    return (y_p.reshape(bp, sp, d), y_s.reshape(bs, 1, d),
            p_win_k, p_win_v, p_dn_conv, p_dn_rec[None], p_ffn_conv,
            s_win_k[None], s_win_v[None], s_dn_conv[None], s_dn_rec[None], s_ffn_conv)
```

```python
import functools
import math

import numpy as np
import jax
import jax.numpy as jnp
from jax import lax
from jax.experimental import pallas as pl
from jax.experimental.pallas import tpu as pltpu

F32 = jnp.float32
BF16 = jnp.bfloat16

D_MODEL = 2048
HEAD_DIM = 128
WIDTH_A = 1024
HEADS_A = 8
DILATIONS = (1, 4, 16)
BLK = 128
N_BUCKETS = 32
MAX_DISTANCE = 2048
DK = 128
DV = 128
V_HEADS_B = 8
QK_HEADS_B = 4
WIDTH_BQK = 512
WIDTH_BV = 1024
CONV_W = 4
CONV_DIM = 2048
CHUNK = 128
SUB = 64
D_FF = 5632
FFN_CONV_W = 3
EPS = 1e-6
NEG = -1e30
ATT_SCALE = HEAD_DIM ** -0.5
QK_SCALE = DK ** -0.5

OFF_AQ, OFF_AK, OFF_AV = 0, 1024, 2048
OFF_BQ, OFF_BK, OFF_BV, OFF_BZ = 3072, 3584, 4096, 5120
OFF_GATES = 6144
PROJ_MAIN = 6144

MIB = 2 ** 20


def _params(semantics, vmem_mib):
    return pltpu.CompilerParams(dimension_semantics=semantics, vmem_limit_bytes=vmem_mib * MIB)


def _bdot(a, b):
    return jnp.dot(a.astype(BF16), b.astype(BF16), preferred_element_type=F32)


def _bdot_nt(a, b):
    return lax.dot_general(a.astype(BF16), b.astype(BF16), (((1,), (1,)), ((), ())),
                           preferred_element_type=F32)


def _bdot_tn(a, b):
    return lax.dot_general(a.astype(BF16), b.astype(BF16), (((0,), (0,)), ((), ())),
                           preferred_element_type=F32)


def _fdot(a, b):
    return jnp.dot(a, b, preferred_element_type=F32, precision=lax.Precision.HIGHEST)


def _silu(x):
    return x * (1.0 / (1.0 + jnp.exp(-x)))


def _sigmoid(x):
    return 1.0 / (1.0 + jnp.exp(-x))


def _softplus(x):
    return jnp.maximum(x, 0.0) + jnp.log(1.0 + jnp.exp(-jnp.abs(x)))


def _gelu_tanh(x):
    c = math.sqrt(2.0 / math.pi)
    half = 0.5 * x
    return half + half * jnp.tanh(x * (c + (c * 0.044715) * (x * x)))


def _rms(x, w):
    return x * lax.rsqrt(jnp.mean(x * x, axis=-1, keepdims=True) + EPS) * w


def _matmul_body(x_ref, w_ref, o_ref):
    o_ref[...] = jnp.dot(x_ref[...], w_ref[...], preferred_element_type=F32)


def _matmul(x, w, tm, tn, name, n=None, n0=0):
    m, k = x.shape
    n = w.shape[1] if n is None else n
    j0 = n0 // tn
    return pl.pallas_call(
        _matmul_body,
        grid=((n - n0) // tn, m // tm),
        in_specs=[pl.BlockSpec((tm, k), lambda j, i: (i, 0)),
                  pl.BlockSpec((k, tn), lambda j, i: (0, j0 + j))],
        out_specs=pl.BlockSpec((tm, tn), lambda j, i: (i, j)),
        out_shape=jax.ShapeDtypeStruct((m, n - n0), F32),
        compiler_params=_params(("arbitrary", "arbitrary"), 48),
        name=name,
    )(x, w)


def _norm_matmul_body(x_ref, ln_ref, w_ref, o_ref, h_ref):
    tm = x_ref.shape[0]
    piece = min(tm, NORM_ROWS)
    for r in range(0, tm, piece):
        rows = slice(r, r + piece)
        h = _rms(x_ref[rows, :], ln_ref[...]).astype(BF16)
        h_ref[rows, :] = h
        o_ref[rows, :] = jnp.dot(h, w_ref[...], preferred_element_type=F32)


def _norm_matmul(x, ln, w, n, tm, tn, name):
    m, k = x.shape
    return pl.pallas_call(
        _norm_matmul_body,
        grid=(m // tm, n // tn),
        in_specs=[pl.BlockSpec((tm, k), lambda i, j: (i, 0)),
                  pl.BlockSpec((1, k), lambda i, j: (0, 0)),
                  pl.BlockSpec((k, tn), lambda i, j: (0, j))],
        out_specs=[pl.BlockSpec((tm, tn), lambda i, j: (i, j)),
                   pl.BlockSpec((tm, k), lambda i, j: (i, 0))],
        out_shape=[jax.ShapeDtypeStruct((m, n), F32), jax.ShapeDtypeStruct((m, k), BF16)],
        compiler_params=_params(("arbitrary", "arbitrary"), 48),
        name=name,
    )(x, ln, w)


def _rel_bucket_np(dist):
    dist = np.asarray(dist, np.int64)
    max_exact = N_BUCKETS // 2
    d = np.maximum(dist, 1).astype(np.float64)
    val = np.log(d / max_exact) / math.log(MAX_DISTANCE / max_exact) * (N_BUCKETS - max_exact)
    frac = np.abs(val - np.round(val))
    near = (frac < 2e-5) &(dist >= max_exact) & (dist != max_exact) & (dist < MAX_DISTANCE)
    assert not near.any(), "distance on a bucket boundary"
    val = np.where(dist == max_exact, 0.0, val)
    large = np.minimum(max_exact + np.trunc(val).astype(np.int64), N_BUCKETS - 1)
    return np.where(dist < max_exact, dist, large).astype(np.int32)


def _prompt_bucket_tables():
    qi = np.arange(BLK)[:, None]
    kj = np.arange(2 * BLK)[None, :]
    delta = BLK + qi - kj
    inwin = (delta >= 0) & (delta <= BLK)
    tabs = []
    for dil in DILATIONS:
        b = _rel_bucket_np(np.clip(delta, 0, BLK) * dil)
        tabs.append(np.where(inwin, b, -1))
    return np.stack(tabs).astype(np.int32)


def _sample_bucket_tables():
    j = BLK - np.arange(BLK)
    return np.stack([_rel_bucket_np(j * dil)[None, :] for dil in DILATIONS]).astype(np.int32)


def _attn_prompt_body(bucket_ref, relb_ref, q_ref, k_ref, v_ref, o_ref, wk_ref, wv_ref,
                      bias_scr, acc_scr, m_scr, l_scr):
    h = pl.program_id(1)
    s, keep = k_ref.shape[1], wk_ref.shape[1] // HEADS_A
    wk_ref[0, pl.ds(h, keep, stride=HEADS_A), :] = k_ref[0, s - keep:s, :]
    wv_ref[0, pl.ds(h, keep, stride=HEADS_A), :] = v_ref[0, s - keep:s, :]
    col = lax.broadcasted_iota(jnp.int32, (BLK, 2 * BLK), 1)
    tables = _prompt_bucket_tables()
    for br in range(3):
        bk = bucket_ref[br]
        bias = jnp.zeros((BLK, 2 * BLK), F32)
        for kb in sorted(set(tables[br].ravel().tolist()) - {-1}):
            bias = jnp.where(bk == kb, relb_ref[kb, h], bias)
        full = jnp.where(bk >= 0, bias, NEG)
        bias_scr[2 * br] = full
        bias_scr[2 * br + 1] = jnp.where(col >= BLK, full, NEG)

    def run_branch(br, dil, is_first_branch, is_last_branch):
        shift = int(math.log2(dil))
        span = BLK * dil
        stride = None if dil == 1 else dil

        def rows(start):
            return pl.ds(start, BLK, stride=stride) if stride else pl.ds(start, BLK)

        nb = q_ref.shape[1] // span
        run_len = min(nb, ATTN_UNROLL)
        runs_per_it = ATTN_UNROLL // run_len
        runs_per_res = nb // run_len
        starts_at_zero = runs_per_res == 1

        def tasks(it, carry):
            q_starts, firsts, qs_, ks_, vs_ = [], [], [], [], []
            for rr in range(runs_per_it):
                ri = it * runs_per_it + rr
                n0 = (ri % runs_per_res) * run_len
                if dil == 1:
                    base = pl.multiple_of(n0 * span, BLK)
                else:
                    base = n0 * span + ri // runs_per_res
                first = jnp.where(n0 == 0, 1, 0)
                starts = [base + u * span for u in range(run_len)]
                kb = [k_ref[0, rows(st), :].astype(BF16) for st in starts]
                vb = [v_ref[0, rows(st), :].astype(BF16) for st in starts]
                if starts_at_zero:
                    k_prev, v_prev = None, None
                else:
                    p_start = base - span * (1 - first)
                    if dil == 1:
                        p_start = pl.multiple_of(p_start, BLK)
                    k_prev = k_ref[0, rows(p_start), :].astype(BF16)
                    v_prev = v_ref[0, rows(p_start), :].astype(BF16)
                for u, st in enumerate(starts):
                    q_starts.append(st)
                    qs_.append(q_ref[0, rows(st), :].astype(BF16))
                    kp, vp = (k_prev, v_prev) if u == 0 else (kb[u - 1], vb[u - 1])
                    if kp is None:
                        firsts.append(None)
                        ks_.append(kb[u])
                        vs_.append(vb[u])
                    else:
                        firsts.append(first if u == 0 else 0)
                        ks_.append(jnp.concatenate([kp, kb[u]], axis=0))
                        vs_.append(jnp.concatenate([vp, vb[u]], axis=0))
            if not is_first_branch:
                runs = [(m_scr[rows(qs), :], l_scr[rows(qs), :], acc_scr[rows(qs), :]) for qs in q_starts]
            ss = [_bdot_nt(q, k) * ATT_SCALE
                  + (bias_scr[2 * br, :, BLK:] if f is None else bias_scr[2 * br + f])
                  for q, k, f in zip(qs_, ks_, firsts)]
            ms = [jnp.max(s, axis=-1, keepdims=True) for s in ss]
            ps_ = [jnp.exp(s - m) for s, m in zip(ss, ms)]
            accs = [_bdot(p, jnp.concatenate([v, jnp.ones_like(v)], axis=1)) for p, v in zip(ps_, vs_)]
            outs = []
            for u in range(ATTN_UNROLL):
                m_b = jnp.broadcast_to(ms[u], (BLK, HEAD_DIM))
                l_b = accs[u][:, HEAD_DIM:]
                acc_t = accs[u][:, :HEAD_DIM]
                if not is_first_branch:
                    m_run, l_run, acc_run = runs[u]
                    m_new = jnp.maximum(m_run, m_b)
                    a = jnp.exp(m_run - m_new)
                    b = jnp.exp(m_b - m_new)
                    acc_t = a * acc_run + b * acc_t
                    l_b = a * l_run + b * l_b
                    m_b = m_new
                outs.append((m_b, l_b, acc_t))
            for qs, (m_b, l_b, acc_t) in zip(q_starts, outs):
                if is_last_branch:
                    o_ref[0, rows(qs), :] = (acc_t / l_b).astype(o_ref.dtype)
                else:
                    m_scr[rows(qs), :] = m_b
                    l_scr[rows(qs), :] = l_b
                    acc_scr[rows(qs), :] = acc_t
            return carry

        lax.fori_loop(0, nb * dil // ATTN_UNROLL, tasks, 0)

    run_branch(2, 16, True, False)
    run_branch(1, 4, False, False)
    run_branch(0, 1, False, True)


def _attn_prompt(proj3, rel_bias, keep):
    b, s, _ = proj3.shape
    buckets = jnp.asarray(_prompt_bucket_tables())
    blk = (1, s, HEAD_DIM)
    win = pl.BlockSpec((1, keep * HEADS_A, HEAD_DIM), lambda i, h: (i, 0, 0), pipeline_mode=pl.Buffered(1))
    win_shape = jax.ShapeDtypeStruct((b, keep * HEADS_A, HEAD_DIM), F32)
    return pl.pallas_call(
        _attn_prompt_body,
        grid=(b, HEADS_A),
        in_specs=[pl.BlockSpec((3, BLK, 2 * BLK), lambda i, h: (0, 0, 0)),
                  pl.BlockSpec(memory_space=pltpu.SMEM),
                  pl.BlockSpec(blk, lambda i, h: (i, 0, OFF_AQ // HEAD_DIM + h)),
                  pl.BlockSpec(blk, lambda i, h: (i, 0, OFF_AK // HEAD_DIM + h)),
                  pl.BlockSpec(blk, lambda i, h: (i, 0, OFF_AV // HEAD_DIM + h))],
        out_specs=[pl.BlockSpec(blk, lambda i, h: (i, 0, h)), win, win],
        out_shape=[jax.ShapeDtypeStruct((b, s, WIDTH_A), BF16), win_shape, win_shape],
        scratch_shapes=[pltpu.VMEM((6, BLK, 2 * BLK), F32),
                        pltpu.VMEM((s, HEAD_DIM), F32),
                        pltpu.VMEM((s, HEAD_DIM), F32),
                        pltpu.VMEM((s, HEAD_DIM), F32)],
        compiler_params=_params(("arbitrary", "arbitrary"), 48),
        name="attn_prompt",
    )(buckets, rel_bias, proj3, proj3, proj3)


def _attn_sample_body(bucket_ref, relbt_ref, q_ref, kn_ref, vn_ref,
                      k1_ref, k4_ref, k16_ref, v1_ref, v4_ref, v16_ref, o_ref, bias_scr):
    relbt = relbt_ref[...]
    tile = (HEADS_A, HEAD_DIM)

    @pl.when(pl.program_id(0) == 0)
    def _():
        for br in range(3):
            bk = bucket_ref[br]
            bias = jnp.zeros((BLK,) + tile, F32)
            for kb in range(N_BUCKETS):
                col = jnp.broadcast_to(relbt[:, kb:kb + 1], tile)
                bias = jnp.where(bk == kb, col[None], bias)
            bias_scr[br] = bias

    def lane_sum(x):
        return jnp.broadcast_to(jnp.sum(x, axis=-1, keepdims=True), x.shape)

    q = q_ref[0]
    s_self = lane_sum(q * kn_ref[0]) * ATT_SCALE + jnp.broadcast_to(relbt[:, 0:1], tile)
    scores = []
    m = s_self
    for br, k_ref in enumerate((k1_ref, k4_ref, k16_ref)):
        s = lane_sum(k_ref[...] * q[None]) * ATT_SCALE + bias_scr[br]
        scores.append(s)
        m = jnp.maximum(m, jnp.max(s, axis=0))
    p_self = 3.0 * jnp.exp(s_self - m)
    l = p_self
    acc = p_self * vn_ref[0]
    for s, v_ref in zip(scores, (v1_ref, v4_ref, v16_ref)):
        p = jnp.exp(s - m[None])
        l = l + jnp.sum(p, axis=0)
        acc = acc + jnp.sum(p * v_ref[...], axis=0)
    o_ref[0] = (acc / l).astype(o_ref.dtype)


def _attn_sample(q, k_new, v_new, cache_k, cache_v, rel_bias):
    b, past = cache_k.shape[:2]
    tile = (HEADS_A, HEAD_DIM)
    buckets = jnp.asarray(np.broadcast_to(_sample_bucket_tables().reshape(3, BLK, 1, 1), (3, BLK) + tile))
    row = pl.BlockSpec((1,) + tile, lambda i: (i, 0, 0))
    views, specs = [], []
    for cache in (cache_k, cache_v):
        for dil in DILATIONS:
            views.append(cache.reshape((b, past // dil, dil) + tile))
            last = past // dil // BLK - 1
            specs.append(pl.BlockSpec((None, BLK, None) + tile,
                                      functools.partial(lambda last, i: (i, last, 0, 0, 0), last)))
    return pl.pallas_call(
        _attn_sample_body,
        grid=(b,),
        in_specs=[pl.BlockSpec((3, BLK) + tile, lambda i: (0, 0, 0, 0)),
                  pl.BlockSpec((HEADS_A, N_BUCKETS), lambda i: (0, 0)),
                  row, row, row] + specs,
        out_specs=row,
        out_shape=jax.ShapeDtypeStruct((b,) + tile, BF16),
        scratch_shapes=[pltpu.VMEM((3, BLK) + tile, F32)],
        compiler_params=_params(("arbitrary",), 40),
        name="attn_sample",
    )(buckets, rel_bias.T, q, k_new, v_new, *views)


GROUP = 4
ATTN_UNROLL = 8
OUTPROJ_ROWS = 128
NORM_ROWS = 128
FFN_TM = 1024
FFN_TF = 512
FFN_ROWS = 64
FFN_COLS = 512
FFN_PIECE = 256


def _dn_prompt_body(q_ref, k_ref, v_ref, z_ref, h_ref, wgate_ref, cw_ref, cs_ref, s0_ref, alog_ref, dtb_ref, nw_ref,
                    o_ref, s_out_ref,
                    s_scr, e_scr, qn_scr, kn_scr, vv_scr, g_scr, beta_scr,
                    w_scr, u_scr, qg_scr, kdt_scr, attn_scr, gl_scr, o_scr, *, tt):
    t = pl.program_id(1)
    nt = pl.num_programs(1)

    @pl.when(t == 0)
    def _():
        s_scr[...] = s0_ref[0]
        e_scr[5:8, :] = cs_ref[0]

    e_scr[8:8 + tt, 0:WIDTH_BQK] = q_ref[0]
    e_scr[8:8 + tt, WIDTH_BQK:2 * WIDTH_BQK] = k_ref[0]
    e_scr[8:8 + tt, 2 * WIDTH_BQK:CONV_DIM] = v_ref[0]

    def l2n(x):
        return x * lax.rsqrt(jnp.sum(x * x, axis=-1, keepdims=True) + EPS)

    for c0 in range(0, CONV_DIM, DK):
        cols = slice(c0, c0 + DK)
        w = cw_ref[:, cols]
        y = w[0:1, :] * e_scr[5:5 + tt, cols]
        for i in range(1, CONV_W):
            y = y + w[i:i + 1, :] * e_scr[5 + i:5 + i + tt, cols]
        y = _silu(y)
        if c0 < WIDTH_BQK:
            qn_scr[:, cols] = l2n(y) * QK_SCALE
        elif c0 < 2 * WIDTH_BQK:
            kn_scr[:, c0 - WIDTH_BQK:c0 - WIDTH_BQK + DK] = l2n(y)
        else:
            vv_scr[:, c0 - 2 * WIDTH_BQK:c0 - 2 * WIDTH_BQK + DK] = y
    e_scr[5:8, :] = e_scr[tt + 5:tt + 8, :]
    gates = jnp.dot(h_ref[0], wgate_ref[...], preferred_element_type=F32)
    beta_scr[...] = _sigmoid(gates)
    g_scr[...] = -jnp.exp(alog_ref[...]) * _softplus(gates + dtb_ref[...])

    ri = lax.broadcasted_iota(jnp.int32, (CHUNK, CHUNK), 0)
    ci = lax.broadcasted_iota(jnp.int32, (CHUNK, CHUNK), 1)
    tri = ri >= ci
    strict = ri > ci
    same_sub = (ri // SUB) == (ci // SUB)
    tril_ones = tri.astype(F32)
    nw = nw_ref[...]

    for c in range(tt // CHUNK):
        rows = slice(c * CHUNK, (c + 1) * CHUNK)
        beta_all = beta_scr[rows, :]
        gc_all = _fdot(tril_ones, g_scr[rows, :])
        gc_all_t = gc_all.T
        g_scr[rows, :] = gc_all
        for hq0 in range(0, QK_HEADS_B, GROUP):
            units = []
            for hq in range(hq0, hq0 + GROUP):
                qn = qn_scr[rows, hq * DK:(hq + 1) * DK]
                kn = kn_scr[rows, hq * DK:(hq + 1) * DK]
                kk = _bdot_nt(kn, kn)
                qk = _bdot_nt(qn, kn)
                for hv in range(2 * hq, 2 * hq + 2):
                    beta = beta_all[:, hv:hv + 1]
                    gc = gc_all[:, V_HEADS_B + hv:V_HEADS_B + hv + 1]
                    gc_row = gc_all_t[V_HEADS_B + hv:V_HEADS_B + hv + 1, :]
                    gc_last = gc_row[:, CHUNK - 1:CHUNK]
                    decay = jnp.exp(jnp.where(tri, gc - gc_row, NEG))
                    a = jnp.where(strict, beta * kk * decay, 0.0)
                    egc = jnp.exp(gc)
                    attn_scr[hv, rows, :] = (qk * decay).astype(BF16)
                    qg_scr[hv, rows, :] = (qn * egc).astype(BF16)
                    kd = kn * jnp.exp(gc_last - gc)
                    kdt_scr[hv, c * DK:(c + 1) * DK, :] = kd.T.astype(BF16)
                    gl_scr[hv, c * 8:(c + 1) * 8, :] = jnp.broadcast_to(jnp.exp(gc_last), (8, DV))
                    units.append((hv, a))
            ds = [jnp.where(same_sub, a, 0.0) for _, a in units]
            ns = [-dd for dd in ds]
            pws = ds
            for _ in range(SUB.bit_length() - 2):
                pws = [_bdot(pw, pw) for pw in pws]
                ns = [n + pw + _bdot(n, pw) for n, pw in zip(ns, pws)]
            ls = [jnp.where(same_sub, 0.0, a) for _, a in units]
            ps = [lo + _bdot(lo, n) for lo, n in zip(ls, ns)]
            ns = [n - (p + _bdot(n, p)) for n, p in zip(ns, ps)]
            xs = []
            for hv, _ in units:
                beta = beta_scr[rows, hv:hv + 1]
                kscale = beta * jnp.exp(g_scr[rows, V_HEADS_B + hv:V_HEADS_B + hv + 1])
                xs.append(jnp.concatenate([kn_scr[rows, (hv // 2) * DK:(hv // 2 + 1) * DK] * kscale,
                                           vv_scr[rows, hv * DV:(hv + 1) * DV] * beta], axis=-1))
            wus = [x + _bdot(n, x) for n, x in zip(ns, xs)]
            for wu, (hv, _) in zip(wus, units):
                w_scr[hv, rows, :] = wu[:, :DK].astype(BF16)
                u_scr[hv, rows, :] = wu[:, DK:]

    heads = range(V_HEADS_B)
    for c in range(tt // CHUNK):
        rows = slice(c * CHUNK, (c + 1) * CHUNK)
        states = [s_scr[hv] for hv in heads]
        states_b = [s.astype(BF16) for s in states]
        v_news = [u_scr[hv, rows, :] - jnp.dot(w_scr[hv, rows, :], states_b[hv], preferred_element_type=F32)
                  for hv in heads]
        v_news_b = [v.astype(BF16) for v in v_news]
        for hv in heads:
            s_scr[hv] = (states[hv] * gl_scr[hv, c * 8:c * 8 + 1, :]
                         + jnp.dot(kdt_scr[hv, c * DK:(c + 1) * DK, :], v_news_b[hv],
                                   preferred_element_type=F32))
        for hv in heads:
            o_scr[rows, hv * DV:(hv + 1) * DV] = (
                jnp.dot(qg_scr[hv, rows, :], states_b[hv], preferred_element_type=F32)
                + jnp.dot(attn_scr[hv, rows, :], v_news_b[hv], preferred_element_type=F32))

    for hv in heads:
        o = o_scr[:, hv * DV:(hv + 1) * DV]
        z = z_ref[0, :, hv * DV:(hv + 1) * DV]
        o = o * lax.rsqrt(jnp.mean(o * o, axis=-1, keepdims=True) + EPS) * nw * _silu(z)
        o_ref[0, :, hv * DV:(hv + 1) * DV] = o.astype(o_ref.dtype)

    @pl.when(t == nt - 1)
    def _():
        s_out_ref[0] = s_scr[...]


def _dn_prompt(proj3, h3, w_gate, conv_state, s0, conv_w, alog_vec, dtb_vec, norm_w, tt):
    b, s, _ = proj3.shape
    body = functools.partial(_dn_prompt_body, tt=tt)
    nh = V_HEADS_B
    in_specs = [
        pl.BlockSpec((1, tt, WIDTH_BQK), lambda i, t: (i, t, OFF_BQ // WIDTH_BQK)),
        pl.BlockSpec((1, tt, WIDTH_BQK), lambda i, t: (i, t, OFF_BK // WIDTH_BQK)),
        pl.BlockSpec((1, tt, WIDTH_BV), lambda i, t: (i, t, OFF_BV // WIDTH_BV)),
        pl.BlockSpec((1, tt, WIDTH_BV), lambda i, t: (i, t, OFF_BZ // WIDTH_BV)),
        pl.BlockSpec((1, tt, D_MODEL), lambda i, t: (i, t, 0)),
        pl.BlockSpec((D_MODEL, 128), lambda i, t: (0, 0)),
        pl.BlockSpec((CONV_W, CONV_DIM), lambda i, t: (0, 0)),
        pl.BlockSpec((1, CONV_W - 1, CONV_DIM), lambda i, t: (i, 0, 0)),
        pl.BlockSpec((1, nh, DK, DV), lambda i, t: (i, 0, 0, 0)),
        pl.BlockSpec((1, 128), lambda i, t: (0, 0)),
        pl.BlockSpec((1, 128), lambda i, t: (0, 0)),
        pl.BlockSpec((1, DV), lambda i, t: (0, 0)),
    ]
    return pl.pallas_call(
        body,
        grid=(b, s // tt),
        in_specs=in_specs,
        out_specs=[pl.BlockSpec((1, tt, WIDTH_BV), lambda i, t: (i, t, 0)),
                   pl.BlockSpec((1, nh, DK, DV), lambda i, t: (i, 0, 0, 0))],
        out_shape=[jax.ShapeDtypeStruct((b, s, WIDTH_BV), BF16),
                   jax.ShapeDtypeStruct((b, nh, DK, DV), F32)],
        scratch_shapes=[pltpu.VMEM((nh, DK, DV), F32),
                        pltpu.VMEM((tt + 8, CONV_DIM), F32),
                        pltpu.VMEM((tt, WIDTH_BQK), F32),
                        pltpu.VMEM((tt, WIDTH_BQK), F32),
                        pltpu.VMEM((tt, WIDTH_BV), F32),
                        pltpu.VMEM((tt, 128), F32),
                        pltpu.VMEM((tt, 128), F32),
                        pltpu.VMEM((nh, tt, DK), BF16),
                        pltpu.VMEM((nh, tt, DV), F32),
                        pltpu.VMEM((nh, tt, DK), BF16),
                        pltpu.VMEM((nh, tt // CHUNK * DK, CHUNK), BF16),
                        pltpu.VMEM((nh, tt, CHUNK), BF16),
                        pltpu.VMEM((nh, tt // CHUNK * 8, DV), F32),
                        pltpu.VMEM((tt, WIDTH_BV), F32)],
        compiler_params=_params(("arbitrary", "arbitrary"), 48),
        name="deltanet_prompt",
    )(proj3, proj3, proj3, proj3, h3, w_gate, conv_w, conv_state, s0, alog_vec, dtb_vec, norm_w)


def _dn_sample_body(proj_ref, gates_ref, cw_ref, cs_ref, s0_ref, alog_ref, dtb_ref, nw_ref,
                    o_ref, cs_out_ref, s_out_ref):
    pre = proj_ref[0, :, OFF_BQ:OFF_BQ + CONV_DIM]
    buf = cs_ref[0]
    w = cw_ref[...]
    y = w[CONV_W - 1:CONV_W, :] * pre
    for i in range(CONV_W - 1):
        y = y + w[i:i + 1, :] * buf[i:i + 1, :]
    y = _silu(y)
    cs_out_ref[0, 0:CONV_W - 2, :] = buf[1:CONV_W - 1, :]
    cs_out_ref[0, CONV_W - 2:CONV_W - 1, :] = pre

    gates = gates_ref[0]
    beta_all = _sigmoid(gates)
    g_all = -jnp.exp(alog_ref[...]) * _softplus(gates + dtb_ref[...])
    nw = nw_ref[...]

    def l2n(x):
        return x * lax.rsqrt(jnp.sum(x * x, axis=-1, keepdims=True) + EPS)

    row8 = lax.broadcasted_iota(jnp.int32, (8, DK), 0) == 0
    for hv in range(V_HEADS_B):
        hq = hv // 2
        q = l2n(y[:, hq * DK:(hq + 1) * DK]) * QK_SCALE
        k = l2n(y[:, WIDTH_BQK + hq * DK:WIDTH_BQK + (hq + 1) * DK])
        v = y[:, 2 * WIDTH_BQK + hv * DV:2 * WIDTH_BQK + (hv + 1) * DV]
        beta = beta_all[:, hv:hv + 1]
        g = g_all[:, V_HEADS_B + hv:V_HEADS_B + hv + 1]
        eg = jnp.exp(g)
        state = s0_ref[0, hv]

        def pad8(x):
            return jnp.where(row8, jnp.broadcast_to(x, (8, x.shape[-1])), 0.0)

        v_new = v * beta - _bdot(pad8(k * (beta * eg)), state)[0:1, :]
        qk = jnp.sum(q.astype(BF16).astype(F32) * k.astype(BF16).astype(F32), axis=-1, keepdims=True)
        o = _bdot(pad8(q * eg), state)[0:1, :] + qk.astype(BF16).astype(F32) * v_new.astype(BF16).astype(F32)
        s_out_ref[0, hv] = state * eg + _bdot_tn(pad8(k), pad8(v_new))
        z = proj_ref[0, :, OFF_BZ + hv * DV:OFF_BZ + (hv + 1) * DV]
        o = o * lax.rsqrt(jnp.mean(o * o, axis=-1, keepdims=True) + EPS) * nw * _silu(z)
        o_ref[0, :, hv * DV:(hv + 1) * DV] = o.astype(o_ref.dtype)


def _dn_sample(proj, gates, conv_state, s0, conv_w, alog_vec, dtb_vec, norm_w):
    b = proj.shape[0]
    return pl.pallas_call(
        _dn_sample_body,
        grid=(b,),
        in_specs=[pl.BlockSpec((1, 1, PROJ_MAIN), lambda i: (i, 0, 0)),
                  pl.BlockSpec((1, 1, 128), lambda i: (i, 0, 0)),
                  pl.BlockSpec((CONV_W, CONV_DIM), lambda i: (0, 0)),
                  pl.BlockSpec((1, CONV_W - 1, CONV_DIM), lambda i: (i, 0, 0)),
                  pl.BlockSpec((1, V_HEADS_B, DK, DV), lambda i: (i, 0, 0, 0)),
                  pl.BlockSpec((1, 128), lambda i: (0, 0)),
                  pl.BlockSpec((1, 128), lambda i: (0, 0)),
                  pl.BlockSpec((1, DV), lambda i: (0, 0))],
        out_specs=[pl.BlockSpec((1, 1, WIDTH_BV), lambda i: (i, 0, 0)),
                   pl.BlockSpec((1, CONV_W - 1, CONV_DIM), lambda i: (i, 0, 0)),
                   pl.BlockSpec((1, V_HEADS_B, DK, DV), lambda i: (i, 0, 0, 0))],
        out_shape=[jax.ShapeDtypeStruct((b, 1, WIDTH_BV), BF16),
                   jax.ShapeDtypeStruct((b, CONV_W - 1, CONV_DIM), F32),
                   jax.ShapeDtypeStruct((b, V_HEADS_B, DK, DV), F32)],
        compiler_params=_params(("arbitrary",), 40),
        name="deltanet_sample",
    )(proj, gates, conv_w, conv_state, s0, alog_vec, dtb_vec, norm_w)


def _outproj_body(att_ref, dn_ref, wa_ref, wb_ref, x_ref, lnpost_ref, lnpre_ref, x1_ref, h2_ref):
    tm = x_ref.shape[0]
    piece = min(tm, OUTPROJ_ROWS)
    for rows in [slice(r, r + piece) for r in range(0, tm, piece)]:
        mix = (jnp.dot(att_ref[rows, :], wa_ref[...], preferred_element_type=F32)
               + jnp.dot(dn_ref[rows, :], wb_ref[...], preferred_element_type=F32))
        x1 = x_ref[rows, :] + _rms(mix, lnpost_ref[...])
        x1_ref[rows, :] = x1
        h2_ref[rows, :] = _rms(x1, lnpre_ref[...]).astype(h2_ref.dtype)


def _outproj(att, dn, w, x, ln_post, ln_pre, tm):
    m, d = x.shape
    assert WIDTH_A == WIDTH_BV
    return pl.pallas_call(
        _outproj_body,
        grid=(m // tm,),
        in_specs=[pl.BlockSpec((tm, WIDTH_A), lambda i: (i, 0)),
                  pl.BlockSpec((tm, WIDTH_BV), lambda i: (i, 0)),
                  pl.BlockSpec((WIDTH_A, d), lambda i: (0, 0)),
                  pl.BlockSpec((WIDTH_BV, d), lambda i: (1, 0)),
                  pl.BlockSpec((tm, d), lambda i: (i, 0)),
                  pl.BlockSpec((1, d), lambda i: (0, 0)),
                  pl.BlockSpec((1, d), lambda i: (0, 0))],
        out_specs=[pl.BlockSpec((tm, d), lambda i: (i, 0)),
                   pl.BlockSpec((tm, d), lambda i: (i, 0))],
        out_shape=[jax.ShapeDtypeStruct((m, d), F32),
                   jax.ShapeDtypeStruct((m, d), BF16)],
        compiler_params=_params(("arbitrary",), 48),
        name="outproj",
    )(att, dn, w, w, x, ln_post, ln_pre)


def _ffn_body(*refs, tm, tiles_per_seq, single_token):
    if single_token:
        (h_ref, wg_ref, wv_ref, cwb_ref, wo_prev_ref, wo_last_ref, x1_hbm, ln_ref, pg_ref, pv_ref,
         o_ref, ng_ref, nv_ref, eg_scr, ev_scr, carry_scr, act_scr, x1_scr, x1_sem) = refs
    else:
        (h_ref, wg_ref, wv_ref, cwb_ref, wo_prev_ref, wo_last_ref, x1_hbm, ln_ref, prev_ref,
         o_ref, new_ref, eg_scr, ev_scr, carry_scr, act_scr, x1_scr, x1_sem) = refs
    i = pl.program_id(0)
    j = pl.program_id(1)
    nj = pl.num_programs(1)
    d = o_ref.shape[-1]
    tf = act_scr.shape[-1]
    cur = j % 2
    act_cur = act_scr.at[cur]
    act_prev = act_scr.at[1 - cur]

    def x1_copy():
        return pltpu.make_async_copy(x1_hbm.at[pl.ds(pl.multiple_of(i * tm, tm), tm), :], x1_scr, x1_sem)

    def down_proj(act_ref, wo_ref):
        for n in range(0, d, FFN_COLS):
            o_ref[:, n:n + FFN_COLS] += jnp.dot(act_ref[...], wo_ref[:, n:n + FFN_COLS],
                                                preferred_element_type=F32)

    @pl.when(j == 0)
    def _():
        x1_copy().start()
        o_ref[...] = jnp.zeros_like(o_ref)
        act_prev[...] = jnp.zeros_like(act_prev)

    if single_token:
        def up_conv(w_ref, tile, prev_ref, new_ref):
            up = jnp.dot(h_ref[...], w_ref[...], preferred_element_type=F32)
            cw = cwb_ref[tile]
            new_ref[...] = up
            return cw[0:1, :] * prev_ref[0] + cw[1:2, :] * prev_ref[1] + cw[2:3, :] * up + cw[3:4, :]

        gate = up_conv(wg_ref, j, pg_ref, ng_ref)
        val = up_conv(wv_ref, nj + j, pv_ref, nv_ref)
        act_cur[...] = (_gelu_tanh(gate) * val).astype(BF16)
        down_proj(act_prev, wo_prev_ref)
    else:
        first_tile = i % tiles_per_seq == 0
        pieces = [slice(c, c + FFN_PIECE) for c in range(0, tf, FFN_PIECE)]

        def up_proj(cols):
            for w_ref, tile, e_scr, slot in ((wg_ref, j, eg_scr, 0), (wv_ref, nj + j, ev_scr, 1)):
                e_scr[8:8 + tm, cols] = jnp.dot(h_ref[...], w_ref[:, cols], preferred_element_type=F32)
                e_scr[6:8, cols] = jnp.where(first_tile, prev_ref[0, tile, :, cols],
                                             carry_scr[slot, j, 6:8, cols])
                tail = e_scr[tm + 6:tm + 8, cols]
                carry_scr[slot, j, 6:8, cols] = tail
                new_ref[0, tile, :, cols] = tail

        def conv(e_scr, tile, cols, r):
            cw = cwb_ref[tile, :, cols]
            return (cw[0:1, :] * e_scr[6 + r:6 + r + FFN_ROWS, cols]
                    + cw[1:2, :] * e_scr[7 + r:7 + r + FFN_ROWS, cols]
                    + cw[2:3, :] * e_scr[8 + r:8 + r + FFN_ROWS, cols] + cw[3:4, :])

        def conv_geglu(cols):
            for r in range(0, tm, FFN_ROWS):
                act_cur[r:r + FFN_ROWS, cols] = (_gelu_tanh(conv(eg_scr, j, cols, r))
                                                 * conv(ev_scr, nj + j, cols, r)).astype(BF16)

        up_proj(pieces[0])
        for c in range(1, len(pieces)):
            up_proj(pieces[c])
            conv_geglu(pieces[c - 1])
        down_proj(act_prev, wo_prev_ref)
        conv_geglu(pieces[-1])

    @pl.when(j == nj - 1)
    def _():
        down_proj(act_cur, wo_last_ref)
        x1_copy().wait()
        o_ref[...] = x1_scr[...] + _rms(o_ref[...], ln_ref[...])


def _ffn(h2, w_in, conv_w, conv_b, w_out, x1, ln_post, prev, tm, tf, seq_len):
    m, d = h2.shape
    single = seq_len == 1
    nj = D_FF // tf
    tiles_per_seq = 1 if single else seq_len // tm
    cwb = jnp.concatenate([conv_w, conv_b], axis=0).reshape(FFN_CONV_W + 1, 2 * nj, tf).transpose(1, 0, 2)
    cwb_spec = pl.BlockSpec((2 * nj, FFN_CONV_W + 1, tf), lambda i, j: (0, 0, 0))
    if single:
        prev_args = (prev, prev)
        prev_specs = [pl.BlockSpec((2, tm, tf), lambda i, j: (0, i, j)),
                      pl.BlockSpec((2, tm, tf), lambda i, j: (0, i, nj + j))]
        new_specs = [pl.BlockSpec((tm, tf), lambda i, j: (i, j))] * 2
        new_shapes = [jax.ShapeDtypeStruct((m, D_FF), F32)] * 2
    else:
        prev_args = (prev.reshape(-1, 2, 2 * nj, tf).transpose(0, 2, 1, 3),)
        prev_specs = [pl.BlockSpec((1, 2 * nj, 2, tf), lambda i, j: (i // tiles_per_seq, 0, 0, 0))]
        new_specs = [pl.BlockSpec((1, 2 * nj, 2, tf), lambda i, j: (i, 0, 0, 0))]
        new_shapes = [jax.ShapeDtypeStruct((m // tm, 2 * nj, 2, tf), F32)]
    body = functools.partial(_ffn_body, tm=tm, tiles_per_seq=tiles_per_seq, single_token=single)
    once = dict(pipeline_mode=pl.Buffered(1)) if tm >= 1024 else {}
    outs = pl.pallas_call(
        body,
        grid=(m // tm, nj),
        in_specs=[pl.BlockSpec((tm, d), lambda i, j: (i, 0), **once),
                  pl.BlockSpec((d, tf), lambda i, j: (0, j)),
                  pl.BlockSpec((d, tf), lambda i, j: (0, nj + j)),
                  cwb_spec,
                  pl.BlockSpec((tf, d), lambda i, j: (jnp.maximum(j - 1, 0), 0)),
                  pl.BlockSpec((tf, d), lambda i, j: (nj - 1, 0), pipeline_mode=pl.Buffered(1)),
                  pl.BlockSpec(memory_space=pl.ANY),
                  pl.BlockSpec((1, d), lambda i, j: (0, 0))] + prev_specs,
        out_specs=[pl.BlockSpec((tm, d), lambda i, j: (i, 0), **once)] + new_specs,
        out_shape=[jax.ShapeDtypeStruct((m, d), F32)] + new_shapes,
        scratch_shapes=[pltpu.VMEM((tm + 8, tf), F32),
                        pltpu.VMEM((tm + 8, tf), F32),
                        pltpu.VMEM((2, nj, 8, tf), F32),
                        pltpu.VMEM((2, tm, tf), BF16),
                        pltpu.VMEM((tm, d), F32),
                        pltpu.SemaphoreType.DMA(())],
        compiler_params=_params(("arbitrary", "arbitrary"), 57),
        name="convffn",
    )(h2, w_in, w_in, cwb, w_out, w_out, x1, ln_post, *prev_args)
    if single:
        y, new_g, new_v = outs
        return y, jnp.concatenate([new_g, new_v], axis=-1)
    y, new = outs
    return y, new.transpose(0, 2, 1, 3).reshape(m // tm, 2, 2 * D_FF)


def _cache_shift_body(ck_ref, cv_ref, ck_next_ref, cv_next_ref, nk_ref, nv_ref, ok_ref, ov_ref, *, tr):
    last = pl.program_id(1) == pl.num_programs(1) - 1
    for c_ref, nxt_ref, n_ref, o_ref in ((ck_ref, ck_next_ref, nk_ref, ok_ref),
                                         (cv_ref, cv_next_ref, nv_ref, ov_ref)):
        o_ref[0, 0:tr - 1] = c_ref[0, 1:tr]
        o_ref[0, tr - 1] = jnp.where(last, n_ref[0, 0], nxt_ref[0, 0])


def _cache_shift(cache_k, cache_v, new_k, new_v, tr):
    nb, rows, nh, dh = cache_k.shape
    main = pl.BlockSpec((1, tr, nh, dh), lambda b, i: (b, i, 0, 0))
    nxt = pl.BlockSpec((1, 1, nh, dh), lambda b, i: (b, jnp.minimum((i + 1) * tr, rows - 1), 0, 0))
    new = pl.BlockSpec((1, 1, nh, dh), lambda b, i: (b, 0, 0, 0))
    shape = jax.ShapeDtypeStruct(cache_k.shape, cache_k.dtype)
    return pl.pallas_call(
        functools.partial(_cache_shift_body, tr=tr),
        grid=(nb, rows // tr),
        in_specs=[main, main, nxt, nxt, new, new],
        out_specs=[main, main],
        out_shape=[shape, shape],
        compiler_params=_params(("arbitrary", "arbitrary"), 40),
        name="cache_shift",
    )(cache_k, cache_v, cache_k, cache_v, new_k, new_v)


def _lane_vec(values, offset):
    return jnp.zeros((1, 128), F32).at[0, offset:offset + V_HEADS_B].set(values.astype(F32))


def kernel(x_prompt, x_sample, cache_win_k, cache_win_v, state_dn_conv, state_dn_rec, state_ffn_conv,
           rel_bias, ln_mix_pre, w_in, dn_conv_w, dn_A_log, dn_dt_bias, dn_norm_w, w_out, ln_mix_post,
           ln_ffn_pre, w_ffn_in, ffn_conv_w, ffn_conv_b, w_ffn_out, ln_ffn_post):
    bp, sp, d = x_prompt.shape
    bs = x_sample.shape[0]
    l = 0

    w_main = w_in[l].astype(BF16)
    w_gate = jnp.pad(w_main[:, PROJ_MAIN:], ((0, 0), (0, 128 - 2 * V_HEADS_B)))
    wo = w_out[l].astype(BF16)
    wf_in = w_ffn_in[l].astype(BF16)
    wf_out = w_ffn_out[l].astype(BF16)
    ln1 = ln_mix_pre[l][None, :]
    ln2 = ln_mix_post[l][None, :]
    ln3 = ln_ffn_pre[l][None, :]
    ln4 = ln_ffn_post[l][None, :]
    conv_w = dn_conv_w[l]
    alog_vec = _lane_vec(dn_A_log[l], V_HEADS_B)
    dtb_vec = _lane_vec(dn_dt_bias[l], V_HEADS_B)
    norm_w = dn_norm_w[l][None, :]
    fcw = ffn_conv_w[l]
    fcb = ffn_conv_b[l][None, :]

    xp = x_prompt.reshape(bp * sp, d)
    proj_p, hp = _norm_matmul(xp, ln1, w_main, PROJ_MAIN, 1024, 1024, "inproj_prompt")
    proj3 = proj_p.reshape(bp, sp, PROJ_MAIN)
    keep = min(MAX_DISTANCE, sp)
    att_p, win_k, win_v = _attn_prompt(proj3, rel_bias, keep)
    dn_p, p_dn_rec = _dn_prompt(
        proj3, hp.reshape(bp, sp, d), w_gate,
        jnp.zeros((bp, CONV_W - 1, CONV_DIM), F32), jnp.zeros((bp, V_HEADS_B, DK, DV), F32),
        conv_w, alog_vec, dtb_vec, norm_w, 256)
    x1_p, h2_p = _outproj(att_p.reshape(bp * sp, WIDTH_A), dn_p.reshape(bp * sp, WIDTH_BV),
                          wo, xp, ln2, ln3, 512)
    y_p, fc = _ffn(h2_p, wf_in, fcw, fcb, wf_out, x1_p, ln4,
                   jnp.zeros((bp, FFN_CONV_W - 1, 2 * D_FF), F32), FFN_TM, FFN_TF, sp)
    p_win_k = win_k.reshape(1, bp, keep, HEADS_A, HEAD_DIM)
    p_win_v = win_v.reshape(1, bp, keep, HEADS_A, HEAD_DIM)
    p_dn_conv = proj3[:, sp - (CONV_W - 1):, OFF_BQ:OFF_BQ + CONV_DIM][None]
    tiles = sp // FFN_TM
    p_ffn_conv = fc[tiles - 1::tiles][None]

    xs = x_sample.reshape(bs, d)
    proj_s, hs = _norm_matmul(xs, ln1, w_main, PROJ_MAIN, bs, 1024, "inproj_sample")
    gates_s = _matmul(hs, w_gate, bs, 128, "gates_sample")
    past = cache_win_k.shape[2]
    ck = cache_win_k[l]
    cv = cache_win_v[l]
    new_k = proj_s[:, OFF_AK:OFF_AK + WIDTH_A]
    new_v = proj_s[:, OFF_AV:OFF_AV + WIDTH_A]
    new_q = proj_s[:, OFF_AQ:OFF_AQ + WIDTH_A].reshape(bs, HEADS_A, HEAD_DIM)
    new_k = new_k.reshape(bs, HEADS_A, HEAD_DIM)
    new_v = new_v.reshape(bs, HEADS_A, HEAD_DIM)
    att_s = _attn_sample(new_q, new_k, new_v, ck, cv, rel_bias)
    s_win_k, s_win_v = _cache_shift(ck, cv, new_k[:, None], new_v[:, None], 1024)
    dn_s, s_dn_conv, s_dn_rec = _dn_sample(proj_s[:, None], gates_s[:, None], state_dn_conv[l],
                                           state_dn_rec[l], conv_w, alog_vec, dtb_vec, norm_w)
    x1_s, h2_s = _outproj(att_s.reshape(bs, WIDTH_A), dn_s.reshape(bs, WIDTH_BV),
                          wo, xs, ln2, ln3, bs)
    prev_s = jnp.swapaxes(state_ffn_conv[l], 0, 1)
    y_s, up_s = _ffn(h2_s, wf_in, fcw, fcb, wf_out, x1_s, ln4, prev_s, bs, FFN_TF, 1)
    s_ffn_conv = jnp.stack([prev_s[1], up_s], axis=1)[None]

    return (y_p.reshape(bp, sp, d), y_s.reshape(bs, 1, d),
            p_win_k, p_win_v, p_dn_conv, p_dn_rec[None], p_ffn_conv,
            s_win_k[None], s_win_v[None], s_dn_conv[None], s_dn_rec[None], s_ffn_conv)
```

```python
import functools
import math

import numpy as np
import jax
import jax.numpy as jnp
from jax import lax
from jax.experimental import pallas as pl
from jax.experimental.pallas import tpu as pltpu

F32 = jnp.float32
BF16 = jnp.bfloat16

D_MODEL = 2048
HEAD_DIM = 128
WIDTH_A = 1024
HEADS_A = 8
DILATIONS = (1, 4, 16)
BLK = 128
N_BUCKETS = 32
MAX_DISTANCE = 2048
DK = 128
DV = 128
V_HEADS_B = 8
QK_HEADS_B = 4
WIDTH_BQK = 512
WIDTH_BV = 1024
CONV_W = 4
CONV_DIM = 2048
CHUNK = 128
SUB = 64
D_FF = 5632
FFN_CONV_W = 3
EPS = 1e-6
NEG = -1e30
ATT_SCALE = HEAD_DIM ** -0.5
QK_SCALE = DK ** -0.5

OFF_AQ, OFF_AK, OFF_AV = 0, 1024, 2048
OFF_BQ, OFF_BK, OFF_BV, OFF_BZ = 3072, 3584, 4096, 5120
OFF_GATES = 6144
PROJ_MAIN = 6144

MIB = 2 ** 20


def _params(semantics, vmem_mib):
    return pltpu.CompilerParams(dimension_semantics=semantics, vmem_limit_bytes=vmem_mib * MIB)


def _bdot(a, b):
    return jnp.dot(a.astype(BF16), b.astype(BF16), preferred_element_type=F32)


def _bdot_nt(a, b):
    return lax.dot_general(a.astype(BF16), b.astype(BF16), (((1,), (1,)), ((), ())),
                           preferred_element_type=F32)


def _bdot_tn(a, b):
    return lax.dot_general(a.astype(BF16), b.astype(BF16), (((0,), (0,)), ((), ())),
                           preferred_element_type=F32)


def _fdot(a, b):
    return jnp.dot(a, b, preferred_element_type=F32, precision=lax.Precision.HIGHEST)


def _silu(x):
    return x * (1.0 / (1.0 + jnp.exp(-x)))


def _sigmoid(x):
    return 1.0 / (1.0 + jnp.exp(-x))


def _softplus(x):
    return jnp.maximum(x, 0.0) + jnp.log(1.0 + jnp.exp(-jnp.abs(x)))


def _gelu_tanh(x):
    c = math.sqrt(2.0 / math.pi)
    half = 0.5 * x
    return half + half * jnp.tanh(x * (c + (c * 0.044715) * (x * x)))


def _rms(x, w):
    return x * lax.rsqrt(jnp.mean(x * x, axis=-1, keepdims=True) + EPS) * w


def _matmul_body(x_ref, w_ref, o_ref):
    o_ref[...] = jnp.dot(x_ref[...], w_ref[...], preferred_element_type=F32)


def _matmul(x, w, tm, tn, name, n=None, n0=0):
    m, k = x.shape
    n = w.shape[1] if n is None else n
    j0 = n0 // tn
    return pl.pallas_call(
        _matmul_body,
        grid=((n - n0) // tn, m // tm),
        in_specs=[pl.BlockSpec((tm, k), lambda j, i: (i, 0)),
                  pl.BlockSpec((k, tn), lambda j, i: (0, j0 + j))],
        out_specs=pl.BlockSpec((tm, tn), lambda j, i: (i, j)),
        out_shape=jax.ShapeDtypeStruct((m, n - n0), F32),
        compiler_params=_params(("arbitrary", "arbitrary"), 48),
        name=name,
    )(x, w)


def _norm_matmul_body(x_ref, ln_ref, w_ref, o_ref, h_ref):
    tm = x_ref.shape[0]
    piece = min(tm, NORM_ROWS)
    for r in range(0, tm, piece):
        rows = slice(r, r + piece)
        h = _rms(x_ref[rows, :], ln_ref[...]).astype(BF16)
        h_ref[rows, :] = h
        o_ref[rows, :] = jnp.dot(h, w_ref[...], preferred_element_type=F32)


def _norm_matmul(x, ln, w, n, tm, tn, name):
    m, k = x.shape
    once = dict(pipeline_mode=pl.Buffered(1)) if tm >= 1024 else {}
    return pl.pallas_call(
        _norm_matmul_body,
        grid=(m // tm, n // tn),
        in_specs=[pl.BlockSpec((tm, k), lambda i, j: (i, 0), **once),
                  pl.BlockSpec((1, k), lambda i, j: (0, 0)),
                  pl.BlockSpec((k, tn), lambda i, j: (0, j))],
        out_specs=[pl.BlockSpec((tm, tn), lambda i, j: (i, j)),
                   pl.BlockSpec((tm, k), lambda i, j: (i, 0), **once)],
        out_shape=[jax.ShapeDtypeStruct((m, n), F32), jax.ShapeDtypeStruct((m, k), BF16)],
        compiler_params=_params(("arbitrary", "arbitrary"), 52),
        name=name,
    )(x, ln, w)


def _rel_bucket_np(dist):
    dist = np.asarray(dist, np.int64)
    max_exact = N_BUCKETS // 2
    d = np.maximum(dist, 1).astype(np.float64)
    val = np.log(d / max_exact) / math.log(MAX_DISTANCE / max_exact) * (N_BUCKETS - max_exact)
    frac = np.abs(val - np.round(val))
    near = (frac < 2e-5) &(dist >= max_exact) & (dist != max_exact) & (dist < MAX_DISTANCE)
    assert not near.any(), "distance on a bucket boundary"
    val = np.where(dist == max_exact, 0.0, val)
    large = np.minimum(max_exact + np.trunc(val).astype(np.int64), N_BUCKETS - 1)
    return np.where(dist < max_exact, dist, large).astype(np.int32)


def _prompt_bucket_tables():
    qi = np.arange(BLK)[:, None]
    kj = np.arange(2 * BLK)[None, :]
    delta = BLK + qi - kj
    inwin = (delta >= 0) & (delta <= BLK)
    tabs = []
    for dil in DILATIONS:
        b = _rel_bucket_np(np.clip(delta, 0, BLK) * dil)
        tabs.append(np.where(inwin, b, -1))
    return np.stack(tabs).astype(np.int32)


def _sample_bucket_tables():
    j = BLK - np.arange(BLK)
    return np.stack([_rel_bucket_np(j * dil)[None, :] for dil in DILATIONS]).astype(np.int32)


def _attn_prompt_body(bucket_ref, relb_ref, q_ref, k_ref, v_ref, o_ref, wk_ref, wv_ref,
                      bias_scr, acc_scr, m_scr, l_scr):
    h = pl.program_id(1)
    s, keep = k_ref.shape[1], wk_ref.shape[1] // HEADS_A
    wk_ref[0, pl.ds(h, keep, stride=HEADS_A), :] = k_ref[0, s - keep:s, :]
    wv_ref[0, pl.ds(h, keep, stride=HEADS_A), :] = v_ref[0, s - keep:s, :]
    col = lax.broadcasted_iota(jnp.int32, (BLK, 2 * BLK), 1)
    tables = _prompt_bucket_tables()
    for br in range(3):
        bk = bucket_ref[br]
        bias = jnp.zeros((BLK, 2 * BLK), F32)
        for kb in sorted(set(tables[br].ravel().tolist()) - {-1}):
            bias = jnp.where(bk == kb, relb_ref[kb, h], bias)
        full = jnp.where(bk >= 0, bias, NEG)
        bias_scr[2 * br] = full
        bias_scr[2 * br + 1] = jnp.where(col >= BLK, full, NEG)

    def run_branch(br, dil, is_first_branch, is_last_branch):
        shift = int(math.log2(dil))
        span = BLK * dil
        stride = None if dil == 1 else dil

        def rows(start):
            return pl.ds(start, BLK, stride=stride) if stride else pl.ds(start, BLK)

        nb = q_ref.shape[1] // span
        run_len = min(nb, ATTN_UNROLL)
        runs_per_it = ATTN_UNROLL // run_len
        runs_per_res = nb // run_len
        starts_at_zero = runs_per_res == 1

        def tasks(it, carry):
            q_starts, firsts, qs_, ks_, vs_ = [], [], [], [], []
            for rr in range(runs_per_it):
                ri = it * runs_per_it + rr
                n0 = (ri % runs_per_res) * run_len
                if dil == 1:
                    base = pl.multiple_of(n0 * span, BLK)
                else:
                    base = n0 * span + ri // runs_per_res
                first = jnp.where(n0 == 0, 1, 0)
                starts = [base + u * span for u in range(run_len)]
                kb = [k_ref[0, rows(st), :].astype(BF16) for st in starts]
                vb = [v_ref[0, rows(st), :].astype(BF16) for st in starts]
                if starts_at_zero:
                    k_prev, v_prev = None, None
                else:
                    p_start = base - span * (1 - first)
                    if dil == 1:
                        p_start = pl.multiple_of(p_start, BLK)
                    k_prev = k_ref[0, rows(p_start), :].astype(BF16)
                    v_prev = v_ref[0, rows(p_start), :].astype(BF16)
                for u, st in enumerate(starts):
                    q_starts.append(st)
                    qs_.append(q_ref[0, rows(st), :].astype(BF16))
                    kp, vp = (k_prev, v_prev) if u == 0 else (kb[u - 1], vb[u - 1])
                    if kp is None:
                        firsts.append(None)
                        ks_.append(kb[u])
                        vs_.append(vb[u])
                    else:
                        firsts.append(first if u == 0 else 0)
                        ks_.append(jnp.concatenate([kp, kb[u]], axis=0))
                        vs_.append(jnp.concatenate([vp, vb[u]], axis=0))
            if not is_first_branch:
                runs = [(m_scr[rows(qs), :], l_scr[rows(qs), :], acc_scr[rows(qs), :]) for qs in q_starts]
            ss = [_bdot_nt(q, k) * ATT_SCALE
                  + (bias_scr[2 * br, :, BLK:] if f is None else bias_scr[2 * br + f])
                  for q, k, f in zip(qs_, ks_, firsts)]
            ms = [jnp.max(s, axis=-1, keepdims=True) for s in ss]
            ps_ = [jnp.exp(s - m) for s, m in zip(ss, ms)]
            accs = [_bdot(p, jnp.concatenate([v, jnp.ones_like(v)], axis=1)) for p, v in zip(ps_, vs_)]
            outs = []
            for u in range(ATTN_UNROLL):
                m_b = jnp.broadcast_to(ms[u], (BLK, HEAD_DIM))
                l_b = accs[u][:, HEAD_DIM:]
                acc_t = accs[u][:, :HEAD_DIM]
                if not is_first_branch:
                    m_run, l_run, acc_run = runs[u]
                    m_new = jnp.maximum(m_run, m_b)
                    a = jnp.exp(m_run - m_new)
                    b = jnp.exp(m_b - m_new)
                    acc_t = a * acc_run + b * acc_t
                    l_b = a * l_run + b * l_b
                    m_b = m_new
                outs.append((m_b, l_b, acc_t))
            for qs, (m_b, l_b, acc_t) in zip(q_starts, outs):
                if is_last_branch:
                    o_ref[0, rows(qs), :] = (acc_t / l_b).astype(o_ref.dtype)
                else:
                    m_scr[rows(qs), :] = m_b
                    l_scr[rows(qs), :] = l_b
                    acc_scr[rows(qs), :] = acc_t
            return carry

        lax.fori_loop(0, nb * dil // ATTN_UNROLL, tasks, 0)

    run_branch(2, 16, True, False)
    run_branch(1, 4, False, False)
    run_branch(0, 1, False, True)


def _attn_prompt(proj3, rel_bias, keep):
    b, s, _ = proj3.shape
    buckets = jnp.asarray(_prompt_bucket_tables())
    blk = (1, s, HEAD_DIM)
    win = pl.BlockSpec((1, keep * HEADS_A, HEAD_DIM), lambda i, h: (i, 0, 0), pipeline_mode=pl.Buffered(1))
    win_shape = jax.ShapeDtypeStruct((b, keep * HEADS_A, HEAD_DIM), F32)
    return pl.pallas_call(
        _attn_prompt_body,
        grid=(b, HEADS_A),
        in_specs=[pl.BlockSpec((3, BLK, 2 * BLK), lambda i, h: (0, 0, 0)),
                  pl.BlockSpec(memory_space=pltpu.SMEM),
                  pl.BlockSpec(blk, lambda i, h: (i, 0, OFF_AQ // HEAD_DIM + h)),
                  pl.BlockSpec(blk, lambda i, h: (i, 0, OFF_AK // HEAD_DIM + h)),
                  pl.BlockSpec(blk, lambda i, h: (i, 0, OFF_AV // HEAD_DIM + h))],
        out_specs=[pl.BlockSpec(blk, lambda i, h: (i, 0, h)), win, win],
        out_shape=[jax.ShapeDtypeStruct((b, s, WIDTH_A), BF16), win_shape, win_shape],
        scratch_shapes=[pltpu.VMEM((6, BLK, 2 * BLK), F32),
                        pltpu.VMEM((s, HEAD_DIM), F32),
                        pltpu.VMEM((s, HEAD_DIM), F32),
                        pltpu.VMEM((s, HEAD_DIM), F32)],
        compiler_params=_params(("arbitrary", "arbitrary"), 48),
        name="attn_prompt",
    )(buckets, rel_bias, proj3, proj3, proj3)


def _attn_sample_body(bucket_ref, relbt_ref, q_ref, kn_ref, vn_ref,
                      k1_ref, k4_ref, k16_ref, v1_ref, v4_ref, v16_ref, o_ref, bias_scr):
    relbt = relbt_ref[...]
    tile = (HEADS_A, HEAD_DIM)

    @pl.when(pl.program_id(0) == 0)
    def _():
        for br in range(3):
            bk = bucket_ref[br]
            bias = jnp.zeros((BLK,) + tile, F32)
            for kb in range(N_BUCKETS):
                col = jnp.broadcast_to(relbt[:, kb:kb + 1], tile)
                bias = jnp.where(bk == kb, col[None], bias)
            bias_scr[br] = bias

    def lane_sum(x):
        return jnp.broadcast_to(jnp.sum(x, axis=-1, keepdims=True), x.shape)

    q = q_ref[0]
    s_self = lane_sum(q * kn_ref[0]) * ATT_SCALE + jnp.broadcast_to(relbt[:, 0:1], tile)
    scores = []
    m = s_self
    for br, k_ref in enumerate((k1_ref, k4_ref, k16_ref)):
        s = lane_sum(k_ref[...] * q[None]) * ATT_SCALE + bias_scr[br]
        scores.append(s)
        m = jnp.maximum(m, jnp.max(s, axis=0))
    p_self = 3.0 * jnp.exp(s_self - m)
    l = p_self
    acc = p_self * vn_ref[0]
    for s, v_ref in zip(scores, (v1_ref, v4_ref, v16_ref)):
        p = jnp.exp(s - m[None])
        l = l + jnp.sum(p, axis=0)
        acc = acc + jnp.sum(p * v_ref[...], axis=0)
    o_ref[0] = (acc / l).astype(o_ref.dtype)


def _attn_sample(q, k_new, v_new, cache_k, cache_v, rel_bias):
    b, past = cache_k.shape[:2]
    tile = (HEADS_A, HEAD_DIM)
    buckets = jnp.asarray(np.broadcast_to(_sample_bucket_tables().reshape(3, BLK, 1, 1), (3, BLK) + tile))
    row = pl.BlockSpec((1,) + tile, lambda i: (i, 0, 0))
    views, specs = [], []
    for cache in (cache_k, cache_v):
        for dil in DILATIONS:
            views.append(cache.reshape((b, past // dil, dil) + tile))
            last = past // dil // BLK - 1
            specs.append(pl.BlockSpec((None, BLK, None) + tile,
                                      functools.partial(lambda last, i: (i, last, 0, 0, 0), last)))
    return pl.pallas_call(
        _attn_sample_body,
        grid=(b,),
        in_specs=[pl.BlockSpec((3, BLK) + tile, lambda i: (0, 0, 0, 0)),
                  pl.BlockSpec((HEADS_A, N_BUCKETS), lambda i: (0, 0)),
                  row, row, row] + specs,
        out_specs=row,
        out_shape=jax.ShapeDtypeStruct((b,) + tile, BF16),
        scratch_shapes=[pltpu.VMEM((3, BLK) + tile, F32)],
        compiler_params=_params(("arbitrary",), 40),
        name="attn_sample",
    )(buckets, rel_bias.T, q, k_new, v_new, *views)


GROUP = 4
ATTN_UNROLL = 8
OUTPROJ_ROWS = 128
NORM_ROWS = 128
FFN_TM = 1024
FFN_TF = 512
FFN_ROWS = 64
FFN_COLS = 512
FFN_PIECE = 256


def _dn_prompt_body(q_ref, k_ref, v_ref, z_ref, h_ref, wgate_ref, cw_ref, cs_ref, s0_ref, alog_ref, dtb_ref, nw_ref,
                    o_ref, s_out_ref,
                    s_scr, e_scr, qn_scr, kn_scr, vv_scr, g_scr, beta_scr,
                    w_scr, u_scr, qg_scr, kdt_scr, attn_scr, gl_scr, o_scr, *, tt):
    t = pl.program_id(1)
    nt = pl.num_programs(1)

    @pl.when(t == 0)
    def _():
        s_scr[...] = s0_ref[0]
        e_scr[5:8, :] = cs_ref[0]

    e_scr[8:8 + tt, 0:WIDTH_BQK] = q_ref[0]
    e_scr[8:8 + tt, WIDTH_BQK:2 * WIDTH_BQK] = k_ref[0]
    e_scr[8:8 + tt, 2 * WIDTH_BQK:CONV_DIM] = v_ref[0]

    def l2n(x):
        return x * lax.rsqrt(jnp.sum(x * x, axis=-1, keepdims=True) + EPS)

    for c0 in range(0, CONV_DIM, DK):
        cols = slice(c0, c0 + DK)
        w = cw_ref[:, cols]
        y = w[0:1, :] * e_scr[5:5 + tt, cols]
        for i in range(1, CONV_W):
            y = y + w[i:i + 1, :] * e_scr[5 + i:5 + i + tt, cols]
        y = _silu(y)
        if c0 < WIDTH_BQK:
            qn_scr[:, cols] = l2n(y) * QK_SCALE
        elif c0 < 2 * WIDTH_BQK:
            kn_scr[:, c0 - WIDTH_BQK:c0 - WIDTH_BQK + DK] = l2n(y)
        else:
            vv_scr[:, c0 - 2 * WIDTH_BQK:c0 - 2 * WIDTH_BQK + DK] = y
    e_scr[5:8, :] = e_scr[tt + 5:tt + 8, :]
    gates = jnp.dot(h_ref[0], wgate_ref[...], preferred_element_type=F32)
    beta_scr[...] = _sigmoid(gates)
    g_scr[...] = -jnp.exp(alog_ref[...]) * _softplus(gates + dtb_ref[...])

    ri = lax.broadcasted_iota(jnp.int32, (CHUNK, CHUNK), 0)
    ci = lax.broadcasted_iota(jnp.int32, (CHUNK, CHUNK), 1)
    tri = ri >= ci
    strict = ri > ci
    same_sub = (ri // SUB) == (ci // SUB)
    tril_ones = tri.astype(F32)
    nw = nw_ref[...]

    for c in range(tt // CHUNK):
        rows = slice(c * CHUNK, (c + 1) * CHUNK)
        beta_all = beta_scr[rows, :]
        gc_all = _fdot(tril_ones, g_scr[rows, :])
        gc_all_t = gc_all.T
        g_scr[rows, :] = gc_all
        for hq0 in range(0, QK_HEADS_B, GROUP):
            units = []
            for hq in range(hq0, hq0 + GROUP):
                qn = qn_scr[rows, hq * DK:(hq + 1) * DK]
                kn = kn_scr[rows, hq * DK:(hq + 1) * DK]
                kk = _bdot_nt(kn, kn)
                qk = _bdot_nt(qn, kn)
                for hv in range(2 * hq, 2 * hq + 2):
                    beta = beta_all[:, hv:hv + 1]
                    gc = gc_all[:, V_HEADS_B + hv:V_HEADS_B + hv + 1]
                    gc_row = gc_all_t[V_HEADS_B + hv:V_HEADS_B + hv + 1, :]
                    gc_last = gc_row[:, CHUNK - 1:CHUNK]
                    decay = jnp.exp(jnp.where(tri, gc - gc_row, NEG))
                    a = jnp.where(strict, beta * kk * decay, 0.0)
                    egc = jnp.exp(gc)
                    attn_scr[hv, rows, :] = (qk * decay).astype(BF16)
                    qg_scr[hv, rows, :] = (qn * egc).astype(BF16)
                    kd = kn * jnp.exp(gc_last - gc)
                    kdt_scr[hv, c * DK:(c + 1) * DK, :] = kd.T.astype(BF16)
                    gl_scr[hv, c * 8:(c + 1) * 8, :] = jnp.broadcast_to(jnp.exp(gc_last), (8, DV))
                    units.append((hv, a))
            ds = [jnp.where(same_sub, a, 0.0) for _, a in units]
            ns = [-dd for dd in ds]
            pws = ds
            for _ in range(SUB.bit_length() - 2):
                pws = [_bdot(pw, pw) for pw in pws]
                ns = [n + pw + _bdot(n, pw) for n, pw in zip(ns, pws)]
            ls = [jnp.where(same_sub, 0.0, a) for _, a in units]
            ps = [lo + _bdot(lo, n) for lo, n in zip(ls, ns)]
            ns = [n - (p + _bdot(n, p)) for n, p in zip(ns, ps)]
            xs = []
            for hv, _ in units:
                beta = beta_scr[rows, hv:hv + 1]
                kscale = beta * jnp.exp(g_scr[rows, V_HEADS_B + hv:V_HEADS_B + hv + 1])
                xs.append(jnp.concatenate([kn_scr[rows, (hv // 2) * DK:(hv // 2 + 1) * DK] * kscale,
                                           vv_scr[rows, hv * DV:(hv + 1) * DV] * beta], axis=-1))
            wus = [x + _bdot(n, x) for n, x in zip(ns, xs)]
            for wu, (hv, _) in zip(wus, units):
                w_scr[hv, rows, :] = wu[:, :DK].astype(BF16)
                u_scr[hv, rows, :] = wu[:, DK:]

    heads = range(V_HEADS_B)
    for c in range(tt // CHUNK):
        rows = slice(c * CHUNK, (c + 1) * CHUNK)
        states = [s_scr[hv] for hv in heads]
        states_b = [s.astype(BF16) for s in states]
        v_news = [u_scr[hv, rows, :] - jnp.dot(w_scr[hv, rows, :], states_b[hv], preferred_element_type=F32)
                  for hv in heads]
        v_news_b = [v.astype(BF16) for v in v_news]
        for hv in heads:
            s_scr[hv] = (states[hv] * gl_scr[hv, c * 8:c * 8 + 1, :]
                         + jnp.dot(kdt_scr[hv, c * DK:(c + 1) * DK, :], v_news_b[hv],
                                   preferred_element_type=F32))
        for hv in heads:
            o_scr[rows, hv * DV:(hv + 1) * DV] = (
                jnp.dot(qg_scr[hv, rows, :], states_b[hv], preferred_element_type=F32)
                + jnp.dot(attn_scr[hv, rows, :], v_news_b[hv], preferred_element_type=F32))

    for hv in heads:
        o = o_scr[:, hv * DV:(hv + 1) * DV]
        z = z_ref[0, :, hv * DV:(hv + 1) * DV]
        o = o * lax.rsqrt(jnp.mean(o * o, axis=-1, keepdims=True) + EPS) * nw * _silu(z)
        o_ref[0, :, hv * DV:(hv + 1) * DV] = o.astype(o_ref.dtype)

    @pl.when(t == nt - 1)
    def _():
        s_out_ref[0] = s_scr[...]


def _dn_prompt(proj3, h3, w_gate, conv_state, s0, conv_w, alog_vec, dtb_vec, norm_w, tt):
    b, s, _ = proj3.shape
    body = functools.partial(_dn_prompt_body, tt=tt)
    nh = V_HEADS_B
    in_specs = [
        pl.BlockSpec((1, tt, WIDTH_BQK), lambda i, t: (i, t, OFF_BQ // WIDTH_BQK)),
        pl.BlockSpec((1, tt, WIDTH_BQK), lambda i, t: (i, t, OFF_BK // WIDTH_BQK)),
        pl.BlockSpec((1, tt, WIDTH_BV), lambda i, t: (i, t, OFF_BV // WIDTH_BV)),
        pl.BlockSpec((1, tt, WIDTH_BV), lambda i, t: (i, t, OFF_BZ // WIDTH_BV)),
        pl.BlockSpec((1, tt, D_MODEL), lambda i, t: (i, t, 0)),
        pl.BlockSpec((D_MODEL, 128), lambda i, t: (0, 0)),
        pl.BlockSpec((CONV_W, CONV_DIM), lambda i, t: (0, 0)),
        pl.BlockSpec((1, CONV_W - 1, CONV_DIM), lambda i, t: (i, 0, 0)),
        pl.BlockSpec((1, nh, DK, DV), lambda i, t: (i, 0, 0, 0)),
        pl.BlockSpec((1, 128), lambda i, t: (0, 0)),
        pl.BlockSpec((1, 128), lambda i, t: (0, 0)),
        pl.BlockSpec((1, DV), lambda i, t: (0, 0)),
    ]
    return pl.pallas_call(
        body,
        grid=(b, s // tt),
        in_specs=in_specs,
        out_specs=[pl.BlockSpec((1, tt, WIDTH_BV), lambda i, t: (i, t, 0)),
                   pl.BlockSpec((1, nh, DK, DV), lambda i, t: (i, 0, 0, 0))],
        out_shape=[jax.ShapeDtypeStruct((b, s, WIDTH_BV), BF16),
                   jax.ShapeDtypeStruct((b, nh, DK, DV), F32)],
        scratch_shapes=[pltpu.VMEM((nh, DK, DV), F32),
                        pltpu.VMEM((tt + 8, CONV_DIM), F32),
                        pltpu.VMEM((tt, WIDTH_BQK), F32),
                        pltpu.VMEM((tt, WIDTH_BQK), F32),
                        pltpu.VMEM((tt, WIDTH_BV), F32),
                        pltpu.VMEM((tt, 128), F32),
                        pltpu.VMEM((tt, 128), F32),
                        pltpu.VMEM((nh, tt, DK), BF16),
                        pltpu.VMEM((nh, tt, DV), F32),
                        pltpu.VMEM((nh, tt, DK), BF16),
                        pltpu.VMEM((nh, tt // CHUNK * DK, CHUNK), BF16),
                        pltpu.VMEM((nh, tt, CHUNK), BF16),
                        pltpu.VMEM((nh, tt // CHUNK * 8, DV), F32),
                        pltpu.VMEM((tt, WIDTH_BV), F32)],
        compiler_params=_params(("arbitrary", "arbitrary"), 48),
        name="deltanet_prompt",
    )(proj3, proj3, proj3, proj3, h3, w_gate, conv_w, conv_state, s0, alog_vec, dtb_vec, norm_w)


def _dn_sample_body(proj_ref, gates_ref, cw_ref, cs_ref, s0_ref, alog_ref, dtb_ref, nw_ref,
                    o_ref, cs_out_ref, s_out_ref):
    pre = proj_ref[0, :, OFF_BQ:OFF_BQ + CONV_DIM]
    buf = cs_ref[0]
    w = cw_ref[...]
    y = w[CONV_W - 1:CONV_W, :] * pre
    for i in range(CONV_W - 1):
        y = y + w[i:i + 1, :] * buf[i:i + 1, :]
    y = _silu(y)
    cs_out_ref[0, 0:CONV_W - 2, :] = buf[1:CONV_W - 1, :]
    cs_out_ref[0, CONV_W - 2:CONV_W - 1, :] = pre

    gates = gates_ref[0]
    beta_all = _sigmoid(gates)
    g_all = -jnp.exp(alog_ref[...]) * _softplus(gates + dtb_ref[...])
    nw = nw_ref[...]

    def l2n(x):
        return x * lax.rsqrt(jnp.sum(x * x, axis=-1, keepdims=True) + EPS)

    row8 = lax.broadcasted_iota(jnp.int32, (8, DK), 0) == 0
    for hv in range(V_HEADS_B):
        hq = hv // 2
        q = l2n(y[:, hq * DK:(hq + 1) * DK]) * QK_SCALE
        k = l2n(y[:, WIDTH_BQK + hq * DK:WIDTH_BQK + (hq + 1) * DK])
        v = y[:, 2 * WIDTH_BQK + hv * DV:2 * WIDTH_BQK + (hv + 1) * DV]
        beta = beta_all[:, hv:hv + 1]
        g = g_all[:, V_HEADS_B + hv:V_HEADS_B + hv + 1]
        eg = jnp.exp(g)
        state = s0_ref[0, hv]

        def pad8(x):
            return jnp.where(row8, jnp.broadcast_to(x, (8, x.shape[-1])), 0.0)

        v_new = v * beta - _bdot(pad8(k * (beta * eg)), state)[0:1, :]
        qk = jnp.sum(q.astype(BF16).astype(F32) * k.astype(BF16).astype(F32), axis=-1, keepdims=True)
        o = _bdot(pad8(q * eg), state)[0:1, :] + qk.astype(BF16).astype(F32) * v_new.astype(BF16).astype(F32)
        s_out_ref[0, hv] = state * eg + _bdot_tn(pad8(k), pad8(v_new))
        z = proj_ref[0, :, OFF_BZ + hv * DV:OFF_BZ + (hv + 1) * DV]
        o = o * lax.rsqrt(jnp.mean(o * o, axis=-1, keepdims=True) + EPS) * nw * _silu(z)
        o_ref[0, :, hv * DV:(hv + 1) * DV] = o.astype(o_ref.dtype)


def _dn_sample(proj, gates, conv_state, s0, conv_w, alog_vec, dtb_vec, norm_w):
    b = proj.shape[0]
    return pl.pallas_call(
        _dn_sample_body,
        grid=(b,),
        in_specs=[pl.BlockSpec((1, 1, PROJ_MAIN), lambda i: (i, 0, 0)),
                  pl.BlockSpec((1, 1, 128), lambda i: (i, 0, 0)),
                  pl.BlockSpec((CONV_W, CONV_DIM), lambda i: (0, 0)),
                  pl.BlockSpec((1, CONV_W - 1, CONV_DIM), lambda i: (i, 0, 0)),
                  pl.BlockSpec((1, V_HEADS_B, DK, DV), lambda i: (i, 0, 0, 0)),
                  pl.BlockSpec((1, 128), lambda i: (0, 0)),
                  pl.BlockSpec((1, 128), lambda i: (0, 0)),
                  pl.BlockSpec((1, DV), lambda i: (0, 0))],
        out_specs=[pl.BlockSpec((1, 1, WIDTH_BV), lambda i: (i, 0, 0)),
                   pl.BlockSpec((1, CONV_W - 1, CONV_DIM), lambda i: (i, 0, 0)),
                   pl.BlockSpec((1, V_HEADS_B, DK, DV), lambda i: (i, 0, 0, 0))],
        out_shape=[jax.ShapeDtypeStruct((b, 1, WIDTH_BV), BF16),
                   jax.ShapeDtypeStruct((b, CONV_W - 1, CONV_DIM), F32),
                   jax.ShapeDtypeStruct((b, V_HEADS_B, DK, DV), F32)],
        compiler_params=_params(("arbitrary",), 40),
        name="deltanet_sample",
    )(proj, gates, conv_w, conv_state, s0, alog_vec, dtb_vec, norm_w)


def _outproj_body(att_ref, dn_ref, wa_ref, wb_ref, x_ref, lnpost_ref, lnpre_ref, x1_ref, h2_ref):
    tm = x_ref.shape[0]
    piece = min(tm, OUTPROJ_ROWS)
    for rows in [slice(r, r + piece) for r in range(0, tm, piece)]:
        mix = (jnp.dot(att_ref[rows, :], wa_ref[...], preferred_element_type=F32)
               + jnp.dot(dn_ref[rows, :], wb_ref[...], preferred_element_type=F32))
        x1 = x_ref[rows, :] + _rms(mix, lnpost_ref[...])
        x1_ref[rows, :] = x1
        h2_ref[rows, :] = _rms(x1, lnpre_ref[...]).astype(h2_ref.dtype)


def _outproj(att, dn, w, x, ln_post, ln_pre, tm):
    m, d = x.shape
    assert WIDTH_A == WIDTH_BV
    return pl.pallas_call(
        _outproj_body,
        grid=(m // tm,),
        in_specs=[pl.BlockSpec((tm, WIDTH_A), lambda i: (i, 0)),
                  pl.BlockSpec((tm, WIDTH_BV), lambda i: (i, 0)),
                  pl.BlockSpec((WIDTH_A, d), lambda i: (0, 0)),
                  pl.BlockSpec((WIDTH_BV, d), lambda i: (1, 0)),
                  pl.BlockSpec((tm, d), lambda i: (i, 0)),
                  pl.BlockSpec((1, d), lambda i: (0, 0)),
                  pl.BlockSpec((1, d), lambda i: (0, 0))],
        out_specs=[pl.BlockSpec((tm, d), lambda i: (i, 0)),
                   pl.BlockSpec((tm, d), lambda i: (i, 0))],
        out_shape=[jax.ShapeDtypeStruct((m, d), F32),
                   jax.ShapeDtypeStruct((m, d), BF16)],
        compiler_params=_params(("arbitrary",), 48),
        name="outproj",
    )(att, dn, w, w, x, ln_post, ln_pre)


def _ffn_body(*refs, tm, tiles_per_seq, single_token):
    if single_token:
        (h_ref, wg_ref, wv_ref, cwb_ref, wo_prev_ref, wo_last_ref, x1_hbm, ln_ref, pg_ref, pv_ref,
         o_ref, ng_ref, nv_ref, eg_scr, ev_scr, carry_scr, act_scr, x1_scr, x1_sem) = refs
    else:
        (h_ref, wg_ref, wv_ref, cwb_ref, wo_prev_ref, wo_last_ref, x1_hbm, ln_ref, prev_ref,
         o_ref, new_ref, eg_scr, ev_scr, carry_scr, act_scr, x1_scr, x1_sem) = refs
    i = pl.program_id(0)
    j = pl.program_id(1)
    nj = pl.num_programs(1)
    d = o_ref.shape[-1]
    tf = act_scr.shape[-1]
    cur = j % 2
    act_cur = act_scr.at[cur]
    act_prev = act_scr.at[1 - cur]

    def x1_copy():
        return pltpu.make_async_copy(x1_hbm.at[pl.ds(pl.multiple_of(i * tm, tm), tm), :], x1_scr, x1_sem)

    def down_proj(act_ref, wo_ref):
        for n in range(0, d, FFN_COLS):
            o_ref[:, n:n + FFN_COLS] += jnp.dot(act_ref[...], wo_ref[:, n:n + FFN_COLS],
                                                preferred_element_type=F32)

    @pl.when(j == 0)
    def _():
        x1_copy().start()
        o_ref[...] = jnp.zeros_like(o_ref)
        act_prev[...] = jnp.zeros_like(act_prev)

    if single_token:
        def up_conv(w_ref, tile, prev_ref, new_ref):
            up = jnp.dot(h_ref[...], w_ref[...], preferred_element_type=F32)
            cw = cwb_ref[tile]
            new_ref[...] = up
            return cw[0:1, :] * prev_ref[0] + cw[1:2, :] * prev_ref[1] + cw[2:3, :] * up + cw[3:4, :]

        gate = up_conv(wg_ref, j, pg_ref, ng_ref)
        val = up_conv(wv_ref, nj + j, pv_ref, nv_ref)
        act_cur[...] = (_gelu_tanh(gate) * val).astype(BF16)
        down_proj(act_prev, wo_prev_ref)
    else:
        first_tile = i % tiles_per_seq == 0
        pieces = [slice(c, c + FFN_PIECE) for c in range(0, tf, FFN_PIECE)]

        def up_proj(cols):
            for w_ref, tile, e_scr, slot in ((wg_ref, j, eg_scr, 0), (wv_ref, nj + j, ev_scr, 1)):
                e_scr[8:8 + tm, cols] = jnp.dot(h_ref[...], w_ref[:, cols], preferred_element_type=F32)
                e_scr[6:8, cols] = jnp.where(first_tile, prev_ref[0, tile, :, cols],
                                             carry_scr[slot, j, 6:8, cols])
                tail = e_scr[tm + 6:tm + 8, cols]
                carry_scr[slot, j, 6:8, cols] = tail
                new_ref[0, tile, :, cols] = tail

        def conv(e_scr, tile, cols, r):
            cw = cwb_ref[tile, :, cols]
            return (cw[0:1, :] * e_scr[6 + r:6 + r + FFN_ROWS, cols]
                    + cw[1:2, :] * e_scr[7 + r:7 + r + FFN_ROWS, cols]
                    + cw[2:3, :] * e_scr[8 + r:8 + r + FFN_ROWS, cols] + cw[3:4, :])

        def conv_geglu(cols):
            for r in range(0, tm, FFN_ROWS):
                act_cur[r:r + FFN_ROWS, cols] = (_gelu_tanh(conv(eg_scr, j, cols, r))
                                                 * conv(ev_scr, nj + j, cols, r)).astype(BF16)

        up_proj(pieces[0])
        for c in range(1, len(pieces)):
            up_proj(pieces[c])
            conv_geglu(pieces[c - 1])
        down_proj(act_prev, wo_prev_ref)
        conv_geglu(pieces[-1])

    @pl.when(j == nj - 1)
    def _():
        x1_copy().wait()
        piece = min(tm, OUTPROJ_ROWS)
        for r in range(0, tm, piece):
            rows = slice(r, r + piece)
            f = o_ref[rows, :] + jnp.dot(act_cur[rows, :], wo_last_ref[...], preferred_element_type=F32)
            o_ref[rows, :] = x1_scr[rows, :] + _rms(f, ln_ref[...])


def _ffn(h2, w_in, conv_w, conv_b, w_out, x1, ln_post, prev, tm, tf, seq_len):
    m, d = h2.shape
    single = seq_len == 1
    nj = D_FF // tf
    tiles_per_seq = 1 if single else seq_len // tm
    cwb = jnp.concatenate([conv_w, conv_b], axis=0).reshape(FFN_CONV_W + 1, 2 * nj, tf).transpose(1, 0, 2)
    cwb_spec = pl.BlockSpec((2 * nj, FFN_CONV_W + 1, tf), lambda i, j: (0, 0, 0))
    if single:
        prev_args = (prev, prev)
        prev_specs = [pl.BlockSpec((2, tm, tf), lambda i, j: (0, i, j)),
                      pl.BlockSpec((2, tm, tf), lambda i, j: (0, i, nj + j))]
        new_specs = [pl.BlockSpec((tm, tf), lambda i, j: (i, j))] * 2
        new_shapes = [jax.ShapeDtypeStruct((m, D_FF), F32)] * 2
    else:
        prev_args = (prev.reshape(-1, 2, 2 * nj, tf).transpose(0, 2, 1, 3),)
        prev_specs = [pl.BlockSpec((1, 2 * nj, 2, tf), lambda i, j: (i // tiles_per_seq, 0, 0, 0))]
        new_specs = [pl.BlockSpec((1, 2 * nj, 2, tf), lambda i, j: (i, 0, 0, 0))]
        new_shapes = [jax.ShapeDtypeStruct((m // tm, 2 * nj, 2, tf), F32)]
    body = functools.partial(_ffn_body, tm=tm, tiles_per_seq=tiles_per_seq, single_token=single)
    once = dict(pipeline_mode=pl.Buffered(1)) if tm >= 1024 else {}
    outs = pl.pallas_call(
        body,
        grid=(m // tm, nj),
        in_specs=[pl.BlockSpec((tm, d), lambda i, j: (i, 0), **once),
                  pl.BlockSpec((d, tf), lambda i, j: (0, j)),
                  pl.BlockSpec((d, tf), lambda i, j: (0, nj + j)),
                  cwb_spec,
                  pl.BlockSpec((tf, d), lambda i, j: (jnp.maximum(j - 1, 0), 0)),
                  pl.BlockSpec((tf, d), lambda i, j: (nj - 1, 0), pipeline_mode=pl.Buffered(1)),
                  pl.BlockSpec(memory_space=pl.ANY),
                  pl.BlockSpec((1, d), lambda i, j: (0, 0))] + prev_specs,
        out_specs=[pl.BlockSpec((tm, d), lambda i, j: (i, 0), **once)] + new_specs,
        out_shape=[jax.ShapeDtypeStruct((m, d), F32)] + new_shapes,
        scratch_shapes=[pltpu.VMEM((tm + 8, tf), F32),
                        pltpu.VMEM((tm + 8, tf), F32),
                        pltpu.VMEM((2, nj, 8, tf), F32),
                        pltpu.VMEM((2, tm, tf), BF16),
                        pltpu.VMEM((tm, d), F32),
                        pltpu.SemaphoreType.DMA(())],
        compiler_params=_params(("arbitrary", "arbitrary"), 57),
        name="convffn",
    )(h2, w_in, w_in, cwb, w_out, w_out, x1, ln_post, *prev_args)
    if single:
        y, new_g, new_v = outs
        return y, jnp.concatenate([new_g, new_v], axis=-1)
    y, new = outs
    return y, new.transpose(0, 2, 1, 3).reshape(m // tm, 2, 2 * D_FF)


def _cache_shift_body(ck_ref, cv_ref, ck_next_ref, cv_next_ref, nk_ref, nv_ref, ok_ref, ov_ref, *, tr):
    last = pl.program_id(1) == pl.num_programs(1) - 1
    for c_ref, nxt_ref, n_ref, o_ref in ((ck_ref, ck_next_ref, nk_ref, ok_ref),
                                         (cv_ref, cv_next_ref, nv_ref, ov_ref)):
        o_ref[0, 0:tr - 1] = c_ref[0, 1:tr]
        o_ref[0, tr - 1] = jnp.where(last, n_ref[0, 0], nxt_ref[0, 0])


def _cache_shift(cache_k, cache_v, new_k, new_v, tr):
    nb, rows, nh, dh = cache_k.shape
    main = pl.BlockSpec((1, tr, nh, dh), lambda b, i: (b, i, 0, 0))
    nxt = pl.BlockSpec((1, 1, nh, dh), lambda b, i: (b, jnp.minimum((i + 1) * tr, rows - 1), 0, 0))
    new = pl.BlockSpec((1, 1, nh, dh), lambda b, i: (b, 0, 0, 0))
    shape = jax.ShapeDtypeStruct(cache_k.shape, cache_k.dtype)
    return pl.pallas_call(
        functools.partial(_cache_shift_body, tr=tr),
        grid=(nb, rows // tr),
        in_specs=[main, main, nxt, nxt, new, new],
        out_specs=[main, main],
        out_shape=[shape, shape],
        compiler_params=_params(("arbitrary", "arbitrary"), 40),
        name="cache_shift",
    )(cache_k, cache_v, cache_k, cache_v, new_k, new_v)


def _lane_vec(values, offset):
    return jnp.zeros((1, 128), F32).at[0, offset:offset + V_HEADS_B].set(values.astype(F32))


def kernel(x_prompt, x_sample, cache_win_k, cache_win_v, state_dn_conv, state_dn_rec, state_ffn_conv,
           rel_bias, ln_mix_pre, w_in, dn_conv_w, dn_A_log, dn_dt_bias, dn_norm_w, w_out, ln_mix_post,
           ln_ffn_pre, w_ffn_in, ffn_conv_w, ffn_conv_b, w_ffn_out, ln_ffn_post):
    bp, sp, d = x_prompt.shape
    bs = x_sample.shape[0]
    l = 0

    w_main = w_in[l].astype(BF16)
    w_gate = jnp.pad(w_main[:, PROJ_MAIN:], ((0, 0), (0, 128 - 2 * V_HEADS_B)))
    wo = w_out[l].astype(BF16)
    wf_in = w_ffn_in[l].astype(BF16)
    wf_out = w_ffn_out[l].astype(BF16)
    ln1 = ln_mix_pre[l][None, :]
    ln2 = ln_mix_post[l][None, :]
    ln3 = ln_ffn_pre[l][None, :]
    ln4 = ln_ffn_post[l][None, :]
    conv_w = dn_conv_w[l]
    alog_vec = _lane_vec(dn_A_log[l], V_HEADS_B)
    dtb_vec = _lane_vec(dn_dt_bias[l], V_HEADS_B)
    norm_w = dn_norm_w[l][None, :]
    fcw = ffn_conv_w[l]
    fcb = ffn_conv_b[l][None, :]

    xp = x_prompt.reshape(bp * sp, d)
    proj_p, hp = _norm_matmul(xp, ln1, w_main, PROJ_MAIN, 1024, 2048, "inproj_prompt")
    proj3 = proj_p.reshape(bp, sp, PROJ_MAIN)
    keep = min(MAX_DISTANCE, sp)
    att_p, win_k, win_v = _attn_prompt(proj3, rel_bias, keep)
    dn_p, p_dn_rec = _dn_prompt(
        proj3, hp.reshape(bp, sp, d), w_gate,
        jnp.zeros((bp, CONV_W - 1, CONV_DIM), F32), jnp.zeros((bp, V_HEADS_B, DK, DV), F32),
        conv_w, alog_vec, dtb_vec, norm_w, 512)
    x1_p, h2_p = _outproj(att_p.reshape(bp * sp, WIDTH_A), dn_p.reshape(bp * sp, WIDTH_BV),
                          wo, xp, ln2, ln3, 512)
    y_p, fc = _ffn(h2_p, wf_in, fcw, fcb, wf_out, x1_p, ln4,
                   jnp.zeros((bp, FFN_CONV_W - 1, 2 * D_FF), F32), FFN_TM, FFN_TF, sp)
    p_win_k = win_k.reshape(1, bp, keep, HEADS_A, HEAD_DIM)
    p_win_v = win_v.reshape(1, bp, keep, HEADS_A, HEAD_DIM)
    p_dn_conv = proj3[:, sp - (CONV_W - 1):, OFF_BQ:OFF_BQ + CONV_DIM][None]
    tiles = sp // FFN_TM
    p_ffn_conv = fc[tiles - 1::tiles][None]

    xs = x_sample.reshape(bs, d)
    proj_s, hs = _norm_matmul(xs, ln1, w_main, PROJ_MAIN, bs, 1024, "inproj_sample")
    gates_s = _matmul(hs, w_gate, bs, 128, "gates_sample")
    past = cache_win_k.shape[2]
    ck = cache_win_k[l]
    cv = cache_win_v[l]
    new_k = proj_s[:, OFF_AK:OFF_AK + WIDTH_A]
    new_v = proj_s[:, OFF_AV:OFF_AV + WIDTH_A]
    new_q = proj_s[:, OFF_AQ:OFF_AQ + WIDTH_A].reshape(bs, HEADS_A, HEAD_DIM)
    new_k = new_k.reshape(bs, HEADS_A, HEAD_DIM)
    new_v = new_v.reshape(bs, HEADS_A, HEAD_DIM)
    att_s = _attn_sample(new_q, new_k, new_v, ck, cv, rel_bias)
    s_win_k, s_win_v = _cache_shift(ck, cv, new_k[:, None], new_v[:, None], 1024)
    dn_s, s_dn_conv, s_dn_rec = _dn_sample(proj_s[:, None], gates_s[:, None], state_dn_conv[l],
                                           state_dn_rec[l], conv_w, alog_vec, dtb_vec, norm_w)
    x1_s, h2_s = _outproj(att_s.reshape(bs, WIDTH_A), dn_s.reshape(bs, WIDTH_BV),
                          wo, xs, ln2, ln3, bs)
    prev_s = jnp.swapaxes(state_ffn_conv[l], 0, 1)
    y_s, up_s = _ffn(h2_s, wf_in, fcw, fcb, wf_out, x1_s, ln4, prev_s, bs, FFN_TF, 1)
    s_ffn_conv = jnp.stack([prev_s[1], up_s], axis=1)[None]

    return (y_p.reshape(bp, sp, d), y_s.reshape(bs, 1, d),
            p_win_k, p_win_v, p_dn_conv, p_dn_rec[None], p_ffn_conv,
            s_win_k[None], s_win_v[None], s_dn_conv[None], s_dn_rec[None], s_ffn_conv)
```

```python
import functools
import math

import numpy as np
import jax
import jax.numpy as jnp
from jax import lax
from jax.experimental import pallas as pl
from jax.experimental.pallas import tpu as pltpu

F32 = jnp.float32
BF16 = jnp.bfloat16

D_MODEL = 2048
HEAD_DIM = 128
WIDTH_A = 1024
HEADS_A = 8
DILATIONS = (1, 4, 16)
BLK = 128
N_BUCKETS = 32
MAX_DISTANCE = 2048
DK = 128
DV = 128
V_HEADS_B = 8
QK_HEADS_B = 4
WIDTH_BQK = 512
WIDTH_BV = 1024
CONV_W = 4
CONV_DIM = 2048
CHUNK = 128
SUB = 64
D_FF = 5632
FFN_CONV_W = 3
EPS = 1e-6
NEG = -1e30
ATT_SCALE = HEAD_DIM ** -0.5
QK_SCALE = DK ** -0.5

OFF_AQ, OFF_AK, OFF_AV = 0, 1024, 2048
OFF_BQ, OFF_BK, OFF_BV, OFF_BZ = 3072, 3584, 4096, 5120
OFF_GATES = 6144
PROJ_MAIN = 6144

MIB = 2 ** 20


def _params(semantics, vmem_mib):
    return pltpu.CompilerParams(dimension_semantics=semantics, vmem_limit_bytes=vmem_mib * MIB)


def _bdot(a, b):
    return jnp.dot(a.astype(BF16), b.astype(BF16), preferred_element_type=F32)


def _bdot_nt(a, b):
    return lax.dot_general(a.astype(BF16), b.astype(BF16), (((1,), (1,)), ((), ())),
                           preferred_element_type=F32)


def _bdot_tn(a, b):
    return lax.dot_general(a.astype(BF16), b.astype(BF16), (((0,), (0,)), ((), ())),
                           preferred_element_type=F32)


def _fdot(a, b):
    return jnp.dot(a, b, preferred_element_type=F32, precision=lax.Precision.HIGHEST)


def _silu(x):
    return x * (1.0 / (1.0 + jnp.exp(-x)))


def _sigmoid(x):
    return 1.0 / (1.0 + jnp.exp(-x))


def _softplus(x):
    return jnp.maximum(x, 0.0) + jnp.log(1.0 + jnp.exp(-jnp.abs(x)))


def _gelu_tanh(x):
    c = math.sqrt(2.0 / math.pi)
    half = 0.5 * x
    return half + half * jnp.tanh(x * (c + (c * 0.044715) * (x * x)))


def _rms(x, w):
    return x * lax.rsqrt(jnp.mean(x * x, axis=-1, keepdims=True) + EPS) * w


def _matmul_body(x_ref, w_ref, o_ref):
    o_ref[...] = jnp.dot(x_ref[...], w_ref[...], preferred_element_type=F32)


def _matmul(x, w, tm, tn, name, n=None, n0=0):
    m, k = x.shape
    n = w.shape[1] if n is None else n
    j0 = n0 // tn
    return pl.pallas_call(
        _matmul_body,
        grid=((n - n0) // tn, m // tm),
        in_specs=[pl.BlockSpec((tm, k), lambda j, i: (i, 0)),
                  pl.BlockSpec((k, tn), lambda j, i: (0, j0 + j))],
        out_specs=pl.BlockSpec((tm, tn), lambda j, i: (i, j)),
        out_shape=jax.ShapeDtypeStruct((m, n - n0), F32),
        compiler_params=_params(("arbitrary", "arbitrary"), 48),
        name=name,
    )(x, w)


def _norm_matmul_body(x_ref, ln_ref, w_ref, o_ref, h_ref):
    tm = x_ref.shape[0]
    piece = min(tm, NORM_ROWS)
    for r in range(0, tm, piece):
        rows = slice(r, r + piece)
        h = _rms(x_ref[rows, :], ln_ref[...]).astype(BF16)
        h_ref[rows, :] = h
        o_ref[rows, :] = jnp.dot(h, w_ref[...], preferred_element_type=F32)


def _norm_matmul(x, ln, w, tm, name):
    m, k = x.shape
    nt, _, tn = w.shape
    n = nt * tn
    return pl.pallas_call(
        _norm_matmul_body,
        grid=(m // tm, nt),
        in_specs=[pl.BlockSpec((tm, k), lambda i, j: (i, 0)),
                  pl.BlockSpec((1, k), lambda i, j: (0, 0)),
                  pl.BlockSpec((None, k, tn), lambda i, j: (j, 0, 0))],
        out_specs=[pl.BlockSpec((tm, tn), lambda i, j: (i, j)),
                   pl.BlockSpec((tm, k), lambda i, j: (i, 0))],
        out_shape=[jax.ShapeDtypeStruct((m, n), F32), jax.ShapeDtypeStruct((m, k), BF16)],
        compiler_params=_params(("arbitrary", "arbitrary"), 48),
        name=name,
    )(x, ln, w)


def _rel_bucket_np(dist):
    dist = np.asarray(dist, np.int64)
    max_exact = N_BUCKETS // 2
    d = np.maximum(dist, 1).astype(np.float64)
    val = np.log(d / max_exact) / math.log(MAX_DISTANCE / max_exact) * (N_BUCKETS - max_exact)
    frac = np.abs(val - np.round(val))
    near = (frac < 2e-5) &(dist >= max_exact) & (dist != max_exact) & (dist < MAX_DISTANCE)
    assert not near.any(), "distance on a bucket boundary"
    val = np.where(dist == max_exact, 0.0, val)
    large = np.minimum(max_exact + np.trunc(val).astype(np.int64), N_BUCKETS - 1)
    return np.where(dist < max_exact, dist, large).astype(np.int32)


def _prompt_bucket_tables():
    qi = np.arange(BLK)[:, None]
    kj = np.arange(2 * BLK)[None, :]
    delta = BLK + qi - kj
    inwin = (delta >= 0) & (delta <= BLK)
    tabs = []
    for dil in DILATIONS:
        b = _rel_bucket_np(np.clip(delta, 0, BLK) * dil)
        tabs.append(np.where(inwin, b, -1))
    return np.stack(tabs).astype(np.int32)


def _sample_bucket_tables():
    j = BLK - np.arange(BLK)
    return np.stack([_rel_bucket_np(j * dil)[None, :] for dil in DILATIONS]).astype(np.int32)


def _attn_prompt_body(bucket_ref, relb_ref, q_ref, k_ref, v_ref, o_ref, wk_ref, wv_ref,
                      bias_scr, acc_scr, m_scr, l_scr):
    h = pl.program_id(1)
    s, keep = k_ref.shape[1], wk_ref.shape[1] // HEADS_A
    wk_ref[0, pl.ds(h, keep, stride=HEADS_A), :] = k_ref[0, s - keep:s, :]
    wv_ref[0, pl.ds(h, keep, stride=HEADS_A), :] = v_ref[0, s - keep:s, :]
    col = lax.broadcasted_iota(jnp.int32, (BLK, 2 * BLK), 1)
    tables = _prompt_bucket_tables()
    for br in range(3):
        bk = bucket_ref[br]
        bias = jnp.zeros((BLK, 2 * BLK), F32)
        for kb in sorted(set(tables[br].ravel().tolist()) - {-1}):
            bias = jnp.where(bk == kb, relb_ref[kb, h], bias)
        full = jnp.where(bk >= 0, bias, NEG)
        bias_scr[2 * br] = full
        bias_scr[2 * br + 1] = jnp.where(col >= BLK, full, NEG)

    def run_branch(br, dil, is_first_branch, is_last_branch):
        shift = int(math.log2(dil))
        span = BLK * dil
        stride = None if dil == 1 else dil

        def rows(start):
            return pl.ds(start, BLK, stride=stride) if stride else pl.ds(start, BLK)

        nb = q_ref.shape[1] // span
        run_len = min(nb, ATTN_UNROLL)
        runs_per_it = ATTN_UNROLL // run_len
        runs_per_res = nb // run_len
        starts_at_zero = runs_per_res == 1

        def tasks(it, carry):
            q_starts, firsts, qs_, ks_, vs_ = [], [], [], [], []
            for rr in range(runs_per_it):
                ri = it * runs_per_it + rr
                n0 = (ri % runs_per_res) * run_len
                if dil == 1:
                    base = pl.multiple_of(n0 * span, BLK)
                else:
                    base = n0 * span + ri // runs_per_res
                first = jnp.where(n0 == 0, 1, 0)
                starts = [base + u * span for u in range(run_len)]
                kb = [k_ref[0, rows(st), :].astype(BF16) for st in starts]
                vb = [v_ref[0, rows(st), :].astype(BF16) for st in starts]
                if starts_at_zero:
                    k_prev, v_prev = None, None
                else:
                    p_start = base - span * (1 - first)
                    if dil == 1:
                        p_start = pl.multiple_of(p_start, BLK)
                    k_prev = k_ref[0, rows(p_start), :].astype(BF16)
                    v_prev = v_ref[0, rows(p_start), :].astype(BF16)
                for u, st in enumerate(starts):
                    q_starts.append(st)
                    qs_.append(q_ref[0, rows(st), :].astype(BF16))
                    kp, vp = (k_prev, v_prev) if u == 0 else (kb[u - 1], vb[u - 1])
                    if kp is None:
                        firsts.append(None)
                        ks_.append(kb[u])
                        vs_.append(vb[u])
                    else:
                        firsts.append(first if u == 0 else 0)
                        ks_.append(jnp.concatenate([kp, kb[u]], axis=0))
                        vs_.append(jnp.concatenate([vp, vb[u]], axis=0))
            if not is_first_branch:
                runs = [(m_scr[rows(qs), :], l_scr[rows(qs), :], acc_scr[rows(qs), :]) for qs in q_starts]
            ss = [_bdot_nt(q, k) * ATT_SCALE
                  + (bias_scr[2 * br, :, BLK:] if f is None else bias_scr[2 * br + f])
                  for q, k, f in zip(qs_, ks_, firsts)]
            ms = [jnp.max(s, axis=-1, keepdims=True) for s in ss]
            ps_ = [jnp.exp(s - m) for s, m in zip(ss, ms)]
            accs = [_bdot(p, jnp.concatenate([v, jnp.ones_like(v)], axis=1)) for p, v in zip(ps_, vs_)]
            outs = []
            for u in range(ATTN_UNROLL):
                m_b = jnp.broadcast_to(ms[u], (BLK, HEAD_DIM))
                l_b = accs[u][:, HEAD_DIM:]
                acc_t = accs[u][:, :HEAD_DIM]
                if not is_first_branch:
                    m_run, l_run, acc_run = runs[u]
                    m_new = jnp.maximum(m_run, m_b)
                    a = jnp.exp(m_run - m_new)
                    b = jnp.exp(m_b - m_new)
                    acc_t = a * acc_run + b * acc_t
                    l_b = a * l_run + b * l_b
                    m_b = m_new
                outs.append((m_b, l_b, acc_t))
            for qs, (m_b, l_b, acc_t) in zip(q_starts, outs):
                if is_last_branch:
                    o_ref[0, rows(qs), :] = (acc_t / l_b).astype(o_ref.dtype)
                else:
                    m_scr[rows(qs), :] = m_b
                    l_scr[rows(qs), :] = l_b
                    acc_scr[rows(qs), :] = acc_t
            return carry

        lax.fori_loop(0, nb * dil // ATTN_UNROLL, tasks, 0)

    run_branch(2, 16, True, False)
    run_branch(1, 4, False, False)
    run_branch(0, 1, False, True)


def _attn_prompt(proj3, rel_bias, keep):
    b, s, _ = proj3.shape
    buckets = jnp.asarray(_prompt_bucket_tables())
    blk = (1, s, HEAD_DIM)
    win = pl.BlockSpec((1, keep * HEADS_A, HEAD_DIM), lambda i, h: (i, 0, 0), pipeline_mode=pl.Buffered(1))
    win_shape = jax.ShapeDtypeStruct((b, keep * HEADS_A, HEAD_DIM), F32)
    return pl.pallas_call(
        _attn_prompt_body,
        grid=(b, HEADS_A),
        in_specs=[pl.BlockSpec((3, BLK, 2 * BLK), lambda i, h: (0, 0, 0)),
                  pl.BlockSpec(memory_space=pltpu.SMEM),
                  pl.BlockSpec(blk, lambda i, h: (i, 0, OFF_AQ // HEAD_DIM + h)),
                  pl.BlockSpec(blk, lambda i, h: (i, 0, OFF_AK // HEAD_DIM + h)),
                  pl.BlockSpec(blk, lambda i, h: (i, 0, OFF_AV // HEAD_DIM + h))],
        out_specs=[pl.BlockSpec(blk, lambda i, h: (i, 0, h)), win, win],
        out_shape=[jax.ShapeDtypeStruct((b, s, WIDTH_A), BF16), win_shape, win_shape],
        scratch_shapes=[pltpu.VMEM((6, BLK, 2 * BLK), F32),
                        pltpu.VMEM((s, HEAD_DIM), F32),
                        pltpu.VMEM((s, HEAD_DIM), F32),
                        pltpu.VMEM((s, HEAD_DIM), F32)],
        compiler_params=_params(("arbitrary", "arbitrary"), 48),
        name="attn_prompt",
    )(buckets, rel_bias, proj3, proj3, proj3)


def _attn_sample_body(bucket_ref, relbt_ref, q_ref, kn_ref, vn_ref,
                      k1_ref, k4_ref, k16_ref, v1_ref, v4_ref, v16_ref, o_ref, bias_scr):
    relbt = relbt_ref[...]
    tile = (HEADS_A, HEAD_DIM)

    @pl.when(pl.program_id(0) == 0)
    def _():
        for br in range(3):
            bk = bucket_ref[br]
            bias = jnp.zeros((BLK,) + tile, F32)
            for kb in range(N_BUCKETS):
                col = jnp.broadcast_to(relbt[:, kb:kb + 1], tile)
                bias = jnp.where(bk == kb, col[None], bias)
            bias_scr[br] = bias

    def lane_sum(x):
        return jnp.broadcast_to(jnp.sum(x, axis=-1, keepdims=True), x.shape)

    q = q_ref[0]
    s_self = lane_sum(q * kn_ref[0]) * ATT_SCALE + jnp.broadcast_to(relbt[:, 0:1], tile)
    scores = []
    m = s_self
    for br, k_ref in enumerate((k1_ref, k4_ref, k16_ref)):
        s = lane_sum(k_ref[...] * q[None]) * ATT_SCALE + bias_scr[br]
        scores.append(s)
        m = jnp.maximum(m, jnp.max(s, axis=0))
    p_self = 3.0 * jnp.exp(s_self - m)
    l = p_self
    acc = p_self * vn_ref[0]
    for s, v_ref in zip(scores, (v1_ref, v4_ref, v16_ref)):
        p = jnp.exp(s - m[None])
        l = l + jnp.sum(p, axis=0)
        acc = acc + jnp.sum(p * v_ref[...], axis=0)
    o_ref[0] = (acc / l).astype(o_ref.dtype)


def _attn_sample(q, k_new, v_new, cache_k, cache_v, rel_bias):
    b, past = cache_k.shape[:2]
    tile = (HEADS_A, HEAD_DIM)
    buckets = jnp.asarray(np.broadcast_to(_sample_bucket_tables().reshape(3, BLK, 1, 1), (3, BLK) + tile))
    row = pl.BlockSpec((1,) + tile, lambda i: (i, 0, 0))
    views, specs = [], []
    for cache in (cache_k, cache_v):
        for dil in DILATIONS:
            views.append(cache.reshape((b, past // dil, dil) + tile))
            last = past // dil // BLK - 1
            specs.append(pl.BlockSpec((None, BLK, None) + tile,
                                      functools.partial(lambda last, i: (i, last, 0, 0, 0), last)))
    return pl.pallas_call(
        _attn_sample_body,
        grid=(b,),
        in_specs=[pl.BlockSpec((3, BLK) + tile, lambda i: (0, 0, 0, 0)),
                  pl.BlockSpec((HEADS_A, N_BUCKETS), lambda i: (0, 0)),
                  row, row, row] + specs,
        out_specs=row,
        out_shape=jax.ShapeDtypeStruct((b,) + tile, BF16),
        scratch_shapes=[pltpu.VMEM((3, BLK) + tile, F32)],
        compiler_params=_params(("arbitrary",), 40),
        name="attn_sample",
    )(buckets, rel_bias.T, q, k_new, v_new, *views)


GROUP = 4
ATTN_UNROLL = 8
OUTPROJ_ROWS = 128
NORM_ROWS = 128
INPROJ_TN = 1024
FFN_TM = 1024
FFN_TF = 512
FFN_ROWS = 64
FFN_COLS = 512
FFN_PIECE = 256


def _dn_prompt_body(q_ref, k_ref, v_ref, z_ref, h_ref, wgate_ref, cw_ref, cs_ref, s0_ref, alog_ref, dtb_ref, nw_ref,
                    o_ref, s_out_ref,
                    s_scr, e_scr, qn_scr, kn_scr, vv_scr, g_scr, beta_scr,
                    w_scr, u_scr, qg_scr, kdt_scr, attn_scr, gl_scr, o_scr, *, tt):
    t = pl.program_id(1)
    nt = pl.num_programs(1)

    @pl.when(t == 0)
    def _():
        s_scr[...] = s0_ref[0]
        e_scr[5:8, :] = cs_ref[0]

    e_scr[8:8 + tt, 0:WIDTH_BQK] = q_ref[0]
    e_scr[8:8 + tt, WIDTH_BQK:2 * WIDTH_BQK] = k_ref[0]
    e_scr[8:8 + tt, 2 * WIDTH_BQK:CONV_DIM] = v_ref[0]

    def l2n(x):
        return x * lax.rsqrt(jnp.sum(x * x, axis=-1, keepdims=True) + EPS)

    for c0 in range(0, CONV_DIM, DK):
        cols = slice(c0, c0 + DK)
        w = cw_ref[:, cols]
        y = w[0:1, :] * e_scr[5:5 + tt, cols]
        for i in range(1, CONV_W):
            y = y + w[i:i + 1, :] * e_scr[5 + i:5 + i + tt, cols]
        y = _silu(y)
        if c0 < WIDTH_BQK:
            qn_scr[:, cols] = l2n(y) * QK_SCALE
        elif c0 < 2 * WIDTH_BQK:
            kn_scr[:, c0 - WIDTH_BQK:c0 - WIDTH_BQK + DK] = l2n(y)
        else:
            vv_scr[:, c0 - 2 * WIDTH_BQK:c0 - 2 * WIDTH_BQK + DK] = y
    e_scr[5:8, :] = e_scr[tt + 5:tt + 8, :]
    gates = jnp.dot(h_ref[0], wgate_ref[...], preferred_element_type=F32)
    beta_scr[...] = _sigmoid(gates)
    g_scr[...] = -jnp.exp(alog_ref[...]) * _softplus(gates + dtb_ref[...])

    ri = lax.broadcasted_iota(jnp.int32, (CHUNK, CHUNK), 0)
    ci = lax.broadcasted_iota(jnp.int32, (CHUNK, CHUNK), 1)
    tri = ri >= ci
    strict = ri > ci
    same_sub = (ri // SUB) == (ci // SUB)
    tril_ones = tri.astype(F32)
    nw = nw_ref[...]

    for c in range(tt // CHUNK):
        rows = slice(c * CHUNK, (c + 1) * CHUNK)
        beta_all = beta_scr[rows, :]
        gc_all = _fdot(tril_ones, g_scr[rows, :])
        gc_all_t = gc_all.T
        g_scr[rows, :] = gc_all
        for hq0 in range(0, QK_HEADS_B, GROUP):
            units = []
            for hq in range(hq0, hq0 + GROUP):
                qn = qn_scr[rows, hq * DK:(hq + 1) * DK]
                kn = kn_scr[rows, hq * DK:(hq + 1) * DK]
                kk = _bdot_nt(kn, kn)
                qk = _bdot_nt(qn, kn)
                for hv in range(2 * hq, 2 * hq + 2):
                    beta = beta_all[:, hv:hv + 1]
                    gc = gc_all[:, V_HEADS_B + hv:V_HEADS_B + hv + 1]
                    gc_row = gc_all_t[V_HEADS_B + hv:V_HEADS_B + hv + 1, :]
                    gc_last = gc_row[:, CHUNK - 1:CHUNK]
                    decay = jnp.exp(jnp.where(tri, gc - gc_row, NEG))
                    a = jnp.where(strict, beta * kk * decay, 0.0)
                    egc = jnp.exp(gc)
                    attn_scr[hv, rows, :] = (qk * decay).astype(BF16)
                    qg_scr[hv, rows, :] = (qn * egc).astype(BF16)
                    kd = kn * jnp.exp(gc_last - gc)
                    kdt_scr[hv, c * DK:(c + 1) * DK, :] = kd.T.astype(BF16)
                    gl_scr[hv, c * 8:(c + 1) * 8, :] = jnp.broadcast_to(jnp.exp(gc_last), (8, DV))
                    units.append((hv, a))
            ds = [jnp.where(same_sub, a, 0.0) for _, a in units]
            ns = [-dd for dd in ds]
            pws = ds
            for _ in range(SUB.bit_length() - 2):
                pws = [_bdot(pw, pw) for pw in pws]
                ns = [n + pw + _bdot(n, pw) for n, pw in zip(ns, pws)]
            ls = [jnp.where(same_sub, 0.0, a) for _, a in units]
            ps = [lo + _bdot(lo, n) for lo, n in zip(ls, ns)]
            ns = [n - (p + _bdot(n, p)) for n, p in zip(ns, ps)]
            xs = []
            for hv, _ in units:
                beta = beta_scr[rows, hv:hv + 1]
                kscale = beta * jnp.exp(g_scr[rows, V_HEADS_B + hv:V_HEADS_B + hv + 1])
                xs.append(jnp.concatenate([kn_scr[rows, (hv // 2) * DK:(hv // 2 + 1) * DK] * kscale,
                                           vv_scr[rows, hv * DV:(hv + 1) * DV] * beta], axis=-1))
            wus = [x + _bdot(n, x) for n, x in zip(ns, xs)]
            for wu, (hv, _) in zip(wus, units):
                w_scr[hv, rows, :] = wu[:, :DK].astype(BF16)
                u_scr[hv, rows, :] = wu[:, DK:]

    heads = range(V_HEADS_B)
    for c in range(tt // CHUNK):
        rows = slice(c * CHUNK, (c + 1) * CHUNK)
        states = [s_scr[hv] for hv in heads]
        states_b = [s.astype(BF16) for s in states]
        v_news = [u_scr[hv, rows, :] - jnp.dot(w_scr[hv, rows, :], states_b[hv], preferred_element_type=F32)
                  for hv in heads]
        v_news_b = [v.astype(BF16) for v in v_news]
        for hv in heads:
            s_scr[hv] = (states[hv] * gl_scr[hv, c * 8:c * 8 + 1, :]
                         + jnp.dot(kdt_scr[hv, c * DK:(c + 1) * DK, :], v_news_b[hv],
                                   preferred_element_type=F32))
        for hv in heads:
            o_scr[rows, hv * DV:(hv + 1) * DV] = (
                jnp.dot(qg_scr[hv, rows, :], states_b[hv], preferred_element_type=F32)
                + jnp.dot(attn_scr[hv, rows, :], v_news_b[hv], preferred_element_type=F32))

    for hv in heads:
        o = o_scr[:, hv * DV:(hv + 1) * DV]
        z = z_ref[0, :, hv * DV:(hv + 1) * DV]
        o = o * lax.rsqrt(jnp.mean(o * o, axis=-1, keepdims=True) + EPS) * nw * _silu(z)
        o_ref[0, :, hv * DV:(hv + 1) * DV] = o.astype(o_ref.dtype)

    @pl.when(t == nt - 1)
    def _():
        s_out_ref[0] = s_scr[...]


def _dn_prompt(proj3, h3, w_gate, conv_state, s0, conv_w, alog_vec, dtb_vec, norm_w, tt):
    b, s, _ = proj3.shape
    body = functools.partial(_dn_prompt_body, tt=tt)
    nh = V_HEADS_B
    in_specs = [
        pl.BlockSpec((1, tt, WIDTH_BQK), lambda i, t: (i, t, OFF_BQ // WIDTH_BQK)),
        pl.BlockSpec((1, tt, WIDTH_BQK), lambda i, t: (i, t, OFF_BK // WIDTH_BQK)),
        pl.BlockSpec((1, tt, WIDTH_BV), lambda i, t: (i, t, OFF_BV // WIDTH_BV)),
        pl.BlockSpec((1, tt, WIDTH_BV), lambda i, t: (i, t, OFF_BZ // WIDTH_BV)),
        pl.BlockSpec((1, tt, D_MODEL), lambda i, t: (i, t, 0)),
        pl.BlockSpec((D_MODEL, 128), lambda i, t: (0, 0)),
        pl.BlockSpec((CONV_W, CONV_DIM), lambda i, t: (0, 0)),
        pl.BlockSpec((1, CONV_W - 1, CONV_DIM), lambda i, t: (i, 0, 0)),
        pl.BlockSpec((1, nh, DK, DV), lambda i, t: (i, 0, 0, 0)),
        pl.BlockSpec((1, 128), lambda i, t: (0, 0)),
        pl.BlockSpec((1, 128), lambda i, t: (0, 0)),
        pl.BlockSpec((1, DV), lambda i, t: (0, 0)),
    ]
    return pl.pallas_call(
        body,
        grid=(b, s // tt),
        in_specs=in_specs,
        out_specs=[pl.BlockSpec((1, tt, WIDTH_BV), lambda i, t: (i, t, 0)),
                   pl.BlockSpec((1, nh, DK, DV), lambda i, t: (i, 0, 0, 0))],
        out_shape=[jax.ShapeDtypeStruct((b, s, WIDTH_BV), BF16),
                   jax.ShapeDtypeStruct((b, nh, DK, DV), F32)],
        scratch_shapes=[pltpu.VMEM((nh, DK, DV), F32),
                        pltpu.VMEM((tt + 8, CONV_DIM), F32),
                        pltpu.VMEM((tt, WIDTH_BQK), F32),
                        pltpu.VMEM((tt, WIDTH_BQK), F32),
                        pltpu.VMEM((tt, WIDTH_BV), F32),
                        pltpu.VMEM((tt, 128), F32),
                        pltpu.VMEM((tt, 128), F32),
                        pltpu.VMEM((nh, tt, DK), BF16),
                        pltpu.VMEM((nh, tt, DV), F32),
                        pltpu.VMEM((nh, tt, DK), BF16),
                        pltpu.VMEM((nh, tt // CHUNK * DK, CHUNK), BF16),
                        pltpu.VMEM((nh, tt, CHUNK), BF16),
                        pltpu.VMEM((nh, tt // CHUNK * 8, DV), F32),
                        pltpu.VMEM((tt, WIDTH_BV), F32)],
        compiler_params=_params(("arbitrary", "arbitrary"), 48),
        name="deltanet_prompt",
    )(proj3, proj3, proj3, proj3, h3, w_gate, conv_w, conv_state, s0, alog_vec, dtb_vec, norm_w)


def _dn_sample_body(proj_ref, gates_ref, cw_ref, cs_ref, s0_ref, alog_ref, dtb_ref, nw_ref,
                    o_ref, cs_out_ref, s_out_ref):
    pre = proj_ref[0, :, OFF_BQ:OFF_BQ + CONV_DIM]
    buf = cs_ref[0]
    w = cw_ref[...]
    y = w[CONV_W - 1:CONV_W, :] * pre
    for i in range(CONV_W - 1):
        y = y + w[i:i + 1, :] * buf[i:i + 1, :]
    y = _silu(y)
    cs_out_ref[0, 0:CONV_W - 2, :] = buf[1:CONV_W - 1, :]
    cs_out_ref[0, CONV_W - 2:CONV_W - 1, :] = pre

    gates = gates_ref[0]
    beta_all = _sigmoid(gates)
    g_all = -jnp.exp(alog_ref[...]) * _softplus(gates + dtb_ref[...])
    nw = nw_ref[...]

    def l2n(x):
        return x * lax.rsqrt(jnp.sum(x * x, axis=-1, keepdims=True) + EPS)

    row8 = lax.broadcasted_iota(jnp.int32, (8, DK), 0) == 0
    for hv in range(V_HEADS_B):
        hq = hv // 2
        q = l2n(y[:, hq * DK:(hq + 1) * DK]) * QK_SCALE
        k = l2n(y[:, WIDTH_BQK + hq * DK:WIDTH_BQK + (hq + 1) * DK])
        v = y[:, 2 * WIDTH_BQK + hv * DV:2 * WIDTH_BQK + (hv + 1) * DV]
        beta = beta_all[:, hv:hv + 1]
        g = g_all[:, V_HEADS_B + hv:V_HEADS_B + hv + 1]
        eg = jnp.exp(g)
        state = s0_ref[0, hv]

        def pad8(x):
            return jnp.where(row8, jnp.broadcast_to(x, (8, x.shape[-1])), 0.0)

        v_new = v * beta - _bdot(pad8(k * (beta * eg)), state)[0:1, :]
        qk = jnp.sum(q.astype(BF16).astype(F32) * k.astype(BF16).astype(F32), axis=-1, keepdims=True)
        o = _bdot(pad8(q * eg), state)[0:1, :] + qk.astype(BF16).astype(F32) * v_new.astype(BF16).astype(F32)
        s_out_ref[0, hv] = state * eg + _bdot_tn(pad8(k), pad8(v_new))
        z = proj_ref[0, :, OFF_BZ + hv * DV:OFF_BZ + (hv + 1) * DV]
        o = o * lax.rsqrt(jnp.mean(o * o, axis=-1, keepdims=True) + EPS) * nw * _silu(z)
        o_ref[0, :, hv * DV:(hv + 1) * DV] = o.astype(o_ref.dtype)


def _dn_sample(proj, gates, conv_state, s0, conv_w, alog_vec, dtb_vec, norm_w):
    b = proj.shape[0]
    return pl.pallas_call(
        _dn_sample_body,
        grid=(b,),
        in_specs=[pl.BlockSpec((1, 1, PROJ_MAIN), lambda i: (i, 0, 0)),
                  pl.BlockSpec((1, 1, 128), lambda i: (i, 0, 0)),
                  pl.BlockSpec((CONV_W, CONV_DIM), lambda i: (0, 0)),
                  pl.BlockSpec((1, CONV_W - 1, CONV_DIM), lambda i: (i, 0, 0)),
                  pl.BlockSpec((1, V_HEADS_B, DK, DV), lambda i: (i, 0, 0, 0)),
                  pl.BlockSpec((1, 128), lambda i: (0, 0)),
                  pl.BlockSpec((1, 128), lambda i: (0, 0)),
                  pl.BlockSpec((1, DV), lambda i: (0, 0))],
        out_specs=[pl.BlockSpec((1, 1, WIDTH_BV), lambda i: (i, 0, 0)),
                   pl.BlockSpec((1, CONV_W - 1, CONV_DIM), lambda i: (i, 0, 0)),
                   pl.BlockSpec((1, V_HEADS_B, DK, DV), lambda i: (i, 0, 0, 0))],
        out_shape=[jax.ShapeDtypeStruct((b, 1, WIDTH_BV), BF16),
                   jax.ShapeDtypeStruct((b, CONV_W - 1, CONV_DIM), F32),
                   jax.ShapeDtypeStruct((b, V_HEADS_B, DK, DV), F32)],
        compiler_params=_params(("arbitrary",), 40),
        name="deltanet_sample",
    )(proj, gates, conv_w, conv_state, s0, alog_vec, dtb_vec, norm_w)


def _outproj_body(att_ref, dn_ref, wa_ref, wb_ref, x_ref, lnpost_ref, lnpre_ref, x1_ref, h2_ref):
    tm = x_ref.shape[0]
    piece = min(tm, OUTPROJ_ROWS)
    for rows in [slice(r, r + piece) for r in range(0, tm, piece)]:
        mix = (jnp.dot(att_ref[rows, :], wa_ref[...], preferred_element_type=F32)
               + jnp.dot(dn_ref[rows, :], wb_ref[...], preferred_element_type=F32))
        x1 = x_ref[rows, :] + _rms(mix, lnpost_ref[...])
        x1_ref[rows, :] = x1
        h2_ref[rows, :] = _rms(x1, lnpre_ref[...]).astype(h2_ref.dtype)


def _outproj(att, dn, w, x, ln_post, ln_pre, tm):
    m, d = x.shape
    assert WIDTH_A == WIDTH_BV
    return pl.pallas_call(
        _outproj_body,
        grid=(m // tm,),
        in_specs=[pl.BlockSpec((tm, WIDTH_A), lambda i: (i, 0)),
                  pl.BlockSpec((tm, WIDTH_BV), lambda i: (i, 0)),
                  pl.BlockSpec((WIDTH_A, d), lambda i: (0, 0)),
                  pl.BlockSpec((WIDTH_BV, d), lambda i: (1, 0)),
                  pl.BlockSpec((tm, d), lambda i: (i, 0)),
                  pl.BlockSpec((1, d), lambda i: (0, 0)),
                  pl.BlockSpec((1, d), lambda i: (0, 0))],
        out_specs=[pl.BlockSpec((tm, d), lambda i: (i, 0)),
                   pl.BlockSpec((tm, d), lambda i: (i, 0))],
        out_shape=[jax.ShapeDtypeStruct((m, d), F32),
                   jax.ShapeDtypeStruct((m, d), BF16)],
        compiler_params=_params(("arbitrary",), 48),
        name="outproj",
    )(att, dn, w, w, x, ln_post, ln_pre)


def _ffn_body(*refs, tm, tiles_per_seq, single_token):
    if single_token:
        (h_ref, wg_ref, wv_ref, cwb_ref, wo_prev_ref, wo_last_ref, x1_hbm, ln_ref, pg_ref, pv_ref,
         o_ref, ng_ref, nv_ref, eg_scr, ev_scr, carry_scr, act_scr, x1_scr, x1_sem) = refs
    else:
        (h_ref, wg_ref, wv_ref, cwb_ref, wo_prev_ref, wo_last_ref, x1_hbm, ln_ref, prev_ref,
         o_ref, new_ref, eg_scr, ev_scr, carry_scr, act_scr, x1_scr, x1_sem) = refs
    i = pl.program_id(0)
    j = pl.program_id(1)
    nj = pl.num_programs(1)
    d = o_ref.shape[-1]
    tf = act_scr.shape[-1]
    cur = j % 2
    act_cur = act_scr.at[cur]
    act_prev = act_scr.at[1 - cur]

    def x1_copy():
        return pltpu.make_async_copy(x1_hbm.at[pl.ds(pl.multiple_of(i * tm, tm), tm), :], x1_scr, x1_sem)

    def down_proj(act_ref, wo_ref):
        for n in range(0, d, FFN_COLS):
            o_ref[:, n:n + FFN_COLS] += jnp.dot(act_ref[...], wo_ref[:, n:n + FFN_COLS],
                                                preferred_element_type=F32)

    @pl.when(j == 0)
    def _():
        x1_copy().start()
        o_ref[...] = jnp.zeros_like(o_ref)
        act_prev[...] = jnp.zeros_like(act_prev)

    if single_token:
        def up_conv(w_ref, tile, prev_ref, new_ref):
            up = jnp.dot(h_ref[...], w_ref[...], preferred_element_type=F32)
            cw = cwb_ref[tile]
            new_ref[...] = up
            return cw[0:1, :] * prev_ref[0] + cw[1:2, :] * prev_ref[1] + cw[2:3, :] * up + cw[3:4, :]

        gate = up_conv(wg_ref, j, pg_ref, ng_ref)
        val = up_conv(wv_ref, nj + j, pv_ref, nv_ref)
        act_cur[...] = (_gelu_tanh(gate) * val).astype(BF16)
        down_proj(act_prev, wo_prev_ref)
    else:
        first_tile = i % tiles_per_seq == 0
        pieces = [slice(c, c + FFN_PIECE) for c in range(0, tf, FFN_PIECE)]

        def up_proj(cols):
            for w_ref, tile, e_scr, slot in ((wg_ref, j, eg_scr, 0), (wv_ref, nj + j, ev_scr, 1)):
                e_scr[8:8 + tm, cols] = jnp.dot(h_ref[...], w_ref[:, cols], preferred_element_type=F32)
                e_scr[6:8, cols] = jnp.where(first_tile, prev_ref[0, tile, :, cols],
                                             carry_scr[slot, j, 6:8, cols])
                tail = e_scr[tm + 6:tm + 8, cols]
                carry_scr[slot, j, 6:8, cols] = tail
                new_ref[0, tile, :, cols] = tail

        def conv(e_scr, tile, cols, r):
            cw = cwb_ref[tile, :, cols]
            return (cw[0:1, :] * e_scr[6 + r:6 + r + FFN_ROWS, cols]
                    + cw[1:2, :] * e_scr[7 + r:7 + r + FFN_ROWS, cols]
                    + cw[2:3, :] * e_scr[8 + r:8 + r + FFN_ROWS, cols] + cw[3:4, :])

        def conv_geglu(cols):
            for r in range(0, tm, FFN_ROWS):
                act_cur[r:r + FFN_ROWS, cols] = (_gelu_tanh(conv(eg_scr, j, cols, r))
                                                 * conv(ev_scr, nj + j, cols, r)).astype(BF16)

        up_proj(pieces[0])
        for c in range(1, len(pieces)):
            up_proj(pieces[c])
            conv_geglu(pieces[c - 1])
        down_proj(act_prev, wo_prev_ref)
        conv_geglu(pieces[-1])

    @pl.when(j == nj - 1)
    def _():
        x1_copy().wait()
        piece = min(tm, OUTPROJ_ROWS)
        for r in range(0, tm, piece):
            rows = slice(r, r + piece)
            f = o_ref[rows, :] + jnp.dot(act_cur[rows, :], wo_last_ref[...], preferred_element_type=F32)
            o_ref[rows, :] = x1_scr[rows, :] + _rms(f, ln_ref[...])


def _ffn(h2, w_in, conv_w, conv_b, w_out, x1, ln_post, prev, tm, tf, seq_len):
    m, d = h2.shape
    single = seq_len == 1
    nj = D_FF // tf
    tiles_per_seq = 1 if single else seq_len // tm
    cwb = jnp.concatenate([conv_w, conv_b], axis=0).reshape(FFN_CONV_W + 1, 2 * nj, tf).transpose(1, 0, 2)
    cwb_spec = pl.BlockSpec((2 * nj, FFN_CONV_W + 1, tf), lambda i, j: (0, 0, 0))
    if single:
        prev_args = (prev, prev)
        prev_specs = [pl.BlockSpec((2, tm, tf), lambda i, j: (0, i, j)),
                      pl.BlockSpec((2, tm, tf), lambda i, j: (0, i, nj + j))]
        new_specs = [pl.BlockSpec((tm, tf), lambda i, j: (i, j))] * 2
        new_shapes = [jax.ShapeDtypeStruct((m, D_FF), F32)] * 2
    else:
        prev_args = (prev.reshape(-1, 2, 2 * nj, tf).transpose(0, 2, 1, 3),)
        prev_specs = [pl.BlockSpec((1, 2 * nj, 2, tf), lambda i, j: (i // tiles_per_seq, 0, 0, 0))]
        new_specs = [pl.BlockSpec((1, 2 * nj, 2, tf), lambda i, j: (i, 0, 0, 0))]
        new_shapes = [jax.ShapeDtypeStruct((m // tm, 2 * nj, 2, tf), F32)]
    body = functools.partial(_ffn_body, tm=tm, tiles_per_seq=tiles_per_seq, single_token=single)
    once = dict(pipeline_mode=pl.Buffered(1)) if tm >= 1024 else {}
    outs = pl.pallas_call(
        body,
        grid=(m // tm, nj),
        in_specs=[pl.BlockSpec((tm, d), lambda i, j: (i, 0), **once),
                  pl.BlockSpec((d, tf), lambda i, j: (0, j)),
                  pl.BlockSpec((d, tf), lambda i, j: (0, nj + j)),
                  cwb_spec,
                  pl.BlockSpec((tf, d), lambda i, j: (jnp.maximum(j - 1, 0), 0)),
                  pl.BlockSpec((tf, d), lambda i, j: (nj - 1, 0), pipeline_mode=pl.Buffered(1)),
                  pl.BlockSpec(memory_space=pl.ANY),
                  pl.BlockSpec((1, d), lambda i, j: (0, 0))] + prev_specs,
        out_specs=[pl.BlockSpec((tm, d), lambda i, j: (i, 0), **once)] + new_specs,
        out_shape=[jax.ShapeDtypeStruct((m, d), F32)] + new_shapes,
        scratch_shapes=[pltpu.VMEM((tm + 8, tf), F32),
                        pltpu.VMEM((tm + 8, tf), F32),
                        pltpu.VMEM((2, nj, 8, tf), F32),
                        pltpu.VMEM((2, tm, tf), BF16),
                        pltpu.VMEM((tm, d), F32),
                        pltpu.SemaphoreType.DMA(())],
        compiler_params=_params(("arbitrary", "arbitrary"), 57),
        name="convffn",
    )(h2, w_in, w_in, cwb, w_out, w_out, x1, ln_post, *prev_args)
    if single:
        y, new_g, new_v = outs
        return y, jnp.concatenate([new_g, new_v], axis=-1)
    y, new = outs
    return y, new.transpose(0, 2, 1, 3).reshape(m // tm, 2, 2 * D_FF)


def _cache_shift_body(ck_ref, cv_ref, ck_next_ref, cv_next_ref, nk_ref, nv_ref, ok_ref, ov_ref, *, tr):
    last = pl.program_id(1) == pl.num_programs(1) - 1
    for c_ref, nxt_ref, n_ref, o_ref in ((ck_ref, ck_next_ref, nk_ref, ok_ref),
                                         (cv_ref, cv_next_ref, nv_ref, ov_ref)):
        o_ref[0, 0:tr - 1] = c_ref[0, 1:tr]
        o_ref[0, tr - 1] = jnp.where(last, n_ref[0, 0], nxt_ref[0, 0])


def _cache_shift(cache_k, cache_v, new_k, new_v, tr):
    nb, rows, nh, dh = cache_k.shape
    main = pl.BlockSpec((1, tr, nh, dh), lambda b, i: (b, i, 0, 0))
    nxt = pl.BlockSpec((1, 1, nh, dh), lambda b, i: (b, jnp.minimum((i + 1) * tr, rows - 1), 0, 0))
    new = pl.BlockSpec((1, 1, nh, dh), lambda b, i: (b, 0, 0, 0))
    shape = jax.ShapeDtypeStruct(cache_k.shape, cache_k.dtype)
    return pl.pallas_call(
        functools.partial(_cache_shift_body, tr=tr),
        grid=(nb, rows // tr),
        in_specs=[main, main, nxt, nxt, new, new],
        out_specs=[main, main],
        out_shape=[shape, shape],
        compiler_params=_params(("arbitrary", "arbitrary"), 40),
        name="cache_shift",
    )(cache_k, cache_v, cache_k, cache_v, new_k, new_v)


def _lane_vec(values, offset):
    return jnp.zeros((1, 128), F32).at[0, offset:offset + V_HEADS_B].set(values.astype(F32))


def kernel(x_prompt, x_sample, cache_win_k, cache_win_v, state_dn_conv, state_dn_rec, state_ffn_conv,
           rel_bias, ln_mix_pre, w_in, dn_conv_w, dn_A_log, dn_dt_bias, dn_norm_w, w_out, ln_mix_post,
           ln_ffn_pre, w_ffn_in, ffn_conv_w, ffn_conv_b, w_ffn_out, ln_ffn_post):
    bp, sp, d = x_prompt.shape
    bs = x_sample.shape[0]
    l = 0

    w_main = w_in[l, :, :PROJ_MAIN].reshape(d, PROJ_MAIN // INPROJ_TN, INPROJ_TN).transpose(1, 0, 2).astype(BF16)
    w_gate = jnp.pad(w_in[l, :, PROJ_MAIN:], ((0, 0), (0, 128 - 2 * V_HEADS_B))).astype(BF16)
    wo = w_out[l].astype(BF16)
    wf_in = w_ffn_in[l].astype(BF16)
    wf_out = w_ffn_out[l].astype(BF16)
    ln1 = ln_mix_pre[l][None, :]
    ln2 = ln_mix_post[l][None, :]
    ln3 = ln_ffn_pre[l][None, :]
    ln4 = ln_ffn_post[l][None, :]
    conv_w = dn_conv_w[l]
    alog_vec = _lane_vec(dn_A_log[l], V_HEADS_B)
    dtb_vec = _lane_vec(dn_dt_bias[l], V_HEADS_B)
    norm_w = dn_norm_w[l][None, :]
    fcw = ffn_conv_w[l]
    fcb = ffn_conv_b[l][None, :]

    xp = x_prompt.reshape(bp * sp, d)
    proj_p, hp = _norm_matmul(xp, ln1, w_main, 1024, "inproj_prompt")
    proj3 = proj_p.reshape(bp, sp, PROJ_MAIN)
    keep = min(MAX_DISTANCE, sp)
    att_p, win_k, win_v = _attn_prompt(proj3, rel_bias, keep)
    dn_p, p_dn_rec = _dn_prompt(
        proj3, hp.reshape(bp, sp, d), w_gate,
        jnp.zeros((bp, CONV_W - 1, CONV_DIM), F32), jnp.zeros((bp, V_HEADS_B, DK, DV), F32),
        conv_w, alog_vec, dtb_vec, norm_w, 256)
    x1_p, h2_p = _outproj(att_p.reshape(bp * sp, WIDTH_A), dn_p.reshape(bp * sp, WIDTH_BV),
                          wo, xp, ln2, ln3, 512)
    y_p, fc = _ffn(h2_p, wf_in, fcw, fcb, wf_out, x1_p, ln4,
                   jnp.zeros((bp, FFN_CONV_W - 1, 2 * D_FF), F32), FFN_TM, FFN_TF, sp)
    p_win_k = win_k.reshape(1, bp, keep, HEADS_A, HEAD_DIM)
    p_win_v = win_v.reshape(1, bp, keep, HEADS_A, HEAD_DIM)
    p_dn_conv = proj3[:, sp - (CONV_W - 1):, OFF_BQ:OFF_BQ + CONV_DIM][None]
    tiles = sp // FFN_TM
    p_ffn_conv = fc[tiles - 1::tiles][None]

    xs = x_sample.reshape(bs, d)
    proj_s, hs = _norm_matmul(xs, ln1, w_main, bs, "inproj_sample")
    gates_s = _matmul(hs, w_gate, bs, 128, "gates_sample")
    past = cache_win_k.shape[2]
    ck = cache_win_k[l]
    cv = cache_win_v[l]
    new_k = proj_s[:, OFF_AK:OFF_AK + WIDTH_A]
    new_v = proj_s[:, OFF_AV:OFF_AV + WIDTH_A]
    new_q = proj_s[:, OFF_AQ:OFF_AQ + WIDTH_A].reshape(bs, HEADS_A, HEAD_DIM)
    new_k = new_k.reshape(bs, HEADS_A, HEAD_DIM)
    new_v = new_v.reshape(bs, HEADS_A, HEAD_DIM)
    att_s = _attn_sample(new_q, new_k, new_v, ck, cv, rel_bias)
    s_win_k, s_win_v = _cache_shift(ck, cv, new_k[:, None], new_v[:, None], 1024)
    dn_s, s_dn_conv, s_dn_rec = _dn_sample(proj_s[:, None], gates_s[:, None], state_dn_conv[l],
                                           state_dn_rec[l], conv_w, alog_vec, dtb_vec, norm_w)
    x1_s, h2_s = _outproj(att_s.reshape(bs, WIDTH_A), dn_s.reshape(bs, WIDTH_BV),
                          wo, xs, ln2, ln3, bs)
    prev_s = jnp.swapaxes(state_ffn_conv[l], 0, 1)
    y_s, up_s = _ffn(h2_s, wf_in, fcw, fcb, wf_out, x1_s, ln4, prev_s, bs, FFN_TF, 1)
    s_ffn_conv = jnp.stack([prev_s[1], up_s], axis=1)[None]

    return (y_p.reshape(bp, sp, d), y_s.reshape(bs, 1, d),
            p_win_k, p_win_v, p_dn_conv, p_dn_rec[None], p_ffn_conv,
            s_win_k[None], s_win_v[None], s_dn_conv[None], s_dn_rec[None], s_ffn_conv)
```

```python
import functools
import math

import numpy as np
import jax
import jax.numpy as jnp
from jax import lax
from jax.experimental import pallas as pl
from jax.experimental.pallas import tpu as pltpu

F32 = jnp.float32
BF16 = jnp.bfloat16

D_MODEL = 2048
HEAD_DIM = 128
WIDTH_A = 1024
HEADS_A = 8
DILATIONS = (1, 4, 16)
BLK = 128
N_BUCKETS = 32
MAX_DISTANCE = 2048
DK = 128
DV = 128
V_HEADS_B = 8
QK_HEADS_B = 4
WIDTH_BQK = 512
WIDTH_BV = 1024
CONV_W = 4
CONV_DIM = 2048
CHUNK = 128
SUB = 64
D_FF = 5632
FFN_CONV_W = 3
EPS = 1e-6
NEG = -1e30
ATT_SCALE = HEAD_DIM ** -0.5
QK_SCALE = DK ** -0.5

OFF_AQ, OFF_AK, OFF_AV = 0, 1024, 2048
OFF_BQ, OFF_BK, OFF_BV, OFF_BZ = 3072, 3584, 4096, 5120
OFF_GATES = 6144
PROJ_MAIN = 6144

MIB = 2 ** 20
VMEM_SMALL = 40
VMEM_MID = 48
VMEM_FFN = 57


def _params(semantics, vmem_mib):
    return pltpu.CompilerParams(dimension_semantics=semantics, vmem_limit_bytes=vmem_mib * MIB)


def _bdot(a, b):
    return jnp.dot(a.astype(BF16), b.astype(BF16), preferred_element_type=F32)


def _bdot_nt(a, b):
    return lax.dot_general(a.astype(BF16), b.astype(BF16), (((1,), (1,)), ((), ())),
                           preferred_element_type=F32)


def _bdot_tn(a, b):
    return lax.dot_general(a.astype(BF16), b.astype(BF16), (((0,), (0,)), ((), ())),
                           preferred_element_type=F32)


def _fdot(a, b):
    return jnp.dot(a, b, preferred_element_type=F32, precision=lax.Precision.HIGHEST)


def _silu(x):
    return x * (1.0 / (1.0 + jnp.exp(-x)))


def _sigmoid(x):
    return 1.0 / (1.0 + jnp.exp(-x))


def _softplus(x):
    return jnp.maximum(x, 0.0) + jnp.log(1.0 + jnp.exp(-jnp.abs(x)))


def _gelu_tanh(x):
    c = math.sqrt(2.0 / math.pi)
    half = 0.5 * x
    return half + half * jnp.tanh(x * (c + (c * 0.044715) * (x * x)))


def _rms(x, w):
    return x * lax.rsqrt(jnp.mean(x * x, axis=-1, keepdims=True) + EPS) * w


def _matmul_body(x_ref, w_ref, o_ref):
    o_ref[...] = jnp.dot(x_ref[...], w_ref[...], preferred_element_type=F32)


def _matmul(x, w, tm, tn, name):
    m, k = x.shape
    n = w.shape[1]
    return pl.pallas_call(
        _matmul_body,
        grid=(n // tn, m // tm),
        in_specs=[pl.BlockSpec((tm, k), lambda j, i: (i, 0)),
                  pl.BlockSpec((k, tn), lambda j, i: (0, j))],
        out_specs=pl.BlockSpec((tm, tn), lambda j, i: (i, j)),
        out_shape=jax.ShapeDtypeStruct((m, n), F32),
        compiler_params=_params(("arbitrary", "arbitrary"), VMEM_MID),
        name=name,
    )(x, w)


def _norm_matmul_body(x_ref, ln_ref, w_ref, o_ref, h_ref):
    tm = x_ref.shape[0]
    piece = min(tm, NORM_ROWS)
    for r in range(0, tm, piece):
        rows = slice(r, r + piece)
        h = _rms(x_ref[rows, :], ln_ref[...]).astype(BF16)
        h_ref[rows, :] = h
        o_ref[rows, :] = jnp.dot(h, w_ref[...], preferred_element_type=F32)


def _norm_matmul(x, ln, w, n, tm, tn, name):
    m, k = x.shape
    return pl.pallas_call(
        _norm_matmul_body,
        grid=(m // tm, n // tn),
        in_specs=[pl.BlockSpec((tm, k), lambda i, j: (i, 0)),
                  pl.BlockSpec((1, k), lambda i, j: (0, 0)),
                  pl.BlockSpec((k, tn), lambda i, j: (0, j))],
        out_specs=[pl.BlockSpec((tm, tn), lambda i, j: (i, j)),
                   pl.BlockSpec((tm, k), lambda i, j: (i, 0))],
        out_shape=[jax.ShapeDtypeStruct((m, n), F32), jax.ShapeDtypeStruct((m, k), BF16)],
        compiler_params=_params(("arbitrary", "arbitrary"), VMEM_MID),
        name=name,
    )(x, ln, w)


def _rel_bucket_np(dist):
    dist = np.asarray(dist, np.int64)
    max_exact = N_BUCKETS // 2
    d = np.maximum(dist, 1).astype(np.float64)
    val = np.log(d / max_exact) / math.log(MAX_DISTANCE / max_exact) * (N_BUCKETS - max_exact)
    frac = np.abs(val - np.round(val))
    near = (frac < 2e-5) &(dist >= max_exact) & (dist != max_exact) & (dist < MAX_DISTANCE)
    assert not near.any(), "distance on a bucket boundary"
    val = np.where(dist == max_exact, 0.0, val)
    large = np.minimum(max_exact + np.trunc(val).astype(np.int64), N_BUCKETS - 1)
    return np.where(dist < max_exact, dist, large).astype(np.int32)


def _prompt_bucket_tables():
    qi = np.arange(BLK)[:, None]
    kj = np.arange(2 * BLK)[None, :]
    delta = BLK + qi - kj
    inwin = (delta >= 0) & (delta <= BLK)
    tabs = []
    for dil in DILATIONS:
        b = _rel_bucket_np(np.clip(delta, 0, BLK) * dil)
        tabs.append(np.where(inwin, b, -1))
    return np.stack(tabs).astype(np.int32)


def _sample_bucket_tables():
    j = BLK - np.arange(BLK)
    return np.stack([_rel_bucket_np(j * dil)[None, :] for dil in DILATIONS]).astype(np.int32)


def _attn_prompt_body(bucket_ref, relb_ref, q_ref, k_ref, v_ref, o_ref, wk_ref, wv_ref,
                      bias_scr, acc_scr, m_scr, l_scr):
    h = pl.program_id(1)
    s, keep = k_ref.shape[1], wk_ref.shape[1] // HEADS_A
    wk_ref[0, pl.ds(h, keep, stride=HEADS_A), :] = k_ref[0, s - keep:s, :]
    wv_ref[0, pl.ds(h, keep, stride=HEADS_A), :] = v_ref[0, s - keep:s, :]
    col = lax.broadcasted_iota(jnp.int32, (BLK, 2 * BLK), 1)
    tables = _prompt_bucket_tables()
    for br in range(3):
        bk = bucket_ref[br]
        bias = jnp.zeros((BLK, 2 * BLK), F32)
        for kb in sorted(set(tables[br].ravel().tolist()) - {-1}):
            bias = jnp.where(bk == kb, relb_ref[kb, h], bias)
        full = jnp.where(bk >= 0, bias, NEG)
        bias_scr[2 * br] = full
        bias_scr[2 * br + 1] = jnp.where(col >= BLK, full, NEG)

    def run_branch(br, dil, is_first_branch, is_last_branch):
        shift = int(math.log2(dil))
        span = BLK * dil
        stride = None if dil == 1 else dil

        def rows(start):
            return pl.ds(start, BLK, stride=stride) if stride else pl.ds(start, BLK)

        nb = q_ref.shape[1] // span
        run_len = min(nb, ATTN_UNROLL)
        runs_per_it = ATTN_UNROLL // run_len
        runs_per_res = nb // run_len
        starts_at_zero = runs_per_res == 1

        def tasks(it, carry):
            q_starts, firsts, qs_, ks_, vs_ = [], [], [], [], []
            for rr in range(runs_per_it):
                ri = it * runs_per_it + rr
                n0 = (ri % runs_per_res) * run_len
                if dil == 1:
                    base = pl.multiple_of(n0 * span, BLK)
                else:
                    base = n0 * span + ri // runs_per_res
                first = jnp.where(n0 == 0, 1, 0)
                starts = [base + u * span for u in range(run_len)]
                kb = [k_ref[0, rows(st), :].astype(BF16) for st in starts]
                vb = [v_ref[0, rows(st), :].astype(BF16) for st in starts]
                if starts_at_zero:
                    k_prev, v_prev = None, None
                else:
                    p_start = base - span * (1 - first)
                    if dil == 1:
                        p_start = pl.multiple_of(p_start, BLK)
                    k_prev = k_ref[0, rows(p_start), :].astype(BF16)
                    v_prev = v_ref[0, rows(p_start), :].astype(BF16)
                for u, st in enumerate(starts):
                    q_starts.append(st)
                    qs_.append(q_ref[0, rows(st), :].astype(BF16))
                    kp, vp = (k_prev, v_prev) if u == 0 else (kb[u - 1], vb[u - 1])
                    if kp is None:
                        firsts.append(None)
                        ks_.append(kb[u])
                        vs_.append(vb[u])
                    else:
                        firsts.append(first if u == 0 else 0)
                        ks_.append(jnp.concatenate([kp, kb[u]], axis=0))
                        vs_.append(jnp.concatenate([vp, vb[u]], axis=0))
            if not is_first_branch:
                runs = [(m_scr[rows(qs), :], l_scr[rows(qs), :], acc_scr[rows(qs), :]) for qs in q_starts]
            ss = [_bdot_nt(q, k) * ATT_SCALE
                  + (bias_scr[2 * br, :, BLK:] if f is None else bias_scr[2 * br + f])
                  for q, k, f in zip(qs_, ks_, firsts)]
            ms = [jnp.max(s, axis=-1, keepdims=True) for s in ss]
            ps_ = [jnp.exp(s - m) for s, m in zip(ss, ms)]
            accs = [_bdot(p, jnp.concatenate([v, jnp.ones_like(v)], axis=1)) for p, v in zip(ps_, vs_)]
            outs = []
            for u in range(ATTN_UNROLL):
                m_b = jnp.broadcast_to(ms[u], (BLK, HEAD_DIM))
                l_b = accs[u][:, HEAD_DIM:]
                acc_t = accs[u][:, :HEAD_DIM]
                if not is_first_branch:
                    m_run, l_run, acc_run = runs[u]
                    m_new = jnp.maximum(m_run, m_b)
                    a = jnp.exp(m_run - m_new)
                    b = jnp.exp(m_b - m_new)
                    acc_t = a * acc_run + b * acc_t
                    l_b = a * l_run + b * l_b
                    m_b = m_new
                outs.append((m_b, l_b, acc_t))
            for qs, (m_b, l_b, acc_t) in zip(q_starts, outs):
                if is_last_branch:
                    o_ref[0, rows(qs), :] = (acc_t / l_b).astype(o_ref.dtype)
                else:
                    m_scr[rows(qs), :] = m_b
                    l_scr[rows(qs), :] = l_b
                    acc_scr[rows(qs), :] = acc_t
            return carry

        lax.fori_loop(0, nb * dil // ATTN_UNROLL, tasks, 0)

    run_branch(2, 16, True, False)
    run_branch(1, 4, False, False)
    run_branch(0, 1, False, True)


def _attn_prompt(proj3, rel_bias, keep):
    b, s, _ = proj3.shape
    buckets = jnp.asarray(_prompt_bucket_tables())
    blk = (1, s, HEAD_DIM)
    win = pl.BlockSpec((1, keep * HEADS_A, HEAD_DIM), lambda i, h: (i, 0, 0), pipeline_mode=pl.Buffered(1))
    win_shape = jax.ShapeDtypeStruct((b, keep * HEADS_A, HEAD_DIM), F32)
    return pl.pallas_call(
        _attn_prompt_body,
        grid=(b, HEADS_A),
        in_specs=[pl.BlockSpec((3, BLK, 2 * BLK), lambda i, h: (0, 0, 0)),
                  pl.BlockSpec(memory_space=pltpu.SMEM),
                  pl.BlockSpec(blk, lambda i, h: (i, 0, OFF_AQ // HEAD_DIM + h)),
                  pl.BlockSpec(blk, lambda i, h: (i, 0, OFF_AK // HEAD_DIM + h)),
                  pl.BlockSpec(blk, lambda i, h: (i, 0, OFF_AV // HEAD_DIM + h))],
        out_specs=[pl.BlockSpec(blk, lambda i, h: (i, 0, h)), win, win],
        out_shape=[jax.ShapeDtypeStruct((b, s, WIDTH_A), BF16), win_shape, win_shape],
        scratch_shapes=[pltpu.VMEM((6, BLK, 2 * BLK), F32),
                        pltpu.VMEM((s, HEAD_DIM), F32),
                        pltpu.VMEM((s, HEAD_DIM), F32),
                        pltpu.VMEM((s, HEAD_DIM), F32)],
        compiler_params=_params(("arbitrary", "arbitrary"), VMEM_MID),
        name="attn_prompt",
    )(buckets, rel_bias, proj3, proj3, proj3)


def _attn_sample_body(bucket_ref, relbt_ref, q_ref, kn_ref, vn_ref,
                      k1_ref, k4_ref, k16_ref, v1_ref, v4_ref, v16_ref, o_ref, bias_scr):
    relbt = relbt_ref[...]
    tile = (HEADS_A, HEAD_DIM)

    @pl.when(pl.program_id(0) == 0)
    def _():
        for br in range(3):
            bk = bucket_ref[br]
            bias = jnp.zeros((BLK,) + tile, F32)
            for kb in range(N_BUCKETS):
                col = jnp.broadcast_to(relbt[:, kb:kb + 1], tile)
                bias = jnp.where(bk == kb, col[None], bias)
            bias_scr[br] = bias

    def lane_sum(x):
        return jnp.broadcast_to(jnp.sum(x, axis=-1, keepdims=True), x.shape)

    q = q_ref[0]
    s_self = lane_sum(q * kn_ref[0]) * ATT_SCALE + jnp.broadcast_to(relbt[:, 0:1], tile)
    scores = []
    m = s_self
    for br, k_ref in enumerate((k1_ref, k4_ref, k16_ref)):
        s = lane_sum(k_ref[...] * q[None]) * ATT_SCALE + bias_scr[br]
        scores.append(s)
        m = jnp.maximum(m, jnp.max(s, axis=0))
    p_self = 3.0 * jnp.exp(s_self - m)
    l = p_self
    acc = p_self * vn_ref[0]
    for s, v_ref in zip(scores, (v1_ref, v4_ref, v16_ref)):
        p = jnp.exp(s - m[None])
        l = l + jnp.sum(p, axis=0)
        acc = acc + jnp.sum(p * v_ref[...], axis=0)
    o_ref[0] = (acc / l).astype(o_ref.dtype)


def _attn_sample(q, k_new, v_new, cache_k, cache_v, rel_bias):
    b, past = cache_k.shape[:2]
    tile = (HEADS_A, HEAD_DIM)
    buckets = jnp.asarray(np.broadcast_to(_sample_bucket_tables().reshape(3, BLK, 1, 1), (3, BLK) + tile))
    row = pl.BlockSpec((1,) + tile, lambda i: (i, 0, 0))
    views, specs = [], []
    for cache in (cache_k, cache_v):
        for dil in DILATIONS:
            views.append(cache.reshape((b, past // dil, dil) + tile))
            last = past // dil // BLK - 1
            specs.append(pl.BlockSpec((None, BLK, None) + tile,
                                      functools.partial(lambda last, i: (i, last, 0, 0, 0), last)))
    return pl.pallas_call(
        _attn_sample_body,
        grid=(b,),
        in_specs=[pl.BlockSpec((3, BLK) + tile, lambda i: (0, 0, 0, 0)),
                  pl.BlockSpec((HEADS_A, N_BUCKETS), lambda i: (0, 0)),
                  row, row, row] + specs,
        out_specs=row,
        out_shape=jax.ShapeDtypeStruct((b,) + tile, BF16),
        scratch_shapes=[pltpu.VMEM((3, BLK) + tile, F32)],
        compiler_params=_params(("arbitrary",), VMEM_SMALL),
        name="attn_sample",
    )(buckets, rel_bias.T, q, k_new, v_new, *views)


INPROJ_TM = 1024
INPROJ_TN = 1024
NORM_ROWS = 128
ATTN_UNROLL = 8
DN_TT = 256
GROUP = 4
OUTPROJ_TM = 512
OUTPROJ_ROWS = 128
CACHE_ROWS = 1024
FFN_TM = 1024
FFN_TF = 512
FFN_ROWS = 64
FFN_COLS = 512
FFN_PIECE = 256


def _dn_prompt_body(q_ref, k_ref, v_ref, z_ref, h_ref, wgate_ref, cw_ref, cs_ref, s0_ref, alog_ref, dtb_ref, nw_ref,
                    o_ref, s_out_ref,
                    s_scr, e_scr, qn_scr, kn_scr, vv_scr, g_scr, beta_scr,
                    w_scr, u_scr, qg_scr, kdt_scr, attn_scr, gl_scr, o_scr, *, tt):
    t = pl.program_id(1)
    nt = pl.num_programs(1)

    @pl.when(t == 0)
    def _():
        s_scr[...] = s0_ref[0]
        e_scr[5:8, :] = cs_ref[0]

    e_scr[8:8 + tt, 0:WIDTH_BQK] = q_ref[0]
    e_scr[8:8 + tt, WIDTH_BQK:2 * WIDTH_BQK] = k_ref[0]
    e_scr[8:8 + tt, 2 * WIDTH_BQK:CONV_DIM] = v_ref[0]

    def l2n(x):
        return x * lax.rsqrt(jnp.sum(x * x, axis=-1, keepdims=True) + EPS)

    for c0 in range(0, CONV_DIM, DK):
        cols = slice(c0, c0 + DK)
        w = cw_ref[:, cols]
        y = w[0:1, :] * e_scr[5:5 + tt, cols]
        for i in range(1, CONV_W):
            y = y + w[i:i + 1, :] * e_scr[5 + i:5 + i + tt, cols]
        y = _silu(y)
        if c0 < WIDTH_BQK:
            qn_scr[:, cols] = l2n(y) * QK_SCALE
        elif c0 < 2 * WIDTH_BQK:
            kn_scr[:, c0 - WIDTH_BQK:c0 - WIDTH_BQK + DK] = l2n(y)
        else:
            vv_scr[:, c0 - 2 * WIDTH_BQK:c0 - 2 * WIDTH_BQK + DK] = y
    e_scr[5:8, :] = e_scr[tt + 5:tt + 8, :]
    gates = jnp.dot(h_ref[0], wgate_ref[...], preferred_element_type=F32)
    beta_scr[...] = _sigmoid(gates)
    g_scr[...] = -jnp.exp(alog_ref[...]) * _softplus(gates + dtb_ref[...])

    ri = lax.broadcasted_iota(jnp.int32, (CHUNK, CHUNK), 0)
    ci = lax.broadcasted_iota(jnp.int32, (CHUNK, CHUNK), 1)
    tri = ri >= ci
    strict = ri > ci
    same_sub = (ri // SUB) == (ci // SUB)
    tril_ones = tri.astype(F32)
    nw = nw_ref[...]

    for c in range(tt // CHUNK):
        rows = slice(c * CHUNK, (c + 1) * CHUNK)
        beta_all = beta_scr[rows, :]
        gc_all = _fdot(tril_ones, g_scr[rows, :])
        gc_all_t = gc_all.T
        g_scr[rows, :] = gc_all
        for hq0 in range(0, QK_HEADS_B, GROUP):
            units = []
            for hq in range(hq0, hq0 + GROUP):
                qn = qn_scr[rows, hq * DK:(hq + 1) * DK]
                kn = kn_scr[rows, hq * DK:(hq + 1) * DK]
                kk = _bdot_nt(kn, kn)
                qk = _bdot_nt(qn, kn)
                for hv in range(2 * hq, 2 * hq + 2):
                    beta = beta_all[:, hv:hv + 1]
                    gc = gc_all[:, V_HEADS_B + hv:V_HEADS_B + hv + 1]
                    gc_row = gc_all_t[V_HEADS_B + hv:V_HEADS_B + hv + 1, :]
                    gc_last = gc_row[:, CHUNK - 1:CHUNK]
                    decay = jnp.exp(jnp.where(tri, gc - gc_row, NEG))
                    a = jnp.where(strict, beta * kk * decay, 0.0)
                    egc = jnp.exp(gc)
                    attn_scr[hv, rows, :] = (qk * decay).astype(BF16)
                    qg_scr[hv, rows, :] = (qn * egc).astype(BF16)
                    kd = kn * jnp.exp(gc_last - gc)
                    kdt_scr[hv, c * DK:(c + 1) * DK, :] = kd.T.astype(BF16)
                    gl_scr[hv, c * 8:(c + 1) * 8, :] = jnp.broadcast_to(jnp.exp(gc_last), (8, DV))
                    units.append((hv, a))
            ds = [jnp.where(same_sub, a, 0.0) for _, a in units]
            ns = [-dd for dd in ds]
            pws = ds
            for _ in range(SUB.bit_length() - 2):
                pws = [_bdot(pw, pw) for pw in pws]
                ns = [n + pw + _bdot(n, pw) for n, pw in zip(ns, pws)]
            ls = [jnp.where(same_sub, 0.0, a) for _, a in units]
            ps = [lo + _bdot(lo, n) for lo, n in zip(ls, ns)]
            ns = [n - (p + _bdot(n, p)) for n, p in zip(ns, ps)]
            xs = []
            for hv, _ in units:
                beta = beta_scr[rows, hv:hv + 1]
                kscale = beta * jnp.exp(g_scr[rows, V_HEADS_B + hv:V_HEADS_B + hv + 1])
                xs.append(jnp.concatenate([kn_scr[rows, (hv // 2) * DK:(hv // 2 + 1) * DK] * kscale,
                                           vv_scr[rows, hv * DV:(hv + 1) * DV] * beta], axis=-1))
            wus = [x + _bdot(n, x) for n, x in zip(ns, xs)]
            for wu, (hv, _) in zip(wus, units):
                w_scr[hv, rows, :] = wu[:, :DK].astype(BF16)
                u_scr[hv, rows, :] = wu[:, DK:]

    heads = range(V_HEADS_B)
    for c in range(tt // CHUNK):
        rows = slice(c * CHUNK, (c + 1) * CHUNK)
        states = [s_scr[hv] for hv in heads]
        states_b = [s.astype(BF16) for s in states]
        v_news = [u_scr[hv, rows, :] - jnp.dot(w_scr[hv, rows, :], states_b[hv], preferred_element_type=F32)
                  for hv in heads]
        v_news_b = [v.astype(BF16) for v in v_news]
        for hv in heads:
            s_scr[hv] = (states[hv] * gl_scr[hv, c * 8:c * 8 + 1, :]
                         + jnp.dot(kdt_scr[hv, c * DK:(c + 1) * DK, :], v_news_b[hv],
                                   preferred_element_type=F32))
        for hv in heads:
            o_scr[rows, hv * DV:(hv + 1) * DV] = (
                jnp.dot(qg_scr[hv, rows, :], states_b[hv], preferred_element_type=F32)
                + jnp.dot(attn_scr[hv, rows, :], v_news_b[hv], preferred_element_type=F32))

    for hv in heads:
        o = o_scr[:, hv * DV:(hv + 1) * DV]
        z = z_ref[0, :, hv * DV:(hv + 1) * DV]
        o = o * lax.rsqrt(jnp.mean(o * o, axis=-1, keepdims=True) + EPS) * nw * _silu(z)
        o_ref[0, :, hv * DV:(hv + 1) * DV] = o.astype(o_ref.dtype)

    @pl.when(t == nt - 1)
    def _():
        s_out_ref[0] = s_scr[...]


def _dn_prompt(proj3, h3, w_gate, conv_state, s0, conv_w, alog_vec, dtb_vec, norm_w, tt):
    b, s, _ = proj3.shape
    body = functools.partial(_dn_prompt_body, tt=tt)
    nh = V_HEADS_B
    in_specs = [
        pl.BlockSpec((1, tt, WIDTH_BQK), lambda i, t: (i, t, OFF_BQ // WIDTH_BQK)),
        pl.BlockSpec((1, tt, WIDTH_BQK), lambda i, t: (i, t, OFF_BK // WIDTH_BQK)),
        pl.BlockSpec((1, tt, WIDTH_BV), lambda i, t: (i, t, OFF_BV // WIDTH_BV)),
        pl.BlockSpec((1, tt, WIDTH_BV), lambda i, t: (i, t, OFF_BZ // WIDTH_BV)),
        pl.BlockSpec((1, tt, D_MODEL), lambda i, t: (i, t, 0)),
        pl.BlockSpec((D_MODEL, 128), lambda i, t: (0, 0)),
        pl.BlockSpec((CONV_W, CONV_DIM), lambda i, t: (0, 0)),
        pl.BlockSpec((1, CONV_W - 1, CONV_DIM), lambda i, t: (i, 0, 0)),
        pl.BlockSpec((1, nh, DK, DV), lambda i, t: (i, 0, 0, 0)),
        pl.BlockSpec((1, 128), lambda i, t: (0, 0)),
        pl.BlockSpec((1, 128), lambda i, t: (0, 0)),
        pl.BlockSpec((1, DV), lambda i, t: (0, 0)),
    ]
    return pl.pallas_call(
        body,
        grid=(b, s // tt),
        in_specs=in_specs,
        out_specs=[pl.BlockSpec((1, tt, WIDTH_BV), lambda i, t: (i, t, 0)),
                   pl.BlockSpec((1, nh, DK, DV), lambda i, t: (i, 0, 0, 0))],
        out_shape=[jax.ShapeDtypeStruct((b, s, WIDTH_BV), BF16),
                   jax.ShapeDtypeStruct((b, nh, DK, DV), F32)],
        scratch_shapes=[pltpu.VMEM((nh, DK, DV), F32),
                        pltpu.VMEM((tt + 8, CONV_DIM), F32),
                        pltpu.VMEM((tt, WIDTH_BQK), F32),
                        pltpu.VMEM((tt, WIDTH_BQK), F32),
                        pltpu.VMEM((tt, WIDTH_BV), F32),
                        pltpu.VMEM((tt, 128), F32),
                        pltpu.VMEM((tt, 128), F32),
                        pltpu.VMEM((nh, tt, DK), BF16),
                        pltpu.VMEM((nh, tt, DV), F32),
                        pltpu.VMEM((nh, tt, DK), BF16),
                        pltpu.VMEM((nh, tt // CHUNK * DK, CHUNK), BF16),
                        pltpu.VMEM((nh, tt, CHUNK), BF16),
                        pltpu.VMEM((nh, tt // CHUNK * 8, DV), F32),
                        pltpu.VMEM((tt, WIDTH_BV), F32)],
        compiler_params=_params(("arbitrary", "arbitrary"), VMEM_MID),
        name="deltanet_prompt",
    )(proj3, proj3, proj3, proj3, h3, w_gate, conv_w, conv_state, s0, alog_vec, dtb_vec, norm_w)


def _dn_sample_body(proj_ref, gates_ref, cw_ref, cs_ref, s0_ref, alog_ref, dtb_ref, nw_ref,
                    o_ref, cs_out_ref, s_out_ref):
    pre = proj_ref[0, :, OFF_BQ:OFF_BQ + CONV_DIM]
    buf = cs_ref[0]
    w = cw_ref[...]
    y = w[CONV_W - 1:CONV_W, :] * pre
    for i in range(CONV_W - 1):
        y = y + w[i:i + 1, :] * buf[i:i + 1, :]
    y = _silu(y)
    cs_out_ref[0, 0:CONV_W - 2, :] = buf[1:CONV_W - 1, :]
    cs_out_ref[0, CONV_W - 2:CONV_W - 1, :] = pre

    gates = gates_ref[0]
    beta_all = _sigmoid(gates)
    g_all = -jnp.exp(alog_ref[...]) * _softplus(gates + dtb_ref[...])
    nw = nw_ref[...]

    def l2n(x):
        return x * lax.rsqrt(jnp.sum(x * x, axis=-1, keepdims=True) + EPS)

    row8 = lax.broadcasted_iota(jnp.int32, (8, DK), 0) == 0
    for hv in range(V_HEADS_B):
        hq = hv // 2
        q = l2n(y[:, hq * DK:(hq + 1) * DK]) * QK_SCALE
        k = l2n(y[:, WIDTH_BQK + hq * DK:WIDTH_BQK + (hq + 1) * DK])
        v = y[:, 2 * WIDTH_BQK + hv * DV:2 * WIDTH_BQK + (hv + 1) * DV]
        beta = beta_all[:, hv:hv + 1]
        g = g_all[:, V_HEADS_B + hv:V_HEADS_B + hv + 1]
        eg = jnp.exp(g)
        state = s0_ref[0, hv]

        def pad8(x):
            return jnp.where(row8, jnp.broadcast_to(x, (8, x.shape[-1])), 0.0)

        v_new = v * beta - _bdot(pad8(k * (beta * eg)), state)[0:1, :]
        qk = jnp.sum(q.astype(BF16).astype(F32) * k.astype(BF16).astype(F32), axis=-1, keepdims=True)
        o = _bdot(pad8(q * eg), state)[0:1, :] + qk.astype(BF16).astype(F32) * v_new.astype(BF16).astype(F32)
        s_out_ref[0, hv] = state * eg + _bdot_tn(pad8(k), pad8(v_new))
        z = proj_ref[0, :, OFF_BZ + hv * DV:OFF_BZ + (hv + 1) * DV]
        o = o * lax.rsqrt(jnp.mean(o * o, axis=-1, keepdims=True) + EPS) * nw * _silu(z)
        o_ref[0, :, hv * DV:(hv + 1) * DV] = o.astype(o_ref.dtype)


def _dn_sample(proj, gates, conv_state, s0, conv_w, alog_vec, dtb_vec, norm_w):
    b = proj.shape[0]
    return pl.pallas_call(
        _dn_sample_body,
        grid=(b,),
        in_specs=[pl.BlockSpec((1, 1, PROJ_MAIN), lambda i: (i, 0, 0)),
                  pl.BlockSpec((1, 1, 128), lambda i: (i, 0, 0)),
                  pl.BlockSpec((CONV_W, CONV_DIM), lambda i: (0, 0)),
                  pl.BlockSpec((1, CONV_W - 1, CONV_DIM), lambda i: (i, 0, 0)),
                  pl.BlockSpec((1, V_HEADS_B, DK, DV), lambda i: (i, 0, 0, 0)),
                  pl.BlockSpec((1, 128), lambda i: (0, 0)),
                  pl.BlockSpec((1, 128), lambda i: (0, 0)),
                  pl.BlockSpec((1, DV), lambda i: (0, 0))],
        out_specs=[pl.BlockSpec((1, 1, WIDTH_BV), lambda i: (i, 0, 0)),
                   pl.BlockSpec((1, CONV_W - 1, CONV_DIM), lambda i: (i, 0, 0)),
                   pl.BlockSpec((1, V_HEADS_B, DK, DV), lambda i: (i, 0, 0, 0))],
        out_shape=[jax.ShapeDtypeStruct((b, 1, WIDTH_BV), BF16),
                   jax.ShapeDtypeStruct((b, CONV_W - 1, CONV_DIM), F32),
                   jax.ShapeDtypeStruct((b, V_HEADS_B, DK, DV), F32)],
        compiler_params=_params(("arbitrary",), VMEM_SMALL),
        name="deltanet_sample",
    )(proj, gates, conv_w, conv_state, s0, alog_vec, dtb_vec, norm_w)


def _outproj_body(att_ref, dn_ref, wa_ref, wb_ref, x_ref, lnpost_ref, lnpre_ref, x1_ref, h2_ref):
    tm = x_ref.shape[0]
    piece = min(tm, OUTPROJ_ROWS)
    for rows in [slice(r, r + piece) for r in range(0, tm, piece)]:
        mix = (jnp.dot(att_ref[rows, :], wa_ref[...], preferred_element_type=F32)
               + jnp.dot(dn_ref[rows, :], wb_ref[...], preferred_element_type=F32))
        x1 = x_ref[rows, :] + _rms(mix, lnpost_ref[...])
        x1_ref[rows, :] = x1
        h2_ref[rows, :] = _rms(x1, lnpre_ref[...]).astype(h2_ref.dtype)


def _outproj(att, dn, w, x, ln_post, ln_pre, tm):
    m, d = x.shape
    assert WIDTH_A == WIDTH_BV
    return pl.pallas_call(
        _outproj_body,
        grid=(m // tm,),
        in_specs=[pl.BlockSpec((tm, WIDTH_A), lambda i: (i, 0)),
                  pl.BlockSpec((tm, WIDTH_BV), lambda i: (i, 0)),
                  pl.BlockSpec((WIDTH_A, d), lambda i: (0, 0)),
                  pl.BlockSpec((WIDTH_BV, d), lambda i: (1, 0)),
                  pl.BlockSpec((tm, d), lambda i: (i, 0)),
                  pl.BlockSpec((1, d), lambda i: (0, 0)),
                  pl.BlockSpec((1, d), lambda i: (0, 0))],
        out_specs=[pl.BlockSpec((tm, d), lambda i: (i, 0)),
                   pl.BlockSpec((tm, d), lambda i: (i, 0))],
        out_shape=[jax.ShapeDtypeStruct((m, d), F32),
                   jax.ShapeDtypeStruct((m, d), BF16)],
        compiler_params=_params(("arbitrary",), VMEM_MID),
        name="outproj",
    )(att, dn, w, w, x, ln_post, ln_pre)


def _ffn_body(*refs, tm, tiles_per_seq, single_token):
    if single_token:
        (h_ref, wg_ref, wv_ref, cwb_ref, wo_prev_ref, wo_last_ref, x1_hbm, ln_ref, pg_ref, pv_ref,
         o_ref, ng_ref, nv_ref, eg_scr, ev_scr, carry_scr, act_scr, x1_scr, x1_sem) = refs
    else:
        (h_ref, wg_ref, wv_ref, cwb_ref, wo_prev_ref, wo_last_ref, x1_hbm, ln_ref, prev_ref,
         o_ref, new_ref, eg_scr, ev_scr, carry_scr, act_scr, x1_scr, x1_sem) = refs
    i = pl.program_id(0)
    j = pl.program_id(1)
    nj = pl.num_programs(1)
    d = o_ref.shape[-1]
    tf = act_scr.shape[-1]
    cur = j % 2
    act_cur = act_scr.at[cur]
    act_prev = act_scr.at[1 - cur]

    def x1_copy():
        return pltpu.make_async_copy(x1_hbm.at[pl.ds(pl.multiple_of(i * tm, tm), tm), :], x1_scr, x1_sem)

    def down_proj(act_ref, wo_ref):
        for n in range(0, d, FFN_COLS):
            o_ref[:, n:n + FFN_COLS] += jnp.dot(act_ref[...], wo_ref[:, n:n + FFN_COLS],
                                                preferred_element_type=F32)

    @pl.when(j == 0)
    def _():
        x1_copy().start()
        o_ref[...] = jnp.zeros_like(o_ref)
        act_prev[...] = jnp.zeros_like(act_prev)

    if single_token:
        def up_conv(w_ref, tile, prev_ref, new_ref):
            up = jnp.dot(h_ref[...], w_ref[...], preferred_element_type=F32)
            cw = cwb_ref[tile]
            new_ref[...] = up
            return cw[0:1, :] * prev_ref[0] + cw[1:2, :] * prev_ref[1] + cw[2:3, :] * up + cw[3:4, :]

        gate = up_conv(wg_ref, j, pg_ref, ng_ref)
        val = up_conv(wv_ref, nj + j, pv_ref, nv_ref)
        act_cur[...] = (_gelu_tanh(gate) * val).astype(BF16)
        down_proj(act_prev, wo_prev_ref)
    else:
        first_tile = i % tiles_per_seq == 0
        pieces = [slice(c, c + FFN_PIECE) for c in range(0, tf, FFN_PIECE)]

        def up_proj(cols):
            for w_ref, tile, e_scr, slot in ((wg_ref, j, eg_scr, 0), (wv_ref, nj + j, ev_scr, 1)):
                e_scr[8:8 + tm, cols] = jnp.dot(h_ref[...], w_ref[:, cols], preferred_element_type=F32)
                e_scr[6:8, cols] = jnp.where(first_tile, prev_ref[0, tile, :, cols],
                                             carry_scr[slot, j, 6:8, cols])
                tail = e_scr[tm + 6:tm + 8, cols]
                carry_scr[slot, j, 6:8, cols] = tail
                new_ref[0, tile, :, cols] = tail

        def conv(e_scr, tile, cols, r):
            cw = cwb_ref[tile, :, cols]
            return (cw[0:1, :] * e_scr[6 + r:6 + r + FFN_ROWS, cols]
                    + cw[1:2, :] * e_scr[7 + r:7 + r + FFN_ROWS, cols]
                    + cw[2:3, :] * e_scr[8 + r:8 + r + FFN_ROWS, cols] + cw[3:4, :])

        def conv_geglu(cols):
            for r in range(0, tm, FFN_ROWS):
                act_cur[r:r + FFN_ROWS, cols] = (_gelu_tanh(conv(eg_scr, j, cols, r))
                                                 * conv(ev_scr, nj + j, cols, r)).astype(BF16)

        up_proj(pieces[0])
        for c in range(1, len(pieces)):
            up_proj(pieces[c])
            conv_geglu(pieces[c - 1])
        down_proj(act_prev, wo_prev_ref)
        conv_geglu(pieces[-1])

    @pl.when(j == nj - 1)
    def _():
        x1_copy().wait()
        piece = min(tm, OUTPROJ_ROWS)
        for r in range(0, tm, piece):
            rows = slice(r, r + piece)
            f = o_ref[rows, :] + jnp.dot(act_cur[rows, :], wo_last_ref[...], preferred_element_type=F32)
            o_ref[rows, :] = x1_scr[rows, :] + _rms(f, ln_ref[...])


def _ffn(h2, w_in, conv_w, conv_b, w_out, x1, ln_post, prev, tm, tf, seq_len):
    m, d = h2.shape
    single = seq_len == 1
    nj = D_FF // tf
    tiles_per_seq = 1 if single else seq_len // tm
    cwb = jnp.concatenate([conv_w, conv_b], axis=0).reshape(FFN_CONV_W + 1, 2 * nj, tf).transpose(1, 0, 2)
    cwb_spec = pl.BlockSpec((2 * nj, FFN_CONV_W + 1, tf), lambda i, j: (0, 0, 0))
    if single:
        prev_args = (prev, prev)
        prev_specs = [pl.BlockSpec((2, tm, tf), lambda i, j: (0, i, j)),
                      pl.BlockSpec((2, tm, tf), lambda i, j: (0, i, nj + j))]
        new_specs = [pl.BlockSpec((tm, tf), lambda i, j: (i, j))] * 2
        new_shapes = [jax.ShapeDtypeStruct((m, D_FF), F32)] * 2
    else:
        prev_args = (prev.reshape(-1, 2, 2 * nj, tf).transpose(0, 2, 1, 3),)
        prev_specs = [pl.BlockSpec((1, 2 * nj, 2, tf), lambda i, j: (i // tiles_per_seq, 0, 0, 0))]
        new_specs = [pl.BlockSpec((1, 2 * nj, 2, tf), lambda i, j: (i, 0, 0, 0))]
        new_shapes = [jax.ShapeDtypeStruct((m // tm, 2 * nj, 2, tf), F32)]
    body = functools.partial(_ffn_body, tm=tm, tiles_per_seq=tiles_per_seq, single_token=single)
    once = dict(pipeline_mode=pl.Buffered(1)) if tm >= 1024 else {}
    outs = pl.pallas_call(
        body,
        grid=(m // tm, nj),
        in_specs=[pl.BlockSpec((tm, d), lambda i, j: (i, 0), **once),
                  pl.BlockSpec((d, tf), lambda i, j: (0, j)),
                  pl.BlockSpec((d, tf), lambda i, j: (0, nj + j)),
                  cwb_spec,
                  pl.BlockSpec((tf, d), lambda i, j: (jnp.maximum(j - 1, 0), 0)),
                  pl.BlockSpec((tf, d), lambda i, j: (nj - 1, 0), pipeline_mode=pl.Buffered(1)),
                  pl.BlockSpec(memory_space=pl.ANY),
                  pl.BlockSpec((1, d), lambda i, j: (0, 0))] + prev_specs,
        out_specs=[pl.BlockSpec((tm, d), lambda i, j: (i, 0), **once)] + new_specs,
        out_shape=[jax.ShapeDtypeStruct((m, d), F32)] + new_shapes,
        scratch_shapes=[pltpu.VMEM((tm + 8, tf), F32),
                        pltpu.VMEM((tm + 8, tf), F32),
                        pltpu.VMEM((2, nj, 8, tf), F32),
                        pltpu.VMEM((2, tm, tf), BF16),
                        pltpu.VMEM((tm, d), F32),
                        pltpu.SemaphoreType.DMA(())],
        compiler_params=_params(("arbitrary", "arbitrary"), VMEM_FFN),
        name="convffn",
    )(h2, w_in, w_in, cwb, w_out, w_out, x1, ln_post, *prev_args)
    if single:
        y, new_g, new_v = outs
        return y, jnp.concatenate([new_g, new_v], axis=-1)
    y, new = outs
    return y, new.transpose(0, 2, 1, 3).reshape(m // tm, 2, 2 * D_FF)


def _cache_shift_body(ck_ref, cv_ref, ck_next_ref, cv_next_ref, nk_ref, nv_ref, ok_ref, ov_ref, *, tr):
    last = pl.program_id(1) == pl.num_programs(1) - 1
    for c_ref, nxt_ref, n_ref, o_ref in ((ck_ref, ck_next_ref, nk_ref, ok_ref),
                                         (cv_ref, cv_next_ref, nv_ref, ov_ref)):
        o_ref[0, 0:tr - 1] = c_ref[0, 1:tr]
        o_ref[0, tr - 1] = jnp.where(last, n_ref[0, 0], nxt_ref[0, 0])


def _cache_shift(cache_k, cache_v, new_k, new_v, tr):
    nb, rows, nh, dh = cache_k.shape
    main = pl.BlockSpec((1, tr, nh, dh), lambda b, i: (b, i, 0, 0))
    nxt = pl.BlockSpec((1, 1, nh, dh), lambda b, i: (b, jnp.minimum((i + 1) * tr, rows - 1), 0, 0))
    new = pl.BlockSpec((1, 1, nh, dh), lambda b, i: (b, 0, 0, 0))
    shape = jax.ShapeDtypeStruct(cache_k.shape, cache_k.dtype)
    return pl.pallas_call(
        functools.partial(_cache_shift_body, tr=tr),
        grid=(nb, rows // tr),
        in_specs=[main, main, nxt, nxt, new, new],
        out_specs=[main, main],
        out_shape=[shape, shape],
        compiler_params=_params(("arbitrary", "arbitrary"), VMEM_SMALL),
        name="cache_shift",
    )(cache_k, cache_v, cache_k, cache_v, new_k, new_v)


def _lane_vec(values, offset):
    return jnp.zeros((1, 128), F32).at[0, offset:offset + V_HEADS_B].set(values.astype(F32))


def kernel(x_prompt, x_sample, cache_win_k, cache_win_v, state_dn_conv, state_dn_rec, state_ffn_conv,
           rel_bias, ln_mix_pre, w_in, dn_conv_w, dn_A_log, dn_dt_bias, dn_norm_w, w_out, ln_mix_post,
           ln_ffn_pre, w_ffn_in, ffn_conv_w, ffn_conv_b, w_ffn_out, ln_ffn_post):
    bp, sp, d = x_prompt.shape
    bs = x_sample.shape[0]
    l = 0

    w_main = w_in[l].astype(BF16)
    w_gate = jnp.pad(w_main[:, PROJ_MAIN:], ((0, 0), (0, 128 - 2 * V_HEADS_B)))
    wo = w_out[l].astype(BF16)
    wf_in = w_ffn_in[l].astype(BF16)
    wf_out = w_ffn_out[l].astype(BF16)
    ln1 = ln_mix_pre[l][None, :]
    ln2 = ln_mix_post[l][None, :]
    ln3 = ln_ffn_pre[l][None, :]
    ln4 = ln_ffn_post[l][None, :]
    conv_w = dn_conv_w[l]
    alog_vec = _lane_vec(dn_A_log[l], V_HEADS_B)
    dtb_vec = _lane_vec(dn_dt_bias[l], V_HEADS_B)
    norm_w = dn_norm_w[l][None, :]
    fcw = ffn_conv_w[l]
    fcb = ffn_conv_b[l][None, :]

    xp = x_prompt.reshape(bp * sp, d)
    proj_p, hp = _norm_matmul(xp, ln1, w_main, PROJ_MAIN, INPROJ_TM, INPROJ_TN, "inproj_prompt")
    proj3 = proj_p.reshape(bp, sp, PROJ_MAIN)
    keep = min(MAX_DISTANCE, sp)
    att_p, win_k, win_v = _attn_prompt(proj3, rel_bias, keep)
    dn_p, p_dn_rec = _dn_prompt(
        proj3, hp.reshape(bp, sp, d), w_gate,
        jnp.zeros((bp, CONV_W - 1, CONV_DIM), F32), jnp.zeros((bp, V_HEADS_B, DK, DV), F32),
        conv_w, alog_vec, dtb_vec, norm_w, DN_TT)
    x1_p, h2_p = _outproj(att_p.reshape(bp * sp, WIDTH_A), dn_p.reshape(bp * sp, WIDTH_BV),
                          wo, xp, ln2, ln3, OUTPROJ_TM)
    y_p, fc = _ffn(h2_p, wf_in, fcw, fcb, wf_out, x1_p, ln4,
                   jnp.zeros((bp, FFN_CONV_W - 1, 2 * D_FF), F32), FFN_TM, FFN_TF, sp)
    p_win_k = win_k.reshape(1, bp, keep, HEADS_A, HEAD_DIM)
    p_win_v = win_v.reshape(1, bp, keep, HEADS_A, HEAD_DIM)
    p_dn_conv = proj3[:, sp - (CONV_W - 1):, OFF_BQ:OFF_BQ + CONV_DIM][None]
    tiles = sp // FFN_TM
    p_ffn_conv = fc[tiles - 1::tiles][None]

    xs = x_sample.reshape(bs, d)
    proj_s, hs = _norm_matmul(xs, ln1, w_main, PROJ_MAIN, bs, INPROJ_TN, "inproj_sample")
    gates_s = _matmul(hs, w_gate, bs, 128, "gates_sample")
    ck = cache_win_k[l]
    cv = cache_win_v[l]
    new_k = proj_s[:, OFF_AK:OFF_AK + WIDTH_A]
    new_v = proj_s[:, OFF_AV:OFF_AV + WIDTH_A]
    new_q = proj_s[:, OFF_AQ:OFF_AQ + WIDTH_A].reshape(bs, HEADS_A, HEAD_DIM)
    new_k = new_k.reshape(bs, HEADS_A, HEAD_DIM)
    new_v = new_v.reshape(bs, HEADS_A, HEAD_DIM)
    att_s = _attn_sample(new_q, new_k, new_v, ck, cv, rel_bias)
    s_win_k, s_win_v = _cache_shift(ck, cv, new_k[:, None], new_v[:, None], CACHE_ROWS)
    dn_s, s_dn_conv, s_dn_rec = _dn_sample(proj_s[:, None], gates_s[:, None], state_dn_conv[l],
                                           state_dn_rec[l], conv_w, alog_vec, dtb_vec, norm_w)
    x1_s, h2_s = _outproj(att_s.reshape(bs, WIDTH_A), dn_s.reshape(bs, WIDTH_BV),
                          wo, xs, ln2, ln3, bs)
    prev_s = jnp.swapaxes(state_ffn_conv[l], 0, 1)
    y_s, up_s = _ffn(h2_s, wf_in, fcw, fcb, wf_out, x1_s, ln4, prev_s, bs, FFN_TF, 1)
    s_ffn_conv = jnp.stack([prev_s[1], up_s], axis=1)[None]

    return (y_p.reshape(bp, sp, d), y_s.reshape(bs, 1, d),
            p_win_k, p_win_v, p_dn_conv, p_dn_rec[None], p_ffn_conv,
            s_win_k[None], s_win_v[None], s_dn_conv[None], s_dn_rec[None], s_ffn_conv)
```

```python
import functools
import math

import numpy as np
import jax
import jax.numpy as jnp
from jax import lax
from jax.experimental import pallas as pl
from jax.experimental.pallas import tpu as pltpu

F32 = jnp.float32
BF16 = jnp.bfloat16

D_MODEL = 2048
HEAD_DIM = 128
WIDTH_A = 1024
HEADS_A = 8
DILATIONS = (1, 4, 16)
BLK = 128
N_BUCKETS = 32
MAX_DISTANCE = 2048
DK = 128
DV = 128
V_HEADS_B = 8
QK_HEADS_B = 4
WIDTH_BQK = 512
WIDTH_BV = 1024
CONV_W = 4
CONV_DIM = 2048
CHUNK = 128
SUB = 64
D_FF = 5632
FFN_CONV_W = 3
EPS = 1e-6
NEG = -1e30
ATT_SCALE = HEAD_DIM ** -0.5
QK_SCALE = DK ** -0.5

OFF_AQ, OFF_AK, OFF_AV = 0, 1024, 2048
OFF_BQ, OFF_BK, OFF_BV, OFF_BZ = 3072, 3584, 4096, 5120
OFF_GATES = 6144
PROJ_MAIN = 6144

MIB = 2 ** 20
VMEM_SMALL = 40
VMEM_MID = 48
VMEM_FFN = 57


def _params(semantics, vmem_mib):
    return pltpu.CompilerParams(dimension_semantics=semantics, vmem_limit_bytes=vmem_mib * MIB)


def _bdot(a, b):
    return jnp.dot(a.astype(BF16), b.astype(BF16), preferred_element_type=F32)


def _bdot_nt(a, b):
    return lax.dot_general(a.astype(BF16), b.astype(BF16), (((1,), (1,)), ((), ())),
                           preferred_element_type=F32)


def _bdot_tn(a, b):
    return lax.dot_general(a.astype(BF16), b.astype(BF16), (((0,), (0,)), ((), ())),
                           preferred_element_type=F32)


def _fdot(a, b):
    return jnp.dot(a, b, preferred_element_type=F32, precision=lax.Precision.HIGHEST)


def _silu(x):
    return x * (1.0 / (1.0 + jnp.exp(-x)))


def _sigmoid(x):
    return 1.0 / (1.0 + jnp.exp(-x))


def _softplus(x):
    return jnp.maximum(x, 0.0) + jnp.log(1.0 + jnp.exp(-jnp.abs(x)))


def _gelu_tanh(x):
    c = math.sqrt(2.0 / math.pi)
    half = 0.5 * x
    return half + half * jnp.tanh(x * (c + (c * 0.044715) * (x * x)))


def _rms(x, w):
    return x * lax.rsqrt(jnp.mean(x * x, axis=-1, keepdims=True) + EPS) * w


def _gate_weights(wgate_ref):
    lane = lax.broadcasted_iota(jnp.int32, wgate_ref.shape, 1)
    return jnp.where(lane < 2 * V_HEADS_B, wgate_ref[...], 0.0).astype(BF16)


def _gate_spec(grid_rank):
    idx = (0, 0, PROJ_MAIN // 128)
    return pl.BlockSpec((None, D_MODEL, 128), (lambda i: idx) if grid_rank == 1 else (lambda i, t: idx))


def _gates_body(h_ref, wgate_ref, o_ref):
    o_ref[...] = jnp.dot(h_ref[...], _gate_weights(wgate_ref), preferred_element_type=F32)


def _gates(h, w_in):
    m, k = h.shape
    return pl.pallas_call(
        _gates_body,
        grid=(1,),
        in_specs=[pl.BlockSpec((m, k), lambda i: (0, 0)), _gate_spec(1)],
        out_specs=pl.BlockSpec((m, 128), lambda i: (0, 0)),
        out_shape=jax.ShapeDtypeStruct((m, 128), F32),
        compiler_params=_params(("arbitrary",), VMEM_SMALL),
        name="gates_sample",
    )(h, w_in)


def _norm_matmul_body(x_ref, ln_ref, w_ref, o_ref, h_ref):
    tm = x_ref.shape[0]
    piece = min(tm, NORM_ROWS)
    for r in range(0, tm, piece):
        rows = slice(r, r + piece)
        h = _rms(x_ref[rows, :], ln_ref[...]).astype(BF16)
        h_ref[rows, :] = h
        o_ref[rows, :] = jnp.dot(h, w_ref[...], preferred_element_type=F32)


def _norm_matmul(x, ln, w, n, tm, tn, name):
    m, k = x.shape
    return pl.pallas_call(
        _norm_matmul_body,
        grid=(m // tm, n // tn),
        in_specs=[pl.BlockSpec((tm, k), lambda i, j: (i, 0)),
                  pl.BlockSpec((1, k), lambda i, j: (0, 0)),
                  pl.BlockSpec((k, tn), lambda i, j: (0, j))],
        out_specs=[pl.BlockSpec((tm, tn), lambda i, j: (i, j)),
                   pl.BlockSpec((tm, k), lambda i, j: (i, 0))],
        out_shape=[jax.ShapeDtypeStruct((m, n), F32), jax.ShapeDtypeStruct((m, k), BF16)],
        compiler_params=_params(("arbitrary", "arbitrary"), VMEM_MID),
        name=name,
    )(x, ln, w)


def _rel_bucket_np(dist):
    dist = np.asarray(dist, np.int64)
    max_exact = N_BUCKETS // 2
    d = np.maximum(dist, 1).astype(np.float64)
    val = np.log(d / max_exact) / math.log(MAX_DISTANCE / max_exact) * (N_BUCKETS - max_exact)
    frac = np.abs(val - np.round(val))
    near = (frac < 2e-5) &(dist >= max_exact) & (dist != max_exact) & (dist < MAX_DISTANCE)
    assert not near.any(), "distance on a bucket boundary"
    val = np.where(dist == max_exact, 0.0, val)
    large = np.minimum(max_exact + np.trunc(val).astype(np.int64), N_BUCKETS - 1)
    return np.where(dist < max_exact, dist, large).astype(np.int32)


def _prompt_bucket_tables():
    qi = np.arange(BLK)[:, None]
    kj = np.arange(2 * BLK)[None, :]
    delta = BLK + qi - kj
    inwin = (delta >= 0) & (delta <= BLK)
    tabs = []
    for dil in DILATIONS:
        b = _rel_bucket_np(np.clip(delta, 0, BLK) * dil)
        tabs.append(np.where(inwin, b, -1))
    return np.stack(tabs).astype(np.int32)


def _sample_bucket_tables():
    j = BLK - np.arange(BLK)
    return np.stack([_rel_bucket_np(j * dil)[None, :] for dil in DILATIONS]).astype(np.int32)


def _attn_prompt_body(bucket_ref, relb_ref, q_ref, k_ref, v_ref, o_ref, wk_ref, wv_ref,
                      bias_scr, acc_scr, m_scr, l_scr):
    h = pl.program_id(1)
    s, keep = k_ref.shape[1], wk_ref.shape[1] // HEADS_A
    wk_ref[0, pl.ds(h, keep, stride=HEADS_A), :] = k_ref[0, s - keep:s, :]
    wv_ref[0, pl.ds(h, keep, stride=HEADS_A), :] = v_ref[0, s - keep:s, :]
    col = lax.broadcasted_iota(jnp.int32, (BLK, 2 * BLK), 1)
    tables = _prompt_bucket_tables()
    for br in range(3):
        bk = bucket_ref[br]
        bias = jnp.zeros((BLK, 2 * BLK), F32)
        for kb in sorted(set(tables[br].ravel().tolist()) - {-1}):
            bias = jnp.where(bk == kb, relb_ref[kb, h], bias)
        full = jnp.where(bk >= 0, bias, NEG)
        bias_scr[2 * br] = full
        bias_scr[2 * br + 1] = jnp.where(col >= BLK, full, NEG)

    def run_branch(br, dil, is_first_branch, is_last_branch):
        shift = int(math.log2(dil))
        span = BLK * dil
        stride = None if dil == 1 else dil

        def rows(start):
            return pl.ds(start, BLK, stride=stride) if stride else pl.ds(start, BLK)

        nb = q_ref.shape[1] // span
        run_len = min(nb, ATTN_UNROLL)
        runs_per_it = ATTN_UNROLL // run_len
        runs_per_res = nb // run_len
        starts_at_zero = runs_per_res == 1

        def tasks(it, carry):
            q_starts, firsts, qs_, ks_, vs_ = [], [], [], [], []
            for rr in range(runs_per_it):
                ri = it * runs_per_it + rr
                n0 = (ri % runs_per_res) * run_len
                if dil == 1:
                    base = pl.multiple_of(n0 * span, BLK)
                else:
                    base = n0 * span + ri // runs_per_res
                first = jnp.where(n0 == 0, 1, 0)
                starts = [base + u * span for u in range(run_len)]
                kb = [k_ref[0, rows(st), :].astype(BF16) for st in starts]
                vb = [v_ref[0, rows(st), :].astype(BF16) for st in starts]
                if starts_at_zero:
                    k_prev, v_prev = None, None
                else:
                    p_start = base - span * (1 - first)
                    if dil == 1:
                        p_start = pl.multiple_of(p_start, BLK)
                    k_prev = k_ref[0, rows(p_start), :].astype(BF16)
                    v_prev = v_ref[0, rows(p_start), :].astype(BF16)
                for u, st in enumerate(starts):
                    q_starts.append(st)
                    qs_.append(q_ref[0, rows(st), :].astype(BF16))
                    kp, vp = (k_prev, v_prev) if u == 0 else (kb[u - 1], vb[u - 1])
                    if kp is None:
                        firsts.append(None)
                        ks_.append(kb[u])
                        vs_.append(vb[u])
                    else:
                        firsts.append(first if u == 0 else 0)
                        ks_.append(jnp.concatenate([kp, kb[u]], axis=0))
                        vs_.append(jnp.concatenate([vp, vb[u]], axis=0))
            if not is_first_branch:
                runs = [(m_scr[rows(qs), :], l_scr[rows(qs), :], acc_scr[rows(qs), :]) for qs in q_starts]
            ss = [_bdot_nt(q, k) * ATT_SCALE
                  + (bias_scr[2 * br, :, BLK:] if f is None else bias_scr[2 * br + f])
                  for q, k, f in zip(qs_, ks_, firsts)]
            ms = [jnp.max(s, axis=-1, keepdims=True) for s in ss]
            ps_ = [jnp.exp(s - m) for s, m in zip(ss, ms)]
            accs = [_bdot(p, jnp.concatenate([v, jnp.ones_like(v)], axis=1)) for p, v in zip(ps_, vs_)]
            outs = []
            for u in range(ATTN_UNROLL):
                m_b = jnp.broadcast_to(ms[u], (BLK, HEAD_DIM))
                l_b = accs[u][:, HEAD_DIM:]
                acc_t = accs[u][:, :HEAD_DIM]
                if not is_first_branch:
                    m_run, l_run, acc_run = runs[u]
                    m_new = jnp.maximum(m_run, m_b)
                    a = jnp.exp(m_run - m_new)
                    b = jnp.exp(m_b - m_new)
                    acc_t = a * acc_run + b * acc_t
                    l_b = a * l_run + b * l_b
                    m_b = m_new
                outs.append((m_b, l_b, acc_t))
            for qs, (m_b, l_b, acc_t) in zip(q_starts, outs):
                if is_last_branch:
                    o_ref[0, rows(qs), :] = (acc_t / l_b).astype(o_ref.dtype)
                else:
                    m_scr[rows(qs), :] = m_b
                    l_scr[rows(qs), :] = l_b
                    acc_scr[rows(qs), :] = acc_t
            return carry

        lax.fori_loop(0, nb * dil // ATTN_UNROLL, tasks, 0)

    run_branch(2, 16, True, False)
    run_branch(1, 4, False, False)
    run_branch(0, 1, False, True)


def _attn_prompt(proj3, rel_bias, keep):
    b, s, _ = proj3.shape
    buckets = jnp.asarray(_prompt_bucket_tables())
    blk = (1, s, HEAD_DIM)
    win = pl.BlockSpec((1, keep * HEADS_A, HEAD_DIM), lambda i, h: (i, 0, 0), pipeline_mode=pl.Buffered(1))
    win_shape = jax.ShapeDtypeStruct((b, keep * HEADS_A, HEAD_DIM), F32)
    return pl.pallas_call(
        _attn_prompt_body,
        grid=(b, HEADS_A),
        in_specs=[pl.BlockSpec((3, BLK, 2 * BLK), lambda i, h: (0, 0, 0)),
                  pl.BlockSpec(memory_space=pltpu.SMEM),
                  pl.BlockSpec(blk, lambda i, h: (i, 0, OFF_AQ // HEAD_DIM + h)),
                  pl.BlockSpec(blk, lambda i, h: (i, 0, OFF_AK // HEAD_DIM + h)),
                  pl.BlockSpec(blk, lambda i, h: (i, 0, OFF_AV // HEAD_DIM + h))],
        out_specs=[pl.BlockSpec(blk, lambda i, h: (i, 0, h)), win, win],
        out_shape=[jax.ShapeDtypeStruct((b, s, WIDTH_A), BF16), win_shape, win_shape],
        scratch_shapes=[pltpu.VMEM((6, BLK, 2 * BLK), F32),
                        pltpu.VMEM((s, HEAD_DIM), F32),
                        pltpu.VMEM((s, HEAD_DIM), F32),
                        pltpu.VMEM((s, HEAD_DIM), F32)],
        compiler_params=_params(("arbitrary", "arbitrary"), VMEM_MID),
        name="attn_prompt",
    )(buckets, rel_bias, proj3, proj3, proj3)


def _attn_sample_body(bucket_ref, relbt_ref, q_ref, kn_ref, vn_ref,
                      k1_ref, k4_ref, k16_ref, v1_ref, v4_ref, v16_ref, o_ref, bias_scr):
    relbt = relbt_ref[...]
    tile = (HEADS_A, HEAD_DIM)

    @pl.when(pl.program_id(0) == 0)
    def _():
        for br in range(3):
            bk = bucket_ref[br]
            bias = jnp.zeros((BLK,) + tile, F32)
            for kb in range(N_BUCKETS):
                col = jnp.broadcast_to(relbt[:, kb:kb + 1], tile)
                bias = jnp.where(bk == kb, col[None], bias)
            bias_scr[br] = bias

    def lane_sum(x):
        return jnp.broadcast_to(jnp.sum(x, axis=-1, keepdims=True), x.shape)

    q = q_ref[0]
    s_self = lane_sum(q * kn_ref[0]) * ATT_SCALE + jnp.broadcast_to(relbt[:, 0:1], tile)
    scores = []
    m = s_self
    for br, k_ref in enumerate((k1_ref, k4_ref, k16_ref)):
        s = lane_sum(k_ref[...] * q[None]) * ATT_SCALE + bias_scr[br]
        scores.append(s)
        m = jnp.maximum(m, jnp.max(s, axis=0))
    p_self = 3.0 * jnp.exp(s_self - m)
    l = p_self
    acc = p_self * vn_ref[0]
    for s, v_ref in zip(scores, (v1_ref, v4_ref, v16_ref)):
        p = jnp.exp(s - m[None])
        l = l + jnp.sum(p, axis=0)
        acc = acc + jnp.sum(p * v_ref[...], axis=0)
    o_ref[0] = (acc / l).astype(o_ref.dtype)


def _attn_sample(q, k_new, v_new, cache_k, cache_v, rel_bias):
    b, past = cache_k.shape[:2]
    tile = (HEADS_A, HEAD_DIM)
    buckets = jnp.asarray(np.broadcast_to(_sample_bucket_tables().reshape(3, BLK, 1, 1), (3, BLK) + tile))
    row = pl.BlockSpec((1,) + tile, lambda i: (i, 0, 0))
    views, specs = [], []
    for cache in (cache_k, cache_v):
        for dil in DILATIONS:
            views.append(cache.reshape((b, past // dil, dil) + tile))
            last = past // dil // BLK - 1
            specs.append(pl.BlockSpec((None, BLK, None) + tile,
                                      functools.partial(lambda last, i: (i, last, 0, 0, 0), last)))
    return pl.pallas_call(
        _attn_sample_body,
        grid=(b,),
        in_specs=[pl.BlockSpec((3, BLK) + tile, lambda i: (0, 0, 0, 0)),
                  pl.BlockSpec((HEADS_A, N_BUCKETS), lambda i: (0, 0)),
                  row, row, row] + specs,
        out_specs=row,
        out_shape=jax.ShapeDtypeStruct((b,) + tile, BF16),
        scratch_shapes=[pltpu.VMEM((3, BLK) + tile, F32)],
        compiler_params=_params(("arbitrary",), VMEM_SMALL),
        name="attn_sample",
    )(buckets, rel_bias.T, q, k_new, v_new, *views)


INPROJ_TM = 1024
INPROJ_TN = 1024
NORM_ROWS = 128
ATTN_UNROLL = 8
DN_TT = 256
GROUP = 4
OUTPROJ_TM = 512
OUTPROJ_ROWS = 128
CACHE_ROWS = 1024
FFN_TM = 1024
FFN_TF = 512
FFN_ROWS = 64
FFN_COLS = 512
FFN_PIECE = 256


def _dn_prompt_body(q_ref, k_ref, v_ref, z_ref, h_ref, wgate_ref, cw_ref, cs_ref, s0_ref, alog_ref, dtb_ref, nw_ref,
                    o_ref, s_out_ref,
                    s_scr, e_scr, qn_scr, kn_scr, vv_scr, g_scr, beta_scr,
                    w_scr, u_scr, qg_scr, kdt_scr, attn_scr, gl_scr, o_scr, *, tt):
    t = pl.program_id(1)
    nt = pl.num_programs(1)

    @pl.when(t == 0)
    def _():
        s_scr[...] = s0_ref[0]
        e_scr[5:8, :] = cs_ref[0]

    e_scr[8:8 + tt, 0:WIDTH_BQK] = q_ref[0]
    e_scr[8:8 + tt, WIDTH_BQK:2 * WIDTH_BQK] = k_ref[0]
    e_scr[8:8 + tt, 2 * WIDTH_BQK:CONV_DIM] = v_ref[0]

    def l2n(x):
        return x * lax.rsqrt(jnp.sum(x * x, axis=-1, keepdims=True) + EPS)

    for c0 in range(0, CONV_DIM, DK):
        cols = slice(c0, c0 + DK)
        w = cw_ref[:, cols]
        y = w[0:1, :] * e_scr[5:5 + tt, cols]
        for i in range(1, CONV_W):
            y = y + w[i:i + 1, :] * e_scr[5 + i:5 + i + tt, cols]
        y = _silu(y)
        if c0 < WIDTH_BQK:
            qn_scr[:, cols] = l2n(y) * QK_SCALE
        elif c0 < 2 * WIDTH_BQK:
            kn_scr[:, c0 - WIDTH_BQK:c0 - WIDTH_BQK + DK] = l2n(y)
        else:
            vv_scr[:, c0 - 2 * WIDTH_BQK:c0 - 2 * WIDTH_BQK + DK] = y
    e_scr[5:8, :] = e_scr[tt + 5:tt + 8, :]
    gates = jnp.dot(h_ref[0], _gate_weights(wgate_ref), preferred_element_type=F32)
    beta_scr[...] = _sigmoid(gates)
    g_scr[...] = -jnp.exp(alog_ref[...]) * _softplus(gates + dtb_ref[...])

    ri = lax.broadcasted_iota(jnp.int32, (CHUNK, CHUNK), 0)
    ci = lax.broadcasted_iota(jnp.int32, (CHUNK, CHUNK), 1)
    tri = ri >= ci
    strict = ri > ci
    same_sub = (ri // SUB) == (ci // SUB)
    tril_ones = tri.astype(F32)
    nw = nw_ref[...]

    for c in range(tt // CHUNK):
        rows = slice(c * CHUNK, (c + 1) * CHUNK)
        beta_all = beta_scr[rows, :]
        gc_all = _fdot(tril_ones, g_scr[rows, :])
        gc_all_t = gc_all.T
        g_scr[rows, :] = gc_all
        for hq0 in range(0, QK_HEADS_B, GROUP):
            units = []
            for hq in range(hq0, hq0 + GROUP):
                qn = qn_scr[rows, hq * DK:(hq + 1) * DK]
                kn = kn_scr[rows, hq * DK:(hq + 1) * DK]
                kk = _bdot_nt(kn, kn)
                qk = _bdot_nt(qn, kn)
                for hv in range(2 * hq, 2 * hq + 2):
                    beta = beta_all[:, hv:hv + 1]
                    gc = gc_all[:, V_HEADS_B + hv:V_HEADS_B + hv + 1]
                    gc_row = gc_all_t[V_HEADS_B + hv:V_HEADS_B + hv + 1, :]
                    gc_last = gc_row[:, CHUNK - 1:CHUNK]
                    decay = jnp.exp(jnp.where(tri, gc - gc_row, NEG))
                    a = jnp.where(strict, beta * kk * decay, 0.0)
                    egc = jnp.exp(gc)
                    attn_scr[hv, rows, :] = (qk * decay).astype(BF16)
                    qg_scr[hv, rows, :] = (qn * egc).astype(BF16)
                    kd = kn * jnp.exp(gc_last - gc)
                    kdt_scr[hv, c * DK:(c + 1) * DK, :] = kd.T.astype(BF16)
                    gl_scr[hv, c * 8:(c + 1) * 8, :] = jnp.broadcast_to(jnp.exp(gc_last), (8, DV))
                    units.append((hv, a))
            ds = [jnp.where(same_sub, a, 0.0) for _, a in units]
            ns = [-dd for dd in ds]
            pws = ds
            for _ in range(SUB.bit_length() - 2):
                pws = [_bdot(pw, pw) for pw in pws]
                ns = [n + pw + _bdot(n, pw) for n, pw in zip(ns, pws)]
            ls = [jnp.where(same_sub, 0.0, a) for _, a in units]
            ps = [lo + _bdot(lo, n) for lo, n in zip(ls, ns)]
            ns = [n - (p + _bdot(n, p)) for n, p in zip(ns, ps)]
            xs = []
            for hv, _ in units:
                beta = beta_scr[rows, hv:hv + 1]
                kscale = beta * jnp.exp(g_scr[rows, V_HEADS_B + hv:V_HEADS_B + hv + 1])
                xs.append(jnp.concatenate([kn_scr[rows, (hv // 2) * DK:(hv // 2 + 1) * DK] * kscale,
                                           vv_scr[rows, hv * DV:(hv + 1) * DV] * beta], axis=-1))
            wus = [x + _bdot(n, x) for n, x in zip(ns, xs)]
            for wu, (hv, _) in zip(wus, units):
                w_scr[hv, rows, :] = wu[:, :DK].astype(BF16)
                u_scr[hv, rows, :] = wu[:, DK:]

    heads = range(V_HEADS_B)
    for c in range(tt // CHUNK):
        rows = slice(c * CHUNK, (c + 1) * CHUNK)
        states = [s_scr[hv] for hv in heads]
        states_b = [s.astype(BF16) for s in states]
        v_news = [u_scr[hv, rows, :] - jnp.dot(w_scr[hv, rows, :], states_b[hv], preferred_element_type=F32)
                  for hv in heads]
        v_news_b = [v.astype(BF16) for v in v_news]
        for hv in heads:
            s_scr[hv] = (states[hv] * gl_scr[hv, c * 8:c * 8 + 1, :]
                         + jnp.dot(kdt_scr[hv, c * DK:(c + 1) * DK, :], v_news_b[hv],
                                   preferred_element_type=F32))
        for hv in heads:
            o_scr[rows, hv * DV:(hv + 1) * DV] = (
                jnp.dot(qg_scr[hv, rows, :], states_b[hv], preferred_element_type=F32)
                + jnp.dot(attn_scr[hv, rows, :], v_news_b[hv], preferred_element_type=F32))

    for hv in heads:
        o = o_scr[:, hv * DV:(hv + 1) * DV]
        z = z_ref[0, :, hv * DV:(hv + 1) * DV]
        o = o * lax.rsqrt(jnp.mean(o * o, axis=-1, keepdims=True) + EPS) * nw * _silu(z)
        o_ref[0, :, hv * DV:(hv + 1) * DV] = o.astype(o_ref.dtype)

    @pl.when(t == nt - 1)
    def _():
        s_out_ref[0] = s_scr[...]


def _dn_prompt(proj3, h3, w_gate, conv_state, s0, conv_w, alog_vec, dtb_vec, norm_w, tt):
    b, s, _ = proj3.shape
    body = functools.partial(_dn_prompt_body, tt=tt)
    nh = V_HEADS_B
    in_specs = [
        pl.BlockSpec((1, tt, WIDTH_BQK), lambda i, t: (i, t, OFF_BQ // WIDTH_BQK)),
        pl.BlockSpec((1, tt, WIDTH_BQK), lambda i, t: (i, t, OFF_BK // WIDTH_BQK)),
        pl.BlockSpec((1, tt, WIDTH_BV), lambda i, t: (i, t, OFF_BV // WIDTH_BV)),
        pl.BlockSpec((1, tt, WIDTH_BV), lambda i, t: (i, t, OFF_BZ // WIDTH_BV)),
        pl.BlockSpec((1, tt, D_MODEL), lambda i, t: (i, t, 0)),
        _gate_spec(2),
        pl.BlockSpec((CONV_W, CONV_DIM), lambda i, t: (0, 0)),
        pl.BlockSpec((1, CONV_W - 1, CONV_DIM), lambda i, t: (i, 0, 0)),
        pl.BlockSpec((1, nh, DK, DV), lambda i, t: (i, 0, 0, 0)),
        pl.BlockSpec((1, 128), lambda i, t: (0, 0)),
        pl.BlockSpec((1, 128), lambda i, t: (0, 0)),
        pl.BlockSpec((1, DV), lambda i, t: (0, 0)),
    ]
    return pl.pallas_call(
        body,
        grid=(b, s // tt),
        in_specs=in_specs,
        out_specs=[pl.BlockSpec((1, tt, WIDTH_BV), lambda i, t: (i, t, 0)),
                   pl.BlockSpec((1, nh, DK, DV), lambda i, t: (i, 0, 0, 0))],
        out_shape=[jax.ShapeDtypeStruct((b, s, WIDTH_BV), BF16),
                   jax.ShapeDtypeStruct((b, nh, DK, DV), F32)],
        scratch_shapes=[pltpu.VMEM((nh, DK, DV), F32),
                        pltpu.VMEM((tt + 8, CONV_DIM), F32),
                        pltpu.VMEM((tt, WIDTH_BQK), F32),
                        pltpu.VMEM((tt, WIDTH_BQK), F32),
                        pltpu.VMEM((tt, WIDTH_BV), F32),
                        pltpu.VMEM((tt, 128), F32),
                        pltpu.VMEM((tt, 128), F32),
                        pltpu.VMEM((nh, tt, DK), BF16),
                        pltpu.VMEM((nh, tt, DV), F32),
                        pltpu.VMEM((nh, tt, DK), BF16),
                        pltpu.VMEM((nh, tt // CHUNK * DK, CHUNK), BF16),
                        pltpu.VMEM((nh, tt, CHUNK), BF16),
                        pltpu.VMEM((nh, tt // CHUNK * 8, DV), F32),
                        pltpu.VMEM((tt, WIDTH_BV), F32)],
        compiler_params=_params(("arbitrary", "arbitrary"), VMEM_MID),
        name="deltanet_prompt",
    )(proj3, proj3, proj3, proj3, h3, w_gate, conv_w, conv_state, s0, alog_vec, dtb_vec, norm_w)


def _dn_sample_body(proj_ref, gates_ref, cw_ref, cs_ref, s0_ref, alog_ref, dtb_ref, nw_ref,
                    o_ref, cs_out_ref, s_out_ref):
    pre = proj_ref[0, :, OFF_BQ:OFF_BQ + CONV_DIM]
    buf = cs_ref[0]
    w = cw_ref[...]
    y = w[CONV_W - 1:CONV_W, :] * pre
    for i in range(CONV_W - 1):
        y = y + w[i:i + 1, :] * buf[i:i + 1, :]
    y = _silu(y)
    cs_out_ref[0, 0:CONV_W - 2, :] = buf[1:CONV_W - 1, :]
    cs_out_ref[0, CONV_W - 2:CONV_W - 1, :] = pre

    gates = gates_ref[0]
    beta_all = _sigmoid(gates)
    g_all = -jnp.exp(alog_ref[...]) * _softplus(gates + dtb_ref[...])
    nw = nw_ref[...]

    def l2n(x):
        return x * lax.rsqrt(jnp.sum(x * x, axis=-1, keepdims=True) + EPS)

    row8 = lax.broadcasted_iota(jnp.int32, (8, DK), 0) == 0
    for hv in range(V_HEADS_B):
        hq = hv // 2
        q = l2n(y[:, hq * DK:(hq + 1) * DK]) * QK_SCALE
        k = l2n(y[:, WIDTH_BQK + hq * DK:WIDTH_BQK + (hq + 1) * DK])
        v = y[:, 2 * WIDTH_BQK + hv * DV:2 * WIDTH_BQK + (hv + 1) * DV]
        beta = beta_all[:, hv:hv + 1]
        g = g_all[:, V_HEADS_B + hv:V_HEADS_B + hv + 1]
        eg = jnp.exp(g)
        state = s0_ref[0, hv]

        def pad8(x):
            return jnp.where(row8, jnp.broadcast_to(x, (8, x.shape[-1])), 0.0)

        v_new = v * beta - _bdot(pad8(k * (beta * eg)), state)[0:1, :]
        qk = jnp.sum(q.astype(BF16).astype(F32) * k.astype(BF16).astype(F32), axis=-1, keepdims=True)
        o = _bdot(pad8(q * eg), state)[0:1, :] + qk.astype(BF16).astype(F32) * v_new.astype(BF16).astype(F32)
        s_out_ref[0, hv] = state * eg + _bdot_tn(pad8(k), pad8(v_new))
        z = proj_ref[0, :, OFF_BZ + hv * DV:OFF_BZ + (hv + 1) * DV]
        o = o * lax.rsqrt(jnp.mean(o * o, axis=-1, keepdims=True) + EPS) * nw * _silu(z)
        o_ref[0, :, hv * DV:(hv + 1) * DV] = o.astype(o_ref.dtype)


def _dn_sample(proj, gates, conv_state, s0, conv_w, alog_vec, dtb_vec, norm_w):
    b = proj.shape[0]
    return pl.pallas_call(
        _dn_sample_body,
        grid=(b,),
        in_specs=[pl.BlockSpec((1, 1, PROJ_MAIN), lambda i: (i, 0, 0)),
                  pl.BlockSpec((1, 1, 128), lambda i: (i, 0, 0)),
                  pl.BlockSpec((CONV_W, CONV_DIM), lambda i: (0, 0)),
                  pl.BlockSpec((1, CONV_W - 1, CONV_DIM), lambda i: (i, 0, 0)),
                  pl.BlockSpec((1, V_HEADS_B, DK, DV), lambda i: (i, 0, 0, 0)),
                  pl.BlockSpec((1, 128), lambda i: (0, 0)),
                  pl.BlockSpec((1, 128), lambda i: (0, 0)),
                  pl.BlockSpec((1, DV), lambda i: (0, 0))],
        out_specs=[pl.BlockSpec((1, 1, WIDTH_BV), lambda i: (i, 0, 0)),
                   pl.BlockSpec((1, CONV_W - 1, CONV_DIM), lambda i: (i, 0, 0)),
                   pl.BlockSpec((1, V_HEADS_B, DK, DV), lambda i: (i, 0, 0, 0))],
        out_shape=[jax.ShapeDtypeStruct((b, 1, WIDTH_BV), BF16),
                   jax.ShapeDtypeStruct((b, CONV_W - 1, CONV_DIM), F32),
                   jax.ShapeDtypeStruct((b, V_HEADS_B, DK, DV), F32)],
        compiler_params=_params(("arbitrary",), VMEM_SMALL),
        name="deltanet_sample",
    )(proj, gates, conv_w, conv_state, s0, alog_vec, dtb_vec, norm_w)


def _outproj_body(att_ref, dn_ref, wa_ref, wb_ref, x_ref, lnpost_ref, lnpre_ref, x1_ref, h2_ref):
    tm = x_ref.shape[0]
    piece = min(tm, OUTPROJ_ROWS)
    for rows in [slice(r, r + piece) for r in range(0, tm, piece)]:
        mix = (jnp.dot(att_ref[rows, :], wa_ref[...], preferred_element_type=F32)
               + jnp.dot(dn_ref[rows, :], wb_ref[...], preferred_element_type=F32))
        x1 = x_ref[rows, :] + _rms(mix, lnpost_ref[...])
        x1_ref[rows, :] = x1
        h2_ref[rows, :] = _rms(x1, lnpre_ref[...]).astype(h2_ref.dtype)


def _outproj(att, dn, w, x, ln_post, ln_pre, tm):
    m, d = x.shape
    assert WIDTH_A == WIDTH_BV
    return pl.pallas_call(
        _outproj_body,
        grid=(m // tm,),
        in_specs=[pl.BlockSpec((tm, WIDTH_A), lambda i: (i, 0)),
                  pl.BlockSpec((tm, WIDTH_BV), lambda i: (i, 0)),
                  pl.BlockSpec((WIDTH_A, d), lambda i: (0, 0)),
                  pl.BlockSpec((WIDTH_BV, d), lambda i: (1, 0)),
                  pl.BlockSpec((tm, d), lambda i: (i, 0)),
                  pl.BlockSpec((1, d), lambda i: (0, 0)),
                  pl.BlockSpec((1, d), lambda i: (0, 0))],
        out_specs=[pl.BlockSpec((tm, d), lambda i: (i, 0)),
                   pl.BlockSpec((tm, d), lambda i: (i, 0))],
        out_shape=[jax.ShapeDtypeStruct((m, d), F32),
                   jax.ShapeDtypeStruct((m, d), BF16)],
        compiler_params=_params(("arbitrary",), VMEM_MID),
        name="outproj",
    )(att, dn, w, w, x, ln_post, ln_pre)


def _ffn_body(*refs, tm, tiles_per_seq, single_token):
    if single_token:
        (h_ref, wg_ref, wv_ref, cwb_ref, wo_prev_ref, wo_last_ref, x1_hbm, ln_ref, pg_ref, pv_ref,
         o_ref, ng_ref, nv_ref, eg_scr, ev_scr, carry_scr, act_scr, x1_scr, x1_sem) = refs
    else:
        (h_ref, wg_ref, wv_ref, cwb_ref, wo_prev_ref, wo_last_ref, x1_hbm, ln_ref, prev_ref,
         o_ref, new_ref, eg_scr, ev_scr, carry_scr, act_scr, x1_scr, x1_sem) = refs
    i = pl.program_id(0)
    j = pl.program_id(1)
    nj = pl.num_programs(1)
    d = o_ref.shape[-1]
    tf = act_scr.shape[-1]
    cur = j % 2
    act_cur = act_scr.at[cur]
    act_prev = act_scr.at[1 - cur]

    def x1_copy():
        return pltpu.make_async_copy(x1_hbm.at[pl.ds(pl.multiple_of(i * tm, tm), tm), :], x1_scr, x1_sem)

    def down_proj(act_ref, wo_ref):
        for n in range(0, d, FFN_COLS):
            o_ref[:, n:n + FFN_COLS] += jnp.dot(act_ref[...], wo_ref[:, n:n + FFN_COLS],
                                                preferred_element_type=F32)

    @pl.when(j == 0)
    def _():
        x1_copy().start()
        o_ref[...] = jnp.zeros_like(o_ref)
        act_prev[...] = jnp.zeros_like(act_prev)

    if single_token:
        def up_conv(w_ref, tile, prev_ref, new_ref):
            up = jnp.dot(h_ref[...], w_ref[...], preferred_element_type=F32)
            cw = cwb_ref[tile]
            new_ref[...] = up
            return cw[0:1, :] * prev_ref[0] + cw[1:2, :] * prev_ref[1] + cw[2:3, :] * up + cw[3:4, :]

        gate = up_conv(wg_ref, j, pg_ref, ng_ref)
        val = up_conv(wv_ref, nj + j, pv_ref, nv_ref)
        act_cur[...] = (_gelu_tanh(gate) * val).astype(BF16)
        down_proj(act_prev, wo_prev_ref)
    else:
        first_tile = i % tiles_per_seq == 0
        pieces = [slice(c, c + FFN_PIECE) for c in range(0, tf, FFN_PIECE)]

        def up_proj(cols):
            for w_ref, tile, e_scr, slot in ((wg_ref, j, eg_scr, 0), (wv_ref, nj + j, ev_scr, 1)):
                e_scr[8:8 + tm, cols] = jnp.dot(h_ref[...], w_ref[:, cols], preferred_element_type=F32)
                e_scr[6:8, cols] = jnp.where(first_tile, prev_ref[0, tile, :, cols],
                                             carry_scr[slot, j, 6:8, cols])
                tail = e_scr[tm + 6:tm + 8, cols]
                carry_scr[slot, j, 6:8, cols] = tail
                new_ref[0, tile, :, cols] = tail

        def conv(e_scr, tile, cols, r):
            cw = cwb_ref[tile, :, cols]
            return (cw[0:1, :] * e_scr[6 + r:6 + r + FFN_ROWS, cols]
                    + cw[1:2, :] * e_scr[7 + r:7 + r + FFN_ROWS, cols]
                    + cw[2:3, :] * e_scr[8 + r:8 + r + FFN_ROWS, cols] + cw[3:4, :])

        def conv_geglu(cols):
            for r in range(0, tm, FFN_ROWS):
                act_cur[r:r + FFN_ROWS, cols] = (_gelu_tanh(conv(eg_scr, j, cols, r))
                                                 * conv(ev_scr, nj + j, cols, r)).astype(BF16)

        up_proj(pieces[0])
        for c in range(1, len(pieces)):
            up_proj(pieces[c])
            conv_geglu(pieces[c - 1])
        down_proj(act_prev, wo_prev_ref)
        conv_geglu(pieces[-1])

    @pl.when(j == nj - 1)
    def _():
        x1_copy().wait()
        piece = min(tm, OUTPROJ_ROWS)
        for r in range(0, tm, piece):
            rows = slice(r, r + piece)
            f = o_ref[rows, :] + jnp.dot(act_cur[rows, :], wo_last_ref[...], preferred_element_type=F32)
            o_ref[rows, :] = x1_scr[rows, :] + _rms(f, ln_ref[...])


def _ffn(h2, w_in, conv_w, conv_b, w_out, x1, ln_post, prev, tm, tf, seq_len):
    m, d = h2.shape
    single = seq_len == 1
    nj = D_FF // tf
    tiles_per_seq = 1 if single else seq_len // tm
    cwb = jnp.concatenate([conv_w, conv_b], axis=0).reshape(FFN_CONV_W + 1, 2 * nj, tf).transpose(1, 0, 2)
    cwb_spec = pl.BlockSpec((2 * nj, FFN_CONV_W + 1, tf), lambda i, j: (0, 0, 0))
    if single:
        prev_args = (prev, prev)
        prev_specs = [pl.BlockSpec((2, tm, tf), lambda i, j: (0, i, j)),
                      pl.BlockSpec((2, tm, tf), lambda i, j: (0, i, nj + j))]
        new_specs = [pl.BlockSpec((tm, tf), lambda i, j: (i, j))] * 2
        new_shapes = [jax.ShapeDtypeStruct((m, D_FF), F32)] * 2
    else:
        prev_args = (prev.reshape(-1, 2, 2 * nj, tf).transpose(0, 2, 1, 3),)
        prev_specs = [pl.BlockSpec((1, 2 * nj, 2, tf), lambda i, j: (i // tiles_per_seq, 0, 0, 0))]
        new_specs = [pl.BlockSpec((1, 2 * nj, 2, tf), lambda i, j: (i, 0, 0, 0))]
        new_shapes = [jax.ShapeDtypeStruct((m // tm, 2 * nj, 2, tf), F32)]
    body = functools.partial(_ffn_body, tm=tm, tiles_per_seq=tiles_per_seq, single_token=single)
    once = dict(pipeline_mode=pl.Buffered(1)) if tm >= 1024 else {}
    outs = pl.pallas_call(
        body,
        grid=(m // tm, nj),
        in_specs=[pl.BlockSpec((tm, d), lambda i, j: (i, 0), **once),
                  pl.BlockSpec((d, tf), lambda i, j: (0, j)),
                  pl.BlockSpec((d, tf), lambda i, j: (0, nj + j)),
                  cwb_spec,
                  pl.BlockSpec((tf, d), lambda i, j: (jnp.maximum(j - 1, 0), 0)),
                  pl.BlockSpec((tf, d), lambda i, j: (nj - 1, 0), pipeline_mode=pl.Buffered(1)),
                  pl.BlockSpec(memory_space=pl.ANY),
                  pl.BlockSpec((1, d), lambda i, j: (0, 0))] + prev_specs,
        out_specs=[pl.BlockSpec((tm, d), lambda i, j: (i, 0), **once)] + new_specs,
        out_shape=[jax.ShapeDtypeStruct((m, d), F32)] + new_shapes,
        scratch_shapes=[pltpu.VMEM((tm + 8, tf), F32),
                        pltpu.VMEM((tm + 8, tf), F32),
                        pltpu.VMEM((2, nj, 8, tf), F32),
                        pltpu.VMEM((2, tm, tf), BF16),
                        pltpu.VMEM((tm, d), F32),
                        pltpu.SemaphoreType.DMA(())],
        compiler_params=_params(("arbitrary", "arbitrary"), VMEM_FFN),
        name="convffn",
    )(h2, w_in, w_in, cwb, w_out, w_out, x1, ln_post, *prev_args)
    if single:
        y, new_g, new_v = outs
        return y, jnp.concatenate([new_g, new_v], axis=-1)
    y, new = outs
    return y, new.transpose(0, 2, 1, 3).reshape(m // tm, 2, 2 * D_FF)


def _cache_shift_body(ck_ref, cv_ref, ck_next_ref, cv_next_ref, nk_ref, nv_ref, ok_ref, ov_ref, *, tr):
    last = pl.program_id(1) == pl.num_programs(1) - 1
    for c_ref, nxt_ref, n_ref, o_ref in ((ck_ref, ck_next_ref, nk_ref, ok_ref),
                                         (cv_ref, cv_next_ref, nv_ref, ov_ref)):
        o_ref[0, 0:tr - 1] = c_ref[0, 1:tr]
        o_ref[0, tr - 1] = jnp.where(last, n_ref[0, 0], nxt_ref[0, 0])


def _cache_shift(cache_k, cache_v, new_k, new_v, tr):
    nb, rows, nh, dh = cache_k.shape
    main = pl.BlockSpec((1, tr, nh, dh), lambda b, i: (b, i, 0, 0))
    nxt = pl.BlockSpec((1, 1, nh, dh), lambda b, i: (b, jnp.minimum((i + 1) * tr, rows - 1), 0, 0))
    new = pl.BlockSpec((1, 1, nh, dh), lambda b, i: (b, 0, 0, 0))
    shape = jax.ShapeDtypeStruct(cache_k.shape, cache_k.dtype)
    return pl.pallas_call(
        functools.partial(_cache_shift_body, tr=tr),
        grid=(nb, rows // tr),
        in_specs=[main, main, nxt, nxt, new, new],
        out_specs=[main, main],
        out_shape=[shape, shape],
        compiler_params=_params(("arbitrary", "arbitrary"), VMEM_SMALL),
        name="cache_shift",
    )(cache_k, cache_v, cache_k, cache_v, new_k, new_v)


def _lane_vec(values, offset):
    return jnp.zeros((1, 128), F32).at[0, offset:offset + V_HEADS_B].set(values.astype(F32))


def kernel(x_prompt, x_sample, cache_win_k, cache_win_v, state_dn_conv, state_dn_rec, state_ffn_conv,
           rel_bias, ln_mix_pre, w_in, dn_conv_w, dn_A_log, dn_dt_bias, dn_norm_w, w_out, ln_mix_post,
           ln_ffn_pre, w_ffn_in, ffn_conv_w, ffn_conv_b, w_ffn_out, ln_ffn_post):
    bp, sp, d = x_prompt.shape
    bs = x_sample.shape[0]
    l = 0

    w_main = w_in[l, :, :PROJ_MAIN].astype(BF16)
    wo = w_out[l].astype(BF16)
    wf_in = w_ffn_in[l].astype(BF16)
    wf_out = w_ffn_out[l].astype(BF16)
    ln1 = ln_mix_pre[l][None, :]
    ln2 = ln_mix_post[l][None, :]
    ln3 = ln_ffn_pre[l][None, :]
    ln4 = ln_ffn_post[l][None, :]
    conv_w = dn_conv_w[l]
    alog_vec = _lane_vec(dn_A_log[l], V_HEADS_B)
    dtb_vec = _lane_vec(dn_dt_bias[l], V_HEADS_B)
    norm_w = dn_norm_w[l][None, :]
    fcw = ffn_conv_w[l]
    fcb = ffn_conv_b[l][None, :]

    xp = x_prompt.reshape(bp * sp, d)
    proj_p, hp = _norm_matmul(xp, ln1, w_main, PROJ_MAIN, INPROJ_TM, INPROJ_TN, "inproj_prompt")
    proj3 = proj_p.reshape(bp, sp, PROJ_MAIN)
    keep = min(MAX_DISTANCE, sp)
    att_p, win_k, win_v = _attn_prompt(proj3, rel_bias, keep)
    dn_p, p_dn_rec = _dn_prompt(
        proj3, hp.reshape(bp, sp, d), w_in,
        jnp.zeros((bp, CONV_W - 1, CONV_DIM), F32), jnp.zeros((bp, V_HEADS_B, DK, DV), F32),
        conv_w, alog_vec, dtb_vec, norm_w, DN_TT)
    x1_p, h2_p = _outproj(att_p.reshape(bp * sp, WIDTH_A), dn_p.reshape(bp * sp, WIDTH_BV),
                          wo, xp, ln2, ln3, OUTPROJ_TM)
    y_p, fc = _ffn(h2_p, wf_in, fcw, fcb, wf_out, x1_p, ln4,
                   jnp.zeros((bp, FFN_CONV_W - 1, 2 * D_FF), F32), FFN_TM, FFN_TF, sp)
    p_win_k = win_k.reshape(1, bp, keep, HEADS_A, HEAD_DIM)
    p_win_v = win_v.reshape(1, bp, keep, HEADS_A, HEAD_DIM)
    p_dn_conv = proj3[:, sp - (CONV_W - 1):, OFF_BQ:OFF_BQ + CONV_DIM][None]
    tiles = sp // FFN_TM
    p_ffn_conv = fc[tiles - 1::tiles][None]

    xs = x_sample.reshape(bs, d)
    proj_s, hs = _norm_matmul(xs, ln1, w_main, PROJ_MAIN, bs, INPROJ_TN, "inproj_sample")
    gates_s = _gates(hs, w_in)
    ck = cache_win_k[l]
    cv = cache_win_v[l]
    new_k = proj_s[:, OFF_AK:OFF_AK + WIDTH_A]
    new_v = proj_s[:, OFF_AV:OFF_AV + WIDTH_A]
    new_q = proj_s[:, OFF_AQ:OFF_AQ + WIDTH_A].reshape(bs, HEADS_A, HEAD_DIM)
    new_k = new_k.reshape(bs, HEADS_A, HEAD_DIM)
    new_v = new_v.reshape(bs, HEADS_A, HEAD_DIM)
    att_s = _attn_sample(new_q, new_k, new_v, ck, cv, rel_bias)
    s_win_k, s_win_v = _cache_shift(ck, cv, new_k[:, None], new_v[:, None], CACHE_ROWS)
    dn_s, s_dn_conv, s_dn_rec = _dn_sample(proj_s[:, None], gates_s[:, None], state_dn_conv[l],
                                           state_dn_rec[l], conv_w, alog_vec, dtb_vec, norm_w)
    x1_s, h2_s = _outproj(att_s.reshape(bs, WIDTH_A), dn_s.reshape(bs, WIDTH_BV),
                          wo, xs, ln2, ln3, bs)
    prev_s = jnp.swapaxes(state_ffn_conv[l], 0, 1)
    y_s, up_s = _ffn(h2_s, wf_in, fcw, fcb, wf_out, x1_s, ln4, prev_s, bs, FFN_TF, 1)
    s_ffn_conv = jnp.stack([prev_s[1], up_s], axis=1)[None]

    return (y_p.reshape(bp, sp, d), y_s.reshape(bs, 1, d),
            p_win_k, p_win_v, p_dn_conv, p_dn_rec[None], p_ffn_conv,
            s_win_k[None], s_win_v[None], s_dn_conv[None], s_dn_rec[None], s_ffn_conv)
```

```python
import functools
import math

import numpy as np
import jax
import jax.numpy as jnp
from jax import lax
from jax.experimental import pallas as pl
from jax.experimental.pallas import tpu as pltpu

F32 = jnp.float32
BF16 = jnp.bfloat16

D_MODEL = 2048
HEAD_DIM = 128
WIDTH_A = 1024
HEADS_A = 8
DILATIONS = (1, 4, 16)
BLK = 128
N_BUCKETS = 32
MAX_DISTANCE = 2048
DK = 128
DV = 128
V_HEADS_B = 8
QK_HEADS_B = 4
WIDTH_BQK = 512
WIDTH_BV = 1024
CONV_W = 4
CONV_DIM = 2048
CHUNK = 128
SUB = 64
D_FF = 5632
FFN_CONV_W = 3
EPS = 1e-6
NEG = -1e30
ATT_SCALE = HEAD_DIM ** -0.5
QK_SCALE = DK ** -0.5

OFF_AQ, OFF_AK, OFF_AV = 0, 1024, 2048
OFF_BQ, OFF_BK, OFF_BV, OFF_BZ = 3072, 3584, 4096, 5120
OFF_GATES = 6144
PROJ_MAIN = 6144

MIB = 2 ** 20
VMEM_SMALL = 40
VMEM_MID = 48
VMEM_FFN = 57


def _params(semantics, vmem_mib):
    return pltpu.CompilerParams(dimension_semantics=semantics, vmem_limit_bytes=vmem_mib * MIB)


def _bdot(a, b):
    return jnp.dot(a.astype(BF16), b.astype(BF16), preferred_element_type=F32)


def _bdot_nt(a, b):
    return lax.dot_general(a.astype(BF16), b.astype(BF16), (((1,), (1,)), ((), ())),
                           preferred_element_type=F32)


def _bdot_tn(a, b):
    return lax.dot_general(a.astype(BF16), b.astype(BF16), (((0,), (0,)), ((), ())),
                           preferred_element_type=F32)


def _fdot(a, b):
    return jnp.dot(a, b, preferred_element_type=F32, precision=lax.Precision.HIGHEST)


def _silu(x):
    return x * (1.0 / (1.0 + jnp.exp(-x)))


def _sigmoid(x):
    return 1.0 / (1.0 + jnp.exp(-x))


def _softplus(x):
    return jnp.maximum(x, 0.0) + jnp.log(1.0 + jnp.exp(-jnp.abs(x)))


def _gelu_tanh(x):
    c = math.sqrt(2.0 / math.pi)
    half = 0.5 * x
    return half + half * jnp.tanh(x * (c + (c * 0.044715) * (x * x)))


def _rms(x, w):
    return x * lax.rsqrt(jnp.mean(x * x, axis=-1, keepdims=True) + EPS) * w


def _matmul_body(x_ref, w_ref, o_ref):
    o_ref[...] = jnp.dot(x_ref[...], w_ref[...], preferred_element_type=F32)


def _matmul(x, w, tm, tn, name):
    m, k = x.shape
    n = w.shape[1]
    return pl.pallas_call(
        _matmul_body,
        grid=(n // tn, m // tm),
        in_specs=[pl.BlockSpec((tm, k), lambda j, i: (i, 0)),
                  pl.BlockSpec((k, tn), lambda j, i: (0, j))],
        out_specs=pl.BlockSpec((tm, tn), lambda j, i: (i, j)),
        out_shape=jax.ShapeDtypeStruct((m, n), F32),
        compiler_params=_params(("arbitrary", "arbitrary"), VMEM_MID),
        name=name,
    )(x, w)


def _norm_matmul_body(x_ref, ln_ref, w_ref, o_ref, h_ref):
    tm = x_ref.shape[0]
    piece = min(tm, NORM_ROWS)
    for r in range(0, tm, piece):
        rows = slice(r, r + piece)
        h = _rms(x_ref[rows, :], ln_ref[...]).astype(BF16)
        h_ref[rows, :] = h
        o_ref[rows, :] = jnp.dot(h, w_ref[...], preferred_element_type=F32)


def _norm_matmul(x, ln, w, n, tm, tn, name):
    m, k = x.shape
    return pl.pallas_call(
        _norm_matmul_body,
        grid=(m // tm, n // tn),
        in_specs=[pl.BlockSpec((tm, k), lambda i, j: (i, 0)),
                  pl.BlockSpec((1, k), lambda i, j: (0, 0)),
                  pl.BlockSpec((k, tn), lambda i, j: (0, j))],
        out_specs=[pl.BlockSpec((tm, tn), lambda i, j: (i, j)),
                   pl.BlockSpec((tm, k), lambda i, j: (i, 0))],
        out_shape=[jax.ShapeDtypeStruct((m, n), F32), jax.ShapeDtypeStruct((m, k), BF16)],
        compiler_params=_params(("arbitrary", "arbitrary"), VMEM_MID),
        name=name,
    )(x, ln, w)


def _rel_bucket_np(dist):
    dist = np.asarray(dist, np.int64)
    max_exact = N_BUCKETS // 2
    d = np.maximum(dist, 1).astype(np.float64)
    val = np.log(d / max_exact) / math.log(MAX_DISTANCE / max_exact) * (N_BUCKETS - max_exact)
    frac = np.abs(val - np.round(val))
    near = (frac < 2e-5) &(dist >= max_exact) & (dist != max_exact) & (dist < MAX_DISTANCE)
    assert not near.any(), "distance on a bucket boundary"
    val = np.where(dist == max_exact, 0.0, val)
    large = np.minimum(max_exact + np.trunc(val).astype(np.int64), N_BUCKETS - 1)
    return np.where(dist < max_exact, dist, large).astype(np.int32)


def _prompt_bucket_tables():
    qi = np.arange(BLK)[:, None]
    kj = np.arange(2 * BLK)[None, :]
    delta = BLK + qi - kj
    inwin = (delta >= 0) & (delta <= BLK)
    tabs = []
    for dil in DILATIONS:
        b = _rel_bucket_np(np.clip(delta, 0, BLK) * dil)
        tabs.append(np.where(inwin, b, -1))
    return np.stack(tabs).astype(np.int32)


def _sample_bucket_tables():
    j = BLK - np.arange(BLK)
    return np.stack([_rel_bucket_np(j * dil)[None, :] for dil in DILATIONS]).astype(np.int32)


def _attn_prompt_body(bucket_ref, relb_ref, q_ref, k_ref, v_ref, o_ref, wk_ref, wv_ref,
                      bias_scr, acc_scr, m_scr, l_scr):
    h = pl.program_id(1)
    s, keep = k_ref.shape[1], wk_ref.shape[1] // HEADS_A
    wk_ref[0, pl.ds(h, keep, stride=HEADS_A), :] = k_ref[0, s - keep:s, :]
    wv_ref[0, pl.ds(h, keep, stride=HEADS_A), :] = v_ref[0, s - keep:s, :]
    col = lax.broadcasted_iota(jnp.int32, (BLK, 2 * BLK), 1)
    tables = _prompt_bucket_tables()
    for br in range(3):
        bk = bucket_ref[br]
        bias = jnp.zeros((BLK, 2 * BLK), F32)
        for kb in sorted(set(tables[br].ravel().tolist()) - {-1}):
            bias = jnp.where(bk == kb, relb_ref[kb, h], bias)
        full = jnp.where(bk >= 0, bias, NEG)
        bias_scr[2 * br] = full
        bias_scr[2 * br + 1] = jnp.where(col >= BLK, full, NEG)

    def run_branch(br, dil, is_first_branch, is_last_branch):
        shift = int(math.log2(dil))
        span = BLK * dil
        stride = None if dil == 1 else dil

        def rows(start):
            return pl.ds(start, BLK, stride=stride) if stride else pl.ds(start, BLK)

        nb = q_ref.shape[1] // span
        run_len = min(nb, ATTN_UNROLL)
        runs_per_it = ATTN_UNROLL // run_len
        runs_per_res = nb // run_len
        starts_at_zero = runs_per_res == 1

        def tasks(it, carry):
            q_starts, firsts, qs_, ks_, vs_ = [], [], [], [], []
            for rr in range(runs_per_it):
                ri = it * runs_per_it + rr
                n0 = (ri % runs_per_res) * run_len
                if dil == 1:
                    base = pl.multiple_of(n0 * span, BLK)
                else:
                    base = n0 * span + ri // runs_per_res
                first = jnp.where(n0 == 0, 1, 0)
                starts = [base + u * span for u in range(run_len)]
                kb = [k_ref[0, rows(st), :].astype(BF16) for st in starts]
                vb = [v_ref[0, rows(st), :].astype(BF16) for st in starts]
                if starts_at_zero:
                    k_prev, v_prev = None, None
                else:
                    p_start = base - span * (1 - first)
                    if dil == 1:
                        p_start = pl.multiple_of(p_start, BLK)
                    k_prev = k_ref[0, rows(p_start), :].astype(BF16)
                    v_prev = v_ref[0, rows(p_start), :].astype(BF16)
                for u, st in enumerate(starts):
                    q_starts.append(st)
                    qs_.append(q_ref[0, rows(st), :].astype(BF16))
                    kp, vp = (k_prev, v_prev) if u == 0 else (kb[u - 1], vb[u - 1])
                    if kp is None:
                        firsts.append(None)
                        ks_.append(kb[u])
                        vs_.append(vb[u])
                    else:
                        firsts.append(first if u == 0 else 0)
                        ks_.append(jnp.concatenate([kp, kb[u]], axis=0))
                        vs_.append(jnp.concatenate([vp, vb[u]], axis=0))
            if not is_first_branch:
                runs = [(m_scr[rows(qs), :], l_scr[rows(qs), :], acc_scr[rows(qs), :]) for qs in q_starts]
            ss = [_bdot_nt(q, k) * ATT_SCALE
                  + (bias_scr[2 * br, :, BLK:] if f is None else bias_scr[2 * br + f])
                  for q, k, f in zip(qs_, ks_, firsts)]
            ms = [jnp.max(s, axis=-1, keepdims=True) for s in ss]
            ps_ = [jnp.exp(s - m) for s, m in zip(ss, ms)]
            accs = [_bdot(p, jnp.concatenate([v, jnp.ones_like(v)], axis=1)) for p, v in zip(ps_, vs_)]
            outs = []
            for u in range(ATTN_UNROLL):
                m_b = jnp.broadcast_to(ms[u], (BLK, HEAD_DIM))
                l_b = accs[u][:, HEAD_DIM:]
                acc_t = accs[u][:, :HEAD_DIM]
                if not is_first_branch:
                    m_run, l_run, acc_run = runs[u]
                    m_new = jnp.maximum(m_run, m_b)
                    a = jnp.exp(m_run - m_new)
                    b = jnp.exp(m_b - m_new)
                    acc_t = a * acc_run + b * acc_t
                    l_b = a * l_run + b * l_b
                    m_b = m_new
                outs.append((m_b, l_b, acc_t))
            for qs, (m_b, l_b, acc_t) in zip(q_starts, outs):
                if is_last_branch:
                    o_ref[0, rows(qs), :] = (acc_t / l_b).astype(o_ref.dtype)
                else:
                    m_scr[rows(qs), :] = m_b
                    l_scr[rows(qs), :] = l_b
                    acc_scr[rows(qs), :] = acc_t
            return carry

        lax.fori_loop(0, nb * dil // ATTN_UNROLL, tasks, 0)

    run_branch(2, 16, True, False)
    run_branch(1, 4, False, False)
    run_branch(0, 1, False, True)


def _attn_prompt(proj3, rel_bias, keep):
    b, s, _ = proj3.shape
    buckets = jnp.asarray(_prompt_bucket_tables())
    blk = (1, s, HEAD_DIM)
    win = pl.BlockSpec((1, keep * HEADS_A, HEAD_DIM), lambda i, h: (i, 0, 0), pipeline_mode=pl.Buffered(1))
    win_shape = jax.ShapeDtypeStruct((b, keep * HEADS_A, HEAD_DIM), F32)
    return pl.pallas_call(
        _attn_prompt_body,
        grid=(b, HEADS_A),
        in_specs=[pl.BlockSpec((3, BLK, 2 * BLK), lambda i, h: (0, 0, 0)),
                  pl.BlockSpec(memory_space=pltpu.SMEM),
                  pl.BlockSpec(blk, lambda i, h: (i, 0, OFF_AQ // HEAD_DIM + h)),
                  pl.BlockSpec(blk, lambda i, h: (i, 0, OFF_AK // HEAD_DIM + h)),
                  pl.BlockSpec(blk, lambda i, h: (i, 0, OFF_AV // HEAD_DIM + h))],
        out_specs=[pl.BlockSpec(blk, lambda i, h: (i, 0, h)), win, win],
        out_shape=[jax.ShapeDtypeStruct((b, s, WIDTH_A), BF16), win_shape, win_shape],
        scratch_shapes=[pltpu.VMEM((6, BLK, 2 * BLK), F32),
                        pltpu.VMEM((s, HEAD_DIM), F32),
                        pltpu.VMEM((s, HEAD_DIM), F32),
                        pltpu.VMEM((s, HEAD_DIM), F32)],
        compiler_params=_params(("arbitrary", "arbitrary"), VMEM_MID),
        name="attn_prompt",
    )(buckets, rel_bias, proj3, proj3, proj3)


def _attn_sample_body(bucket_ref, relbt_ref, q_ref, kn_ref, vn_ref,
                      k1_ref, k4_ref, k16_ref, v1_ref, v4_ref, v16_ref, o_ref, bias_scr):
    relbt = relbt_ref[...]
    tile = (HEADS_A, HEAD_DIM)

    @pl.when(pl.program_id(0) == 0)
    def _():
        for br in range(3):
            bk = bucket_ref[br]
            bias = jnp.zeros((BLK,) + tile, F32)
            for kb in range(N_BUCKETS):
                col = jnp.broadcast_to(relbt[:, kb:kb + 1], tile)
                bias = jnp.where(bk == kb, col[None], bias)
            bias_scr[br] = bias

    def lane_sum(x):
        return jnp.broadcast_to(jnp.sum(x, axis=-1, keepdims=True), x.shape)

    q = q_ref[0]
    s_self = lane_sum(q * kn_ref[0]) * ATT_SCALE + jnp.broadcast_to(relbt[:, 0:1], tile)
    scores = []
    m = s_self
    for br, k_ref in enumerate((k1_ref, k4_ref, k16_ref)):
        s = lane_sum(k_ref[...] * q[None]) * ATT_SCALE + bias_scr[br]
        scores.append(s)
        m = jnp.maximum(m, jnp.max(s, axis=0))
    p_self = 3.0 * jnp.exp(s_self - m)
    l = p_self
    acc = p_self * vn_ref[0]
    for s, v_ref in zip(scores, (v1_ref, v4_ref, v16_ref)):
        p = jnp.exp(s - m[None])
        l = l + jnp.sum(p, axis=0)
        acc = acc + jnp.sum(p * v_ref[...], axis=0)
    o_ref[0] = (acc / l).astype(o_ref.dtype)


def _attn_sample(q, k_new, v_new, cache_k, cache_v, rel_bias):
    b, past = cache_k.shape[:2]
    tile = (HEADS_A, HEAD_DIM)
    buckets = jnp.asarray(np.broadcast_to(_sample_bucket_tables().reshape(3, BLK, 1, 1), (3, BLK) + tile))
    row = pl.BlockSpec((1,) + tile, lambda i: (i, 0, 0))
    views, specs = [], []
    for cache in (cache_k, cache_v):
        for dil in DILATIONS:
            views.append(cache.reshape((b, past // dil, dil) + tile))
            last = past // dil // BLK - 1
            specs.append(pl.BlockSpec((None, BLK, None) + tile,
                                      functools.partial(lambda last, i: (i, last, 0, 0, 0), last)))
    return pl.pallas_call(
        _attn_sample_body,
        grid=(b,),
        in_specs=[pl.BlockSpec((3, BLK) + tile, lambda i: (0, 0, 0, 0)),
                  pl.BlockSpec((HEADS_A, N_BUCKETS), lambda i: (0, 0)),
                  row, row, row] + specs,
        out_specs=row,
        out_shape=jax.ShapeDtypeStruct((b,) + tile, BF16),
        scratch_shapes=[pltpu.VMEM((3, BLK) + tile, F32)],
        compiler_params=_params(("arbitrary",), VMEM_SMALL),
        name="attn_sample",
    )(buckets, rel_bias.T, q, k_new, v_new, *views)


INPROJ_TM = 1024
INPROJ_TN = 1024
NORM_ROWS = 128
ATTN_UNROLL = 8
DN_TT = 256
GROUP = 4
OUTPROJ_TM = 512
OUTPROJ_ROWS = 128
CACHE_ROWS = 1024
FFN_TM = 1024
FFN_TF = 512
FFN_ROWS = 64
FFN_COLS = 512
FFN_PIECE = 256


def _dn_prompt_body(q_ref, k_ref, v_ref, z_ref, h_ref, wgate_ref, cw_ref, cs_ref, s0_ref, alog_ref, dtb_ref, nw_ref,
                    o_ref, s_out_ref,
                    s_scr, e_scr, qn_scr, kn_scr, vv_scr, g_scr, beta_scr,
                    w_scr, u_scr, qg_scr, kdt_scr, attn_scr, gl_scr, o_scr, *, tt):
    t = pl.program_id(1)
    nt = pl.num_programs(1)

    @pl.when(t == 0)
    def _():
        s_scr[...] = s0_ref[0]
        e_scr[5:8, :] = cs_ref[0]

    e_scr[8:8 + tt, 0:WIDTH_BQK] = q_ref[0]
    e_scr[8:8 + tt, WIDTH_BQK:2 * WIDTH_BQK] = k_ref[0]
    e_scr[8:8 + tt, 2 * WIDTH_BQK:CONV_DIM] = v_ref[0]

    def l2n(x):
        return x * lax.rsqrt(jnp.sum(x * x, axis=-1, keepdims=True) + EPS)

    for c0 in range(0, CONV_DIM, DK):
        cols = slice(c0, c0 + DK)
        w = cw_ref[:, cols]
        y = w[0:1, :] * e_scr[5:5 + tt, cols]
        for i in range(1, CONV_W):
            y = y + w[i:i + 1, :] * e_scr[5 + i:5 + i + tt, cols]
        y = _silu(y)
        if c0 < WIDTH_BQK:
            qn_scr[:, cols] = l2n(y) * QK_SCALE
        elif c0 < 2 * WIDTH_BQK:
            kn_scr[:, c0 - WIDTH_BQK:c0 - WIDTH_BQK + DK] = l2n(y)
        else:
            vv_scr[:, c0 - 2 * WIDTH_BQK:c0 - 2 * WIDTH_BQK + DK] = y
    e_scr[5:8, :] = e_scr[tt + 5:tt + 8, :]
    gates = jnp.dot(h_ref[0], wgate_ref[...], preferred_element_type=F32)
    beta_scr[...] = _sigmoid(gates)
    g_scr[...] = -jnp.exp(alog_ref[...]) * _softplus(gates + dtb_ref[...])

    ri = lax.broadcasted_iota(jnp.int32, (CHUNK, CHUNK), 0)
    ci = lax.broadcasted_iota(jnp.int32, (CHUNK, CHUNK), 1)
    tri = ri >= ci
    strict = ri > ci
    same_sub = (ri // SUB) == (ci // SUB)
    tril_ones = tri.astype(F32)
    nw = nw_ref[...]

    for c in range(tt // CHUNK):
        rows = slice(c * CHUNK, (c + 1) * CHUNK)
        beta_all = beta_scr[rows, :]
        gc_all = _fdot(tril_ones, g_scr[rows, :])
        gc_all_t = gc_all.T
        g_scr[rows, :] = gc_all
        for hq0 in range(0, QK_HEADS_B, GROUP):
            units = []
            for hq in range(hq0, hq0 + GROUP):
                qn = qn_scr[rows, hq * DK:(hq + 1) * DK]
                kn = kn_scr[rows, hq * DK:(hq + 1) * DK]
                kk = _bdot_nt(kn, kn)
                qk = _bdot_nt(qn, kn)
                for hv in range(2 * hq, 2 * hq + 2):
                    beta = beta_all[:, hv:hv + 1]
                    gc = gc_all[:, V_HEADS_B + hv:V_HEADS_B + hv + 1]
                    gc_row = gc_all_t[V_HEADS_B + hv:V_HEADS_B + hv + 1, :]
                    gc_last = gc_row[:, CHUNK - 1:CHUNK]
                    decay = jnp.exp(jnp.where(tri, gc - gc_row, NEG))
                    a = jnp.where(strict, beta * kk * decay, 0.0)
                    egc = jnp.exp(gc)
                    attn_scr[hv, rows, :] = (qk * decay).astype(BF16)
                    qg_scr[hv, rows, :] = (qn * egc).astype(BF16)
                    kd = kn * jnp.exp(gc_last - gc)
                    kdt_scr[hv, c * DK:(c + 1) * DK, :] = kd.T.astype(BF16)
                    gl_scr[hv, c * 8:(c + 1) * 8, :] = jnp.broadcast_to(jnp.exp(gc_last), (8, DV))
                    units.append((hv, a))
            ds = [jnp.where(same_sub, a, 0.0) for _, a in units]
            ns = [-dd for dd in ds]
            pws = ds
            for _ in range(SUB.bit_length() - 2):
                pws = [_bdot(pw, pw) for pw in pws]
                ns = [n + pw + _bdot(n, pw) for n, pw in zip(ns, pws)]
            ls = [jnp.where(same_sub, 0.0, a) for _, a in units]
            ps = [lo + _bdot(lo, n) for lo, n in zip(ls, ns)]
            ns = [n - (p + _bdot(n, p)) for n, p in zip(ns, ps)]
            xs = []
            for hv, _ in units:
                beta = beta_scr[rows, hv:hv + 1]
                kscale = beta * jnp.exp(g_scr[rows, V_HEADS_B + hv:V_HEADS_B + hv + 1])
                xs.append(jnp.concatenate([kn_scr[rows, (hv // 2) * DK:(hv // 2 + 1) * DK] * kscale,
                                           vv_scr[rows, hv * DV:(hv + 1) * DV] * beta], axis=-1))
            wus = [x + _bdot(n, x) for n, x in zip(ns, xs)]
            for wu, (hv, _) in zip(wus, units):
                w_scr[hv, rows, :] = wu[:, :DK].astype(BF16)
                u_scr[hv, rows, :] = wu[:, DK:]

    heads = range(V_HEADS_B)
    for c in range(tt // CHUNK):
        rows = slice(c * CHUNK, (c + 1) * CHUNK)
        states = [s_scr[hv] for hv in heads]
        states_b = [s.astype(BF16) for s in states]
        v_news = [u_scr[hv, rows, :] - jnp.dot(w_scr[hv, rows, :], states_b[hv], preferred_element_type=F32)
                  for hv in heads]
        v_news_b = [v.astype(BF16) for v in v_news]
        for hv in heads:
            s_scr[hv] = (states[hv] * gl_scr[hv, c * 8:c * 8 + 1, :]
                         + jnp.dot(kdt_scr[hv, c * DK:(c + 1) * DK, :], v_news_b[hv],
                                   preferred_element_type=F32))
        for hv in heads:
            o_scr[rows, hv * DV:(hv + 1) * DV] = (
                jnp.dot(qg_scr[hv, rows, :], states_b[hv], preferred_element_type=F32)
                + jnp.dot(attn_scr[hv, rows, :], v_news_b[hv], preferred_element_type=F32))

    for hv in heads:
        o = o_scr[:, hv * DV:(hv + 1) * DV]
        z = z_ref[0, :, hv * DV:(hv + 1) * DV]
        o = o * lax.rsqrt(jnp.mean(o * o, axis=-1, keepdims=True) + EPS) * nw * _silu(z)
        o_ref[0, :, hv * DV:(hv + 1) * DV] = o.astype(o_ref.dtype)

    @pl.when(t == nt - 1)
    def _():
        s_out_ref[0] = s_scr[...]


def _dn_prompt(proj3, h3, w_gate, conv_state, s0, conv_w, alog_vec, dtb_vec, norm_w, tt):
    b, s, _ = proj3.shape
    body = functools.partial(_dn_prompt_body, tt=tt)
    nh = V_HEADS_B
    in_specs = [
        pl.BlockSpec((1, tt, WIDTH_BQK), lambda i, t: (i, t, OFF_BQ // WIDTH_BQK)),
        pl.BlockSpec((1, tt, WIDTH_BQK), lambda i, t: (i, t, OFF_BK // WIDTH_BQK)),
        pl.BlockSpec((1, tt, WIDTH_BV), lambda i, t: (i, t, OFF_BV // WIDTH_BV)),
        pl.BlockSpec((1, tt, WIDTH_BV), lambda i, t: (i, t, OFF_BZ // WIDTH_BV)),
        pl.BlockSpec((1, tt, D_MODEL), lambda i, t: (i, t, 0)),
        pl.BlockSpec((D_MODEL, 128), lambda i, t: (0, 0)),
        pl.BlockSpec((CONV_W, CONV_DIM), lambda i, t: (0, 0)),
        pl.BlockSpec((1, CONV_W - 1, CONV_DIM), lambda i, t: (i, 0, 0)),
        pl.BlockSpec((1, nh, DK, DV), lambda i, t: (i, 0, 0, 0)),
        pl.BlockSpec((1, 128), lambda i, t: (0, 0)),
        pl.BlockSpec((1, 128), lambda i, t: (0, 0)),
        pl.BlockSpec((1, DV), lambda i, t: (0, 0)),
    ]
    return pl.pallas_call(
        body,
        grid=(b, s // tt),
        in_specs=in_specs,
        out_specs=[pl.BlockSpec((1, tt, WIDTH_BV), lambda i, t: (i, t, 0)),
                   pl.BlockSpec((1, nh, DK, DV), lambda i, t: (i, 0, 0, 0))],
        out_shape=[jax.ShapeDtypeStruct((b, s, WIDTH_BV), BF16),
                   jax.ShapeDtypeStruct((b, nh, DK, DV), F32)],
        scratch_shapes=[pltpu.VMEM((nh, DK, DV), F32),
                        pltpu.VMEM((tt + 8, CONV_DIM), F32),
                        pltpu.VMEM((tt, WIDTH_BQK), F32),
                        pltpu.VMEM((tt, WIDTH_BQK), F32),
                        pltpu.VMEM((tt, WIDTH_BV), F32),
                        pltpu.VMEM((tt, 128), F32),
                        pltpu.VMEM((tt, 128), F32),
                        pltpu.VMEM((nh, tt, DK), BF16),
                        pltpu.VMEM((nh, tt, DV), F32),
                        pltpu.VMEM((nh, tt, DK), BF16),
                        pltpu.VMEM((nh, tt // CHUNK * DK, CHUNK), BF16),
                        pltpu.VMEM((nh, tt, CHUNK), BF16),
                        pltpu.VMEM((nh, tt // CHUNK * 8, DV), F32),
                        pltpu.VMEM((tt, WIDTH_BV), F32)],
        compiler_params=_params(("arbitrary", "arbitrary"), VMEM_MID),
        name="deltanet_prompt",
    )(proj3, proj3, proj3, proj3, h3, w_gate, conv_w, conv_state, s0, alog_vec, dtb_vec, norm_w)


def _dn_sample_body(proj_ref, gates_ref, cw_ref, cs_ref, s0_ref, alog_ref, dtb_ref, nw_ref,
                    o_ref, cs_out_ref, s_out_ref):
    pre = proj_ref[0, :, OFF_BQ:OFF_BQ + CONV_DIM]
    buf = cs_ref[0]
    w = cw_ref[...]
    y = w[CONV_W - 1:CONV_W, :] * pre
    for i in range(CONV_W - 1):
        y = y + w[i:i + 1, :] * buf[i:i + 1, :]
    y = _silu(y)
    cs_out_ref[0, 0:CONV_W - 2, :] = buf[1:CONV_W - 1, :]
    cs_out_ref[0, CONV_W - 2:CONV_W - 1, :] = pre

    gates = gates_ref[0]
    beta_all = _sigmoid(gates)
    g_all = -jnp.exp(alog_ref[...]) * _softplus(gates + dtb_ref[...])
    nw = nw_ref[...]

    def l2n(x):
        return x * lax.rsqrt(jnp.sum(x * x, axis=-1, keepdims=True) + EPS)

    row8 = lax.broadcasted_iota(jnp.int32, (8, DK), 0) == 0
    for hv in range(V_HEADS_B):
        hq = hv // 2
        q = l2n(y[:, hq * DK:(hq + 1) * DK]) * QK_SCALE
        k = l2n(y[:, WIDTH_BQK + hq * DK:WIDTH_BQK + (hq + 1) * DK])
        v = y[:, 2 * WIDTH_BQK + hv * DV:2 * WIDTH_BQK + (hv + 1) * DV]
        beta = beta_all[:, hv:hv + 1]
        g = g_all[:, V_HEADS_B + hv:V_HEADS_B + hv + 1]
        eg = jnp.exp(g)
        state = s0_ref[0, hv]

        def pad8(x):
            return jnp.where(row8, jnp.broadcast_to(x, (8, x.shape[-1])), 0.0)

        v_new = v * beta - _bdot(pad8(k * (beta * eg)), state)[0:1, :]
        qk = jnp.sum(q.astype(BF16).astype(F32) * k.astype(BF16).astype(F32), axis=-1, keepdims=True)
        o = _bdot(pad8(q * eg), state)[0:1, :] + qk.astype(BF16).astype(F32) * v_new.astype(BF16).astype(F32)
        s_out_ref[0, hv] = state * eg + _bdot_tn(pad8(k), pad8(v_new))
        z = proj_ref[0, :, OFF_BZ + hv * DV:OFF_BZ + (hv + 1) * DV]
        o = o * lax.rsqrt(jnp.mean(o * o, axis=-1, keepdims=True) + EPS) * nw * _silu(z)
        o_ref[0, :, hv * DV:(hv + 1) * DV] = o.astype(o_ref.dtype)


def _dn_sample(proj, gates, conv_state, s0, conv_w, alog_vec, dtb_vec, norm_w):
    b = proj.shape[0]
    return pl.pallas_call(
        _dn_sample_body,
        grid=(b,),
        in_specs=[pl.BlockSpec((1, 1, PROJ_MAIN), lambda i: (i, 0, 0)),
                  pl.BlockSpec((1, 1, 128), lambda i: (i, 0, 0)),
                  pl.BlockSpec((CONV_W, CONV_DIM), lambda i: (0, 0)),
                  pl.BlockSpec((1, CONV_W - 1, CONV_DIM), lambda i: (i, 0, 0)),
                  pl.BlockSpec((1, V_HEADS_B, DK, DV), lambda i: (i, 0, 0, 0)),
                  pl.BlockSpec((1, 128), lambda i: (0, 0)),
                  pl.BlockSpec((1, 128), lambda i: (0, 0)),
                  pl.BlockSpec((1, DV), lambda i: (0, 0))],
        out_specs=[pl.BlockSpec((1, 1, WIDTH_BV), lambda i: (i, 0, 0)),
                   pl.BlockSpec((1, CONV_W - 1, CONV_DIM), lambda i: (i, 0, 0)),
                   pl.BlockSpec((1, V_HEADS_B, DK, DV), lambda i: (i, 0, 0, 0))],
        out_shape=[jax.ShapeDtypeStruct((b, 1, WIDTH_BV), BF16),
                   jax.ShapeDtypeStruct((b, CONV_W - 1, CONV_DIM), F32),
                   jax.ShapeDtypeStruct((b, V_HEADS_B, DK, DV), F32)],
        compiler_params=_params(("arbitrary",), VMEM_SMALL),
        name="deltanet_sample",
    )(proj, gates, conv_w, conv_state, s0, alog_vec, dtb_vec, norm_w)


def _outproj_body(att_ref, dn_ref, wa_ref, wb_ref, x_ref, lnpost_ref, lnpre_ref, x1_ref, h2_ref):
    tm = x_ref.shape[0]
    piece = min(tm, OUTPROJ_ROWS)
    for rows in [slice(r, r + piece) for r in range(0, tm, piece)]:
        mix = (jnp.dot(att_ref[rows, :], wa_ref[...], preferred_element_type=F32)
               + jnp.dot(dn_ref[rows, :], wb_ref[...], preferred_element_type=F32))
        x1 = x_ref[rows, :] + _rms(mix, lnpost_ref[...])
        x1_ref[rows, :] = x1
        h2_ref[rows, :] = _rms(x1, lnpre_ref[...]).astype(h2_ref.dtype)


def _outproj(att, dn, w, x, ln_post, ln_pre, tm):
    m, d = x.shape
    assert WIDTH_A == WIDTH_BV
    return pl.pallas_call(
        _outproj_body,
        grid=(m // tm,),
        in_specs=[pl.BlockSpec((tm, WIDTH_A), lambda i: (i, 0)),
                  pl.BlockSpec((tm, WIDTH_BV), lambda i: (i, 0)),
                  pl.BlockSpec((WIDTH_A, d), lambda i: (0, 0)),
                  pl.BlockSpec((WIDTH_BV, d), lambda i: (1, 0)),
                  pl.BlockSpec((tm, d), lambda i: (i, 0)),
                  pl.BlockSpec((1, d), lambda i: (0, 0)),
                  pl.BlockSpec((1, d), lambda i: (0, 0))],
        out_specs=[pl.BlockSpec((tm, d), lambda i: (i, 0)),
                   pl.BlockSpec((tm, d), lambda i: (i, 0))],
        out_shape=[jax.ShapeDtypeStruct((m, d), F32),
                   jax.ShapeDtypeStruct((m, d), BF16)],
        compiler_params=_params(("arbitrary",), VMEM_MID),
        name="outproj",
    )(att, dn, w, w, x, ln_post, ln_pre)


def _ffn_body(*refs, tm, tiles_per_seq, single_token):
    if single_token:
        (h_ref, wg_ref, wv_ref, cwb_ref, wo_prev_ref, wo_last_ref, x1_hbm, ln_ref, pg_ref, pv_ref,
         o_ref, ng_ref, nv_ref, eg_scr, ev_scr, carry_scr, act_scr, x1_scr, x1_sem) = refs
    else:
        (h_ref, wg_ref, wv_ref, cwb_ref, wo_prev_ref, wo_last_ref, x1_hbm, ln_ref, prev_ref,
         o_ref, new_ref, eg_scr, ev_scr, carry_scr, act_scr, x1_scr, x1_sem) = refs
    i = pl.program_id(0)
    j = pl.program_id(1)
    nj = pl.num_programs(1)
    d = o_ref.shape[-1]
    tf = act_scr.shape[-1]
    cur = j % 2
    act_cur = act_scr.at[cur]
    act_prev = act_scr.at[1 - cur]

    def x1_copy():
        return pltpu.make_async_copy(x1_hbm.at[pl.ds(pl.multiple_of(i * tm, tm), tm), :], x1_scr, x1_sem)

    def down_proj(act_ref, wo_ref):
        for n in range(0, d, FFN_COLS):
            o_ref[:, n:n + FFN_COLS] += jnp.dot(act_ref[...], wo_ref[:, n:n + FFN_COLS],
                                                preferred_element_type=F32)

    @pl.when(j == 0)
    def _():
        x1_copy().start()
        o_ref[...] = jnp.zeros_like(o_ref)
        act_prev[...] = jnp.zeros_like(act_prev)

    if single_token:
        def up_conv(w_ref, tile, prev_ref, new_ref):
            up = jnp.dot(h_ref[...], w_ref[...], preferred_element_type=F32)
            cw = cwb_ref[tile]
            new_ref[...] = up
            return cw[0:1, :] * prev_ref[0] + cw[1:2, :] * prev_ref[1] + cw[2:3, :] * up + cw[3:4, :]

        gate = up_conv(wg_ref, j, pg_ref, ng_ref)
        val = up_conv(wv_ref, nj + j, pv_ref, nv_ref)
        act_cur[...] = (_gelu_tanh(gate) * val).astype(BF16)
        down_proj(act_prev, wo_prev_ref)
    else:
        first_tile = i % tiles_per_seq == 0
        pieces = [slice(c, c + FFN_PIECE) for c in range(0, tf, FFN_PIECE)]

        def up_proj(cols):
            for w_ref, tile, e_scr, slot in ((wg_ref, j, eg_scr, 0), (wv_ref, nj + j, ev_scr, 1)):
                e_scr[8:8 + tm, cols] = jnp.dot(h_ref[...], w_ref[:, cols], preferred_element_type=F32)
                e_scr[6:8, cols] = jnp.where(first_tile, prev_ref[0, tile, :, cols],
                                             carry_scr[slot, j, 6:8, cols])
                tail = e_scr[tm + 6:tm + 8, cols]
                carry_scr[slot, j, 6:8, cols] = tail
                new_ref[0, tile, :, cols] = tail

        def conv(e_scr, tile, cols, r):
            cw = cwb_ref[tile, :, cols]
            return (cw[0:1, :] * e_scr[6 + r:6 + r + FFN_ROWS, cols]
                    + cw[1:2, :] * e_scr[7 + r:7 + r + FFN_ROWS, cols]
                    + cw[2:3, :] * e_scr[8 + r:8 + r + FFN_ROWS, cols] + cw[3:4, :])

        def conv_geglu(cols):
            for r in range(0, tm, FFN_ROWS):
                act_cur[r:r + FFN_ROWS, cols] = (_gelu_tanh(conv(eg_scr, j, cols, r))
                                                 * conv(ev_scr, nj + j, cols, r)).astype(BF16)

        up_proj(pieces[0])
        for c in range(1, len(pieces)):
            up_proj(pieces[c])
            conv_geglu(pieces[c - 1])
        down_proj(act_prev, wo_prev_ref)
        conv_geglu(pieces[-1])

    @pl.when(j == nj - 1)
    def _():
        x1_copy().wait()
        piece = min(tm, OUTPROJ_ROWS)
        for r in range(0, tm, piece):
            rows = slice(r, r + piece)
            f = o_ref[rows, :] + jnp.dot(act_cur[rows, :], wo_last_ref[...], preferred_element_type=F32)
            o_ref[rows, :] = x1_scr[rows, :] + _rms(f, ln_ref[...])


def _ffn(h2, w_in, conv_w, conv_b, w_out, x1, ln_post, prev, tm, tf, seq_len):
    m, d = h2.shape
    single = seq_len == 1
    nj = D_FF // tf
    tiles_per_seq = 1 if single else seq_len // tm
    cwb = jnp.concatenate([conv_w, conv_b], axis=0).reshape(FFN_CONV_W + 1, 2 * nj, tf).transpose(1, 0, 2)
    cwb_spec = pl.BlockSpec((2 * nj, FFN_CONV_W + 1, tf), lambda i, j: (0, 0, 0))
    if single:
        prev_args = (prev, prev)
        prev_specs = [pl.BlockSpec((2, tm, tf), lambda i, j: (0, i, j)),
                      pl.BlockSpec((2, tm, tf), lambda i, j: (0, i, nj + j))]
        new_specs = [pl.BlockSpec((tm, tf), lambda i, j: (i, j))] * 2
        new_shapes = [jax.ShapeDtypeStruct((m, D_FF), F32)] * 2
    else:
        prev_args = (prev.reshape(-1, 2, 2 * nj, tf).transpose(0, 2, 1, 3),)
        prev_specs = [pl.BlockSpec((1, 2 * nj, 2, tf), lambda i, j: (i // tiles_per_seq, 0, 0, 0))]
        new_specs = [pl.BlockSpec((1, 2 * nj, 2, tf), lambda i, j: (i, 0, 0, 0))]
        new_shapes = [jax.ShapeDtypeStruct((m // tm, 2 * nj, 2, tf), F32)]
    body = functools.partial(_ffn_body, tm=tm, tiles_per_seq=tiles_per_seq, single_token=single)
    once = dict(pipeline_mode=pl.Buffered(1)) if tm >= 1024 else {}
    outs = pl.pallas_call(
        body,
        grid=(m // tm, nj),
        in_specs=[pl.BlockSpec((tm, d), lambda i, j: (i, 0), **once),
                  pl.BlockSpec((d, tf), lambda i, j: (0, j)),
                  pl.BlockSpec((d, tf), lambda i, j: (0, nj + j)),
                  cwb_spec,
                  pl.BlockSpec((tf, d), lambda i, j: (jnp.maximum(j - 1, 0), 0)),
                  pl.BlockSpec((tf, d), lambda i, j: (nj - 1, 0), pipeline_mode=pl.Buffered(1)),
                  pl.BlockSpec(memory_space=pl.ANY),
                  pl.BlockSpec((1, d), lambda i, j: (0, 0))] + prev_specs,
        out_specs=[pl.BlockSpec((tm, d), lambda i, j: (i, 0), **once)] + new_specs,
        out_shape=[jax.ShapeDtypeStruct((m, d), F32)] + new_shapes,
        scratch_shapes=[pltpu.VMEM((tm + 8, tf), F32),
                        pltpu.VMEM((tm + 8, tf), F32),
                        pltpu.VMEM((2, nj, 8, tf), F32),
                        pltpu.VMEM((2, tm, tf), BF16),
                        pltpu.VMEM((tm, d), F32),
                        pltpu.SemaphoreType.DMA(())],
        compiler_params=_params(("arbitrary", "arbitrary"), VMEM_FFN),
        name="convffn",
    )(h2, w_in, w_in, cwb, w_out, w_out, x1, ln_post, *prev_args)
    if single:
        y, new_g, new_v = outs
        return y, jnp.concatenate([new_g, new_v], axis=-1)
    y, new = outs
    return y, new.transpose(0, 2, 1, 3).reshape(m // tm, 2, 2 * D_FF)


CACHE_BUFS = 4


def _cache_shift_body(ck_ref, cv_ref, nk_ref, nv_ref, ok_ref, ov_ref, buf, sem_in, sem_new, sem_out, *, tr):
    nb, rows = ck_ref.shape[0], ck_ref.shape[1]
    tasks = [(c_ref, n_ref, o_ref, b, r0)
             for c_ref, n_ref, o_ref in ((ck_ref, nk_ref, ok_ref), (cv_ref, nv_ref, ov_ref))
             for b in range(nb) for r0 in range(0, rows, tr)]

    def reads(t):
        c_ref, n_ref, _, b, r0 = tasks[t]
        slot = t % CACHE_BUFS
        n = min(tr, rows - 1 - r0)
        cps = [pltpu.make_async_copy(c_ref.at[b, pl.ds(r0 + 1, n)], buf.at[slot, pl.ds(0, n)], sem_in.at[slot])]
        if n < tr:
            cps.append(pltpu.make_async_copy(n_ref.at[b], buf.at[slot, pl.ds(n, 1)], sem_new.at[slot]))
        return cps

    def write(t):
        _, _, o_ref, b, r0 = tasks[t]
        slot = t % CACHE_BUFS
        return pltpu.make_async_copy(buf.at[slot], o_ref.at[b, pl.ds(r0, tr)], sem_out.at[slot])

    for t in range(min(CACHE_BUFS, len(tasks))):
        for cp in reads(t):
            cp.start()
    for t in range(len(tasks)):
        for cp in reads(t):
            cp.wait()
        write(t).start()
        if t >= 1:
            write(t - 1).wait()
            if t - 1 + CACHE_BUFS < len(tasks):
                for cp in reads(t - 1 + CACHE_BUFS):
                    cp.start()
    write(len(tasks) - 1).wait()


def _cache_shift(cache_k, cache_v, new_k, new_v, tr):
    nb, rows, nh, dh = cache_k.shape
    any_spec = pl.BlockSpec(memory_space=pl.ANY)
    shape = jax.ShapeDtypeStruct(cache_k.shape, cache_k.dtype)
    return pl.pallas_call(
        functools.partial(_cache_shift_body, tr=tr),
        in_specs=[any_spec] * 4,
        out_specs=[any_spec, any_spec],
        out_shape=[shape, shape],
        scratch_shapes=[pltpu.VMEM((CACHE_BUFS, tr, nh, dh), cache_k.dtype),
                        pltpu.SemaphoreType.DMA((CACHE_BUFS,)),
                        pltpu.SemaphoreType.DMA((CACHE_BUFS,)),
                        pltpu.SemaphoreType.DMA((CACHE_BUFS,))],
        compiler_params=pltpu.CompilerParams(vmem_limit_bytes=VMEM_SMALL * MIB),
        name="cache_shift",
    )(cache_k, cache_v, new_k, new_v)


def _lane_vec(values, offset):
    return jnp.zeros((1, 128), F32).at[0, offset:offset + V_HEADS_B].set(values.astype(F32))


def kernel(x_prompt, x_sample, cache_win_k, cache_win_v, state_dn_conv, state_dn_rec, state_ffn_conv,
           rel_bias, ln_mix_pre, w_in, dn_conv_w, dn_A_log, dn_dt_bias, dn_norm_w, w_out, ln_mix_post,
           ln_ffn_pre, w_ffn_in, ffn_conv_w, ffn_conv_b, w_ffn_out, ln_ffn_post):
    bp, sp, d = x_prompt.shape
    bs = x_sample.shape[0]
    l = 0

    w_main = w_in[l].astype(BF16)
    w_gate = jnp.pad(w_main[:, PROJ_MAIN:], ((0, 0), (0, 128 - 2 * V_HEADS_B)))
    wo = w_out[l].astype(BF16)
    wf_in = w_ffn_in[l].astype(BF16)
    wf_out = w_ffn_out[l].astype(BF16)
    ln1 = ln_mix_pre[l][None, :]
    ln2 = ln_mix_post[l][None, :]
    ln3 = ln_ffn_pre[l][None, :]
    ln4 = ln_ffn_post[l][None, :]
    conv_w = dn_conv_w[l]
    alog_vec = _lane_vec(dn_A_log[l], V_HEADS_B)
    dtb_vec = _lane_vec(dn_dt_bias[l], V_HEADS_B)
    norm_w = dn_norm_w[l][None, :]
    fcw = ffn_conv_w[l]
    fcb = ffn_conv_b[l][None, :]

    xp = x_prompt.reshape(bp * sp, d)
    proj_p, hp = _norm_matmul(xp, ln1, w_main, PROJ_MAIN, INPROJ_TM, INPROJ_TN, "inproj_prompt")
    proj3 = proj_p.reshape(bp, sp, PROJ_MAIN)
    keep = min(MAX_DISTANCE, sp)
    att_p, win_k, win_v = _attn_prompt(proj3, rel_bias, keep)
    dn_p, p_dn_rec = _dn_prompt(
        proj3, hp.reshape(bp, sp, d), w_gate,
        jnp.zeros((bp, CONV_W - 1, CONV_DIM), F32), jnp.zeros((bp, V_HEADS_B, DK, DV), F32),
        conv_w, alog_vec, dtb_vec, norm_w, DN_TT)
    x1_p, h2_p = _outproj(att_p.reshape(bp * sp, WIDTH_A), dn_p.reshape(bp * sp, WIDTH_BV),
                          wo, xp, ln2, ln3, OUTPROJ_TM)
    y_p, fc = _ffn(h2_p, wf_in, fcw, fcb, wf_out, x1_p, ln4,
                   jnp.zeros((bp, FFN_CONV_W - 1, 2 * D_FF), F32), FFN_TM, FFN_TF, sp)
    p_win_k = win_k.reshape(1, bp, keep, HEADS_A, HEAD_DIM)
    p_win_v = win_v.reshape(1, bp, keep, HEADS_A, HEAD_DIM)
    p_dn_conv = proj3[:, sp - (CONV_W - 1):, OFF_BQ:OFF_BQ + CONV_DIM][None]
    tiles = sp // FFN_TM
    p_ffn_conv = fc[tiles - 1::tiles][None]

    xs = x_sample.reshape(bs, d)
    proj_s, hs = _norm_matmul(xs, ln1, w_main, PROJ_MAIN, bs, INPROJ_TN, "inproj_sample")
    gates_s = _matmul(hs, w_gate, bs, 128, "gates_sample")
    ck = cache_win_k[l]
    cv = cache_win_v[l]
    new_k = proj_s[:, OFF_AK:OFF_AK + WIDTH_A]
    new_v = proj_s[:, OFF_AV:OFF_AV + WIDTH_A]
    new_q = proj_s[:, OFF_AQ:OFF_AQ + WIDTH_A].reshape(bs, HEADS_A, HEAD_DIM)
    new_k = new_k.reshape(bs, HEADS_A, HEAD_DIM)
    new_v = new_v.reshape(bs, HEADS_A, HEAD_DIM)
    att_s = _attn_sample(new_q, new_k, new_v, ck, cv, rel_bias)
    s_win_k, s_win_v = _cache_shift(ck, cv, new_k[:, None], new_v[:, None], CACHE_ROWS)
    dn_s, s_dn_conv, s_dn_rec = _dn_sample(proj_s[:, None], gates_s[:, None], state_dn_conv[l],
                                           state_dn_rec[l], conv_w, alog_vec, dtb_vec, norm_w)
    x1_s, h2_s = _outproj(att_s.reshape(bs, WIDTH_A), dn_s.reshape(bs, WIDTH_BV),
                          wo, xs, ln2, ln3, bs)
    prev_s = jnp.swapaxes(state_ffn_conv[l], 0, 1)
    y_s, up_s = _ffn(h2_s, wf_in, fcw, fcb, wf_out, x1_s, ln4, prev_s, bs, FFN_TF, 1)
    s_ffn_conv = jnp.stack([prev_s[1], up_s], axis=1)[None]

    return (y_p.reshape(bp, sp, d), y_s.reshape(bs, 1, d),
            p_win_k, p_win_v, p_dn_conv, p_dn_rec[None], p_ffn_conv,
            s_win_k[None], s_win_v[None], s_dn_conv[None], s_dn_rec[None], s_ffn_conv)
```

```python
import functools
import math

import numpy as np
import jax
import jax.numpy as jnp
from jax import lax
from jax.experimental import pallas as pl
from jax.experimental.pallas import tpu as pltpu

F32 = jnp.float32
BF16 = jnp.bfloat16

D_MODEL = 2048
HEAD_DIM = 128
WIDTH_A = 1024
HEADS_A = 8
DILATIONS = (1, 4, 16)
BLK = 128
N_BUCKETS = 32
MAX_DISTANCE = 2048
DK = 128
DV = 128
V_HEADS_B = 8
QK_HEADS_B = 4
WIDTH_BQK = 512
WIDTH_BV = 1024
CONV_W = 4
CONV_DIM = 2048
CHUNK = 128
SUB = 64
D_FF = 5632
FFN_CONV_W = 3
EPS = 1e-6
NEG = -1e30
ATT_SCALE = HEAD_DIM ** -0.5
QK_SCALE = DK ** -0.5

OFF_AQ, OFF_AK, OFF_AV = 0, 1024, 2048
OFF_BQ, OFF_BK, OFF_BV, OFF_BZ = 3072, 3584, 4096, 5120
OFF_GATES = 6144
PROJ_MAIN = 6144

MIB = 2 ** 20
VMEM_SMALL = 40
VMEM_MID = 48
VMEM_FFN = 57


def _params(semantics, vmem_mib):
    return pltpu.CompilerParams(dimension_semantics=semantics, vmem_limit_bytes=vmem_mib * MIB)


def _bdot(a, b):
    return jnp.dot(a.astype(BF16), b.astype(BF16), preferred_element_type=F32)


def _bdot_nt(a, b):
    return lax.dot_general(a.astype(BF16), b.astype(BF16), (((1,), (1,)), ((), ())),
                           preferred_element_type=F32)


def _bdot_tn(a, b):
    return lax.dot_general(a.astype(BF16), b.astype(BF16), (((0,), (0,)), ((), ())),
                           preferred_element_type=F32)


def _fdot(a, b):
    return jnp.dot(a, b, preferred_element_type=F32, precision=lax.Precision.HIGHEST)


def _silu(x):
    return x * (1.0 / (1.0 + jnp.exp(-x)))


def _sigmoid(x):
    return 1.0 / (1.0 + jnp.exp(-x))


def _softplus(x):
    return jnp.maximum(x, 0.0) + jnp.log(1.0 + jnp.exp(-jnp.abs(x)))


def _gelu_tanh(x):
    c = math.sqrt(2.0 / math.pi)
    half = 0.5 * x
    return half + half * jnp.tanh(x * (c + (c * 0.044715) * (x * x)))


def _rms(x, w):
    return x * lax.rsqrt(jnp.mean(x * x, axis=-1, keepdims=True) + EPS) * w


def _matmul_body(x_ref, w_ref, o_ref):
    o_ref[...] = jnp.dot(x_ref[...], w_ref[...], preferred_element_type=F32)


def _matmul(x, w, tm, tn, name):
    m, k = x.shape
    n = w.shape[1]
    return pl.pallas_call(
        _matmul_body,
        grid=(n // tn, m // tm),
        in_specs=[pl.BlockSpec((tm, k), lambda j, i: (i, 0)),
                  pl.BlockSpec((k, tn), lambda j, i: (0, j))],
        out_specs=pl.BlockSpec((tm, tn), lambda j, i: (i, j)),
        out_shape=jax.ShapeDtypeStruct((m, n), F32),
        compiler_params=_params(("arbitrary", "arbitrary"), VMEM_MID),
        name=name,
    )(x, w)


def _norm_matmul_body(x_ref, ln_ref, w_ref, o_ref, h_ref):
    tm = x_ref.shape[0]
    piece = min(tm, NORM_ROWS)
    for r in range(0, tm, piece):
        rows = slice(r, r + piece)
        h = _rms(x_ref[rows, :], ln_ref[...]).astype(BF16)
        h_ref[rows, :] = h
        o_ref[rows, :] = jnp.dot(h, w_ref[...], preferred_element_type=F32)


def _norm_matmul(x, ln, w, n, tm, tn, name):
    m, k = x.shape
    return pl.pallas_call(
        _norm_matmul_body,
        grid=(m // tm, n // tn),
        in_specs=[pl.BlockSpec((tm, k), lambda i, j: (i, 0)),
                  pl.BlockSpec((1, k), lambda i, j: (0, 0)),
                  pl.BlockSpec((k, tn), lambda i, j: (0, j))],
        out_specs=[pl.BlockSpec((tm, tn), lambda i, j: (i, j)),
                   pl.BlockSpec((tm, k), lambda i, j: (i, 0))],
        out_shape=[jax.ShapeDtypeStruct((m, n), F32), jax.ShapeDtypeStruct((m, k), BF16)],
        compiler_params=_params(("arbitrary", "arbitrary"), VMEM_MID),
        name=name,
    )(x, ln, w)


def _rel_bucket_np(dist):
    dist = np.asarray(dist, np.int64)
    max_exact = N_BUCKETS // 2
    d = np.maximum(dist, 1).astype(np.float64)
    val = np.log(d / max_exact) / math.log(MAX_DISTANCE / max_exact) * (N_BUCKETS - max_exact)
    frac = np.abs(val - np.round(val))
    near = (frac < 2e-5) &(dist >= max_exact) & (dist != max_exact) & (dist < MAX_DISTANCE)
    assert not near.any(), "distance on a bucket boundary"
    val = np.where(dist == max_exact, 0.0, val)
    large = np.minimum(max_exact + np.trunc(val).astype(np.int64), N_BUCKETS - 1)
    return np.where(dist < max_exact, dist, large).astype(np.int32)


def _prompt_bucket_tables():
    qi = np.arange(BLK)[:, None]
    kj = np.arange(2 * BLK)[None, :]
    delta = BLK + qi - kj
    inwin = (delta >= 0) & (delta <= BLK)
    tabs = []
    for dil in DILATIONS:
        b = _rel_bucket_np(np.clip(delta, 0, BLK) * dil)
        tabs.append(np.where(inwin, b, -1))
    return np.stack(tabs).astype(np.int32)


def _sample_bucket_tables():
    j = BLK - np.arange(BLK)
    return np.stack([_rel_bucket_np(j * dil)[None, :] for dil in DILATIONS]).astype(np.int32)


def _attn_prompt_body(bucket_ref, relb_ref, q_ref, k_ref, v_ref, o_ref, wk_ref, wv_ref,
                      bias_scr, acc_scr, m_scr, l_scr):
    h = pl.program_id(1)
    s, keep = k_ref.shape[1], wk_ref.shape[1] // HEADS_A
    wk_ref[0, pl.ds(h, keep, stride=HEADS_A), :] = k_ref[0, s - keep:s, :]
    wv_ref[0, pl.ds(h, keep, stride=HEADS_A), :] = v_ref[0, s - keep:s, :]
    col = lax.broadcasted_iota(jnp.int32, (BLK, 2 * BLK), 1)
    tables = _prompt_bucket_tables()
    for br in range(3):
        bk = bucket_ref[br]
        bias = jnp.zeros((BLK, 2 * BLK), F32)
        for kb in sorted(set(tables[br].ravel().tolist()) - {-1}):
            bias = jnp.where(bk == kb, relb_ref[kb, h], bias)
        full = jnp.where(bk >= 0, bias, NEG)
        bias_scr[2 * br] = full
        bias_scr[2 * br + 1] = jnp.where(col >= BLK, full, NEG)

    def run_branch(br, dil, is_first_branch, is_last_branch):
        shift = int(math.log2(dil))
        span = BLK * dil
        stride = None if dil == 1 else dil

        def rows(start):
            return pl.ds(start, BLK, stride=stride) if stride else pl.ds(start, BLK)

        nb = q_ref.shape[1] // span
        run_len = min(nb, ATTN_UNROLL)
        runs_per_it = ATTN_UNROLL // run_len
        runs_per_res = nb // run_len
        starts_at_zero = runs_per_res == 1

        def tasks(it, carry):
            q_starts, firsts, qs_, ks_, vs_ = [], [], [], [], []
            for rr in range(runs_per_it):
                ri = it * runs_per_it + rr
                n0 = (ri % runs_per_res) * run_len
                if dil == 1:
                    base = pl.multiple_of(n0 * span, BLK)
                else:
                    base = n0 * span + ri // runs_per_res
                first = jnp.where(n0 == 0, 1, 0)
                starts = [base + u * span for u in range(run_len)]
                kb = [k_ref[0, rows(st), :].astype(BF16) for st in starts]
                vb = [v_ref[0, rows(st), :].astype(BF16) for st in starts]
                if starts_at_zero:
                    k_prev, v_prev = None, None
                else:
                    p_start = base - span * (1 - first)
                    if dil == 1:
                        p_start = pl.multiple_of(p_start, BLK)
                    k_prev = k_ref[0, rows(p_start), :].astype(BF16)
                    v_prev = v_ref[0, rows(p_start), :].astype(BF16)
                for u, st in enumerate(starts):
                    q_starts.append(st)
                    qs_.append(q_ref[0, rows(st), :].astype(BF16))
                    kp, vp = (k_prev, v_prev) if u == 0 else (kb[u - 1], vb[u - 1])
                    if kp is None:
                        firsts.append(None)
                        ks_.append(kb[u])
                        vs_.append(vb[u])
                    else:
                        firsts.append(first if u == 0 else 0)
                        ks_.append(jnp.concatenate([kp, kb[u]], axis=0))
                        vs_.append(jnp.concatenate([vp, vb[u]], axis=0))
            if not is_first_branch:
                runs = [(m_scr[rows(qs), :], l_scr[rows(qs), :], acc_scr[rows(qs), :]) for qs in q_starts]
            ss = [_bdot_nt(q, k) * ATT_SCALE
                  + (bias_scr[2 * br, :, BLK:] if f is None else bias_scr[2 * br + f])
                  for q, k, f in zip(qs_, ks_, firsts)]
            ms = [jnp.max(s, axis=-1, keepdims=True) for s in ss]
            ps_ = [jnp.exp(s - m) for s, m in zip(ss, ms)]
            accs = [_bdot(p, jnp.concatenate([v, jnp.ones_like(v)], axis=1)) for p, v in zip(ps_, vs_)]
            outs = []
            for u in range(ATTN_UNROLL):
                m_b = jnp.broadcast_to(ms[u], (BLK, HEAD_DIM))
                l_b = accs[u][:, HEAD_DIM:]
                acc_t = accs[u][:, :HEAD_DIM]
                if not is_first_branch:
                    m_run, l_run, acc_run = runs[u]
                    m_new = jnp.maximum(m_run, m_b)
                    a = jnp.exp(m_run - m_new)
                    b = jnp.exp(m_b - m_new)
                    acc_t = a * acc_run + b * acc_t
                    l_b = a * l_run + b * l_b
                    m_b = m_new
                outs.append((m_b, l_b, acc_t))
            for qs, (m_b, l_b, acc_t) in zip(q_starts, outs):
                if is_last_branch:
                    o_ref[0, rows(qs), :] = (acc_t / l_b).astype(o_ref.dtype)
                else:
                    m_scr[rows(qs), :] = m_b
                    l_scr[rows(qs), :] = l_b
                    acc_scr[rows(qs), :] = acc_t
            return carry

        lax.fori_loop(0, nb * dil // ATTN_UNROLL, tasks, 0)

    run_branch(2, 16, True, False)
    run_branch(1, 4, False, False)
    run_branch(0, 1, False, True)


def _attn_prompt(proj3, rel_bias, keep):
    b, s, _ = proj3.shape
    buckets = jnp.asarray(_prompt_bucket_tables())
    blk = (1, s, HEAD_DIM)
    win = pl.BlockSpec((1, keep * HEADS_A, HEAD_DIM), lambda i, h: (i, 0, 0), pipeline_mode=pl.Buffered(1))
    win_shape = jax.ShapeDtypeStruct((b, keep * HEADS_A, HEAD_DIM), F32)
    return pl.pallas_call(
        _attn_prompt_body,
        grid=(b, HEADS_A),
        in_specs=[pl.BlockSpec((3, BLK, 2 * BLK), lambda i, h: (0, 0, 0)),
                  pl.BlockSpec(memory_space=pltpu.SMEM),
                  pl.BlockSpec(blk, lambda i, h: (i, 0, OFF_AQ // HEAD_DIM + h)),
                  pl.BlockSpec(blk, lambda i, h: (i, 0, OFF_AK // HEAD_DIM + h)),
                  pl.BlockSpec(blk, lambda i, h: (i, 0, OFF_AV // HEAD_DIM + h))],
        out_specs=[pl.BlockSpec(blk, lambda i, h: (i, 0, h)), win, win],
        out_shape=[jax.ShapeDtypeStruct((b, s, WIDTH_A), BF16), win_shape, win_shape],
        scratch_shapes=[pltpu.VMEM((6, BLK, 2 * BLK), F32),
                        pltpu.VMEM((s, HEAD_DIM), F32),
                        pltpu.VMEM((s, HEAD_DIM), F32),
                        pltpu.VMEM((s, HEAD_DIM), F32)],
        compiler_params=_params(("arbitrary", "arbitrary"), VMEM_MID),
        name="attn_prompt",
    )(buckets, rel_bias, proj3, proj3, proj3)


def _attn_sample_body(bucket_ref, relbt_ref, q_ref, kn_ref, vn_ref,
                      k1_ref, k4_ref, k16_ref, v1_ref, v4_ref, v16_ref, o_ref, bias_scr):
    relbt = relbt_ref[...]
    tile = (HEADS_A, HEAD_DIM)

    @pl.when(pl.program_id(0) == 0)
    def _():
        for br in range(3):
            bk = bucket_ref[br]
            bias = jnp.zeros((BLK,) + tile, F32)
            for kb in range(N_BUCKETS):
                col = jnp.broadcast_to(relbt[:, kb:kb + 1], tile)
                bias = jnp.where(bk == kb, col[None], bias)
            bias_scr[br] = bias

    def lane_sum(x):
        return jnp.broadcast_to(jnp.sum(x, axis=-1, keepdims=True), x.shape)

    q = q_ref[0]
    s_self = lane_sum(q * kn_ref[0]) * ATT_SCALE + jnp.broadcast_to(relbt[:, 0:1], tile)
    scores = []
    m = s_self
    for br, k_ref in enumerate((k1_ref, k4_ref, k16_ref)):
        s = lane_sum(k_ref[...] * q[None]) * ATT_SCALE + bias_scr[br]
        scores.append(s)
        m = jnp.maximum(m, jnp.max(s, axis=0))
    p_self = 3.0 * jnp.exp(s_self - m)
    l = p_self
    acc = p_self * vn_ref[0]
    for s, v_ref in zip(scores, (v1_ref, v4_ref, v16_ref)):
        p = jnp.exp(s - m[None])
        l = l + jnp.sum(p, axis=0)
        acc = acc + jnp.sum(p * v_ref[...], axis=0)
    o_ref[0] = (acc / l).astype(o_ref.dtype)


def _attn_sample(q, k_new, v_new, cache_k, cache_v, rel_bias):
    b, past = cache_k.shape[:2]
    tile = (HEADS_A, HEAD_DIM)
    buckets = jnp.asarray(np.broadcast_to(_sample_bucket_tables().reshape(3, BLK, 1, 1), (3, BLK) + tile))
    row = pl.BlockSpec((1,) + tile, lambda i: (i, 0, 0))
    views, specs = [], []
    for cache in (cache_k, cache_v):
        for dil in DILATIONS:
            views.append(cache.reshape((b, past // dil, dil) + tile))
            last = past // dil // BLK - 1
            specs.append(pl.BlockSpec((None, BLK, None) + tile,
                                      functools.partial(lambda last, i: (i, last, 0, 0, 0), last)))
    return pl.pallas_call(
        _attn_sample_body,
        grid=(b,),
        in_specs=[pl.BlockSpec((3, BLK) + tile, lambda i: (0, 0, 0, 0)),
                  pl.BlockSpec((HEADS_A, N_BUCKETS), lambda i: (0, 0)),
                  row, row, row] + specs,
        out_specs=row,
        out_shape=jax.ShapeDtypeStruct((b,) + tile, BF16),
        scratch_shapes=[pltpu.VMEM((3, BLK) + tile, F32)],
        compiler_params=_params(("arbitrary",), VMEM_SMALL),
        name="attn_sample",
    )(buckets, rel_bias.T, q, k_new, v_new, *views)


INPROJ_TM = 1024
INPROJ_TN = 1024
NORM_ROWS = 128
ATTN_UNROLL = 8
DN_TT = 256
GROUP = 4
OUTPROJ_TM = 512
OUTPROJ_ROWS = 128
CACHE_ROWS = 1024
FFN_TM = 1024
FFN_TF = 512
FFN_ROWS = 64
FFN_COLS = 512
FFN_PIECE = 256


def _dn_prompt_body(q_ref, k_ref, v_ref, z_ref, h_ref, wgate_ref, cw_ref, cs_ref, s0_ref, alog_ref, dtb_ref, nw_ref,
                    o_ref, s_out_ref,
                    s_scr, e_scr, qn_scr, kn_scr, vv_scr, g_scr, beta_scr,
                    w_scr, u_scr, qg_scr, kdt_scr, attn_scr, gl_scr, o_scr, *, tt):
    t = pl.program_id(1)
    nt = pl.num_programs(1)

    @pl.when(t == 0)
    def _():
        s_scr[...] = s0_ref[0]
        e_scr[5:8, :] = cs_ref[0]

    e_scr[8:8 + tt, 0:WIDTH_BQK] = q_ref[0]
    e_scr[8:8 + tt, WIDTH_BQK:2 * WIDTH_BQK] = k_ref[0]
    e_scr[8:8 + tt, 2 * WIDTH_BQK:CONV_DIM] = v_ref[0]

    def l2n(x):
        return x * lax.rsqrt(jnp.sum(x * x, axis=-1, keepdims=True) + EPS)

    for c0 in range(0, CONV_DIM, DK):
        cols = slice(c0, c0 + DK)
        w = cw_ref[:, cols]
        y = w[0:1, :] * e_scr[5:5 + tt, cols]
        for i in range(1, CONV_W):
            y = y + w[i:i + 1, :] * e_scr[5 + i:5 + i + tt, cols]
        y = _silu(y)
        if c0 < WIDTH_BQK:
            qn_scr[:, cols] = l2n(y) * QK_SCALE
        elif c0 < 2 * WIDTH_BQK:
            kn_scr[:, c0 - WIDTH_BQK:c0 - WIDTH_BQK + DK] = l2n(y)
        else:
            vv_scr[:, c0 - 2 * WIDTH_BQK:c0 - 2 * WIDTH_BQK + DK] = y
    e_scr[5:8, :] = e_scr[tt + 5:tt + 8, :]
    gates = jnp.dot(h_ref[0], wgate_ref[...], preferred_element_type=F32)
    beta_scr[...] = _sigmoid(gates)
    g_scr[...] = -jnp.exp(alog_ref[...]) * _softplus(gates + dtb_ref[...])

    ri = lax.broadcasted_iota(jnp.int32, (CHUNK, CHUNK), 0)
    ci = lax.broadcasted_iota(jnp.int32, (CHUNK, CHUNK), 1)
    tri = ri >= ci
    strict = ri > ci
    same_sub = (ri // SUB) == (ci // SUB)
    tril_ones = tri.astype(F32)
    nw = nw_ref[...]

    for c in range(tt // CHUNK):
        rows = slice(c * CHUNK, (c + 1) * CHUNK)
        beta_all = beta_scr[rows, :]
        gc_all = _fdot(tril_ones, g_scr[rows, :])
        gc_all_t = gc_all.T
        g_scr[rows, :] = gc_all
        for hq0 in range(0, QK_HEADS_B, GROUP):
            units = []
            for hq in range(hq0, hq0 + GROUP):
                qn = qn_scr[rows, hq * DK:(hq + 1) * DK]
                kn = kn_scr[rows, hq * DK:(hq + 1) * DK]
                kk = _bdot_nt(kn, kn)
                qk = _bdot_nt(qn, kn)
                for hv in range(2 * hq, 2 * hq + 2):
                    beta = beta_all[:, hv:hv + 1]
                    gc = gc_all[:, V_HEADS_B + hv:V_HEADS_B + hv + 1]
                    gc_row = gc_all_t[V_HEADS_B + hv:V_HEADS_B + hv + 1, :]
                    gc_last = gc_row[:, CHUNK - 1:CHUNK]
                    decay = jnp.exp(jnp.where(tri, gc - gc_row, NEG))
                    a = jnp.where(strict, beta * kk * decay, 0.0)
                    egc = jnp.exp(gc)
                    attn_scr[hv, rows, :] = (qk * decay).astype(BF16)
                    qg_scr[hv, rows, :] = (qn * egc).astype(BF16)
                    kd = kn * jnp.exp(gc_last - gc)
                    kdt_scr[hv, c * DK:(c + 1) * DK, :] = kd.T.astype(BF16)
                    gl_scr[hv, c * 8:(c + 1) * 8, :] = jnp.broadcast_to(jnp.exp(gc_last), (8, DV))
                    units.append((hv, a))
            ds = [jnp.where(same_sub, a, 0.0) for _, a in units]
            ns = [-dd for dd in ds]
            pws = ds
            for _ in range(SUB.bit_length() - 2):
                pws = [_bdot(pw, pw) for pw in pws]
                ns = [n + pw + _bdot(n, pw) for n, pw in zip(ns, pws)]
            ls = [jnp.where(same_sub, 0.0, a) for _, a in units]
            ps = [lo + _bdot(lo, n) for lo, n in zip(ls, ns)]
            ns = [n - (p + _bdot(n, p)) for n, p in zip(ns, ps)]
            xs = []
            for hv, _ in units:
                beta = beta_scr[rows, hv:hv + 1]
                kscale = beta * jnp.exp(g_scr[rows, V_HEADS_B + hv:V_HEADS_B + hv + 1])
                xs.append(jnp.concatenate([kn_scr[rows, (hv // 2) * DK:(hv // 2 + 1) * DK] * kscale,
                                           vv_scr[rows, hv * DV:(hv + 1) * DV] * beta], axis=-1))
            wus = [x + _bdot(n, x) for n, x in zip(ns, xs)]
            for wu, (hv, _) in zip(wus, units):
                w_scr[hv, rows, :] = wu[:, :DK].astype(BF16)
                u_scr[hv, rows, :] = wu[:, DK:]

    heads = range(V_HEADS_B)
    for c in range(tt // CHUNK):
        rows = slice(c * CHUNK, (c + 1) * CHUNK)
        states = [s_scr[hv] for hv in heads]
        states_b = [s.astype(BF16) for s in states]
        v_news = [u_scr[hv, rows, :] - jnp.dot(w_scr[hv, rows, :], states_b[hv], preferred_element_type=F32)
                  for hv in heads]
        v_news_b = [v.astype(BF16) for v in v_news]
        for hv in heads:
            s_scr[hv] = (states[hv] * gl_scr[hv, c * 8:c * 8 + 1, :]
                         + jnp.dot(kdt_scr[hv, c * DK:(c + 1) * DK, :], v_news_b[hv],
                                   preferred_element_type=F32))
        for hv in heads:
            o_scr[rows, hv * DV:(hv + 1) * DV] = (
                jnp.dot(qg_scr[hv, rows, :], states_b[hv], preferred_element_type=F32)
                + jnp.dot(attn_scr[hv, rows, :], v_news_b[hv], preferred_element_type=F32))

    for hv in heads:
        o = o_scr[:, hv * DV:(hv + 1) * DV]
        z = z_ref[0, :, hv * DV:(hv + 1) * DV]
        o = o * lax.rsqrt(jnp.mean(o * o, axis=-1, keepdims=True) + EPS) * nw * _silu(z)
        o_ref[0, :, hv * DV:(hv + 1) * DV] = o.astype(o_ref.dtype)

    @pl.when(t == nt - 1)
    def _():
        s_out_ref[0] = s_scr[...]


def _dn_prompt(proj3, h3, w_gate, conv_state, s0, conv_w, alog_vec, dtb_vec, norm_w, tt):
    b, s, _ = proj3.shape
    body = functools.partial(_dn_prompt_body, tt=tt)
    nh = V_HEADS_B
    in_specs = [
        pl.BlockSpec((1, tt, WIDTH_BQK), lambda i, t: (i, t, OFF_BQ // WIDTH_BQK)),
        pl.BlockSpec((1, tt, WIDTH_BQK), lambda i, t: (i, t, OFF_BK // WIDTH_BQK)),
        pl.BlockSpec((1, tt, WIDTH_BV), lambda i, t: (i, t, OFF_BV // WIDTH_BV)),
        pl.BlockSpec((1, tt, WIDTH_BV), lambda i, t: (i, t, OFF_BZ // WIDTH_BV)),
        pl.BlockSpec((1, tt, D_MODEL), lambda i, t: (i, t, 0)),
        pl.BlockSpec((D_MODEL, 128), lambda i, t: (0, 0)),
        pl.BlockSpec((CONV_W, CONV_DIM), lambda i, t: (0, 0)),
        pl.BlockSpec((1, CONV_W - 1, CONV_DIM), lambda i, t: (i, 0, 0)),
        pl.BlockSpec((1, nh, DK, DV), lambda i, t: (i, 0, 0, 0)),
        pl.BlockSpec((1, 128), lambda i, t: (0, 0)),
        pl.BlockSpec((1, 128), lambda i, t: (0, 0)),
        pl.BlockSpec((1, DV), lambda i, t: (0, 0)),
    ]
    return pl.pallas_call(
        body,
        grid=(b, s // tt),
        in_specs=in_specs,
        out_specs=[pl.BlockSpec((1, tt, WIDTH_BV), lambda i, t: (i, t, 0)),
                   pl.BlockSpec((1, nh, DK, DV), lambda i, t: (i, 0, 0, 0))],
        out_shape=[jax.ShapeDtypeStruct((b, s, WIDTH_BV), BF16),
                   jax.ShapeDtypeStruct((b, nh, DK, DV), F32)],
        scratch_shapes=[pltpu.VMEM((nh, DK, DV), F32),
                        pltpu.VMEM((tt + 8, CONV_DIM), F32),
                        pltpu.VMEM((tt, WIDTH_BQK), F32),
                        pltpu.VMEM((tt, WIDTH_BQK), F32),
                        pltpu.VMEM((tt, WIDTH_BV), F32),
                        pltpu.VMEM((tt, 128), F32),
                        pltpu.VMEM((tt, 128), F32),
                        pltpu.VMEM((nh, tt, DK), BF16),
                        pltpu.VMEM((nh, tt, DV), F32),
                        pltpu.VMEM((nh, tt, DK), BF16),
                        pltpu.VMEM((nh, tt // CHUNK * DK, CHUNK), BF16),
                        pltpu.VMEM((nh, tt, CHUNK), BF16),
                        pltpu.VMEM((nh, tt // CHUNK * 8, DV), F32),
                        pltpu.VMEM((tt, WIDTH_BV), F32)],
        compiler_params=_params(("arbitrary", "arbitrary"), VMEM_MID),
        name="deltanet_prompt",
    )(proj3, proj3, proj3, proj3, h3, w_gate, conv_w, conv_state, s0, alog_vec, dtb_vec, norm_w)


def _dn_sample_body(proj_ref, gates_ref, cw_ref, cs_ref, s0_ref, alog_ref, dtb_ref, nw_ref,
                    o_ref, cs_out_ref, s_out_ref):
    pre = proj_ref[0, :, OFF_BQ:OFF_BQ + CONV_DIM]
    buf = cs_ref[0]
    w = cw_ref[...]
    y = w[CONV_W - 1:CONV_W, :] * pre
    for i in range(CONV_W - 1):
        y = y + w[i:i + 1, :] * buf[i:i + 1, :]
    y = _silu(y)
    cs_out_ref[0, 0:CONV_W - 2, :] = buf[1:CONV_W - 1, :]
    cs_out_ref[0, CONV_W - 2:CONV_W - 1, :] = pre

    gates = gates_ref[0]
    beta_all = _sigmoid(gates)
    g_all = -jnp.exp(alog_ref[...]) * _softplus(gates + dtb_ref[...])
    nw = nw_ref[...]

    def l2n(x):
        return x * lax.rsqrt(jnp.sum(x * x, axis=-1, keepdims=True) + EPS)

    row8 = lax.broadcasted_iota(jnp.int32, (8, DK), 0) == 0
    for hv in range(V_HEADS_B):
        hq = hv // 2
        q = l2n(y[:, hq * DK:(hq + 1) * DK]) * QK_SCALE
        k = l2n(y[:, WIDTH_BQK + hq * DK:WIDTH_BQK + (hq + 1) * DK])
        v = y[:, 2 * WIDTH_BQK + hv * DV:2 * WIDTH_BQK + (hv + 1) * DV]
        beta = beta_all[:, hv:hv + 1]
        g = g_all[:, V_HEADS_B + hv:V_HEADS_B + hv + 1]
        eg = jnp.exp(g)
        state = s0_ref[0, hv]

        def pad8(x):
            return jnp.where(row8, jnp.broadcast_to(x, (8, x.shape[-1])), 0.0)

        v_new = v * beta - _bdot(pad8(k * (beta * eg)), state)[0:1, :]
        qk = jnp.sum(q.astype(BF16).astype(F32) * k.astype(BF16).astype(F32), axis=-1, keepdims=True)
        o = _bdot(pad8(q * eg), state)[0:1, :] + qk.astype(BF16).astype(F32) * v_new.astype(BF16).astype(F32)
        s_out_ref[0, hv] = state * eg + _bdot_tn(pad8(k), pad8(v_new))
        z = proj_ref[0, :, OFF_BZ + hv * DV:OFF_BZ + (hv + 1) * DV]
        o = o * lax.rsqrt(jnp.mean(o * o, axis=-1, keepdims=True) + EPS) * nw * _silu(z)
        o_ref[0, :, hv * DV:(hv + 1) * DV] = o.astype(o_ref.dtype)


def _dn_sample(proj, gates, conv_state, s0, conv_w, alog_vec, dtb_vec, norm_w):
    b = proj.shape[0]
    return pl.pallas_call(
        _dn_sample_body,
        grid=(b,),
        in_specs=[pl.BlockSpec((1, 1, PROJ_MAIN), lambda i: (i, 0, 0)),
                  pl.BlockSpec((1, 1, 128), lambda i: (i, 0, 0)),
                  pl.BlockSpec((CONV_W, CONV_DIM), lambda i: (0, 0)),
                  pl.BlockSpec((1, CONV_W - 1, CONV_DIM), lambda i: (i, 0, 0)),
                  pl.BlockSpec((1, V_HEADS_B, DK, DV), lambda i: (i, 0, 0, 0)),
                  pl.BlockSpec((1, 128), lambda i: (0, 0)),
                  pl.BlockSpec((1, 128), lambda i: (0, 0)),
                  pl.BlockSpec((1, DV), lambda i: (0, 0))],
        out_specs=[pl.BlockSpec((1, 1, WIDTH_BV), lambda i: (i, 0, 0)),
                   pl.BlockSpec((1, CONV_W - 1, CONV_DIM), lambda i: (i, 0, 0)),
                   pl.BlockSpec((1, V_HEADS_B, DK, DV), lambda i: (i, 0, 0, 0))],
        out_shape=[jax.ShapeDtypeStruct((b, 1, WIDTH_BV), BF16),
                   jax.ShapeDtypeStruct((b, CONV_W - 1, CONV_DIM), F32),
                   jax.ShapeDtypeStruct((b, V_HEADS_B, DK, DV), F32)],
        compiler_params=_params(("arbitrary",), VMEM_SMALL),
        name="deltanet_sample",
    )(proj, gates, conv_w, conv_state, s0, alog_vec, dtb_vec, norm_w)


def _outproj_body(att_ref, dn_ref, wa_ref, wb_ref, x_ref, lnpost_ref, lnpre_ref, x1_ref, h2_ref):
    tm = x_ref.shape[0]
    piece = min(tm, OUTPROJ_ROWS)
    for rows in [slice(r, r + piece) for r in range(0, tm, piece)]:
        mix = (jnp.dot(att_ref[rows, :], wa_ref[...], preferred_element_type=F32)
               + jnp.dot(dn_ref[rows, :], wb_ref[...], preferred_element_type=F32))
        x1 = x_ref[rows, :] + _rms(mix, lnpost_ref[...])
        x1_ref[rows, :] = x1
        h2_ref[rows, :] = _rms(x1, lnpre_ref[...]).astype(h2_ref.dtype)


def _outproj(att, dn, w, x, ln_post, ln_pre, tm):
    m, d = x.shape
    assert WIDTH_A == WIDTH_BV
    return pl.pallas_call(
        _outproj_body,
        grid=(m // tm,),
        in_specs=[pl.BlockSpec((tm, WIDTH_A), lambda i: (i, 0)),
                  pl.BlockSpec((tm, WIDTH_BV), lambda i: (i, 0)),
                  pl.BlockSpec((WIDTH_A, d), lambda i: (0, 0)),
                  pl.BlockSpec((WIDTH_BV, d), lambda i: (1, 0)),
                  pl.BlockSpec((tm, d), lambda i: (i, 0)),
                  pl.BlockSpec((1, d), lambda i: (0, 0)),
                  pl.BlockSpec((1, d), lambda i: (0, 0))],
        out_specs=[pl.BlockSpec((tm, d), lambda i: (i, 0)),
                   pl.BlockSpec((tm, d), lambda i: (i, 0))],
        out_shape=[jax.ShapeDtypeStruct((m, d), F32),
                   jax.ShapeDtypeStruct((m, d), BF16)],
        compiler_params=_params(("arbitrary",), VMEM_MID),
        name="outproj",
    )(att, dn, w, w, x, ln_post, ln_pre)


def _ffn_body(*refs, tm, tiles_per_seq, single_token):
    if single_token:
        (h_ref, wg_ref, wv_ref, cwb_ref, wo_prev_ref, wo_last_ref, x1_hbm, ln_ref, pg_ref, pv_ref,
         o_ref, ng_ref, nv_ref, eg_scr, ev_scr, carry_scr, act_scr, x1_scr, x1_sem) = refs
    else:
        (h_ref, wg_ref, wv_ref, cwb_ref, wo_prev_ref, wo_last_ref, x1_hbm, ln_ref, prev_ref,
         o_ref, new_ref, eg_scr, ev_scr, carry_scr, act_scr, x1_scr, x1_sem) = refs
    i = pl.program_id(0)
    j = pl.program_id(1)
    nj = pl.num_programs(1)
    d = o_ref.shape[-1]
    tf = act_scr.shape[-1]
    cur = j % 2
    act_cur = act_scr.at[cur]
    act_prev = act_scr.at[1 - cur]

    def x1_copy():
        return pltpu.make_async_copy(x1_hbm.at[pl.ds(pl.multiple_of(i * tm, tm), tm), :], x1_scr, x1_sem)

    def down_proj(act_ref, wo_ref):
        for n in range(0, d, FFN_COLS):
            o_ref[:, n:n + FFN_COLS] += jnp.dot(act_ref[...], wo_ref[:, n:n + FFN_COLS],
                                                preferred_element_type=F32)

    @pl.when(j == 0)
    def _():
        x1_copy().start(priority=1)
        o_ref[...] = jnp.zeros_like(o_ref)
        act_prev[...] = jnp.zeros_like(act_prev)

    if single_token:
        def up_conv(w_ref, tile, prev_ref, new_ref):
            up = jnp.dot(h_ref[...], w_ref[...], preferred_element_type=F32)
            cw = cwb_ref[tile]
            new_ref[...] = up
            return cw[0:1, :] * prev_ref[0] + cw[1:2, :] * prev_ref[1] + cw[2:3, :] * up + cw[3:4, :]

        gate = up_conv(wg_ref, j, pg_ref, ng_ref)
        val = up_conv(wv_ref, nj + j, pv_ref, nv_ref)
        act_cur[...] = (_gelu_tanh(gate) * val).astype(BF16)
        down_proj(act_prev, wo_prev_ref)
    else:
        first_tile = i % tiles_per_seq == 0
        pieces = [slice(c, c + FFN_PIECE) for c in range(0, tf, FFN_PIECE)]

        def up_proj(cols):
            for w_ref, tile, e_scr, slot in ((wg_ref, j, eg_scr, 0), (wv_ref, nj + j, ev_scr, 1)):
                e_scr[8:8 + tm, cols] = jnp.dot(h_ref[...], w_ref[:, cols], preferred_element_type=F32)
                e_scr[6:8, cols] = jnp.where(first_tile, prev_ref[0, tile, :, cols],
                                             carry_scr[slot, j, 6:8, cols])
                tail = e_scr[tm + 6:tm + 8, cols]
                carry_scr[slot, j, 6:8, cols] = tail
                new_ref[0, tile, :, cols] = tail

        def conv(e_scr, tile, cols, r):
            cw = cwb_ref[tile, :, cols]
            return (cw[0:1, :] * e_scr[6 + r:6 + r + FFN_ROWS, cols]
                    + cw[1:2, :] * e_scr[7 + r:7 + r + FFN_ROWS, cols]
                    + cw[2:3, :] * e_scr[8 + r:8 + r + FFN_ROWS, cols] + cw[3:4, :])

        def conv_geglu(cols):
            for r in range(0, tm, FFN_ROWS):
                act_cur[r:r + FFN_ROWS, cols] = (_gelu_tanh(conv(eg_scr, j, cols, r))
                                                 * conv(ev_scr, nj + j, cols, r)).astype(BF16)

        up_proj(pieces[0])
        for c in range(1, len(pieces)):
            up_proj(pieces[c])
            conv_geglu(pieces[c - 1])
        down_proj(act_prev, wo_prev_ref)
        conv_geglu(pieces[-1])

    @pl.when(j == nj - 1)
    def _():
        x1_copy().wait()
        piece = min(tm, OUTPROJ_ROWS)
        for r in range(0, tm, piece):
            rows = slice(r, r + piece)
            f = o_ref[rows, :] + jnp.dot(act_cur[rows, :], wo_last_ref[...], preferred_element_type=F32)
            o_ref[rows, :] = x1_scr[rows, :] + _rms(f, ln_ref[...])


def _ffn(h2, w_in, conv_w, conv_b, w_out, x1, ln_post, prev, tm, tf, seq_len):
    m, d = h2.shape
    single = seq_len == 1
    nj = D_FF // tf
    tiles_per_seq = 1 if single else seq_len // tm
    cwb = jnp.concatenate([conv_w, conv_b], axis=0).reshape(FFN_CONV_W + 1, 2 * nj, tf).transpose(1, 0, 2)
    cwb_spec = pl.BlockSpec((2 * nj, FFN_CONV_W + 1, tf), lambda i, j: (0, 0, 0))
    if single:
        prev_args = (prev, prev)
        prev_specs = [pl.BlockSpec((2, tm, tf), lambda i, j: (0, i, j)),
                      pl.BlockSpec((2, tm, tf), lambda i, j: (0, i, nj + j))]
        new_specs = [pl.BlockSpec((tm, tf), lambda i, j: (i, j))] * 2
        new_shapes = [jax.ShapeDtypeStruct((m, D_FF), F32)] * 2
    else:
        prev_args = (prev.reshape(-1, 2, 2 * nj, tf).transpose(0, 2, 1, 3),)
        prev_specs = [pl.BlockSpec((1, 2 * nj, 2, tf), lambda i, j: (i // tiles_per_seq, 0, 0, 0))]
        new_specs = [pl.BlockSpec((1, 2 * nj, 2, tf), lambda i, j: (i, 0, 0, 0))]
        new_shapes = [jax.ShapeDtypeStruct((m // tm, 2 * nj, 2, tf), F32)]
    body = functools.partial(_ffn_body, tm=tm, tiles_per_seq=tiles_per_seq, single_token=single)
    once = dict(pipeline_mode=pl.Buffered(1)) if tm >= 1024 else {}
    outs = pl.pallas_call(
        body,
        grid=(m // tm, nj),
        in_specs=[pl.BlockSpec((tm, d), lambda i, j: (i, 0), **once),
                  pl.BlockSpec((d, tf), lambda i, j: (0, j)),
                  pl.BlockSpec((d, tf), lambda i, j: (0, nj + j)),
                  cwb_spec,
                  pl.BlockSpec((tf, d), lambda i, j: (jnp.maximum(j - 1, 0), 0)),
                  pl.BlockSpec((tf, d), lambda i, j: (nj - 1, 0), pipeline_mode=pl.Buffered(1)),
                  pl.BlockSpec(memory_space=pl.ANY),
                  pl.BlockSpec((1, d), lambda i, j: (0, 0))] + prev_specs,
        out_specs=[pl.BlockSpec((tm, d), lambda i, j: (i, 0), **once)] + new_specs,
        out_shape=[jax.ShapeDtypeStruct((m, d), F32)] + new_shapes,
        scratch_shapes=[pltpu.VMEM((tm + 8, tf), F32),
                        pltpu.VMEM((tm + 8, tf), F32),
                        pltpu.VMEM((2, nj, 8, tf), F32),
                        pltpu.VMEM((2, tm, tf), BF16),
                        pltpu.VMEM((tm, d), F32),
                        pltpu.SemaphoreType.DMA(())],
        compiler_params=_params(("arbitrary", "arbitrary"), VMEM_FFN),
        name="convffn",
    )(h2, w_in, w_in, cwb, w_out, w_out, x1, ln_post, *prev_args)
    if single:
        y, new_g, new_v = outs
        return y, jnp.concatenate([new_g, new_v], axis=-1)
    y, new = outs
    return y, new.transpose(0, 2, 1, 3).reshape(m // tm, 2, 2 * D_FF)


def _cache_shift_body(ck_ref, cv_ref, ck_next_ref, cv_next_ref, nk_ref, nv_ref, ok_ref, ov_ref, *, tr):
    last = pl.program_id(1) == pl.num_programs(1) - 1
    for c_ref, nxt_ref, n_ref, o_ref in ((ck_ref, ck_next_ref, nk_ref, ok_ref),
                                         (cv_ref, cv_next_ref, nv_ref, ov_ref)):
        o_ref[0, 0:tr - 1] = c_ref[0, 1:tr]
        o_ref[0, tr - 1] = jnp.where(last, n_ref[0, 0], nxt_ref[0, 0])


def _cache_shift(cache_k, cache_v, new_k, new_v, tr):
    nb, rows, nh, dh = cache_k.shape
    main = pl.BlockSpec((1, tr, nh, dh), lambda b, i: (b, i, 0, 0))
    nxt = pl.BlockSpec((1, 1, nh, dh), lambda b, i: (b, jnp.minimum((i + 1) * tr, rows - 1), 0, 0))
    new = pl.BlockSpec((1, 1, nh, dh), lambda b, i: (b, 0, 0, 0))
    shape = jax.ShapeDtypeStruct(cache_k.shape, cache_k.dtype)
    return pl.pallas_call(
        functools.partial(_cache_shift_body, tr=tr),
        grid=(nb, rows // tr),
        in_specs=[main, main, nxt, nxt, new, new],
        out_specs=[main, main],
        out_shape=[shape, shape],
        compiler_params=_params(("arbitrary", "arbitrary"), VMEM_SMALL),
        name="cache_shift",
    )(cache_k, cache_v, cache_k, cache_v, new_k, new_v)


def _lane_vec(values, offset):
    return jnp.zeros((1, 128), F32).at[0, offset:offset + V_HEADS_B].set(values.astype(F32))


def kernel(x_prompt, x_sample, cache_win_k, cache_win_v, state_dn_conv, state_dn_rec, state_ffn_conv,
           rel_bias, ln_mix_pre, w_in, dn_conv_w, dn_A_log, dn_dt_bias, dn_norm_w, w_out, ln_mix_post,
           ln_ffn_pre, w_ffn_in, ffn_conv_w, ffn_conv_b, w_ffn_out, ln_ffn_post):
    bp, sp, d = x_prompt.shape
    bs = x_sample.shape[0]
    l = 0

    w_main = w_in[l].astype(BF16)
    w_gate = jnp.pad(w_main[:, PROJ_MAIN:], ((0, 0), (0, 128 - 2 * V_HEADS_B)))
    wo = w_out[l].astype(BF16)
    wf_in = w_ffn_in[l].astype(BF16)
    wf_out = w_ffn_out[l].astype(BF16)
    ln1 = ln_mix_pre[l][None, :]
    ln2 = ln_mix_post[l][None, :]
    ln3 = ln_ffn_pre[l][None, :]
    ln4 = ln_ffn_post[l][None, :]
    conv_w = dn_conv_w[l]
    alog_vec = _lane_vec(dn_A_log[l], V_HEADS_B)
    dtb_vec = _lane_vec(dn_dt_bias[l], V_HEADS_B)
    norm_w = dn_norm_w[l][None, :]
    fcw = ffn_conv_w[l]
    fcb = ffn_conv_b[l][None, :]

    xp = x_prompt.reshape(bp * sp, d)
    proj_p, hp = _norm_matmul(xp, ln1, w_main, PROJ_MAIN, INPROJ_TM, INPROJ_TN, "inproj_prompt")
    proj3 = proj_p.reshape(bp, sp, PROJ_MAIN)
    keep = min(MAX_DISTANCE, sp)
    att_p, win_k, win_v = _attn_prompt(proj3, rel_bias, keep)
    dn_p, p_dn_rec = _dn_prompt(
        proj3, hp.reshape(bp, sp, d), w_gate,
        jnp.zeros((bp, CONV_W - 1, CONV_DIM), F32), jnp.zeros((bp, V_HEADS_B, DK, DV), F32),
        conv_w, alog_vec, dtb_vec, norm_w, DN_TT)
    x1_p, h2_p = _outproj(att_p.reshape(bp * sp, WIDTH_A), dn_p.reshape(bp * sp, WIDTH_BV),
                          wo, xp, ln2, ln3, OUTPROJ_TM)
    y_p, fc = _ffn(h2_p, wf_in, fcw, fcb, wf_out, x1_p, ln4,
                   jnp.zeros((bp, FFN_CONV_W - 1, 2 * D_FF), F32), FFN_TM, FFN_TF, sp)
    p_win_k = win_k.reshape(1, bp, keep, HEADS_A, HEAD_DIM)
    p_win_v = win_v.reshape(1, bp, keep, HEADS_A, HEAD_DIM)
    p_dn_conv = proj3[:, sp - (CONV_W - 1):, OFF_BQ:OFF_BQ + CONV_DIM][None]
    tiles = sp // FFN_TM
    p_ffn_conv = fc[tiles - 1::tiles][None]

    xs = x_sample.reshape(bs, d)
    proj_s, hs = _norm_matmul(xs, ln1, w_main, PROJ_MAIN, bs, INPROJ_TN, "inproj_sample")
    gates_s = _matmul(hs, w_gate, bs, 128, "gates_sample")
    ck = cache_win_k[l]
    cv = cache_win_v[l]
    new_k = proj_s[:, OFF_AK:OFF_AK + WIDTH_A]
    new_v = proj_s[:, OFF_AV:OFF_AV + WIDTH_A]
    new_q = proj_s[:, OFF_AQ:OFF_AQ + WIDTH_A].reshape(bs, HEADS_A, HEAD_DIM)
    new_k = new_k.reshape(bs, HEADS_A, HEAD_DIM)
    new_v = new_v.reshape(bs, HEADS_A, HEAD_DIM)
    att_s = _attn_sample(new_q, new_k, new_v, ck, cv, rel_bias)
    s_win_k, s_win_v = _cache_shift(ck, cv, new_k[:, None], new_v[:, None], CACHE_ROWS)
    dn_s, s_dn_conv, s_dn_rec = _dn_sample(proj_s[:, None], gates_s[:, None], state_dn_conv[l],
                                           state_dn_rec[l], conv_w, alog_vec, dtb_vec, norm_w)
    x1_s, h2_s = _outproj(att_s.reshape(bs, WIDTH_A), dn_s.reshape(bs, WIDTH_BV),
                          wo, xs, ln2, ln3, bs)
    prev_s = jnp.swapaxes(state_ffn_conv[l], 0, 1)
    y_s, up_s = _ffn(h2_s, wf_in, fcw, fcb, wf_out, x1_s, ln4, prev_s, bs, FFN_TF, 1)
    s_ffn_conv = jnp.stack([prev_s[1], up_s], axis=1)[None]

    return (y_p.reshape(bp, sp, d), y_s.reshape(bs, 1, d),
            p_win_k, p_win_v, p_dn_conv, p_dn_rec[None], p_ffn_conv,
            s_win_k[None], s_win_v[None], s_dn_conv[None], s_dn_rec[None], s_ffn_conv)
```
